```python
import math
import jax
import jax.numpy as jnp
from jax import lax
import numpy as np

D_MODEL = 1024
BATCH = 8
SEQ = 8192
DEPTH = 2

NORM_EPS = 1e-6
POOL_WINDOWS = (2, 4, 8, 16)
POOL_GROUPS = 4
POOL_WIDTH = D_MODEL // 2
POOL_GROUP_DIM = POOL_WIDTH // POOL_GROUPS
ATTN_HEADS = 8
ATTN_HEAD_DIM = 64
ATTN_WIDTH = ATTN_HEADS * ATTN_HEAD_DIM
Q_BLOCK = 128
SSD_HEAD_DIM = 64
SSD_WIDTH = D_MODEL
SSD_HEADS = SSD_WIDTH // SSD_HEAD_DIM
SSD_GROUPS = 2
SSD_STATE = 128
SSD_CONV = 4
SSD_CHUNK = 128
SSD_CONV_CH = SSD_WIDTH + 2 * SSD_GROUPS * SSD_STATE
N_BRANCH = 3
FFN_DIM = 2816
FFN_CONV = 3
IN_SPLITS = (POOL_WIDTH, ATTN_WIDTH, ATTN_WIDTH, ATTN_WIDTH, ATTN_HEADS,
             SSD_WIDTH, SSD_CONV_CH, SSD_HEADS, N_BRANCH * D_MODEL)
IN_TOTAL = sum(IN_SPLITS)

kernel_name = "hybrid_pool_fox_ssd_block"


def rmsnorm(x, w):
    xf = x.astype(jnp.float32)
    y = xf * lax.rsqrt(jnp.mean(xf * xf, axis=-1, keepdims=True) + NORM_EPS)
    return (y * w.astype(jnp.float32)).astype(x.dtype)


def split_columns(proj):
    offsets, acc = [], 0
    for n in IN_SPLITS[:-1]:
        acc += n
        offsets.append(acc)
    return jnp.split(proj, offsets, axis=-1)


def causal_dwconv(x, w, b):
    K, C = w.shape
    y = lax.conv_general_dilated(
        x, w[:, None, :].astype(x.dtype), window_strides=(1,),
        padding=[(K - 1, 0)], dimension_numbers=('NWC', 'WIO', 'NWC'),
        feature_group_count=C)
    return y + b.astype(x.dtype)


def pool_mixer(v, mix_w, scale):
    B_, S, _ = v.shape
    vf = v.astype(jnp.float32).reshape(B_, S, POOL_GROUPS, POOL_GROUP_DIM)
    cs = jnp.cumsum(vf, axis=1)
    t = jnp.arange(S)
    pooled = []
    for g, w in enumerate(POOL_WINDOWS):
        c = cs[:, :, g]
        lag = jnp.pad(c, ((0, 0), (w, 0), (0, 0)))[:, :S]
        cnt = jnp.minimum(t + 1, w).astype(jnp.float32)[None, :, None]
        pooled.append((c - lag) / cnt)
    d = (jnp.stack(pooled, axis=2) - vf).astype(v.dtype)
    y = jnp.einsum('bsgc,gcd->bsgd', d, mix_w)
    return y.reshape(B_, S, POOL_WIDTH) * scale


def forgetting_attention(q, k, v, f_logit, f_bias):
    B_, S, _ = q.shape
    H, Dh = ATTN_HEADS, ATTN_HEAD_DIM
    q = q.reshape(B_, S, H, Dh).transpose(0, 2, 1, 3)
    k = k.reshape(B_, S, H, Dh).transpose(0, 2, 1, 3)
    v = v.reshape(B_, S, H, Dh).transpose(0, 2, 1, 3)
    logf = jax.nn.log_sigmoid((f_logit + f_bias).astype(jnp.float32))
    c = jnp.cumsum(logf, axis=1).transpose(0, 2, 1)
    nb = S // Q_BLOCK
    qb = q.reshape(B_, H, nb, Q_BLOCK, Dh).transpose(2, 0, 1, 3, 4)
    cb = c.reshape(B_, H, nb, Q_BLOCK).transpose(2, 0, 1, 3)
    kpos = jnp.arange(S)
    scale = Dh ** -0.5

    def block(args):
        qi, ci, i = args
        qpos = i * Q_BLOCK + jnp.arange(Q_BLOCK)
        s = jnp.einsum('bhqd,bhkd->bhqk', qi, k).astype(jnp.float32) * scale
        s = s + (ci[..., :, None] - c[:, :, None, :])
        s = jnp.where(kpos[None, :] <= qpos[:, None], s, -jnp.inf)
        p = jax.nn.softmax(s, axis=-1).astype(v.dtype)
        return jnp.einsum('bhqk,bhkd->bhqd', p, v)

    o = lax.map(block, (qb, cb, jnp.arange(nb)))
    return o.transpose(1, 0, 3, 2, 4).reshape(B_, S, H * Dh)


def segsum(a):
    L = a.shape[-1]
    cs = jnp.cumsum(a, axis=-1)
    seg = cs[..., :, None] - cs[..., None, :]
    mask = jnp.tril(jnp.ones((L, L), dtype=bool))
    return jnp.where(mask, seg, -jnp.inf)


def ssd_scan(x, dt, A, Bm, Cm):
    B_, S = x.shape[:2]
    L, G, N, P = SSD_CHUNK, SSD_GROUPS, SSD_STATE, SSD_HEAD_DIM
    E = SSD_HEADS // G
    nc = S // L
    xd = (x * dt[..., None]).reshape(B_, nc, L, G, E, P)
    a = (dt * A).reshape(B_, nc, L, G, E).transpose(0, 3, 4, 1, 2)
    Bc = Bm.reshape(B_, nc, L, G, N)
    Cc = Cm.reshape(B_, nc, L, G, N)
    a_cs = jnp.cumsum(a, axis=-1)
    Lmat = jnp.exp(segsum(a))
    cb = jnp.einsum('bclgn,bcsgn->bgcls', Cc, Bc)
    y_diag = jnp.einsum('bgcls,bgecls,bcsgep->bclgep', cb, Lmat, xd)
    decay = jnp.exp(a_cs[..., -1:] - a_cs)
    states = jnp.einsum('bclgn,bgecl,bclgep->cbgepn', Bc, decay, xd)
    chunk_decay = jnp.exp(a_cs[..., -1]).transpose(3, 0, 1, 2)

    def step(h, inp):
        s_c, d_c = inp
        return d_c[..., None, None] * h + s_c, h

    h0 = jnp.zeros(states.shape[1:], x.dtype)
    _, prev = lax.scan(step, h0, (states, chunk_decay))
    y_off = jnp.einsum('bclgn,cbgepn,bgecl->bclgep', Cc, prev, jnp.exp(a_cs))
    return (y_diag + y_off).reshape(B_, S, SSD_HEADS, P)


def ssd_mixer(z, xbc, dt_raw, conv_w, conv_b, dt_bias, a_log, d_skip, norm_w):
    B_, S, _ = z.shape
    f32 = jnp.float32
    xbc = jax.nn.silu(causal_dwconv(xbc, conv_w, conv_b))
    xs, Bm, Cm = jnp.split(xbc, [SSD_WIDTH, SSD_WIDTH + SSD_GROUPS * SSD_STATE], axis=-1)
    dt = jax.nn.softplus(dt_raw.astype(f32) + dt_bias.astype(f32))
    A = -jnp.exp(a_log.astype(f32))
    x = xs.astype(f32).reshape(B_, S, SSD_HEADS, SSD_HEAD_DIM)
    y = ssd_scan(x, dt, A,
                 Bm.astype(f32).reshape(B_, S, SSD_GROUPS, SSD_STATE),
                 Cm.astype(f32).reshape(B_, S, SSD_GROUPS, SSD_STATE))
    y = y + d_skip.astype(f32)[:, None] * x
    y = y.reshape(B_, S, SSD_WIDTH) * jax.nn.silu(z.astype(f32))
    yg = y.reshape(B_, S, SSD_GROUPS, SSD_WIDTH // SSD_GROUPS)
    yg = yg * lax.rsqrt(jnp.mean(yg * yg, axis=-1, keepdims=True) + NORM_EPS)
    return (yg.reshape(B_, S, SSD_WIDTH) * norm_w.astype(f32)).astype(z.dtype)


def conv_ffn(u, w_up, conv_w, conv_b, w_down):
    h = causal_dwconv(u @ w_up, conv_w, conv_b)
    g, val = jnp.split(h, 2, axis=-1)
    return (jax.nn.silu(g) * val) @ w_down


def _fwd_setup_inputs(seed: int = 0) -> dict:
    key = jax.random.key(seed)
    ks = jax.random.split(key, 24)
    f32 = jnp.float32
    L = DEPTH

    def nrm(k, shape, scale):
        return jax.random.normal(k, shape, f32) * scale

    def gain(k, shape):
        return 1.0 + 0.02 * jax.random.normal(k, shape, f32)

    dt0 = jnp.exp(jax.random.uniform(ks[9], (L, SSD_HEADS), f32,
                                     math.log(1e-3), math.log(1e-1)))
    return {
        'x': jax.random.normal(ks[0], (BATCH, SEQ, D_MODEL), f32),
        'norm_mix': gain(ks[1], (L, D_MODEL)),
        'w_in': nrm(ks[2], (L, D_MODEL, IN_TOTAL), D_MODEL ** -0.5),
        'pool_mix': nrm(ks[3], (L, POOL_GROUPS, POOL_GROUP_DIM, POOL_GROUP_DIM), POOL_GROUP_DIM ** -0.5),
        'pool_scale': 1.0 + 0.1 * jax.random.normal(ks[4], (L, POOL_WIDTH), f32),
        'f_bias': jax.random.uniform(ks[5], (L, ATTN_HEADS), f32, 1.0, 4.0),
        'ssd_conv_w': nrm(ks[6], (L, SSD_CONV, SSD_CONV_CH), SSD_CONV ** -0.5),
        'ssd_conv_b': nrm(ks[7], (L, SSD_CONV_CH), 0.02),
        'ssd_dt_bias': dt0 + jnp.log(-jnp.expm1(-dt0)),
        'ssd_a_log': jnp.log(jax.random.uniform(ks[10], (L, SSD_HEADS), f32, 1.0, 16.0)),
        'ssd_d': 1.0 + 0.1 * jax.random.normal(ks[11], (L, SSD_HEADS), f32),
        'ssd_norm': gain(ks[12], (L, SSD_WIDTH)),
        'p_pool': nrm(ks[13], (L, POOL_WIDTH, D_MODEL), POOL_WIDTH ** -0.5),
        'p_attn': nrm(ks[14], (L, ATTN_WIDTH, D_MODEL), ATTN_WIDTH ** -0.5),
        'p_ssd': nrm(ks[15], (L, SSD_WIDTH, D_MODEL), SSD_WIDTH ** -0.5),
        'w_out': nrm(ks[16], (L, D_MODEL, D_MODEL), D_MODEL ** -0.5),
        'norm_ffn': gain(ks[17], (L, D_MODEL)),
        'ffn_up': nrm(ks[18], (L, D_MODEL, 2 * FFN_DIM), D_MODEL ** -0.5),
        'ffn_conv_w': nrm(ks[19], (L, FFN_CONV, 2 * FFN_DIM), FFN_CONV ** -0.5),
        'ffn_conv_b': nrm(ks[20], (L, 2 * FFN_DIM), 0.02),
        'ffn_down': nrm(ks[21], (L, FFN_DIM, D_MODEL), FFN_DIM ** -0.5),
        'norm_final': gain(ks[22], (D_MODEL,)),
    }


def _fwd_reference(x, norm_mix, w_in, pool_mix, pool_scale, f_bias, ssd_conv_w, ssd_conv_b,
              ssd_dt_bias, ssd_a_log, ssd_d, ssd_norm, p_pool, p_attn, p_ssd, w_out,
              norm_ffn, ffn_up, ffn_conv_w, ffn_conv_b, ffn_down, norm_final):
    B_, S, D = x.shape
    for l in range(DEPTH):
        u = rmsnorm(x, norm_mix[l])
        (pool_v, q, k, v, f_logit, z, xbc, dt_raw, gate_logits) = split_columns(u @ w_in[l])
        y_pool = pool_mixer(pool_v, pool_mix[l], pool_scale[l]) @ p_pool[l]
        y_attn = forgetting_attention(q, k, v, f_logit, f_bias[l]) @ p_attn[l]
        y_ssd = ssd_mixer(z, xbc, dt_raw, ssd_conv_w[l], ssd_conv_b[l], ssd_dt_bias[l],
                          ssd_a_log[l], ssd_d[l], ssd_norm[l]) @ p_ssd[l]
        gates = jax.nn.sigmoid(gate_logits.astype(jnp.float32)).astype(x.dtype)
        gates = gates.reshape(B_, S, N_BRANCH, D)
        merged = gates[:, :, 0] * y_pool + gates[:, :, 1] * y_attn + gates[:, :, 2] * y_ssd
        x = x + merged @ w_out[l]
        x = x + conv_ffn(rmsnorm(x, norm_ffn[l]), ffn_up[l], ffn_conv_w[l], ffn_conv_b[l], ffn_down[l])
    return rmsnorm(x, norm_final)


import jax as _jax
import jax.numpy as _jnp

TWIN_FORMAT = 'train_step'
FWD_PARAMS = ['x', 'norm_mix', 'w_in', 'pool_mix', 'pool_scale', 'f_bias', 'ssd_conv_w', 'ssd_conv_b', 'ssd_dt_bias', 'ssd_a_log', 'ssd_d', 'ssd_norm', 'p_pool', 'p_attn', 'p_ssd', 'w_out', 'norm_ffn', 'ffn_up', 'ffn_conv_w', 'ffn_conv_b', 'ffn_down', 'norm_final']
TWIN_WEIGHTS = ['norm_mix', 'w_in', 'pool_mix', 'pool_scale', 'f_bias', 'ssd_conv_w', 'ssd_conv_b', 'ssd_dt_bias', 'ssd_a_log', 'ssd_d', 'ssd_norm', 'p_pool', 'p_attn', 'p_ssd', 'w_out', 'norm_ffn', 'ffn_up', 'ffn_conv_w', 'ffn_conv_b', 'ffn_down', 'norm_final']
TWIN_DIFF_INPUT = 'x'
TWIN_INPUTS = ['x', 'norm_mix', 'w_in', 'pool_mix', 'pool_scale', 'f_bias', 'ssd_conv_w', 'ssd_conv_b', 'ssd_dt_bias', 'ssd_a_log', 'ssd_d', 'ssd_norm', 'p_pool', 'p_attn', 'p_ssd', 'w_out', 'norm_ffn', 'ffn_up', 'ffn_conv_w', 'ffn_conv_b', 'ffn_down', 'norm_final', 'loss_target', 'm_norm_mix', 'm_w_in', 'm_pool_mix', 'm_pool_scale', 'm_f_bias', 'm_ssd_conv_w', 'm_ssd_conv_b', 'm_ssd_dt_bias', 'm_ssd_a_log', 'm_ssd_d', 'm_ssd_norm', 'm_p_pool', 'm_p_attn', 'm_p_ssd', 'm_w_out', 'm_norm_ffn', 'm_ffn_up', 'm_ffn_conv_w', 'm_ffn_conv_b', 'm_ffn_down', 'm_norm_final', 'v_norm_mix', 'v_w_in', 'v_pool_mix', 'v_pool_scale', 'v_f_bias', 'v_ssd_conv_w', 'v_ssd_conv_b', 'v_ssd_dt_bias', 'v_ssd_a_log', 'v_ssd_d', 'v_ssd_norm', 'v_p_pool', 'v_p_attn', 'v_p_ssd', 'v_w_out', 'v_norm_ffn', 'v_ffn_up', 'v_ffn_conv_w', 'v_ffn_conv_b', 'v_ffn_down', 'v_norm_final']
TWIN_OUTPUTS = ['loss', 'grad_x', 'grad_norm_mix', 'grad_w_in', 'grad_pool_mix', 'grad_pool_scale', 'grad_f_bias', 'grad_ssd_conv_w', 'grad_ssd_conv_b', 'grad_ssd_dt_bias', 'grad_ssd_a_log', 'grad_ssd_d', 'grad_ssd_norm', 'grad_p_pool', 'grad_p_attn', 'grad_p_ssd', 'grad_w_out', 'grad_norm_ffn', 'grad_ffn_up', 'grad_ffn_conv_w', 'grad_ffn_conv_b', 'grad_ffn_down', 'grad_norm_final', 'delta_norm_mix', 'delta_w_in', 'delta_pool_mix', 'delta_pool_scale', 'delta_f_bias', 'delta_ssd_conv_w', 'delta_ssd_conv_b', 'delta_ssd_dt_bias', 'delta_ssd_a_log', 'delta_ssd_d', 'delta_ssd_norm', 'delta_p_pool', 'delta_p_attn', 'delta_p_ssd', 'delta_w_out', 'delta_norm_ffn', 'delta_ffn_up', 'delta_ffn_conv_w', 'delta_ffn_conv_b', 'delta_ffn_down', 'delta_norm_final', 'new_m_norm_mix', 'new_m_w_in', 'new_m_pool_mix', 'new_m_pool_scale', 'new_m_f_bias', 'new_m_ssd_conv_w', 'new_m_ssd_conv_b', 'new_m_ssd_dt_bias', 'new_m_ssd_a_log', 'new_m_ssd_d', 'new_m_ssd_norm', 'new_m_p_pool', 'new_m_p_attn', 'new_m_p_ssd', 'new_m_w_out', 'new_m_norm_ffn', 'new_m_ffn_up', 'new_m_ffn_conv_w', 'new_m_ffn_conv_b', 'new_m_ffn_down', 'new_m_norm_final', 'new_v_norm_mix', 'new_v_w_in', 'new_v_pool_mix', 'new_v_pool_scale', 'new_v_f_bias', 'new_v_ssd_conv_w', 'new_v_ssd_conv_b', 'new_v_ssd_dt_bias', 'new_v_ssd_a_log', 'new_v_ssd_d', 'new_v_ssd_norm', 'new_v_p_pool', 'new_v_p_attn', 'new_v_p_ssd', 'new_v_w_out', 'new_v_norm_ffn', 'new_v_ffn_up', 'new_v_ffn_conv_w', 'new_v_ffn_conv_b', 'new_v_ffn_down', 'new_v_norm_final']
TWIN_LEAF_KINDS = {'loss': 'loss', 'grad_x': 'grad_x', 'grad_norm_mix': 'grad_w', 'grad_w_in': 'grad_w', 'grad_pool_mix': 'grad_w', 'grad_pool_scale': 'grad_w', 'grad_f_bias': 'grad_w', 'grad_ssd_conv_w': 'grad_w', 'grad_ssd_conv_b': 'grad_w', 'grad_ssd_dt_bias': 'grad_w', 'grad_ssd_a_log': 'grad_w', 'grad_ssd_d': 'grad_w', 'grad_ssd_norm': 'grad_w', 'grad_p_pool': 'grad_w', 'grad_p_attn': 'grad_w', 'grad_p_ssd': 'grad_w', 'grad_w_out': 'grad_w', 'grad_norm_ffn': 'grad_w', 'grad_ffn_up': 'grad_w', 'grad_ffn_conv_w': 'grad_w', 'grad_ffn_conv_b': 'grad_w', 'grad_ffn_down': 'grad_w', 'grad_norm_final': 'grad_w', 'delta_norm_mix': 'delta_w', 'delta_w_in': 'delta_w', 'delta_pool_mix': 'delta_w', 'delta_pool_scale': 'delta_w', 'delta_f_bias': 'delta_w', 'delta_ssd_conv_w': 'delta_w', 'delta_ssd_conv_b': 'delta_w', 'delta_ssd_dt_bias': 'delta_w', 'delta_ssd_a_log': 'delta_w', 'delta_ssd_d': 'delta_w', 'delta_ssd_norm': 'delta_w', 'delta_p_pool': 'delta_w', 'delta_p_attn': 'delta_w', 'delta_p_ssd': 'delta_w', 'delta_w_out': 'delta_w', 'delta_norm_ffn': 'delta_w', 'delta_ffn_up': 'delta_w', 'delta_ffn_conv_w': 'delta_w', 'delta_ffn_conv_b': 'delta_w', 'delta_ffn_down': 'delta_w', 'delta_norm_final': 'delta_w', 'new_m_norm_mix': 'new_m', 'new_m_w_in': 'new_m', 'new_m_pool_mix': 'new_m', 'new_m_pool_scale': 'new_m', 'new_m_f_bias': 'new_m', 'new_m_ssd_conv_w': 'new_m', 'new_m_ssd_conv_b': 'new_m', 'new_m_ssd_dt_bias': 'new_m', 'new_m_ssd_a_log': 'new_m', 'new_m_ssd_d': 'new_m', 'new_m_ssd_norm': 'new_m', 'new_m_p_pool': 'new_m', 'new_m_p_attn': 'new_m', 'new_m_p_ssd': 'new_m', 'new_m_w_out': 'new_m', 'new_m_norm_ffn': 'new_m', 'new_m_ffn_up': 'new_m', 'new_m_ffn_conv_w': 'new_m', 'new_m_ffn_conv_b': 'new_m', 'new_m_ffn_down': 'new_m', 'new_m_norm_final': 'new_m', 'new_v_norm_mix': 'new_v', 'new_v_w_in': 'new_v', 'new_v_pool_mix': 'new_v', 'new_v_pool_scale': 'new_v', 'new_v_f_bias': 'new_v', 'new_v_ssd_conv_w': 'new_v', 'new_v_ssd_conv_b': 'new_v', 'new_v_ssd_dt_bias': 'new_v', 'new_v_ssd_a_log': 'new_v', 'new_v_ssd_d': 'new_v', 'new_v_ssd_norm': 'new_v', 'new_v_p_pool': 'new_v', 'new_v_p_attn': 'new_v', 'new_v_p_ssd': 'new_v', 'new_v_w_out': 'new_v', 'new_v_norm_ffn': 'new_v', 'new_v_ffn_up': 'new_v', 'new_v_ffn_conv_w': 'new_v', 'new_v_ffn_conv_b': 'new_v', 'new_v_ffn_down': 'new_v', 'new_v_norm_final': 'new_v'}


def _forward(args):
    return _fwd_reference(*[args[k] for k in FWD_PARAMS])


def _output_shape():
    def fwd():
        inp = _fwd_setup_inputs(0)
        return _fwd_reference(*[inp[k] for k in FWD_PARAMS])
    out = _jax.eval_shape(fwd)
    return out.shape, out.dtype

N_MICROBATCH = 1
ADAM_LR = 0.001
ADAM_B1 = 0.9
ADAM_B2 = 0.999
ADAM_EPS = 1e-08
ADAM_WD = 0.01
ADAM_STEP = 10
PER_EXAMPLE_BATCH_AXIS = {'x': 0, 'loss_target': 0}
SHARED_INPUTS = []
_WEIGHT_DTYPES = {'norm_mix': _jnp.float32, 'w_in': _jnp.float32, 'pool_mix': _jnp.float32, 'pool_scale': _jnp.float32, 'f_bias': _jnp.float32, 'ssd_conv_w': _jnp.float32, 'ssd_conv_b': _jnp.float32, 'ssd_dt_bias': _jnp.float32, 'ssd_a_log': _jnp.float32, 'ssd_d': _jnp.float32, 'ssd_norm': _jnp.float32, 'p_pool': _jnp.float32, 'p_attn': _jnp.float32, 'p_ssd': _jnp.float32, 'w_out': _jnp.float32, 'norm_ffn': _jnp.float32, 'ffn_up': _jnp.float32, 'ffn_conv_w': _jnp.float32, 'ffn_conv_b': _jnp.float32, 'ffn_down': _jnp.float32, 'norm_final': _jnp.float32}
MOMENT_SCALE = {'norm_mix': 2.191662e-01, 'w_in': 8.146440e-02, 'pool_mix': 1.494849e-01, 'pool_scale': 1.658261e-01, 'f_bias': 2.734767e-01, 'ssd_conv_w': 1.042730e-01, 'ssd_conv_b': 1.517563e-01, 'ssd_dt_bias': 2.190574e-01, 'ssd_a_log': 6.256871e-01, 'ssd_d': 5.050593e-01, 'ssd_norm': 1.259686e-01, 'p_pool': 1.041357e-01, 'p_attn': 4.035779e-02, 'p_ssd': 1.199622e-01, 'w_out': 1.623561e-01, 'norm_ffn': 1.642015e-01, 'ffn_up': 6.839964e-02, 'ffn_conv_w': 6.784783e-02, 'ffn_conv_b': 6.834123e-02, 'ffn_down': 1.120348e-01, 'norm_final': 6.407512e+01}


def _to_microbatches(a, axis):
    t = _jnp.moveaxis(a, axis, 0)
    t = t.reshape((N_MICROBATCH, t.shape[0] // N_MICROBATCH) + t.shape[1:])
    return _jnp.moveaxis(t, 1, axis + 1)


def setup_inputs(seed: int = 0) -> dict:
    inp = _fwd_setup_inputs(seed)
    key = _jax.random.fold_in(_jax.random.key(seed), 7919)
    shape, _ = _output_shape()
    out = dict(inp)
    out["loss_target"] = _jax.random.normal(_jax.random.fold_in(key, 0), shape, _jnp.float32)
    for i, name in enumerate(TWIN_WEIGHTS):
        w = inp[name].astype(_jnp.float32)
        if MOMENT_SCALE is None:
            s = _jnp.sqrt(_jnp.mean(_jnp.square(w)) + 1e-30)
        else:
            s = MOMENT_SCALE[name]
        km, kv = _jax.random.split(_jax.random.fold_in(key, i + 1))
        out[name] = w
        out["m_" + name] = s * _jax.random.normal(km, w.shape, _jnp.float32)
        out["v_" + name] = (s * s) * _jax.random.uniform(kv, w.shape, _jnp.float32, 0.5, 1.5)
    if N_MICROBATCH > 1:
        for name, axis in PER_EXAMPLE_BATCH_AXIS.items():
            out[name] = _to_microbatches(out[name], axis)
    return {'x': out['x'], 'norm_mix': out['norm_mix'], 'w_in': out['w_in'], 'pool_mix': out['pool_mix'], 'pool_scale': out['pool_scale'], 'f_bias': out['f_bias'], 'ssd_conv_w': out['ssd_conv_w'], 'ssd_conv_b': out['ssd_conv_b'], 'ssd_dt_bias': out['ssd_dt_bias'], 'ssd_a_log': out['ssd_a_log'], 'ssd_d': out['ssd_d'], 'ssd_norm': out['ssd_norm'], 'p_pool': out['p_pool'], 'p_attn': out['p_attn'], 'p_ssd': out['p_ssd'], 'w_out': out['w_out'], 'norm_ffn': out['norm_ffn'], 'ffn_up': out['ffn_up'], 'ffn_conv_w': out['ffn_conv_w'], 'ffn_conv_b': out['ffn_conv_b'], 'ffn_down': out['ffn_down'], 'norm_final': out['norm_final'], 'loss_target': out['loss_target'], 'm_norm_mix': out['m_norm_mix'], 'm_w_in': out['m_w_in'], 'm_pool_mix': out['m_pool_mix'], 'm_pool_scale': out['m_pool_scale'], 'm_f_bias': out['m_f_bias'], 'm_ssd_conv_w': out['m_ssd_conv_w'], 'm_ssd_conv_b': out['m_ssd_conv_b'], 'm_ssd_dt_bias': out['m_ssd_dt_bias'], 'm_ssd_a_log': out['m_ssd_a_log'], 'm_ssd_d': out['m_ssd_d'], 'm_ssd_norm': out['m_ssd_norm'], 'm_p_pool': out['m_p_pool'], 'm_p_attn': out['m_p_attn'], 'm_p_ssd': out['m_p_ssd'], 'm_w_out': out['m_w_out'], 'm_norm_ffn': out['m_norm_ffn'], 'm_ffn_up': out['m_ffn_up'], 'm_ffn_conv_w': out['m_ffn_conv_w'], 'm_ffn_conv_b': out['m_ffn_conv_b'], 'm_ffn_down': out['m_ffn_down'], 'm_norm_final': out['m_norm_final'], 'v_norm_mix': out['v_norm_mix'], 'v_w_in': out['v_w_in'], 'v_pool_mix': out['v_pool_mix'], 'v_pool_scale': out['v_pool_scale'], 'v_f_bias': out['v_f_bias'], 'v_ssd_conv_w': out['v_ssd_conv_w'], 'v_ssd_conv_b': out['v_ssd_conv_b'], 'v_ssd_dt_bias': out['v_ssd_dt_bias'], 'v_ssd_a_log': out['v_ssd_a_log'], 'v_ssd_d': out['v_ssd_d'], 'v_ssd_norm': out['v_ssd_norm'], 'v_p_pool': out['v_p_pool'], 'v_p_attn': out['v_p_attn'], 'v_p_ssd': out['v_p_ssd'], 'v_w_out': out['v_w_out'], 'v_norm_ffn': out['v_norm_ffn'], 'v_ffn_up': out['v_ffn_up'], 'v_ffn_conv_w': out['v_ffn_conv_w'], 'v_ffn_conv_b': out['v_ffn_conv_b'], 'v_ffn_down': out['v_ffn_down'], 'v_norm_final': out['v_norm_final']}


def _loss(weights, diff, rest, loss_target):
    with _jax.named_scope("forward"):
        args = {**rest, TWIN_DIFF_INPUT: diff, **{k: w.astype(_WEIGHT_DTYPES[k]) for k, w in weights.items()}}
        y = _forward(args)
    with _jax.named_scope("loss_head"):
        err = _jnp.square(y.astype(_jnp.float32) - loss_target)
        return 0.5 * _jnp.sum(_jnp.mean(err, axis=-1)) if err.ndim else 0.5 * err


def _adamw(w, g, m, v):
    m = ADAM_B1 * m + (1.0 - ADAM_B1) * g
    v = ADAM_B2 * v + (1.0 - ADAM_B2) * _jnp.square(g)
    m_hat = m / (1.0 - ADAM_B1 ** ADAM_STEP)
    v_hat = v / (1.0 - ADAM_B2 ** ADAM_STEP)
    delta = -ADAM_LR * (m_hat / (_jnp.sqrt(v_hat) + ADAM_EPS) + ADAM_WD * w)
    return delta, m, v


def reference(x, norm_mix, w_in, pool_mix, pool_scale, f_bias, ssd_conv_w, ssd_conv_b, ssd_dt_bias, ssd_a_log, ssd_d, ssd_norm, p_pool, p_attn, p_ssd, w_out, norm_ffn, ffn_up, ffn_conv_w, ffn_conv_b, ffn_down, norm_final, loss_target, m_norm_mix, m_w_in, m_pool_mix, m_pool_scale, m_f_bias, m_ssd_conv_w, m_ssd_conv_b, m_ssd_dt_bias, m_ssd_a_log, m_ssd_d, m_ssd_norm, m_p_pool, m_p_attn, m_p_ssd, m_w_out, m_norm_ffn, m_ffn_up, m_ffn_conv_w, m_ffn_conv_b, m_ffn_down, m_norm_final, v_norm_mix, v_w_in, v_pool_mix, v_pool_scale, v_f_bias, v_ssd_conv_w, v_ssd_conv_b, v_ssd_dt_bias, v_ssd_a_log, v_ssd_d, v_ssd_norm, v_p_pool, v_p_attn, v_p_ssd, v_w_out, v_norm_ffn, v_ffn_up, v_ffn_conv_w, v_ffn_conv_b, v_ffn_down, v_norm_final):
    given = dict(x=x, norm_mix=norm_mix, w_in=w_in, pool_mix=pool_mix, pool_scale=pool_scale, f_bias=f_bias, ssd_conv_w=ssd_conv_w, ssd_conv_b=ssd_conv_b, ssd_dt_bias=ssd_dt_bias, ssd_a_log=ssd_a_log, ssd_d=ssd_d, ssd_norm=ssd_norm, p_pool=p_pool, p_attn=p_attn, p_ssd=p_ssd, w_out=w_out, norm_ffn=norm_ffn, ffn_up=ffn_up, ffn_conv_w=ffn_conv_w, ffn_conv_b=ffn_conv_b, ffn_down=ffn_down, norm_final=norm_final, loss_target=loss_target, m_norm_mix=m_norm_mix, m_w_in=m_w_in, m_pool_mix=m_pool_mix, m_pool_scale=m_pool_scale, m_f_bias=m_f_bias, m_ssd_conv_w=m_ssd_conv_w, m_ssd_conv_b=m_ssd_conv_b, m_ssd_dt_bias=m_ssd_dt_bias, m_ssd_a_log=m_ssd_a_log, m_ssd_d=m_ssd_d, m_ssd_norm=m_ssd_norm, m_p_pool=m_p_pool, m_p_attn=m_p_attn, m_p_ssd=m_p_ssd, m_w_out=m_w_out, m_norm_ffn=m_norm_ffn, m_ffn_up=m_ffn_up, m_ffn_conv_w=m_ffn_conv_w, m_ffn_conv_b=m_ffn_conv_b, m_ffn_down=m_ffn_down, m_norm_final=m_norm_final, v_norm_mix=v_norm_mix, v_w_in=v_w_in, v_pool_mix=v_pool_mix, v_pool_scale=v_pool_scale, v_f_bias=v_f_bias, v_ssd_conv_w=v_ssd_conv_w, v_ssd_conv_b=v_ssd_conv_b, v_ssd_dt_bias=v_ssd_dt_bias, v_ssd_a_log=v_ssd_a_log, v_ssd_d=v_ssd_d, v_ssd_norm=v_ssd_norm, v_p_pool=v_p_pool, v_p_attn=v_p_attn, v_p_ssd=v_p_ssd, v_w_out=v_w_out, v_norm_ffn=v_norm_ffn, v_ffn_up=v_ffn_up, v_ffn_conv_w=v_ffn_conv_w, v_ffn_conv_b=v_ffn_conv_b, v_ffn_down=v_ffn_down, v_norm_final=v_norm_final)
    weights = {n: given[n] for n in TWIN_WEIGHTS}
    shared = {n: given[n] for n in SHARED_INPUTS}
    per_example = {n: given[n] for n in ['x']}
    grad_fn = _jax.value_and_grad(_loss, argnums=(0, 1))

    def one_microbatch(ex, loss_target):
        ex = dict(ex)
        diff = ex.pop(TWIN_DIFF_INPUT)
        return grad_fn(weights, diff, {**shared, **ex}, loss_target)

    if N_MICROBATCH == 1:
        loss, (grad_w, grad_x) = one_microbatch(per_example, given["loss_target"])
    else:
        def body(carry, xs):
            loss_sum, grad_sum = carry
            l_k, (gw_k, gx_k) = one_microbatch(xs[0], xs[1])
            with _jax.named_scope("update"):
                return (loss_sum + l_k, _jax.tree.map(_jnp.add, grad_sum, gw_k)), gx_k

        init = (_jnp.zeros((), _jnp.float32), _jax.tree.map(_jnp.zeros_like, weights))
        (loss, grad_w), grad_x = _jax.lax.scan(body, init, (per_example, given["loss_target"]))
    with _jax.named_scope("update"):
        delta_w, new_m, new_v = {}, {}, {}
        for n in TWIN_WEIGHTS:
            delta_w[n], new_m[n], new_v[n] = _adamw(weights[n], grad_w[n], given["m_" + n], given["v_" + n])
    return (loss, grad_x, *[grad_w[n] for n in TWIN_WEIGHTS], *[delta_w[n] for n in TWIN_WEIGHTS],
            *[new_m[n] for n in TWIN_WEIGHTS], *[new_v[n] for n in TWIN_WEIGHTS])
```

```python
import functools

import jax
import jax.numpy as jnp
from jax import lax
from jax.experimental import pallas as pl
from jax.experimental.pallas import tpu as pltpu

F32 = jnp.float32
MXU_DTYPE = jnp.bfloat16
NORM_EPS = 1e-6
HALO = 16
LANE = 128
NEG_BIG = -1e30
VMEM_LIMIT = 52 * 1024 * 1024

D_MODEL = 1024
POOL_WINDOWS = (2, 4, 8, 16)
POOL_W = 512
HEADS = 8
HEAD_DIM = 64
ATTN_W = 512
SSD_W = 1024
SSD_HEADS = 16
SSD_P = 64
SSD_N = 128
SSD_CHUNK = 128
SSD_CONV_CH = 1536
FFN = 2816
DT_LANE0 = 8
IN_SPLITS = (512, 512, 512, 512, 8, 1024, 1536, 16, 3072)
IN_TOTAL = sum(IN_SPLITS)
N_PLACES = 4

ADAM_LR, ADAM_B1, ADAM_B2, ADAM_EPS, ADAM_WD, ADAM_STEP = 0.001, 0.9, 0.999, 1e-08, 0.01, 10

BIG = ('w_in', 'p_pool', 'p_attn', 'p_ssd', 'w_out', 'ffn_up', 'ffn_down')
COL_SHARDED = ('w_in', 'p_pool', 'p_attn', 'ffn_up')
SMALL = ('norm_mix', 'pool_mix', 'pool_scale', 'f_bias', 'ssd_conv_w', 'ssd_conv_b', 'ssd_dt_bias',
         'ssd_a_log', 'ssd_d', 'ssd_norm', 'norm_ffn', 'ffn_conv_w', 'ffn_conv_b', 'norm_final')
SMALL_SHARDED = ('ssd_conv_w', 'ffn_conv_w')
WEIGHTS = ('norm_mix', 'w_in', 'pool_mix', 'pool_scale', 'f_bias', 'ssd_conv_w', 'ssd_conv_b', 'ssd_dt_bias',
           'ssd_a_log', 'ssd_d', 'ssd_norm', 'p_pool', 'p_attn', 'p_ssd', 'w_out', 'norm_ffn', 'ffn_up',
           'ffn_conv_w', 'ffn_conv_b', 'ffn_down', 'norm_final')
PACK_W = 1024
PACK_ROW_ALIGN = 1024


def _params(sem):
    return pltpu.CompilerParams(dimension_semantics=sem, vmem_limit_bytes=VMEM_LIMIT)


def _tile(n, prefs=(512, 256, 128)):
    for t in prefs:
        if n % t == 0:
            return t
    return n


def _sigmoid(x):
    return 1.0 / (1.0 + jnp.exp(-x))


def _softplus(x):
    return jnp.maximum(x, 0.0) + jnp.log1p(jnp.exp(-jnp.abs(x)))


def _dot(a, b, dims=((1,), (0,))):
    return lax.dot_general(a.astype(MXU_DTYPE), b.astype(MXU_DTYPE), (dims, ((), ())),
                           preferred_element_type=F32)


NT = ((1,), (1,))


def _mm(a, b, *, ta=False, tb=False, acc=None, out_dtype=F32, name):
    M, K = (a.shape[1], a.shape[0]) if ta else a.shape
    N = b.shape[0] if tb else b.shape[1]
    tm, tn, tk = _tile(M), _tile(N), _tile(K)
    nk = K // tk
    a_spec = pl.BlockSpec((tk, tm), lambda i, j, k: (k, i)) if ta else pl.BlockSpec((tm, tk), lambda i, j, k: (i, k))
    b_spec = pl.BlockSpec((tn, tk), lambda i, j, k: (j, k)) if tb else pl.BlockSpec((tk, tn), lambda i, j, k: (k, j))
    in_specs = [a_spec, b_spec]
    args = [a, b]
    if acc is not None:
        in_specs.append(pl.BlockSpec((tm, tn), lambda i, j, k: (i, j)))
        args.append(acc)

    def body(*refs):
        if acc is not None:
            a_ref, b_ref, c_ref, o_ref, acc_ref = refs
        else:
            a_ref, b_ref, o_ref, acc_ref = refs
        k = pl.program_id(2)

        @pl.when(k == 0)
        def _():
            if acc is not None:
                acc_ref[...] = c_ref[...].astype(F32)
            else:
                acc_ref[...] = jnp.zeros_like(acc_ref)

        av = a_ref[...]
        if ta:
            av = av.astype(F32).T
        acc_ref[...] += _dot(av, b_ref[...], NT if tb else ((1,), (0,)))

        @pl.when(k == nk - 1)
        def _():
            o_ref[...] = acc_ref[...].astype(out_dtype)

    return pl.pallas_call(
        body, grid=(M // tm, N // tn, nk), in_specs=in_specs,
        out_specs=pl.BlockSpec((tm, tn), lambda i, j, k: (i, j)),
        out_shape=jax.ShapeDtypeStruct((M, N), out_dtype),
        scratch_shapes=[pltpu.VMEM((tm, tn), F32)],
        compiler_params=_params(("parallel", "parallel", "arbitrary")), name=name)(*args)


def _rows(body, S, T, *, rows=(), consts=(), prevs=(), nexts=(), out_rows=(), out_accs=(), scratch=(), name):
    n = S // T
    hb = T // HALO
    last_h = S // HALO - 1

    def norm(r):
        return r if isinstance(r, tuple) else (r, r.shape[1], 0)

    rows, prevs, nexts = [norm(r) for r in rows], [norm(r) for r in prevs], [norm(r) for r in nexts]
    in_specs, args = [], []
    for arr, W, cb in rows:
        in_specs.append(pl.BlockSpec((T, W), lambda i, cb=cb: (i, cb)))
        args.append(arr)
    for cst in consts:
        in_specs.append(pl.BlockSpec(cst.shape, lambda i, nd=cst.ndim: (0,) * nd))
        args.append(cst)
    for arr, W, cb in prevs:
        in_specs.append(pl.BlockSpec((HALO, W), lambda i, cb=cb: (jnp.maximum(i * hb - 1, 0), cb)))
        args.append(arr)
    for arr, W, cb in nexts:
        in_specs.append(pl.BlockSpec((HALO, W), lambda i, cb=cb: (jnp.minimum((i + 1) * hb, last_h), cb)))
        args.append(arr)
    out_specs = [pl.BlockSpec((T, W), lambda i: (i, 0)) for W, _ in out_rows]
    out_specs += [pl.BlockSpec(shp, lambda i, nd=len(shp): (0,) * nd) for shp, _ in out_accs]
    out_shape = [jax.ShapeDtypeStruct((S, W), dt) for W, dt in out_rows]
    out_shape += [jax.ShapeDtypeStruct(shp, dt) for shp, dt in out_accs]
    cuts = [len(rows), len(consts), len(prevs), len(nexts), len(out_rows), len(out_accs), len(scratch)]

    def kern(*refs):
        groups, pos = [], 0
        for c in cuts:
            groups.append(list(refs[pos:pos + c]))
            pos += c
        i = pl.program_id(0)

        @pl.when(i == 0)
        def _():
            for a_ref in groups[5]:
                a_ref[...] = jnp.zeros_like(a_ref)

        body(i, n, *groups)

    outs = pl.pallas_call(kern, grid=(n,), in_specs=in_specs, out_specs=out_specs, out_shape=out_shape,
                          scratch_shapes=list(scratch), compiler_params=_params(("arbitrary",)), name=name)(*args)
    return outs


def _fill_prev(ext, prev_ref, cur, i):
    ext[0:HALO, :] = jnp.where(i > 0, prev_ref[...].astype(F32), 0.0)
    ext[HALO:, :] = cur


def _fill_next(ext, cur, next_val, T):
    ext[0:T, :] = cur
    ext[T:, :] = next_val


def _row_ids(i, T, W=1):
    return i * T + lax.broadcasted_iota(jnp.int32, (T, W), 0)


def _norm_fwd(x, w, name):
    S, D = x.shape

    def body(i, n, R, C, P, N, O, A, Sc):
        xv = R[0][...]
        r = lax.rsqrt(jnp.mean(xv * xv, axis=-1, keepdims=True) + NORM_EPS)
        O[0][...] = (xv * r * C[0][...]).astype(MXU_DTYPE)

    return _rows(body, S, _tile(S), rows=[x], consts=[w], out_rows=[(D, MXU_DTYPE)], name=name)[0]


def _norm_bwd_math(xv, w, du):
    r = lax.rsqrt(jnp.mean(xv * xv, axis=-1, keepdims=True) + NORM_EPS)
    xh = xv * r
    g = du * w
    dx = r * (g - xh * jnp.mean(g * xh, axis=-1, keepdims=True))
    dw = jnp.sum(du * xh, axis=0, keepdims=True)
    return dx, dw


def _norm_bwd(x, w, du, dres, name):
    S, D = x.shape

    def body(i, n, R, C, P, N, O, A, Sc):
        dx, dw = _norm_bwd_math(R[0][...], C[0][...], R[1][...])
        O[0][...] = R[2][...] + dx
        A[0][...] += dw

    return _rows(body, S, _tile(S), rows=[x, du, dres], consts=[w], out_rows=[(D, F32)],
                 out_accs=[((1, D), F32)], name=name)


def _loss_head(x, w, target, name):
    S, D = x.shape

    def body(i, n, R, C, P, N, O, A, Sc):
        xv, w_, tg = R[0][...], C[0][...], R[1][...]
        r = lax.rsqrt(jnp.mean(xv * xv, axis=-1, keepdims=True) + NORM_EPS)
        e = xv * r * w_ - tg
        A[1][...] += jnp.broadcast_to(0.5 * jnp.sum(jnp.mean(e * e, axis=-1, keepdims=True)), (1, LANE))
        dx, dw = _norm_bwd_math(xv, w_, e / D)
        O[0][...] = dx
        A[0][...] += dw

    return _rows(body, S, _tile(S), rows=[x, target], consts=[w], out_rows=[(D, F32)],
                 out_accs=[((1, D), F32), ((1, LANE), F32)], name=name)


def _pool_fwd(pqkv, mix, scale, name):
    S = pqkv.shape[0]
    T = _tile(S, (256, 128))

    def body(i, n, R, C, P, N, O, A, Sc):
        ext = Sc[0]
        v = R[0][...]
        _fill_prev(ext, P[0], v, i)
        t1 = (_row_ids(i, T) + 1).astype(F32)
        for g, w in enumerate(POOL_WINDOWS):
            cols = slice(g * LANE, (g + 1) * LANE)
            acc = v[:, cols]
            for j in range(1, w):
                acc = acc + ext[pl.ds(HALO - j, T), cols]
            d = (acc / jnp.minimum(t1, float(w)) - v[:, cols]).astype(MXU_DTYPE)
            O[0][:, cols] = d
            O[1][:, cols] = (_dot(d, C[0][g]) * C[1][:, cols]).astype(MXU_DTYPE)

    return _rows(body, S, T, rows=[(pqkv, POOL_W, 0)], prevs=[(pqkv, POOL_W, 0)], consts=[mix, scale],
                 out_rows=[(POOL_W, MXU_DTYPE), (POOL_W, MXU_DTYPE)],
                 scratch=[pltpu.VMEM((HALO + T, POOL_W), F32)], name=name)


def _pool_bwd_a(dypm, d, mix, scale, name):
    S = d.shape[0]
    T = _tile(S, (256, 128))

    def body(i, n, R, C, P, N, O, A, Sc):
        for g in range(len(POOL_WINDOWS)):
            cols = slice(g * LANE, (g + 1) * LANE)
            dg = R[1][:, cols]
            dy = R[0][:, cols]
            yg = _dot(dg, C[0][g])
            A[0][:, cols] += jnp.sum(dy * yg, axis=0, keepdims=True)
            dys = dy * C[1][:, cols]
            A[1][cols, :] += _dot(dg.astype(F32).T, dys)
            O[0][:, cols] = _dot(dys, C[0][g], NT)

    return _rows(body, S, T, rows=[dypm, d], consts=[mix, scale], out_rows=[(POOL_W, F32)],
                 out_accs=[((1, POOL_W), F32), ((POOL_W, LANE), F32)], name=name)


def _pool_bwd_b(dd, name):
    S = dd.shape[0]
    T = _tile(S, (256, 128))

    def body(i, n, R, C, P, N, O, A, Sc):
        ext = Sc[0]
        ddv = R[0][...]
        t1 = (_row_ids(i, T) + 1).astype(F32)
        nxt = jnp.where(i < n - 1, N[0][...], 0.0)
        for g, w in enumerate(POOL_WINDOWS):
            cols = slice(g * LANE, (g + 1) * LANE)
            ext[0:T, cols] = ddv[:, cols] / jnp.minimum(t1, float(w))
            ext[T:, cols] = nxt[:, cols] / float(w)
        for g, w in enumerate(POOL_WINDOWS):
            cols = slice(g * LANE, (g + 1) * LANE)
            acc = ext[0:T, cols]
            for j in range(1, w):
                acc = acc + ext[pl.ds(j, T), cols]
            O[0][:, cols] = (acc - ddv[:, cols]).astype(MXU_DTYPE)

    return _rows(body, S, T, rows=[dd], nexts=[dd], out_rows=[(POOL_W, MXU_DTYPE)],
                 scratch=[pltpu.VMEM((T + HALO, POOL_W), F32)], name=name)[0]


def _lane_cumsum(seg, reverse=False):
    lane = lax.broadcasted_iota(jnp.int32, seg.shape, 1)
    sh = 1
    while sh < LANE:
        if reverse:
            seg = seg + jnp.where(lane < LANE - sh, pltpu.roll(seg, LANE - sh, axis=1), 0.0)
        else:
            seg = seg + jnp.where(lane >= sh, pltpu.roll(seg, sh, axis=1), 0.0)
        sh *= 2
    return seg


def _logf_cumsum(fT, bias, name):
    H, S = fT.shape
    TB = _tile(S)
    nb = S // TB

    def body(f_ref, b_ref, o_ref, carry):
        @pl.when(pl.program_id(0) == 0)
        def _():
            carry[...] = jnp.zeros_like(carry)

        x = f_ref[...] + b_ref[...]
        lf = jnp.minimum(x, 0.0) - jnp.log1p(jnp.exp(-jnp.abs(x)))
        c = carry[...]
        for j in range(TB // LANE):
            seg = _lane_cumsum(lf[:, j * LANE:(j + 1) * LANE]) + c
            o_ref[:, j * LANE:(j + 1) * LANE] = seg
            c = seg[:, LANE - 1:LANE]
        carry[...] = c

    return pl.pallas_call(
        body, grid=(nb,), in_specs=[pl.BlockSpec((H, TB), lambda i: (0, i)), pl.BlockSpec((H, 1), lambda i: (0, 0))],
        out_specs=pl.BlockSpec((H, TB), lambda i: (0, i)), out_shape=jax.ShapeDtypeStruct((H, S), F32),
        scratch_shapes=[pltpu.VMEM((H, 1), F32)], compiler_params=_params(("arbitrary",)), name=name)(fT, bias)


def _logf_cumsum_bwd(fT, bias, dc, dcq, name):
    H, S = fT.shape
    TB = _tile(S)
    nb = S // TB

    def body(f_ref, b_ref, dc_ref, dcq_ref, o_ref, db_ref, carry):
        @pl.when(pl.program_id(0) == 0)
        def _():
            carry[...] = jnp.zeros_like(carry)
            db_ref[...] = jnp.zeros_like(db_ref)

        x = f_ref[...] + b_ref[...]
        sg = _sigmoid(-x)
        dcv = dc_ref[...] + dcq_ref[...]
        c = carry[...]
        db = jnp.zeros((H, 1), F32)
        for j in reversed(range(TB // LANE)):
            seg = _lane_cumsum(dcv[:, j * LANE:(j + 1) * LANE], reverse=True) + c
            df = seg * sg[:, j * LANE:(j + 1) * LANE]
            o_ref[:, j * LANE:(j + 1) * LANE] = df
            db = db + jnp.sum(df, axis=1, keepdims=True)
            c = seg[:, 0:1]
        carry[...] = c
        db_ref[...] += db

    rev = lambda i: (0, nb - 1 - i)
    return pl.pallas_call(
        body, grid=(nb,),
        in_specs=[pl.BlockSpec((H, TB), rev), pl.BlockSpec((H, 1), lambda i: (0, 0)), pl.BlockSpec((H, TB), rev),
                  pl.BlockSpec((H, TB), rev)],
        out_specs=[pl.BlockSpec((H, TB), rev), pl.BlockSpec((H, 1), lambda i: (0, 0))],
        out_shape=[jax.ShapeDtypeStruct((H, S), F32), jax.ShapeDtypeStruct((H, 1), F32)],
        scratch_shapes=[pltpu.VMEM((H, 1), F32)], compiler_params=_params(("arbitrary",)), name=name)(
            fT, bias, dc, dcq)


def _attn_scores(q, k, cq, ck, qi, ki, T):
    s = _dot(q, k, NT) * (HEAD_DIM ** -0.5) + (cq - ck)
    row = qi * T + lax.broadcasted_iota(jnp.int32, (T, T), 0)
    col = ki * T + lax.broadcasted_iota(jnp.int32, (T, T), 1)
    return jnp.where(col <= row, s, NEG_BIG)


def _attn_fwd(q, k, v, c_col, c_row, name):
    H, S, Dh = q.shape
    T = _tile(S)
    nq = S // T

    def body(q_ref, k_ref, v_ref, cq_ref, ck_ref, o_ref, lse_ref, m_s, l_s, acc_s):
        qi, ki = pl.program_id(1), pl.program_id(2)

        @pl.when(ki == 0)
        def _():
            m_s[...] = jnp.full_like(m_s, NEG_BIG)
            l_s[...] = jnp.zeros_like(l_s)
            acc_s[...] = jnp.zeros_like(acc_s)

        @pl.when(ki <= qi)
        def _():
            s = _attn_scores(q_ref[0], k_ref[0], cq_ref[0], ck_ref[0], qi, ki, T)
            m_new = jnp.maximum(m_s[...], jnp.max(s, axis=1, keepdims=True))
            alpha = jnp.exp(m_s[...] - m_new)
            p = jnp.exp(s - m_new)
            l_s[...] = alpha * l_s[...] + jnp.sum(p, axis=1, keepdims=True)
            acc_s[...] = alpha * acc_s[...] + _dot(p, v_ref[0])
            m_s[...] = m_new

        @pl.when(ki == qi)
        def _():
            o_ref[0] = acc_s[...] / l_s[...]
            lse_ref[0] = m_s[...] + jnp.log(l_s[...])

    qmap = lambda h, qi, ki: (h, qi, 0)
    kmap = lambda h, qi, ki: (h, jnp.minimum(ki, qi), 0)
    return pl.pallas_call(
        body, grid=(H, nq, nq),
        in_specs=[pl.BlockSpec((1, T, Dh), qmap), pl.BlockSpec((1, T, Dh), kmap), pl.BlockSpec((1, T, Dh), kmap),
                  pl.BlockSpec((1, T, 1), qmap), pl.BlockSpec((1, 1, T), lambda h, qi, ki: (h, 0, jnp.minimum(ki, qi)))],
        out_specs=[pl.BlockSpec((1, T, Dh), qmap), pl.BlockSpec((1, T, 1), qmap)],
        out_shape=[jax.ShapeDtypeStruct((H, S, Dh), F32), jax.ShapeDtypeStruct((H, S, 1), F32)],
        scratch_shapes=[pltpu.VMEM((T, 1), F32), pltpu.VMEM((T, 1), F32), pltpu.VMEM((T, Dh), F32)],
        compiler_params=_params(("parallel", "parallel", "arbitrary")), name=name)(q, k, v, c_col, c_row)


def _attn_bwd(q, k, v, do, o, c_col, c_row, lse, name):
    H, S, Dh = q.shape
    T = _tile(S)
    nq = S // T
    scale = HEAD_DIM ** -0.5

    def body(q_ref, k_ref, v_ref, do_ref, o_ref, cq_ref, ck_ref, lse_ref, dq_ref, dk_ref, dv_ref, dc_ref, dcq_ref,
             dk_s, dv_s, dc_s):
        ki, qi = pl.program_id(1), pl.program_id(2)

        @pl.when((ki == 0) & (qi == 0))
        def _():
            dq_ref[...] = jnp.zeros_like(dq_ref)
            dcq_ref[...] = jnp.zeros_like(dcq_ref)

        @pl.when(qi == 0)
        def _():
            dk_s[...] = jnp.zeros_like(dk_s)
            dv_s[...] = jnp.zeros_like(dv_s)
            dc_s[...] = jnp.zeros_like(dc_s)

        @pl.when(qi >= ki)
        def _():
            qv, kv, vv = q_ref[0], k_ref[0], v_ref[0]
            s = _attn_scores(qv, kv, cq_ref[0], ck_ref[0], qi, ki, T)
            p = jnp.exp(s - lse_ref[0])
            dov = do_ref[0]
            delta = jnp.sum(dov * o_ref[0], axis=1, keepdims=True)
            dv_s[...] += _dot(p.T, dov)
            dp = _dot(dov, vv, NT)
            ds = p * (dp - delta)
            dc_s[...] -= jnp.sum(ds, axis=0, keepdims=True)
            dss = ds * scale
            rows = pl.ds(pl.multiple_of(qi * T, T), T)
            dq_ref[0, rows, :] += _dot(dss, kv)
            dcq_ref[0, rows, :] += jnp.sum(ds, axis=1, keepdims=True)
            dk_s[...] += _dot(dss.T, qv)

        @pl.when(qi == nq - 1)
        def _():
            dk_ref[0] = dk_s[...]
            dv_ref[0] = dv_s[...]
            dc_ref[0] = dc_s[...]

    qmap = lambda h, ki, qi: (h, jnp.maximum(qi, ki), 0)
    kmap = lambda h, ki, qi: (h, ki, 0)
    return pl.pallas_call(
        body, grid=(H, nq, nq),
        in_specs=[pl.BlockSpec((1, T, Dh), qmap), pl.BlockSpec((1, T, Dh), kmap), pl.BlockSpec((1, T, Dh), kmap),
                  pl.BlockSpec((1, T, Dh), qmap), pl.BlockSpec((1, T, Dh), qmap), pl.BlockSpec((1, T, 1), qmap),
                  pl.BlockSpec((1, 1, T), lambda h, ki, qi: (h, 0, ki)), pl.BlockSpec((1, T, 1), qmap)],
        out_specs=[pl.BlockSpec((1, S, Dh), lambda h, ki, qi: (h, 0, 0)), pl.BlockSpec((1, T, Dh), kmap),
                   pl.BlockSpec((1, T, Dh), kmap), pl.BlockSpec((1, 1, T), lambda h, ki, qi: (h, 0, ki)),
                   pl.BlockSpec((1, S, 1), lambda h, ki, qi: (h, 0, 0))],
        out_shape=[jax.ShapeDtypeStruct((H, S, Dh), F32), jax.ShapeDtypeStruct((H, S, Dh), F32),
                   jax.ShapeDtypeStruct((H, S, Dh), F32), jax.ShapeDtypeStruct((H, 1, S), F32),
                   jax.ShapeDtypeStruct((H, S, 1), F32)],
        scratch_shapes=[pltpu.VMEM((T, Dh), F32), pltpu.VMEM((T, Dh), F32), pltpu.VMEM((1, T), F32)],
        compiler_params=_params(("parallel", "arbitrary", "arbitrary")), name=name)(
            q, k, v, do, o, c_col, c_row, lse)


def _conv_taps(ext, w_ref, K, T):
    out = None
    for k in range(K):
        term = ext[pl.ds(HALO - (K - 1 - k), T), :] * w_ref[k:k + 1, :]
        out = term if out is None else out + term
    return out


def _conv_bwd_b(dpre, w, name):
    S, C = dpre.shape
    K = w.shape[0]
    T = _tile(S, (128,))

    def body(i, n, R, Cs, P, N, O, A, Sc):
        ext = Sc[0]
        _fill_next(ext, R[0][...], jnp.where(i < n - 1, N[0][...], 0.0), T)
        out = None
        for k in range(K):
            term = ext[pl.ds(K - 1 - k, T), :] * Cs[0][k:k + 1, :]
            out = term if out is None else out + term
        O[0][...] = out.astype(MXU_DTYPE)

    return _rows(body, S, T, rows=[dpre], nexts=[dpre], consts=[w], out_rows=[(C, MXU_DTYPE)],
                 scratch=[pltpu.VMEM((T + HALO, C), F32)], name=name)[0]


def _dt_mask():
    lane = lax.broadcasted_iota(jnp.int32, (1, LANE), 1)
    return ((lane >= DT_LANE0) & (lane < DT_LANE0 + SSD_HEADS)).astype(F32)


def _ssd_pre_fwd(xbc, fdt, cw, cb, dtb, name):
    S, C = xbc.shape
    T = _tile(S, (256, 128))
    K = cw.shape[0]

    def body(i, n, R, Cs, P, N, O, A, Sc):
        ext = Sc[0]
        _fill_prev(ext, P[0], R[0][...], i)
        pre = _conv_taps(ext, Cs[0], K, T) + Cs[1][...]
        O[0][...] = pre * _sigmoid(pre)
        O[1][...] = _softplus(R[1][...] + Cs[2][...]) * _dt_mask()

    return _rows(body, S, T, rows=[xbc, fdt], prevs=[xbc], consts=[cw, cb, dtb],
                 out_rows=[(C, F32), (LANE, F32)], scratch=[pltpu.VMEM((HALO + T, C), F32)], name=name)


def _silu_grad(pre):
    sg = _sigmoid(pre)
    return sg * (1.0 + pre * (1.0 - sg))


def _conv_wgrad(ext, dpre, K, T):
    return jnp.concatenate([jnp.sum(dpre * ext[pl.ds(HALO - (K - 1 - k), T), :], axis=0, keepdims=True)
                            for k in range(K)], axis=0)


def _ssd_pre_bwd_a(xbc, fdt, dxa, ddtw, cw, cb, dtb, name):
    S, C = xbc.shape
    T = _tile(S, (256, 128))
    K = cw.shape[0]

    def body(i, n, R, Cs, P, N, O, A, Sc):
        ext = Sc[0]
        _fill_prev(ext, P[0], R[0][...], i)
        pre = _conv_taps(ext, Cs[0], K, T) + Cs[1][...]
        dpre = R[2][...] * _silu_grad(pre)
        O[0][...] = dpre
        A[0][...] += _conv_wgrad(ext, dpre, K, T)
        A[1][...] += jnp.sum(dpre, axis=0, keepdims=True)
        ddt = R[3][...] * _sigmoid(R[1][...] + Cs[2][...]) * _dt_mask()
        O[1][...] = ddt
        A[2][...] += jnp.sum(ddt, axis=0, keepdims=True)

    return _rows(body, S, T, rows=[xbc, fdt, dxa, ddtw], prevs=[xbc], consts=[cw, cb, dtb],
                 out_rows=[(C, F32), (LANE, F32)],
                 out_accs=[((K, C), F32), ((1, C), F32), ((1, LANE), F32)],
                 scratch=[pltpu.VMEM((HALO + T, C), F32)], name=name)


def _split3(x):
    hi = x.astype(jnp.bfloat16)
    r1 = x - hi.astype(F32)
    mid = r1.astype(jnp.bfloat16)
    lo = (r1 - mid.astype(F32)).astype(jnp.bfloat16)
    return hi, mid, lo


def _expand_mat():
    r = lax.broadcasted_iota(jnp.int32, (LANE, SSD_W), 0)
    c = lax.broadcasted_iota(jnp.int32, (LANE, SSD_W), 1)
    return (r - DT_LANE0 == c // SSD_P).astype(jnp.bfloat16)


def _headsum_mat():
    r = lax.broadcasted_iota(jnp.int32, (SSD_W, LANE), 0)
    c = lax.broadcasted_iota(jnp.int32, (SSD_W, LANE), 1)
    return (c - DT_LANE0 == r // SSD_P).astype(jnp.bfloat16)


def _expand(tile, ex):
    return sum(lax.dot_general(part, ex, (((1,), (0,)), ((), ())), preferred_element_type=F32)
               for part in _split3(tile))


def _headsum(full, hs):
    return sum(lax.dot_general(part, hs, (((1,), (0,)), ((), ())), preferred_element_type=F32)
               for part in _split3(full))


def _sub_cumsum(a, reverse=False):
    n = a.shape[0]
    row = lax.broadcasted_iota(jnp.int32, a.shape, 0)
    sh = 1
    while sh < n:
        if reverse:
            a = a + jnp.where(row < n - sh, pltpu.roll(a, n - sh, axis=0), 0.0)
        else:
            a = a + jnp.where(row >= sh, pltpu.roll(a, sh, axis=0), 0.0)
        sh *= 2
    return a


def _chunk_common(xa_ref, dtw_ref, a_row, ex):
    L = SSD_CHUNK
    xs = xa_ref[:, 0:SSD_W]
    dtv = dtw_ref[...]
    acs = _sub_cumsum(dtv * a_row)
    last = acs[L - 1:L, :]
    dt_full = _expand(dtv, ex)
    xd = xs * dt_full
    dec_full = _expand(jnp.exp(last - acs), ex)
    e_full = _expand(jnp.exp(acs), ex)
    elast_full = _expand(jnp.broadcast_to(jnp.exp(last), (8, LANE)), ex)[0:1, :]
    return xs, dtv, acs, last, dt_full, xd, dec_full, e_full, elast_full


def _decay_mask(acs, acsT, col):
    L = SSD_CHUNK
    diff = acs[:, col:col + 1] - acsT[col:col + 1, :]
    tril = lax.broadcasted_iota(jnp.int32, (L, L), 0) >= lax.broadcasted_iota(jnp.int32, (L, L), 1)
    return jnp.where(tril, jnp.exp(jnp.minimum(diff, 0.0)), 0.0)


def _half_mask(h):
    lane = lax.broadcasted_iota(jnp.int32, (1, LANE), 1)
    return ((lane // SSD_P) == (h % 2)).astype(F32)


def _ssd_chunk_fwd(xa, dtw, a_row, d_full, name):
    S = xa.shape[0]
    L, G = SSD_CHUNK, 2
    nc = S // L
    GW = SSD_W // G

    def body(xa_ref, dtw_ref, a_ref, d_ref, y_ref, hp_ref, state):
        @pl.when(pl.program_id(0) == 0)
        def _():
            state[...] = jnp.zeros_like(state)

        ex = _expand_mat()
        xs, dtv, acs, last, dt_full, xd, dec_full, e_full, elast_full = _chunk_common(xa_ref, dtw_ref, a_ref[...], ex)
        acsT = acs.T
        hp_ref[0] = state[...]
        for g in range(G):
            gc = slice(g * GW, (g + 1) * GW)
            Bg = xa_ref[:, SSD_W + g * SSD_N: SSD_W + (g + 1) * SSD_N]
            Cg = xa_ref[:, SSD_W + G * SSD_N + g * SSD_N: SSD_W + G * SSD_N + (g + 1) * SSD_N]
            cb = _dot(Cg, Bg, NT)
            y_off = e_full[:, gc] * _dot(Cg, state[:, gc])
            for hp in range(GW // LANE):
                pc = slice(g * GW + hp * LANE, g * GW + (hp + 1) * LANE)
                xd_pair = xd[:, pc]
                yp = y_off[:, hp * LANE:(hp + 1) * LANE] + d_ref[:, pc] * xs[:, pc]
                for h2 in range(2):
                    h = (g * GW + hp * LANE) // SSD_P + h2
                    m = cb * _decay_mask(acs, acsT, DT_LANE0 + h)
                    yp = yp + _dot(m, xd_pair * _half_mask(h))
                y_ref[:, pc] = yp
            st_new = _dot(Bg.T, xd[:, gc] * dec_full[:, gc])
            state[:, gc] = elast_full[:, gc] * state[:, gc] + st_new

    return pl.pallas_call(
        body, grid=(nc,),
        in_specs=[pl.BlockSpec((L, SSD_CONV_CH), lambda c: (c, 0)), pl.BlockSpec((L, LANE), lambda c: (c, 0)),
                  pl.BlockSpec((1, LANE), lambda c: (0, 0)), pl.BlockSpec((1, SSD_W), lambda c: (0, 0))],
        out_specs=[pl.BlockSpec((L, SSD_W), lambda c: (c, 0)), pl.BlockSpec((1, SSD_N, SSD_W), lambda c: (c, 0, 0))],
        out_shape=[jax.ShapeDtypeStruct((S, SSD_W), F32), jax.ShapeDtypeStruct((nc, SSD_N, SSD_W), F32)],
        scratch_shapes=[pltpu.VMEM((SSD_N, SSD_W), F32)],
        compiler_params=_params(("arbitrary",)), name=name)(xa, dtw, a_row, d_full)


def _ssd_chunk_bwd(xa, dtw, dy, hprev, a_row, d_full, name):
    S = xa.shape[0]
    L, G = SSD_CHUNK, 2
    nc = S // L
    GW = SSD_W // G

    def body(xa_ref, dtw_ref, dy_ref, hp_ref, a_ref, d_ref, dxa_ref, ddt_ref, da_ref, dd_ref, dstate):
        @pl.when(pl.program_id(0) == 0)
        def _():
            dstate[...] = jnp.zeros_like(dstate)
            da_ref[...] = jnp.zeros_like(da_ref)
            dd_ref[...] = jnp.zeros_like(dd_ref)

        ex, hs = _expand_mat(), _headsum_mat()
        a_row = a_ref[...]
        xs, dtv, acs, last, dt_full, xd, dec_full, e_full, elast_full = _chunk_common(xa_ref, dtw_ref, a_row, ex)
        acsT = acs.T
        dyv = dy_ref[...]
        lane = lax.broadcasted_iota(jnp.int32, (L, LANE), 1)
        sub = lax.broadcasted_iota(jnp.int32, (LANE, L), 0)
        dacs_c = jnp.zeros((L, LANE), F32)
        dacs_r = jnp.zeros((LANE, L), F32)
        dd_ref[...] += jnp.sum(_headsum(dyv * xs, hs), axis=0, keepdims=True)
        dxd_parts, yoff_parts, dxdd_parts, hh_parts = [], [], [], []
        for g in range(G):
            gc = slice(g * GW, (g + 1) * GW)
            b0 = SSD_W + g * SSD_N
            c0 = SSD_W + G * SSD_N + g * SSD_N
            Bg = xa_ref[:, b0:b0 + SSD_N]
            Cg = xa_ref[:, c0:c0 + SSD_N]
            Hp = hp_ref[0, :, gc]
            dH = dstate[:, gc]
            cb = _dot(Cg, Bg, NT)
            Gm = _dot(Cg, Hp)
            yoff_parts.append(e_full[:, gc] * Gm)
            dG = e_full[:, gc] * dyv[:, gc]
            dC = _dot(dG, Hp, NT)
            dHp = _dot(Cg.T, dG)
            xdd = xd[:, gc] * dec_full[:, gc]
            dB = _dot(xdd, dH, NT)
            dxdd = _dot(Bg, dH)
            dxdd_parts.append(dxdd)
            hh_parts.append(dH * Hp)
            dstate[:, gc] = dHp + elast_full[:, gc] * dH
            dcb = jnp.zeros((L, L), F32)
            dxd_g = []
            for hp in range(GW // LANE):
                pc = slice(g * GW + hp * LANE, g * GW + (hp + 1) * LANE)
                xd_pair = xd[:, pc]
                dxd_pair = dxdd[:, hp * LANE:(hp + 1) * LANE] * dec_full[:, pc]
                for h2 in range(2):
                    h = (g * GW + hp * LANE) // SSD_P + h2
                    col = DT_LANE0 + h
                    lm = _decay_mask(acs, acsT, col)
                    m = cb * lm
                    dy_h = dyv[:, pc] * _half_mask(h)
                    dm = _dot(dy_h, xd_pair, NT)
                    dxd_pair = dxd_pair + _dot(m.T, dy_h)
                    wm = dm * m
                    dacs_c = dacs_c + jnp.where(lane == col, jnp.sum(wm, axis=1, keepdims=True), 0.0)
                    dacs_r = dacs_r - jnp.where(sub == col, jnp.sum(wm, axis=0, keepdims=True), 0.0)
                    dcb = dcb + dm * lm
                dxd_g.append(dxd_pair)
            dxd_parts.append(jnp.concatenate(dxd_g, axis=1))
            dxa_ref[:, c0:c0 + SSD_N] = dC + _dot(dcb, Bg)
            dxa_ref[:, b0:b0 + SSD_N] = dB + _dot(dcb.T, Cg)
        dxd = jnp.concatenate(dxd_parts, axis=1)
        y_off = jnp.concatenate(yoff_parts, axis=1)
        dxdd_full = jnp.concatenate(dxdd_parts, axis=1)
        hh = jnp.concatenate(hh_parts, axis=1)
        dxa_ref[:, 0:SSD_W] = d_ref[...] * dyv + dxd * dt_full
        ddt = _headsum(dxd * xs, hs)
        w_dec = _headsum(dxdd_full * xd, hs) * jnp.exp(last - acs)
        dlast = jnp.sum(w_dec, axis=0, keepdims=True) + jnp.exp(last) * jnp.sum(_headsum(hh, hs), axis=0, keepdims=True)
        dacs = dacs_c + dacs_r.T + _headsum(dyv * y_off, hs) - w_dec
        rowid = lax.broadcasted_iota(jnp.int32, (L, LANE), 0)
        dacs = dacs + jnp.where(rowid == L - 1, dlast, 0.0)
        da = _sub_cumsum(dacs, reverse=True)
        ddt_ref[...] = ddt + da * a_row
        da_ref[...] += jnp.sum(da * dtv, axis=0, keepdims=True)

    rev = lambda c: (nc - 1 - c, 0)
    return pl.pallas_call(
        body, grid=(nc,),
        in_specs=[pl.BlockSpec((L, SSD_CONV_CH), rev), pl.BlockSpec((L, LANE), rev), pl.BlockSpec((L, SSD_W), rev),
                  pl.BlockSpec((1, SSD_N, SSD_W), lambda c: (nc - 1 - c, 0, 0)),
                  pl.BlockSpec((1, LANE), lambda c: (0, 0)), pl.BlockSpec((1, SSD_W), lambda c: (0, 0))],
        out_specs=[pl.BlockSpec((L, SSD_CONV_CH), rev), pl.BlockSpec((L, LANE), rev),
                   pl.BlockSpec((1, LANE), lambda c: (0, 0)), pl.BlockSpec((1, LANE), lambda c: (0, 0))],
        out_shape=[jax.ShapeDtypeStruct((S, SSD_CONV_CH), F32), jax.ShapeDtypeStruct((S, LANE), F32),
                   jax.ShapeDtypeStruct((1, LANE), F32), jax.ShapeDtypeStruct((1, LANE), F32)],
        scratch_shapes=[pltpu.VMEM((SSD_N, SSD_W), F32)],
        compiler_params=_params(("arbitrary",)), name=name)(xa, dtw, dy, hprev, a_row, d_full)


def _ssd_post_fwd(y, z, w, name):
    S = y.shape[0]
    GW = SSD_W // 2

    def body(i, n, R, C, P, N, O, A, Sc):
        zv = R[1][...]
        v = R[0][...] * (zv * _sigmoid(zv))
        for g in range(2):
            gc = slice(g * GW, (g + 1) * GW)
            vg = v[:, gc]
            r = lax.rsqrt(jnp.mean(vg * vg, axis=-1, keepdims=True) + NORM_EPS)
            O[0][:, gc] = (vg * r * C[0][:, gc]).astype(MXU_DTYPE)

    return _rows(body, S, _tile(S, (256, 128)), rows=[y, z], consts=[w], out_rows=[(SSD_W, MXU_DTYPE)], name=name)[0]


def _ssd_post_bwd(y, z, dyn, w, name):
    S = y.shape[0]
    GW = SSD_W // 2

    def body(i, n, R, C, P, N, O, A, Sc):
        yv, zv, dn = R[0][...], R[1][...], R[2][...]
        sz = zv * _sigmoid(zv)
        v = yv * sz
        for g in range(2):
            gc = slice(g * GW, (g + 1) * GW)
            dv, dw = _norm_bwd_math(v[:, gc], C[0][:, gc], dn[:, gc])
            A[0][:, gc] += dw
            O[0][:, gc] = dv * sz[:, gc]
            O[1][:, gc] = (dv * yv[:, gc] * _silu_grad(zv[:, gc])).astype(MXU_DTYPE)

    return _rows(body, S, _tile(S, (256, 128)), rows=[y, z, dyn], consts=[w],
                 out_rows=[(SSD_W, F32), (SSD_W, MXU_DTYPE)], out_accs=[((1, SSD_W), F32)], name=name)


def _merge_fwd(gl, yp, ya, ys, name):
    S, D = yp.shape

    def body(i, n, R, C, P, N, O, A, Sc):
        acc = None
        for b in range(3):
            term = _sigmoid(R[0][:, b * D:(b + 1) * D]) * R[1 + b][...]
            acc = term if acc is None else acc + term
        O[0][...] = acc.astype(MXU_DTYPE)

    return _rows(body, S, _tile(S, (256, 128)), rows=[gl, yp, ya, ys], out_rows=[(D, MXU_DTYPE)], name=name)[0]


def _merge_bwd(gl, yp, ya, ys, dm, name):
    S, D = yp.shape

    def body(i, n, R, C, P, N, O, A, Sc):
        dmv = R[4][...]
        for b in range(3):
            gt = _sigmoid(R[0][:, b * D:(b + 1) * D])
            O[b][...] = (gt * dmv).astype(MXU_DTYPE)
            O[3][:, b * D:(b + 1) * D] = (dmv * R[1 + b][...] * gt * (1.0 - gt)).astype(MXU_DTYPE)

    return _rows(body, S, _tile(S, (256, 128)), rows=[gl, yp, ya, ys, dm],
                 out_rows=[(D, MXU_DTYPE)] * 3 + [(3 * D, MXU_DTYPE)], name=name)


def _ffn_act_fwd(hpre, cw, cb, name):
    S, C = hpre.shape
    K = cw.shape[0]
    T = _tile(S, (128,))
    Fd = C // 2

    def body(i, n, R, Cs, P, N, O, A, Sc):
        ext = Sc[0]
        _fill_prev(ext, P[0], R[0][...], i)
        hc = _conv_taps(ext, Cs[0], K, T) + Cs[1][...]
        gt = hc[:, :Fd]
        O[0][...] = (gt * _sigmoid(gt) * hc[:, Fd:]).astype(MXU_DTYPE)

    return _rows(body, S, T, rows=[hpre], prevs=[hpre], consts=[cw, cb], out_rows=[(Fd, MXU_DTYPE)],
                 scratch=[pltpu.VMEM((HALO + T, C), F32)], name=name)[0]


def _ffn_act_bwd_a(hpre, dact, cw, cb, name):
    S, C = hpre.shape
    K = cw.shape[0]
    T = _tile(S, (128,))
    Fd = C // 2

    def body(i, n, R, Cs, P, N, O, A, Sc):
        ext = Sc[0]
        _fill_prev(ext, P[0], R[0][...], i)
        hc = _conv_taps(ext, Cs[0], K, T) + Cs[1][...]
        gt, val, da = hc[:, :Fd], hc[:, Fd:], R[1][...]
        dhc = jnp.concatenate([da * val * _silu_grad(gt), da * gt * _sigmoid(gt)], axis=1)
        O[0][...] = dhc
        A[0][...] += _conv_wgrad(ext, dhc, K, T)
        A[1][...] += jnp.sum(dhc, axis=0, keepdims=True)

    return _rows(body, S, T, rows=[hpre, dact], prevs=[hpre], consts=[cw, cb], out_rows=[(C, F32)],
                 out_accs=[((K, C), F32), ((1, C), F32)], scratch=[pltpu.VMEM((HALO + T, C), F32)], name=name)


def _adamw(g, w, m, v, name):
    R_, W = g.shape
    c1 = 1.0 - ADAM_B1 ** ADAM_STEP
    c2 = 1.0 - ADAM_B2 ** ADAM_STEP

    def body(i, n, R, C, P, N, O, A, Sc):
        gv = R[0][...]
        mn = ADAM_B1 * R[2][...] + (1.0 - ADAM_B1) * gv
        vn = ADAM_B2 * R[3][...] + (1.0 - ADAM_B2) * (gv * gv)
        O[0][...] = -ADAM_LR * ((mn / c1) / (jnp.sqrt(vn / c2) + ADAM_EPS) + ADAM_WD * R[1][...])
        O[1][...] = mn
        O[2][...] = vn

    return _rows(body, R_, _tile(R_, (512, 256, 128, 64, 32, 16, 8)), rows=[g, w, m, v],
                 out_rows=[(W, F32)] * 3, name=name)


def _sum_leading(a, name):
    P_, R_, W = a.shape
    T = _tile(R_)

    def body(a_ref, o_ref):
        acc = a_ref[0]
        for p in range(1, P_):
            acc = acc + a_ref[p]
        o_ref[...] = acc

    return pl.pallas_call(body, grid=(R_ // T,), in_specs=[pl.BlockSpec((P_, T, W), lambda i: (0, i, 0))],
                          out_specs=pl.BlockSpec((T, W), lambda i: (i, 0)),
                          out_shape=jax.ShapeDtypeStruct((R_, W), F32),
                          compiler_params=_params(("parallel",)), name=name)(a)


_ANY = pl.BlockSpec(memory_space=pl.ANY)
_MESH = pl.DeviceIdType.MESH


def _place_exchange(src, scatter, name):
    blk = src.shape[1:] if scatter else src.shape

    def body(src_ref, out_ref, send_sems, recv_sems, local_sem):
        x, y, c = lax.axis_index("x"), lax.axis_index("y"), lax.axis_index("c")
        me = 2 * x + y
        own = pltpu.make_async_copy(src_ref.at[me] if scatter else src_ref, out_ref.at[me], local_sem)
        own.start()
        copies = []
        for j, (px, py) in enumerate([(1 - x, y), (x, 1 - y), (1 - x, 1 - y)]):
            cp = pltpu.make_async_remote_copy(
                src_ref=src_ref.at[2 * px + py] if scatter else src_ref, dst_ref=out_ref.at[me],
                send_sem=send_sems.at[j], recv_sem=recv_sems.at[j], device_id=(px, py, c), device_id_type=_MESH)
            cp.start()
            copies.append(cp)
        for cp in copies:
            cp.wait()
        own.wait()

    return pl.pallas_call(
        body, in_specs=[_ANY], out_specs=_ANY, out_shape=jax.ShapeDtypeStruct((N_PLACES,) + tuple(blk), src.dtype),
        scratch_shapes=[pltpu.SemaphoreType.DMA((3,)), pltpu.SemaphoreType.DMA((3,)), pltpu.SemaphoreType.DMA],
        name=name)(src)


def _sibling_split(g, name):
    P_, R2, W = g.shape
    Rh = R2 // 2

    def body(g_ref, out_ref, send_sem, recv_sem, local_sem):
        x, y, c = lax.axis_index("x"), lax.axis_index("y"), lax.axis_index("c")
        mine = g_ref.at[:, pl.ds(pl.multiple_of(c * Rh, 8), Rh), :]
        theirs = g_ref.at[:, pl.ds(pl.multiple_of((1 - c) * Rh, 8), Rh), :]
        own = pltpu.make_async_copy(mine, out_ref.at[c], local_sem)
        own.start()
        cp = pltpu.make_async_remote_copy(src_ref=theirs, dst_ref=out_ref.at[c], send_sem=send_sem, recv_sem=recv_sem,
                                          device_id=(x, y, 1 - c), device_id_type=_MESH)
        cp.start()
        cp.wait_send()
        pltpu.make_async_remote_copy(src_ref=theirs, dst_ref=out_ref.at[1 - c], send_sem=send_sem, recv_sem=recv_sem,
                                     device_id=(x, y, 1 - c), device_id_type=_MESH).wait_recv()
        own.wait()

    return pl.pallas_call(
        body, in_specs=[_ANY], out_specs=_ANY, out_shape=jax.ShapeDtypeStruct((2, P_, Rh, W), g.dtype),
        scratch_shapes=[pltpu.SemaphoreType.DMA, pltpu.SemaphoreType.DMA, pltpu.SemaphoreType.DMA], name=name)(g)


def _sibling_gather(q, name):
    Rh, W = q.shape

    def body(q_ref, out_ref, send_sem, recv_sem, local_sem):
        x, y, c = lax.axis_index("x"), lax.axis_index("y"), lax.axis_index("c")
        own = pltpu.make_async_copy(q_ref, out_ref.at[c], local_sem)
        own.start()
        cp = pltpu.make_async_remote_copy(src_ref=q_ref, dst_ref=out_ref.at[c], send_sem=send_sem, recv_sem=recv_sem,
                                          device_id=(x, y, 1 - c), device_id_type=_MESH)
        cp.start()
        cp.wait_send()
        pltpu.make_async_remote_copy(src_ref=q_ref, dst_ref=out_ref.at[1 - c], send_sem=send_sem, recv_sem=recv_sem,
                                     device_id=(x, y, 1 - c), device_id_type=_MESH).wait_recv()
        own.wait()

    return pl.pallas_call(
        body, in_specs=[_ANY], out_specs=_ANY, out_shape=jax.ShapeDtypeStruct((2, Rh, W), q.dtype),
        scratch_shapes=[pltpu.SemaphoreType.DMA, pltpu.SemaphoreType.DMA, pltpu.SemaphoreType.DMA],
        name=name)(q)


def _split_w_in(w):
    o = [0]
    for n in IN_SPLITS:
        o.append(o[-1] + n)
    pad = jnp.zeros((w.shape[0], LANE - DT_LANE0 - SSD_HEADS), w.dtype)
    return dict(p=w[:, o[0]:o[4]], z=w[:, o[5]:o[6]], x=w[:, o[6]:o[7]], g=w[:, o[8]:o[9]],
                f=jnp.concatenate([w[:, o[4]:o[5]], w[:, o[7]:o[8]], pad], axis=1))


def _join_w_in(d):
    f = d['f']
    return jnp.concatenate([d['p'], f[:, :DT_LANE0], d['z'], d['x'], f[:, DT_LANE0:DT_LANE0 + SSD_HEADS], d['g']], axis=1)


def _to_heads(a, dtype):
    S = a.shape[0]
    return a.reshape(S, HEADS, HEAD_DIM).transpose(1, 0, 2).astype(dtype)


def _from_heads(a):
    return a.transpose(1, 0, 2).reshape(a.shape[1], HEADS * HEAD_DIM)


def _lane_tile(vec16):
    return jnp.concatenate([jnp.zeros((DT_LANE0,), F32), vec16,
                            jnp.zeros((LANE - DT_LANE0 - SSD_HEADS,), F32)])[None]


def _layer_consts(W, l):
    return dict(
        norm_mix=W['norm_mix'][l][None], mix=W['pool_mix'][l].astype(MXU_DTYPE), scale=W['pool_scale'][l][None],
        f_bias=W['f_bias'][l][:, None], cw=W['ssd_conv_w'][l], cb=W['ssd_conv_b'][l][None],
        dtb=_lane_tile(W['ssd_dt_bias'][l]), a_row=_lane_tile(-jnp.exp(W['ssd_a_log'][l])),
        d_full=jnp.repeat(W['ssd_d'][l], SSD_P)[None], ssd_norm=W['ssd_norm'][l][None],
        norm_ffn=W['norm_ffn'][l][None], fcw=W['ffn_conv_w'][l], fcb=W['ffn_conv_b'][l][None])


def _layer_fwd(x, W, l):
    n = f"l{l}_"
    cs = _layer_consts(W, l)
    win = _split_w_in(W['w_in'][l])
    u = _norm_fwd(x, cs['norm_mix'], n + "norm_mix")
    pqkv = _mm(u, win['p'], name=n + "in_p")
    z = _mm(u, win['z'], name=n + "in_z")
    xbc = _mm(u, win['x'], name=n + "in_x")
    gl = _mm(u, win['g'], name=n + "in_g")
    fdt = _mm(u, win['f'], name=n + "in_f")
    d, ypm = _pool_fwd(pqkv, cs['mix'], cs['scale'], n + "pool")
    yp = _mm(ypm, W['p_pool'][l], name=n + "p_pool")
    fT = fdt[:, :HEADS].T
    c = _logf_cumsum(fT, cs['f_bias'], n + "logf")
    c_col, c_row = c[:, :, None], c[:, None, :]
    qh, kh, vh = (_to_heads(pqkv[:, (1 + j) * ATTN_W:(2 + j) * ATTN_W], MXU_DTYPE) for j in range(3))
    oh, lse = _attn_fwd(qh, kh, vh, c_col, c_row, n + "attn")
    o = _from_heads(oh)
    ya = _mm(o, W['p_attn'][l], name=n + "p_attn")
    xa, dtw = _ssd_pre_fwd(xbc, fdt, cs['cw'], cs['cb'], cs['dtb'], n + "ssd_pre")
    y, hprev = _ssd_chunk_fwd(xa, dtw, cs['a_row'], cs['d_full'], n + "ssd_scan")
    yn = _ssd_post_fwd(y, z, cs['ssd_norm'], n + "ssd_post")
    ys = _mm(yn, W['p_ssd'][l], name=n + "p_ssd")
    merged = _merge_fwd(gl, yp, ya, ys, n + "merge")
    x1 = _mm(merged, W['w_out'][l], acc=x, name=n + "w_out")
    u2 = _norm_fwd(x1, cs['norm_ffn'], n + "norm_ffn")
    hpre = _mm(u2, W['ffn_up'][l], name=n + "ffn_up")
    act = _ffn_act_fwd(hpre, cs['fcw'], cs['fcb'], n + "ffn_act")
    x2 = _mm(act, W['ffn_down'][l], acc=x1, name=n + "ffn_down")
    saved = dict(x=x, u=u, pqkv=pqkv, z=z, xbc=xbc, gl=gl, fdt=fdt, d=d, ypm=ypm, yp=yp, fT=fT, c_col=c_col,
                 c_row=c_row, qh=qh, kh=kh, vh=vh, oh=oh, o=o, lse=lse, ya=ya, xa=xa, dtw=dtw, y=y, hprev=hprev,
                 yn=yn, ys=ys, merged=merged, x1=x1, u2=u2, hpre=hpre, act=act, win=win, cs=cs)
    return x2, saved


def _layer_bwd(dx2, sv, W, l):
    n = f"l{l}_b_"
    cs, win = sv['cs'], sv['win']
    g = {}
    dact = _mm(dx2, W['ffn_down'][l], tb=True, name=n + "ffn_down_dx")
    g['ffn_down'] = _mm(sv['act'], dx2, ta=True, name=n + "ffn_down_dw")
    dhc, g['ffn_conv_w'], dfcb = _ffn_act_bwd_a(sv['hpre'], dact, cs['fcw'], cs['fcb'], n + "ffn_act_a")
    g['ffn_conv_b'] = dfcb[0]
    dhpre = _conv_bwd_b(dhc, cs['fcw'], n + "ffn_act_b")
    du2 = _mm(dhpre, W['ffn_up'][l], tb=True, name=n + "ffn_up_dx")
    g['ffn_up'] = _mm(sv['u2'], dhpre, ta=True, name=n + "ffn_up_dw")
    dx1, dnf = _norm_bwd(sv['x1'], cs['norm_ffn'], du2, dx2, n + "norm_ffn")
    g['norm_ffn'] = dnf[0]
    dm = _mm(dx1, W['w_out'][l], tb=True, name=n + "w_out_dx")
    g['w_out'] = _mm(sv['merged'], dx1, ta=True, name=n + "w_out_dw")
    dyp, dya, dys, dgl = _merge_bwd(sv['gl'], sv['yp'], sv['ya'], sv['ys'], dm, n + "merge")
    dypm = _mm(dyp, W['p_pool'][l], tb=True, name=n + "p_pool_dx")
    g['p_pool'] = _mm(sv['ypm'], dyp, ta=True, name=n + "p_pool_dw")
    dd, dscale, dmix = _pool_bwd_a(dypm, sv['d'], cs['mix'], cs['scale'], n + "pool_a")
    g['pool_scale'] = dscale[0]
    g['pool_mix'] = dmix.reshape(len(POOL_WINDOWS), LANE, LANE)
    dpool_v = _pool_bwd_b(dd, n + "pool_b")
    do = _mm(dya, W['p_attn'][l], tb=True, name=n + "p_attn_dx")
    g['p_attn'] = _mm(sv['o'], dya, ta=True, name=n + "p_attn_dw")
    dqh, dkh, dvh, dc, dcq = _attn_bwd(sv['qh'], sv['kh'], sv['vh'], _to_heads(do, F32), sv['oh'], sv['c_col'],
                                       sv['c_row'], sv['lse'], n + "attn")
    dfT, dfb = _logf_cumsum_bwd(sv['fT'], cs['f_bias'], dc[:, 0, :], dcq[:, :, 0], n + "logf")
    g['f_bias'] = dfb[:, 0]
    dpqkv = jnp.concatenate([dpool_v] + [_from_heads(t).astype(MXU_DTYPE) for t in (dqh, dkh, dvh)], axis=1)
    dyn = _mm(dys, W['p_ssd'][l], tb=True, name=n + "p_ssd_dx")
    g['p_ssd'] = _mm(sv['yn'], dys, ta=True, name=n + "p_ssd_dw")
    dy, dz, dsn = _ssd_post_bwd(sv['y'], sv['z'], dyn, cs['ssd_norm'], n + "ssd_post")
    g['ssd_norm'] = dsn[0]
    dxa, ddtw, dA, dD = _ssd_chunk_bwd(sv['xa'], sv['dtw'], dy, sv['hprev'], cs['a_row'], cs['d_full'], n + "ssd_scan")
    heads = slice(DT_LANE0, DT_LANE0 + SSD_HEADS)
    g['ssd_a_log'] = dA[0, heads] * cs['a_row'][0, heads]
    g['ssd_d'] = dD[0, heads]
    dpre, ddt_raw, g['ssd_conv_w'], dcb, ddtb = _ssd_pre_bwd_a(sv['xbc'], sv['fdt'], dxa, ddtw, cs['cw'], cs['cb'],
                                                              cs['dtb'], n + "ssd_pre_a")
    g['ssd_conv_b'] = dcb[0]
    g['ssd_dt_bias'] = ddtb[0, heads]
    dxbc = _conv_bwd_b(dpre, cs['cw'], n + "ssd_pre_b")
    dfdt = jnp.concatenate([dfT.T, ddt_raw[:, HEADS:]], axis=1).astype(MXU_DTYPE)
    dsegs = dict(p=dpqkv, z=dz, x=dxbc, g=dgl, f=dfdt)
    du, dwin = None, {}
    for key in ('p', 'z', 'x', 'g', 'f'):
        du = _mm(dsegs[key], win[key], tb=True, acc=du, name=n + "in_dx_" + key)
        dwin[key] = _mm(sv['u'], dsegs[key], ta=True, name=n + "in_dw_" + key)
    g['w_in'] = _join_w_in(dwin)
    dx, dnm = _norm_bwd(sv['x'], cs['norm_mix'], du, dx1, n + "norm_mix")
    g['norm_mix'] = dnm[0]
    return dx, g


def _local_step(x, target, W):
    depth = W['norm_mix'].shape[0]
    saved = []
    h = x
    for l in range(depth):
        h, sv = _layer_fwd(h, W, l)
        saved.append(sv)
    dx, dwf, loss = _loss_head(h, W['norm_final'][None], target, "loss_head")
    grads = [None] * depth
    for l in reversed(range(depth)):
        dx, grads[l] = _layer_bwd(dx, saved[l], W, l)
    return loss[0, 0], dx, grads, dwf[0]


def _pack_rows(parts, row_align=1):
    flat = jnp.concatenate([p.reshape(-1) for p in parts])
    n = flat.shape[0]
    total = -(-n // (PACK_W * row_align)) * PACK_W * row_align
    if total > n:
        flat = jnp.concatenate([flat, jnp.zeros((total - n,), flat.dtype)])
    return flat.reshape(-1, PACK_W)


def _unpack_rows(buf, shapes):
    flat = buf.reshape(-1)
    out, pos = [], 0
    for shp in shapes:
        size = 1
        for s in shp:
            size *= s
        out.append(flat[pos:pos + size].reshape(shp))
        pos += size
    return out


def _to_place_major(gfull, name):
    R_, C = gfull.shape
    if name in COL_SHARDED:
        return gfull.reshape(R_, N_PLACES, C // N_PLACES).transpose(1, 0, 2)
    return gfull.reshape(N_PLACES, R_ // N_PLACES, C)


def _from_place_major(gathered, name):
    _, r, c = gathered.shape
    if name in COL_SHARDED:
        return gathered.transpose(1, 0, 2).reshape(r, N_PLACES * c)
    return gathered.reshape(N_PLACES * r, c)


def kernel(x, norm_mix, w_in, pool_mix, pool_scale, f_bias, ssd_conv_w, ssd_conv_b, ssd_dt_bias, ssd_a_log, ssd_d, ssd_norm, p_pool, p_attn, p_ssd, w_out, norm_ffn, ffn_up, ffn_conv_w, ffn_conv_b, ffn_down, norm_final, loss_target, m_norm_mix, m_w_in, m_pool_mix, m_pool_scale, m_f_bias, m_ssd_conv_w, m_ssd_conv_b, m_ssd_dt_bias, m_ssd_a_log, m_ssd_d, m_ssd_norm, m_p_pool, m_p_attn, m_p_ssd, m_w_out, m_norm_ffn, m_ffn_up, m_ffn_conv_w, m_ffn_conv_b, m_ffn_down, m_norm_final, v_norm_mix, v_w_in, v_pool_mix, v_pool_scale, v_f_bias, v_ssd_conv_w, v_ssd_conv_b, v_ssd_dt_bias, v_ssd_a_log, v_ssd_d, v_ssd_norm, v_p_pool, v_p_attn, v_p_ssd, v_w_out, v_norm_ffn, v_ffn_up, v_ffn_conv_w, v_ffn_conv_b, v_ffn_down, v_norm_final):
    args = dict(locals())
    w_sh = {k: args[k] for k in WEIGHTS}
    m_sh = {k: args['m_' + k] for k in WEIGHTS}
    v_sh = {k: args['v_' + k] for k in WEIGHTS}
    depth = norm_mix.shape[0]
    place = 2 * lax.axis_index("x") + lax.axis_index("y")

    big_shapes = [w_sh[k].shape for k in BIG]
    packed_w = _pack_rows([w_sh[k].astype(MXU_DTYPE) for k in BIG])
    gathered = _place_exchange(packed_w, False, "gather_weights")
    small_sh_shapes = [w_sh[k].shape for k in SMALL_SHARDED]
    gathered_small = _place_exchange(_pack_rows([w_sh[k] for k in SMALL_SHARDED]), False, "gather_small")
    per_place = [_unpack_rows(gathered[p], big_shapes) for p in range(N_PLACES)]
    per_place_small = [_unpack_rows(gathered_small[p], small_sh_shapes) for p in range(N_PLACES)]
    W = {k: w_sh[k] for k in SMALL if k not in SMALL_SHARDED}
    for j, k in enumerate(BIG):
        stacked = jnp.stack([per_place[p][j] for p in range(N_PLACES)], axis=1)
        W[k] = jnp.stack([_from_place_major(stacked[l], k) for l in range(depth)])
    for j, k in enumerate(SMALL_SHARDED):
        W[k] = jnp.concatenate([per_place_small[p][j] for p in range(N_PLACES)], axis=-1)

    loss_local, grad_x, grads, g_final = _local_step(x[0], loss_target[0], W)
    loss = lax.psum(loss_local, ("x", "y", "c"))

    small_full = [jnp.stack([grads[l][k] for l in range(depth)]) for k in SMALL if k != 'norm_final'] + [g_final]
    small_full_shapes = [a.shape for a in small_full]
    small_packed = _pack_rows(small_full)
    big_pm = []
    for k in BIG:
        pm = jnp.stack([_to_place_major(grads[l][k], k) for l in range(depth)], axis=1)
        big_pm.append(pm.reshape(N_PLACES, -1))
    big_pm = jnp.concatenate(big_pm, axis=1)
    n_big_rows = big_pm.shape[1] // PACK_W
    rows_used = n_big_rows + small_packed.shape[0]
    rows_total = -(-rows_used // PACK_ROW_ALIGN) * PACK_ROW_ALIGN
    gpack = jnp.concatenate([
        big_pm.reshape(N_PLACES, n_big_rows, PACK_W),
        jnp.broadcast_to(small_packed[None], (N_PLACES,) + small_packed.shape),
        jnp.zeros((N_PLACES, rows_total - rows_used, PACK_W), F32)], axis=1)

    both = _sibling_split(gpack, "reduce_sibling")
    half = _sum_leading(both.reshape(2, -1, PACK_W), "reduce_sibling_add").reshape(N_PLACES, -1, PACK_W)
    recv = _place_exchange(half, True, "reduce_places")
    q = _sum_leading(recv, "reduce_places_add")
    gsum = _sibling_gather(q, "reduce_gather").reshape(rows_total, PACK_W)

    def mine(k, a):
        if k in SMALL_SHARDED:
            c = a.shape[-1] // N_PLACES
            return lax.dynamic_slice_in_dim(a, place * c, c, axis=a.ndim - 1)
        return a

    small_names = [k for k in SMALL if k != 'norm_final'] + ['norm_final']
    g_big = _unpack_rows(gsum[:n_big_rows], big_shapes)
    g_small = [mine(k, a) for k, a in zip(small_names, _unpack_rows(gsum[n_big_rows:], small_full_shapes))]
    names = list(BIG) + small_names
    g_list = g_big + g_small
    shapes = [a.shape for a in g_list]
    gp = _pack_rows(g_list, 512)
    wp, mp, vp = (_pack_rows([d[k] for k in names], 512) for d in (w_sh, m_sh, v_sh))
    delta_p, m_p, v_p = _adamw(gp, wp, mp, vp, "adamw")
    outs = {}
    for prefix, buf in (('grad_', gp), ('delta_', delta_p), ('new_m_', m_p), ('new_v_', v_p)):
        for k, a in zip(names, _unpack_rows(buf, shapes)):
            outs[prefix + k] = a
    result = [loss, grad_x[None]]
    for prefix in ('grad_', 'delta_', 'new_m_', 'new_v_'):
        result += [outs[prefix + k] for k in WEIGHTS]
    return tuple(result)
```

```python
import functools

import jax
import jax.numpy as jnp
from jax import lax
from jax.experimental import pallas as pl
from jax.experimental.pallas import tpu as pltpu

F32 = jnp.float32
MXU_DTYPE = jnp.bfloat16
NORM_EPS = 1e-6
HALO = 16
LANE = 128
NEG_BIG = -1e30
VMEM_LIMIT = 52 * 1024 * 1024

D_MODEL = 1024
POOL_WINDOWS = (2, 4, 8, 16)
POOL_W = 512
HEADS = 8
HEAD_DIM = 64
ATTN_W = 512
ATTN_SCALE = HEAD_DIM ** -0.5
SSD_W = 1024
SSD_HEADS = 16
SSD_P = 64
SSD_N = 128
SSD_CHUNK = 128
SSD_CONV_CH = 1536
FFN = 2816
DT_LANE0 = 8
IN_SPLITS = (512, 512, 512, 512, 8, 1024, 1536, 16, 3072)
IN_TOTAL = sum(IN_SPLITS)
N_PLACES = 4

ADAM_LR, ADAM_B1, ADAM_B2, ADAM_EPS, ADAM_WD, ADAM_STEP = 0.001, 0.9, 0.999, 1e-08, 0.01, 10

BIG = ('w_in', 'p_pool', 'p_attn', 'p_ssd', 'w_out', 'ffn_up', 'ffn_down')
COL_SHARDED = ('w_in', 'p_pool', 'p_attn', 'ffn_up')
SMALL = ('norm_mix', 'pool_mix', 'pool_scale', 'f_bias', 'ssd_conv_w', 'ssd_conv_b', 'ssd_dt_bias',
         'ssd_a_log', 'ssd_d', 'ssd_norm', 'norm_ffn', 'ffn_conv_w', 'ffn_conv_b', 'norm_final')
SMALL_SHARDED = ('ssd_conv_w', 'ffn_conv_w')
WEIGHTS = ('norm_mix', 'w_in', 'pool_mix', 'pool_scale', 'f_bias', 'ssd_conv_w', 'ssd_conv_b', 'ssd_dt_bias',
           'ssd_a_log', 'ssd_d', 'ssd_norm', 'p_pool', 'p_attn', 'p_ssd', 'w_out', 'norm_ffn', 'ffn_up',
           'ffn_conv_w', 'ffn_conv_b', 'ffn_down', 'norm_final')
PACK_W = 1024


def _params(sem):
    return pltpu.CompilerParams(dimension_semantics=sem, vmem_limit_bytes=VMEM_LIMIT)


def _tile(n, prefs=(512, 256, 128)):
    for t in prefs:
        if n % t == 0:
            return t
    return n


def _sigmoid(x):
    return 1.0 / (1.0 + jnp.exp(-x))


def _softplus(x):
    return jnp.maximum(x, 0.0) + jnp.log1p(jnp.exp(-jnp.abs(x)))


def _dot(a, b, dims=((1,), (0,))):
    return lax.dot_general(a.astype(MXU_DTYPE), b.astype(MXU_DTYPE), (dims, ((), ())),
                           preferred_element_type=F32)


NT = ((1,), (1,))


def _mm(a, b, *, ta=False, tb=False, acc=None, out_dtype=F32, name):
    M, K = (a.shape[1], a.shape[0]) if ta else a.shape
    N = b.shape[0] if tb else b.shape[1]
    big = (1024, 1408, 512, 256, 128)
    tm, tn = _tile(M, big), _tile(N, big)
    tk = K if K <= 1024 else _tile(K, (512, 256, 128) if ta else big)
    nk = K // tk
    a_spec = pl.BlockSpec((tk, tm), lambda i, j, k: (k, i)) if ta else pl.BlockSpec((tm, tk), lambda i, j, k: (i, k))
    b_spec = pl.BlockSpec((tn, tk), lambda i, j, k: (j, k)) if tb else pl.BlockSpec((tk, tn), lambda i, j, k: (k, j))
    in_specs = [a_spec, b_spec]
    args = [a, b]
    if acc is not None:
        in_specs.append(pl.BlockSpec((tm, tn), lambda i, j, k: (i, j)))
        args.append(acc)

    def body(*refs):
        if acc is not None:
            a_ref, b_ref, c_ref, o_ref, acc_ref = refs
        else:
            a_ref, b_ref, o_ref, acc_ref = refs
        k = pl.program_id(2)

        @pl.when(k == 0)
        def _():
            if acc is not None:
                acc_ref[...] = c_ref[...].astype(F32)
            else:
                acc_ref[...] = jnp.zeros_like(acc_ref)

        av = a_ref[...]
        if ta:
            av = av.astype(F32).T
        acc_ref[...] += _dot(av, b_ref[...], NT if tb else ((1,), (0,)))

        @pl.when(k == nk - 1)
        def _():
            o_ref[...] = acc_ref[...].astype(out_dtype)

    return pl.pallas_call(
        body, grid=(M // tm, N // tn, nk), in_specs=in_specs,
        out_specs=pl.BlockSpec((tm, tn), lambda i, j, k: (i, j)),
        out_shape=jax.ShapeDtypeStruct((M, N), out_dtype),
        scratch_shapes=[pltpu.VMEM((tm, tn), F32)],
        compiler_params=_params(("parallel", "parallel", "arbitrary")), name=name)(*args)


def _rows(body, S, T, *, rows=(), consts=(), prevs=(), nexts=(), out_rows=(), out_accs=(), scratch=(), name):
    n = S // T
    hb = T // HALO
    last_h = S // HALO - 1

    def norm(r):
        return r if isinstance(r, tuple) else (r, r.shape[1], 0)

    rows, prevs, nexts = [norm(r) for r in rows], [norm(r) for r in prevs], [norm(r) for r in nexts]
    in_specs, args = [], []
    for arr, W, cb in rows:
        in_specs.append(pl.BlockSpec((T, W), lambda i, cb=cb: (i, cb)))
        args.append(arr)
    for cst in consts:
        in_specs.append(pl.BlockSpec(cst.shape, lambda i, nd=cst.ndim: (0,) * nd))
        args.append(cst)
    for arr, W, cb in prevs:
        in_specs.append(pl.BlockSpec((HALO, W), lambda i, cb=cb: (jnp.maximum(i * hb - 1, 0), cb)))
        args.append(arr)
    for arr, W, cb in nexts:
        in_specs.append(pl.BlockSpec((HALO, W), lambda i, cb=cb: (jnp.minimum((i + 1) * hb, last_h), cb)))
        args.append(arr)
    out_specs = [pl.BlockSpec((T, W), lambda i: (i, 0)) for W, _ in out_rows]
    out_specs += [pl.BlockSpec(shp, lambda i, nd=len(shp): (0,) * nd) for shp, _ in out_accs]
    out_shape = [jax.ShapeDtypeStruct((S, W), dt) for W, dt in out_rows]
    out_shape += [jax.ShapeDtypeStruct(shp, dt) for shp, dt in out_accs]
    cuts = [len(rows), len(consts), len(prevs), len(nexts), len(out_rows), len(out_accs), len(scratch)]

    def kern(*refs):
        groups, pos = [], 0
        for c in cuts:
            groups.append(list(refs[pos:pos + c]))
            pos += c
        i = pl.program_id(0)

        @pl.when(i == 0)
        def _():
            for a_ref in groups[5]:
                a_ref[...] = jnp.zeros_like(a_ref)

        body(i, n, *groups)

    outs = pl.pallas_call(kern, grid=(n,), in_specs=in_specs, out_specs=out_specs, out_shape=out_shape,
                          scratch_shapes=list(scratch), compiler_params=_params(("arbitrary",)), name=name)(*args)
    return outs


def _fill_prev(ext, prev_ref, cur, i):
    ext[0:HALO, :] = jnp.where(i > 0, prev_ref[...].astype(F32), 0.0)
    ext[HALO:, :] = cur


def _fill_next(ext, cur, next_val, T):
    ext[0:T, :] = cur
    ext[T:, :] = next_val


def _row_ids(i, T, W=1):
    return i * T + lax.broadcasted_iota(jnp.int32, (T, W), 0)


def _norm_fwd(x, w, name):
    S, D = x.shape

    def body(i, n, R, C, P, N, O, A, Sc):
        xv = R[0][...]
        r = lax.rsqrt(jnp.mean(xv * xv, axis=-1, keepdims=True) + NORM_EPS)
        O[0][...] = (xv * r * C[0][...]).astype(MXU_DTYPE)

    return _rows(body, S, _tile(S), rows=[x], consts=[w], out_rows=[(D, MXU_DTYPE)], name=name)[0]


def _norm_bwd_math(xv, w, du):
    r = lax.rsqrt(jnp.mean(xv * xv, axis=-1, keepdims=True) + NORM_EPS)
    xh = xv * r
    g = du * w
    dx = r * (g - xh * jnp.mean(g * xh, axis=-1, keepdims=True))
    dw = jnp.sum(du * xh, axis=0, keepdims=True)
    return dx, dw


def _norm_bwd(x, w, du, dres, name):
    S, D = x.shape

    def body(i, n, R, C, P, N, O, A, Sc):
        dx, dw = _norm_bwd_math(R[0][...], C[0][...], R[1][...])
        O[0][...] = R[2][...] + dx
        A[0][...] += dw

    return _rows(body, S, _tile(S), rows=[x, du, dres], consts=[w], out_rows=[(D, F32)],
                 out_accs=[((1, D), F32)], name=name)


def _loss_head(x, w, target, name):
    S, D = x.shape

    def body(i, n, R, C, P, N, O, A, Sc):
        xv, w_, tg = R[0][...], C[0][...], R[1][...]
        r = lax.rsqrt(jnp.mean(xv * xv, axis=-1, keepdims=True) + NORM_EPS)
        e = xv * r * w_ - tg
        A[1][...] += jnp.broadcast_to(0.5 * jnp.sum(jnp.mean(e * e, axis=-1, keepdims=True)), (1, LANE))
        dx, dw = _norm_bwd_math(xv, w_, e / D)
        O[0][...] = dx
        A[0][...] += dw

    return _rows(body, S, _tile(S), rows=[x, target], consts=[w], out_rows=[(D, F32)],
                 out_accs=[((1, D), F32), ((1, LANE), F32)], name=name)


def _pool_fwd(pqkv, mix, scale, name):
    S = pqkv.shape[0]
    T = _tile(S, (256, 128))

    def body(i, n, R, C, P, N, O, A, Sc):
        ext = Sc[0]
        v = R[0][...]
        _fill_prev(ext, P[0], v, i)
        t1 = (_row_ids(i, T) + 1).astype(F32)
        for g, w in enumerate(POOL_WINDOWS):
            cols = slice(g * LANE, (g + 1) * LANE)
            acc = v[:, cols]
            for j in range(1, w):
                acc = acc + ext[pl.ds(HALO - j, T), cols]
            d = (acc / jnp.minimum(t1, float(w)) - v[:, cols]).astype(MXU_DTYPE)
            O[0][:, cols] = d
            O[1][:, cols] = (_dot(d, C[0][g]) * C[1][:, cols]).astype(MXU_DTYPE)

    return _rows(body, S, T, rows=[(pqkv, POOL_W, 0)], prevs=[(pqkv, POOL_W, 0)], consts=[mix, scale],
                 out_rows=[(POOL_W, MXU_DTYPE), (POOL_W, MXU_DTYPE)],
                 scratch=[pltpu.VMEM((HALO + T, POOL_W), F32)], name=name)


def _pool_bwd_a(dypm, d, mix, scale, name):
    S = d.shape[0]
    T = _tile(S, (256, 128))

    def body(i, n, R, C, P, N, O, A, Sc):
        for g in range(len(POOL_WINDOWS)):
            cols = slice(g * LANE, (g + 1) * LANE)
            dg = R[1][:, cols]
            dy = R[0][:, cols]
            yg = _dot(dg, C[0][g])
            A[0][:, cols] += jnp.sum(dy * yg, axis=0, keepdims=True)
            dys = dy * C[1][:, cols]
            A[1][cols, :] += _dot(dg.astype(F32).T, dys)
            O[0][:, cols] = _dot(dys, C[0][g], NT)

    return _rows(body, S, T, rows=[dypm, d], consts=[mix, scale], out_rows=[(POOL_W, F32)],
                 out_accs=[((1, POOL_W), F32), ((POOL_W, LANE), F32)], name=name)


def _pool_bwd_b(dd, name):
    S = dd.shape[0]
    T = _tile(S, (256, 128))

    def body(i, n, R, C, P, N, O, A, Sc):
        ext = Sc[0]
        ddv = R[0][...]
        t1 = (_row_ids(i, T) + 1).astype(F32)
        nxt = jnp.where(i < n - 1, N[0][...], 0.0)
        for g, w in enumerate(POOL_WINDOWS):
            cols = slice(g * LANE, (g + 1) * LANE)
            ext[0:T, cols] = ddv[:, cols] / jnp.minimum(t1, float(w))
            ext[T:, cols] = nxt[:, cols] / float(w)
        for g, w in enumerate(POOL_WINDOWS):
            cols = slice(g * LANE, (g + 1) * LANE)
            acc = ext[0:T, cols]
            for j in range(1, w):
                acc = acc + ext[pl.ds(j, T), cols]
            O[0][:, cols] = (acc - ddv[:, cols]).astype(MXU_DTYPE)

    return _rows(body, S, T, rows=[dd], nexts=[dd], out_rows=[(POOL_W, MXU_DTYPE)],
                 scratch=[pltpu.VMEM((T + HALO, POOL_W), F32)], name=name)[0]


def _lane_cumsum(seg, reverse=False):
    lane = lax.broadcasted_iota(jnp.int32, seg.shape, 1)
    sh = 1
    while sh < LANE:
        if reverse:
            seg = seg + jnp.where(lane < LANE - sh, pltpu.roll(seg, LANE - sh, axis=1), 0.0)
        else:
            seg = seg + jnp.where(lane >= sh, pltpu.roll(seg, sh, axis=1), 0.0)
        sh *= 2
    return seg


def _logf_cumsum(fT, bias, name):
    H, S = fT.shape
    TB = _tile(S)
    nb = S // TB

    def body(f_ref, b_ref, o_ref, carry):
        @pl.when(pl.program_id(0) == 0)
        def _():
            carry[...] = jnp.zeros_like(carry)

        x = f_ref[...] + b_ref[...]
        lf = jnp.minimum(x, 0.0) - jnp.log1p(jnp.exp(-jnp.abs(x)))
        c = carry[...]
        for j in range(TB // LANE):
            seg = _lane_cumsum(lf[:, j * LANE:(j + 1) * LANE]) + c
            o_ref[:, j * LANE:(j + 1) * LANE] = seg
            c = seg[:, LANE - 1:LANE]
        carry[...] = c

    return pl.pallas_call(
        body, grid=(nb,), in_specs=[pl.BlockSpec((H, TB), lambda i: (0, i)), pl.BlockSpec((H, 1), lambda i: (0, 0))],
        out_specs=pl.BlockSpec((H, TB), lambda i: (0, i)), out_shape=jax.ShapeDtypeStruct((H, S), F32),
        scratch_shapes=[pltpu.VMEM((H, 1), F32)], compiler_params=_params(("arbitrary",)), name=name)(fT, bias)


def _logf_cumsum_bwd(fT, bias, dc, dcq, name):
    H, S = fT.shape
    TB = _tile(S)
    nb = S // TB

    def body(f_ref, b_ref, dc_ref, dcq_ref, o_ref, db_ref, carry):
        @pl.when(pl.program_id(0) == 0)
        def _():
            carry[...] = jnp.zeros_like(carry)
            db_ref[...] = jnp.zeros_like(db_ref)

        x = f_ref[...] + b_ref[...]
        sg = _sigmoid(-x)
        dcv = dc_ref[...] + dcq_ref[...]
        c = carry[...]
        db = jnp.zeros((H, 1), F32)
        for j in reversed(range(TB // LANE)):
            seg = _lane_cumsum(dcv[:, j * LANE:(j + 1) * LANE], reverse=True) + c
            df = seg * sg[:, j * LANE:(j + 1) * LANE]
            o_ref[:, j * LANE:(j + 1) * LANE] = df
            db = db + jnp.sum(df, axis=1, keepdims=True)
            c = seg[:, 0:1]
        carry[...] = c
        db_ref[...] += db

    rev = lambda i: (0, nb - 1 - i)
    return pl.pallas_call(
        body, grid=(nb,),
        in_specs=[pl.BlockSpec((H, TB), rev), pl.BlockSpec((H, 1), lambda i: (0, 0)), pl.BlockSpec((H, TB), rev),
                  pl.BlockSpec((H, TB), rev)],
        out_specs=[pl.BlockSpec((H, TB), rev), pl.BlockSpec((H, 1), lambda i: (0, 0))],
        out_shape=[jax.ShapeDtypeStruct((H, S), F32), jax.ShapeDtypeStruct((H, 1), F32)],
        scratch_shapes=[pltpu.VMEM((H, 1), F32)], compiler_params=_params(("arbitrary",)), name=name)(
            fT, bias, dc, dcq)


def _attn_scores(q, k, ck, diagonal, T):
    s = _dot(q, k, NT) - ck
    if diagonal:
        tril = lax.broadcasted_iota(jnp.int32, (T, T), 1) <= lax.broadcasted_iota(jnp.int32, (T, T), 0)
        s = jnp.where(tril, s, NEG_BIG)
    return s


def _attn_fwd(q, k, v, c_row, name):
    H, S, Dh = q.shape
    T = _tile(S)
    nq = S // T

    def body(q_ref, k_ref, v_ref, ck_ref, o_ref, lse_ref, m_s, l_s, acc_s):
        qi, ki = pl.program_id(1), pl.program_id(2)

        @pl.when(ki == 0)
        def _():
            m_s[...] = jnp.full_like(m_s, NEG_BIG)
            l_s[...] = jnp.zeros_like(l_s)
            acc_s[...] = jnp.zeros_like(acc_s)

        def step(diagonal):
            s = _attn_scores(q_ref[0], k_ref[0], ck_ref[0], diagonal, T)
            m_new = jnp.maximum(m_s[...], jnp.max(s, axis=1, keepdims=True))
            alpha = jnp.exp(m_s[...] - m_new)
            p = jnp.exp(s - m_new)
            l_s[...] = alpha * l_s[...] + jnp.sum(p, axis=1, keepdims=True)
            acc_s[...] = alpha * acc_s[...] + _dot(p, v_ref[0])
            m_s[...] = m_new

        @pl.when(ki < qi)
        def _():
            step(False)

        @pl.when(ki == qi)
        def _():
            step(True)
            o_ref[0] = acc_s[...] / l_s[...]
            lse_ref[0] = m_s[...] + jnp.log(l_s[...])

    qmap = lambda h, qi, ki: (h, qi, 0)
    kmap = lambda h, qi, ki: (h, jnp.minimum(ki, qi), 0)
    return pl.pallas_call(
        body, grid=(H, nq, nq),
        in_specs=[pl.BlockSpec((1, T, Dh), qmap), pl.BlockSpec((1, T, Dh), kmap), pl.BlockSpec((1, T, Dh), kmap),
                  pl.BlockSpec((1, 1, T), lambda h, qi, ki: (h, 0, jnp.minimum(ki, qi)))],
        out_specs=[pl.BlockSpec((1, T, Dh), qmap), pl.BlockSpec((1, T, 1), qmap)],
        out_shape=[jax.ShapeDtypeStruct((H, S, Dh), F32), jax.ShapeDtypeStruct((H, S, 1), F32)],
        scratch_shapes=[pltpu.VMEM((T, 1), F32), pltpu.VMEM((T, 1), F32), pltpu.VMEM((T, Dh), F32)],
        compiler_params=_params(("parallel", "parallel", "arbitrary")), name=name)(q, k, v, c_row)


def _attn_bwd(q, k, v, do, o, c_row, lse, name):
    H, S, Dh = q.shape
    T = _tile(S)
    nq = S // T

    def body(q_ref, k_ref, v_ref, do_ref, o_ref, ck_ref, lse_ref, dq_ref, dk_ref, dv_ref, dc_ref, dcq_ref,
             dk_s, dv_s, dc_s):
        ki, qi = pl.program_id(1), pl.program_id(2)

        @pl.when((ki == 0) & (qi == 0))
        def _():
            dq_ref[...] = jnp.zeros_like(dq_ref)
            dcq_ref[...] = jnp.zeros_like(dcq_ref)

        @pl.when(qi == 0)
        def _():
            dk_s[...] = jnp.zeros_like(dk_s)
            dv_s[...] = jnp.zeros_like(dv_s)
            dc_s[...] = jnp.zeros_like(dc_s)

        def step(diagonal):
            qv, kv, vv = q_ref[0], k_ref[0], v_ref[0]
            s = _attn_scores(qv, kv, ck_ref[0], diagonal, T)
            p = jnp.exp(s - lse_ref[0])
            dov = do_ref[0]
            delta = jnp.sum(dov * o_ref[0], axis=1, keepdims=True)
            dv_s[...] += _dot(p.T, dov)
            dp = _dot(dov, vv, NT)
            ds = p * (dp - delta)
            dc_s[...] -= jnp.sum(ds, axis=0, keepdims=True)
            rows = pl.ds(pl.multiple_of(qi * T, T), T)
            dq_ref[0, rows, :] += _dot(ds, kv)
            dcq_ref[0, rows, :] += jnp.sum(ds, axis=1, keepdims=True)
            dk_s[...] += _dot(ds.T, qv)

        @pl.when(qi > ki)
        def _():
            step(False)

        @pl.when(qi == ki)
        def _():
            step(True)

        @pl.when(qi == nq - 1)
        def _():
            dk_ref[0] = dk_s[...]
            dv_ref[0] = dv_s[...]
            dc_ref[0] = dc_s[...]

    qmap = lambda h, ki, qi: (h, jnp.maximum(qi, ki), 0)
    kmap = lambda h, ki, qi: (h, ki, 0)
    return pl.pallas_call(
        body, grid=(H, nq, nq),
        in_specs=[pl.BlockSpec((1, T, Dh), qmap), pl.BlockSpec((1, T, Dh), kmap), pl.BlockSpec((1, T, Dh), kmap),
                  pl.BlockSpec((1, T, Dh), qmap), pl.BlockSpec((1, T, Dh), qmap),
                  pl.BlockSpec((1, 1, T), lambda h, ki, qi: (h, 0, ki)), pl.BlockSpec((1, T, 1), qmap)],
        out_specs=[pl.BlockSpec((1, S, Dh), lambda h, ki, qi: (h, 0, 0)), pl.BlockSpec((1, T, Dh), kmap),
                   pl.BlockSpec((1, T, Dh), kmap), pl.BlockSpec((1, 1, T), lambda h, ki, qi: (h, 0, ki)),
                   pl.BlockSpec((1, S, 1), lambda h, ki, qi: (h, 0, 0))],
        out_shape=[jax.ShapeDtypeStruct((H, S, Dh), F32), jax.ShapeDtypeStruct((H, S, Dh), F32),
                   jax.ShapeDtypeStruct((H, S, Dh), F32), jax.ShapeDtypeStruct((H, 1, S), F32),
                   jax.ShapeDtypeStruct((H, S, 1), F32)],
        scratch_shapes=[pltpu.VMEM((T, Dh), F32), pltpu.VMEM((T, Dh), F32), pltpu.VMEM((1, T), F32)],
        compiler_params=_params(("parallel", "arbitrary", "arbitrary")), name=name)(
            q, k, v, do, o, c_row, lse)


def _conv_taps(ext, w_ref, K, T):
    out = None
    for k in range(K):
        term = ext[pl.ds(HALO - (K - 1 - k), T), :] * w_ref[k:k + 1, :]
        out = term if out is None else out + term
    return out


def _conv_bwd_b(dpre, w, name):
    S, C = dpre.shape
    K = w.shape[0]
    T = _tile(S, (128,))

    def body(i, n, R, Cs, P, N, O, A, Sc):
        ext = Sc[0]
        _fill_next(ext, R[0][...], jnp.where(i < n - 1, N[0][...], 0.0), T)
        out = None
        for k in range(K):
            term = ext[pl.ds(K - 1 - k, T), :] * Cs[0][k:k + 1, :]
            out = term if out is None else out + term
        O[0][...] = out.astype(MXU_DTYPE)

    return _rows(body, S, T, rows=[dpre], nexts=[dpre], consts=[w], out_rows=[(C, MXU_DTYPE)],
                 scratch=[pltpu.VMEM((T + HALO, C), F32)], name=name)[0]


def _dt_mask():
    lane = lax.broadcasted_iota(jnp.int32, (1, LANE), 1)
    return ((lane >= DT_LANE0) & (lane < DT_LANE0 + SSD_HEADS)).astype(F32)


def _ssd_pre_fwd(xbc, fdt, cw, cb, dtb, name):
    S, C = xbc.shape
    T = _tile(S, (256, 128))
    K = cw.shape[0]

    def body(i, n, R, Cs, P, N, O, A, Sc):
        ext = Sc[0]
        _fill_prev(ext, P[0], R[0][...], i)
        pre = _conv_taps(ext, Cs[0], K, T) + Cs[1][...]
        O[0][...] = pre * _sigmoid(pre)
        O[1][...] = _softplus(R[1][...] + Cs[2][...]) * _dt_mask()

    return _rows(body, S, T, rows=[xbc, fdt], prevs=[xbc], consts=[cw, cb, dtb],
                 out_rows=[(C, F32), (LANE, F32)], scratch=[pltpu.VMEM((HALO + T, C), F32)], name=name)


def _silu_grad(pre):
    sg = _sigmoid(pre)
    return sg * (1.0 + pre * (1.0 - sg))


def _conv_wgrad(ext, dpre, K, T):
    return jnp.concatenate([jnp.sum(dpre * ext[pl.ds(HALO - (K - 1 - k), T), :], axis=0, keepdims=True)
                            for k in range(K)], axis=0)


def _ssd_pre_bwd_a(xbc, fdt, dxa, ddtw, cw, cb, dtb, name):
    S, C = xbc.shape
    T = _tile(S, (256, 128))
    K = cw.shape[0]

    def body(i, n, R, Cs, P, N, O, A, Sc):
        ext = Sc[0]
        _fill_prev(ext, P[0], R[0][...], i)
        pre = _conv_taps(ext, Cs[0], K, T) + Cs[1][...]
        dpre = R[2][...] * _silu_grad(pre)
        O[0][...] = dpre
        A[0][...] += _conv_wgrad(ext, dpre, K, T)
        A[1][...] += jnp.sum(dpre, axis=0, keepdims=True)
        ddt = R[3][...] * _sigmoid(R[1][...] + Cs[2][...]) * _dt_mask()
        O[1][...] = ddt
        A[2][...] += jnp.sum(ddt, axis=0, keepdims=True)

    return _rows(body, S, T, rows=[xbc, fdt, dxa, ddtw], prevs=[xbc], consts=[cw, cb, dtb],
                 out_rows=[(C, F32), (LANE, F32)],
                 out_accs=[((K, C), F32), ((1, C), F32), ((1, LANE), F32)],
                 scratch=[pltpu.VMEM((HALO + T, C), F32)], name=name)


def _split3(x):
    hi = x.astype(jnp.bfloat16)
    r1 = x - hi.astype(F32)
    mid = r1.astype(jnp.bfloat16)
    lo = (r1 - mid.astype(F32)).astype(jnp.bfloat16)
    return hi, mid, lo


def _expand_mat():
    r = lax.broadcasted_iota(jnp.int32, (LANE, SSD_W), 0)
    c = lax.broadcasted_iota(jnp.int32, (LANE, SSD_W), 1)
    return (r - DT_LANE0 == c // SSD_P).astype(jnp.bfloat16)


def _headsum_mat():
    r = lax.broadcasted_iota(jnp.int32, (SSD_W, LANE), 0)
    c = lax.broadcasted_iota(jnp.int32, (SSD_W, LANE), 1)
    return (c - DT_LANE0 == r // SSD_P).astype(jnp.bfloat16)


def _expand(tile, ex):
    return sum(lax.dot_general(part, ex, (((1,), (0,)), ((), ())), preferred_element_type=F32)
               for part in _split3(tile))


def _headsum(full, hs):
    return sum(lax.dot_general(part, hs, (((1,), (0,)), ((), ())), preferred_element_type=F32)
               for part in _split3(full))


def _sub_cumsum(a, reverse=False):
    n = a.shape[0]
    row = lax.broadcasted_iota(jnp.int32, a.shape, 0)
    sh = 1
    while sh < n:
        if reverse:
            a = a + jnp.where(row < n - sh, pltpu.roll(a, n - sh, axis=0), 0.0)
        else:
            a = a + jnp.where(row >= sh, pltpu.roll(a, sh, axis=0), 0.0)
        sh *= 2
    return a


def _chunk_common(xa_ref, dtw_ref, a_row, ex):
    L = SSD_CHUNK
    xs = xa_ref[:, 0:SSD_W]
    dtv = dtw_ref[...]
    acs = _sub_cumsum(dtv * a_row)
    last = acs[L - 1:L, :]
    dt_full = _expand(dtv, ex)
    xd = xs * dt_full
    dec_full = _expand(jnp.exp(last - acs), ex)
    e_full = _expand(jnp.exp(acs), ex)
    elast_full = _expand(jnp.broadcast_to(jnp.exp(last), (8, LANE)), ex)[0:1, :]
    return xs, dtv, acs, last, dt_full, xd, dec_full, e_full, elast_full


def _decay_mask(acs, acsT, col):
    L = SSD_CHUNK
    diff = acs[:, col:col + 1] - acsT[col:col + 1, :]
    tril = lax.broadcasted_iota(jnp.int32, (L, L), 0) >= lax.broadcasted_iota(jnp.int32, (L, L), 1)
    return jnp.where(tril, jnp.exp(jnp.minimum(diff, 0.0)), 0.0)


def _half_mask(h):
    lane = lax.broadcasted_iota(jnp.int32, (1, LANE), 1)
    return ((lane // SSD_P) == (h % 2)).astype(F32)


def _ssd_chunk_fwd(xa, dtw, a_row, d_full, name):
    S = xa.shape[0]
    L, G = SSD_CHUNK, 2
    nc = S // L
    GW = SSD_W // G

    def body(xa_ref, dtw_ref, a_ref, d_ref, y_ref, hp_ref, state):
        @pl.when(pl.program_id(0) == 0)
        def _():
            state[...] = jnp.zeros_like(state)

        ex = _expand_mat()
        xs, dtv, acs, last, dt_full, xd, dec_full, e_full, elast_full = _chunk_common(xa_ref, dtw_ref, a_ref[...], ex)
        acsT = acs.T
        hp_ref[0] = state[...]
        for g in range(G):
            gc = slice(g * GW, (g + 1) * GW)
            Bg = xa_ref[:, SSD_W + g * SSD_N: SSD_W + (g + 1) * SSD_N]
            Cg = xa_ref[:, SSD_W + G * SSD_N + g * SSD_N: SSD_W + G * SSD_N + (g + 1) * SSD_N]
            cb = _dot(Cg, Bg, NT)
            y_off = e_full[:, gc] * _dot(Cg, state[:, gc])
            for hp in range(GW // LANE):
                pc = slice(g * GW + hp * LANE, g * GW + (hp + 1) * LANE)
                xd_pair = xd[:, pc]
                yp = y_off[:, hp * LANE:(hp + 1) * LANE] + d_ref[:, pc] * xs[:, pc]
                for h2 in range(2):
                    h = (g * GW + hp * LANE) // SSD_P + h2
                    m = cb * _decay_mask(acs, acsT, DT_LANE0 + h)
                    yp = yp + _dot(m, xd_pair * _half_mask(h))
                y_ref[:, pc] = yp
            st_new = _dot(Bg.T, xd[:, gc] * dec_full[:, gc])
            state[:, gc] = elast_full[:, gc] * state[:, gc] + st_new

    return pl.pallas_call(
        body, grid=(nc,),
        in_specs=[pl.BlockSpec((L, SSD_CONV_CH), lambda c: (c, 0)), pl.BlockSpec((L, LANE), lambda c: (c, 0)),
                  pl.BlockSpec((1, LANE), lambda c: (0, 0)), pl.BlockSpec((1, SSD_W), lambda c: (0, 0))],
        out_specs=[pl.BlockSpec((L, SSD_W), lambda c: (c, 0)), pl.BlockSpec((1, SSD_N, SSD_W), lambda c: (c, 0, 0))],
        out_shape=[jax.ShapeDtypeStruct((S, SSD_W), F32), jax.ShapeDtypeStruct((nc, SSD_N, SSD_W), F32)],
        scratch_shapes=[pltpu.VMEM((SSD_N, SSD_W), F32)],
        compiler_params=_params(("arbitrary",)), name=name)(xa, dtw, a_row, d_full)


def _ssd_chunk_bwd(xa, dtw, dy, hprev, a_row, d_full, name):
    S = xa.shape[0]
    L, G = SSD_CHUNK, 2
    nc = S // L
    GW = SSD_W // G

    def body(xa_ref, dtw_ref, dy_ref, hp_ref, a_ref, d_ref, dxa_ref, ddt_ref, da_ref, dd_ref, dstate):
        @pl.when(pl.program_id(0) == 0)
        def _():
            dstate[...] = jnp.zeros_like(dstate)
            da_ref[...] = jnp.zeros_like(da_ref)
            dd_ref[...] = jnp.zeros_like(dd_ref)

        ex, hs = _expand_mat(), _headsum_mat()
        a_row = a_ref[...]
        xs, dtv, acs, last, dt_full, xd, dec_full, e_full, elast_full = _chunk_common(xa_ref, dtw_ref, a_row, ex)
        acsT = acs.T
        dyv = dy_ref[...]
        lane = lax.broadcasted_iota(jnp.int32, (L, LANE), 1)
        sub = lax.broadcasted_iota(jnp.int32, (LANE, L), 0)
        dacs_c = jnp.zeros((L, LANE), F32)
        dacs_r = jnp.zeros((LANE, L), F32)
        dd_ref[...] += jnp.sum(_headsum(dyv * xs, hs), axis=0, keepdims=True)
        dxd_parts, yoff_parts, dxdd_parts, hh_parts = [], [], [], []
        for g in range(G):
            gc = slice(g * GW, (g + 1) * GW)
            b0 = SSD_W + g * SSD_N
            c0 = SSD_W + G * SSD_N + g * SSD_N
            Bg = xa_ref[:, b0:b0 + SSD_N]
            Cg = xa_ref[:, c0:c0 + SSD_N]
            Hp = hp_ref[0, :, gc]
            dH = dstate[:, gc]
            cb = _dot(Cg, Bg, NT)
            Gm = _dot(Cg, Hp)
            yoff_parts.append(e_full[:, gc] * Gm)
            dG = e_full[:, gc] * dyv[:, gc]
            dC = _dot(dG, Hp, NT)
            dHp = _dot(Cg.T, dG)
            xdd = xd[:, gc] * dec_full[:, gc]
            dB = _dot(xdd, dH, NT)
            dxdd = _dot(Bg, dH)
            dxdd_parts.append(dxdd)
            hh_parts.append(dH * Hp)
            dstate[:, gc] = dHp + elast_full[:, gc] * dH
            dcb = jnp.zeros((L, L), F32)
            dxd_g = []
            for hp in range(GW // LANE):
                pc = slice(g * GW + hp * LANE, g * GW + (hp + 1) * LANE)
                xd_pair = xd[:, pc]
                dxd_pair = dxdd[:, hp * LANE:(hp + 1) * LANE] * dec_full[:, pc]
                for h2 in range(2):
                    h = (g * GW + hp * LANE) // SSD_P + h2
                    col = DT_LANE0 + h
                    lm = _decay_mask(acs, acsT, col)
                    m = cb * lm
                    dy_h = dyv[:, pc] * _half_mask(h)
                    dm = _dot(dy_h, xd_pair, NT)
                    dxd_pair = dxd_pair + _dot(m.T, dy_h)
                    wm = dm * m
                    dacs_c = dacs_c + jnp.where(lane == col, jnp.sum(wm, axis=1, keepdims=True), 0.0)
                    dacs_r = dacs_r - jnp.where(sub == col, jnp.sum(wm, axis=0, keepdims=True), 0.0)
                    dcb = dcb + dm * lm
                dxd_g.append(dxd_pair)
            dxd_parts.append(jnp.concatenate(dxd_g, axis=1))
            dxa_ref[:, c0:c0 + SSD_N] = dC + _dot(dcb, Bg)
            dxa_ref[:, b0:b0 + SSD_N] = dB + _dot(dcb.T, Cg)
        dxd = jnp.concatenate(dxd_parts, axis=1)
        y_off = jnp.concatenate(yoff_parts, axis=1)
        dxdd_full = jnp.concatenate(dxdd_parts, axis=1)
        hh = jnp.concatenate(hh_parts, axis=1)
        dxa_ref[:, 0:SSD_W] = d_ref[...] * dyv + dxd * dt_full
        ddt = _headsum(dxd * xs, hs)
        w_dec = _headsum(dxdd_full * xd, hs) * jnp.exp(last - acs)
        dlast = jnp.sum(w_dec, axis=0, keepdims=True) + jnp.exp(last) * jnp.sum(_headsum(hh, hs), axis=0, keepdims=True)
        dacs = dacs_c + dacs_r.T + _headsum(dyv * y_off, hs) - w_dec
        rowid = lax.broadcasted_iota(jnp.int32, (L, LANE), 0)
        dacs = dacs + jnp.where(rowid == L - 1, dlast, 0.0)
        da = _sub_cumsum(dacs, reverse=True)
        ddt_ref[...] = ddt + da * a_row
        da_ref[...] += jnp.sum(da * dtv, axis=0, keepdims=True)

    rev = lambda c: (nc - 1 - c, 0)
    return pl.pallas_call(
        body, grid=(nc,),
        in_specs=[pl.BlockSpec((L, SSD_CONV_CH), rev), pl.BlockSpec((L, LANE), rev), pl.BlockSpec((L, SSD_W), rev),
                  pl.BlockSpec((1, SSD_N, SSD_W), lambda c: (nc - 1 - c, 0, 0)),
                  pl.BlockSpec((1, LANE), lambda c: (0, 0)), pl.BlockSpec((1, SSD_W), lambda c: (0, 0))],
        out_specs=[pl.BlockSpec((L, SSD_CONV_CH), rev), pl.BlockSpec((L, LANE), rev),
                   pl.BlockSpec((1, LANE), lambda c: (0, 0)), pl.BlockSpec((1, LANE), lambda c: (0, 0))],
        out_shape=[jax.ShapeDtypeStruct((S, SSD_CONV_CH), F32), jax.ShapeDtypeStruct((S, LANE), F32),
                   jax.ShapeDtypeStruct((1, LANE), F32), jax.ShapeDtypeStruct((1, LANE), F32)],
        scratch_shapes=[pltpu.VMEM((SSD_N, SSD_W), F32)],
        compiler_params=_params(("arbitrary",)), name=name)(xa, dtw, dy, hprev, a_row, d_full)


def _ssd_post_fwd(y, z, w, name):
    S = y.shape[0]
    GW = SSD_W // 2

    def body(i, n, R, C, P, N, O, A, Sc):
        zv = R[1][...]
        v = R[0][...] * (zv * _sigmoid(zv))
        for g in range(2):
            gc = slice(g * GW, (g + 1) * GW)
            vg = v[:, gc]
            r = lax.rsqrt(jnp.mean(vg * vg, axis=-1, keepdims=True) + NORM_EPS)
            O[0][:, gc] = (vg * r * C[0][:, gc]).astype(MXU_DTYPE)

    return _rows(body, S, _tile(S, (256, 128)), rows=[y, z], consts=[w], out_rows=[(SSD_W, MXU_DTYPE)], name=name)[0]


def _ssd_post_bwd(y, z, dyn, w, name):
    S = y.shape[0]
    GW = SSD_W // 2

    def body(i, n, R, C, P, N, O, A, Sc):
        yv, zv, dn = R[0][...], R[1][...], R[2][...]
        sz = zv * _sigmoid(zv)
        v = yv * sz
        for g in range(2):
            gc = slice(g * GW, (g + 1) * GW)
            dv, dw = _norm_bwd_math(v[:, gc], C[0][:, gc], dn[:, gc])
            A[0][:, gc] += dw
            O[0][:, gc] = dv * sz[:, gc]
            O[1][:, gc] = (dv * yv[:, gc] * _silu_grad(zv[:, gc])).astype(MXU_DTYPE)

    return _rows(body, S, _tile(S, (256, 128)), rows=[y, z, dyn], consts=[w],
                 out_rows=[(SSD_W, F32), (SSD_W, MXU_DTYPE)], out_accs=[((1, SSD_W), F32)], name=name)


def _merge_fwd(gl, yp, ya, ys, name):
    S, D = yp.shape

    def body(i, n, R, C, P, N, O, A, Sc):
        acc = None
        for b in range(3):
            term = _sigmoid(R[0][:, b * D:(b + 1) * D]) * R[1 + b][...]
            acc = term if acc is None else acc + term
        O[0][...] = acc.astype(MXU_DTYPE)

    return _rows(body, S, _tile(S, (256, 128)), rows=[gl, yp, ya, ys], out_rows=[(D, MXU_DTYPE)], name=name)[0]


def _merge_bwd(gl, yp, ya, ys, dm, name):
    S, D = yp.shape

    def body(i, n, R, C, P, N, O, A, Sc):
        dmv = R[4][...]
        for b in range(3):
            gt = _sigmoid(R[0][:, b * D:(b + 1) * D])
            O[b][...] = (gt * dmv).astype(MXU_DTYPE)
            O[3][:, b * D:(b + 1) * D] = (dmv * R[1 + b][...] * gt * (1.0 - gt)).astype(MXU_DTYPE)

    return _rows(body, S, _tile(S, (256, 128)), rows=[gl, yp, ya, ys, dm],
                 out_rows=[(D, MXU_DTYPE)] * 3 + [(3 * D, MXU_DTYPE)], name=name)


def _ffn_act_fwd(hpre, cw, cb, name):
    S, C = hpre.shape
    K = cw.shape[0]
    T = _tile(S, (128,))
    Fd = C // 2

    def body(i, n, R, Cs, P, N, O, A, Sc):
        ext = Sc[0]
        _fill_prev(ext, P[0], R[0][...], i)
        hc = _conv_taps(ext, Cs[0], K, T) + Cs[1][...]
        gt = hc[:, :Fd]
        O[0][...] = (gt * _sigmoid(gt) * hc[:, Fd:]).astype(MXU_DTYPE)

    return _rows(body, S, T, rows=[hpre], prevs=[hpre], consts=[cw, cb], out_rows=[(Fd, MXU_DTYPE)],
                 scratch=[pltpu.VMEM((HALO + T, C), F32)], name=name)[0]


def _ffn_act_bwd_a(hpre, dact, cw, cb, name):
    S, C = hpre.shape
    K = cw.shape[0]
    T = _tile(S, (128,))
    Fd = C // 2

    def body(i, n, R, Cs, P, N, O, A, Sc):
        ext = Sc[0]
        _fill_prev(ext, P[0], R[0][...], i)
        hc = _conv_taps(ext, Cs[0], K, T) + Cs[1][...]
        gt, val, da = hc[:, :Fd], hc[:, Fd:], R[1][...]
        dhc = jnp.concatenate([da * val * _silu_grad(gt), da * gt * _sigmoid(gt)], axis=1)
        O[0][...] = dhc
        A[0][...] += _conv_wgrad(ext, dhc, K, T)
        A[1][...] += jnp.sum(dhc, axis=0, keepdims=True)

    return _rows(body, S, T, rows=[hpre, dact], prevs=[hpre], consts=[cw, cb], out_rows=[(C, F32)],
                 out_accs=[((K, C), F32), ((1, C), F32)], scratch=[pltpu.VMEM((HALO + T, C), F32)], name=name)


def _adamw(g, w, m, v, name):
    R_, W = g.shape
    c1 = 1.0 - ADAM_B1 ** ADAM_STEP
    c2 = 1.0 - ADAM_B2 ** ADAM_STEP

    def body(i, n, R, C, P, N, O, A, Sc):
        gv = R[0][...]
        mn = ADAM_B1 * R[2][...] + (1.0 - ADAM_B1) * gv
        vn = ADAM_B2 * R[3][...] + (1.0 - ADAM_B2) * (gv * gv)
        O[0][...] = -ADAM_LR * ((mn / c1) / (jnp.sqrt(vn / c2) + ADAM_EPS) + ADAM_WD * R[1][...])
        O[1][...] = mn
        O[2][...] = vn

    return _rows(body, R_, _row_tile(R_, W, 7), rows=[g, w, m, v], out_rows=[(W, F32)] * 3, name=name)


def _row_tile(rows, width, n_blocks, budget=14 * 1024 * 1024):
    wpad = -(-width // LANE) * LANE
    for t in (512, 256, 128, 64, 32, 16, 8):
        if rows % t == 0 and n_blocks * t * wpad * 4 <= budget:
            return t
    return rows


def _sum_leading(a, name):
    P_, R_, W = a.shape
    T = _row_tile(R_, W, P_ + 1)

    def body(a_ref, o_ref):
        acc = a_ref[0]
        for p in range(1, P_):
            acc = acc + a_ref[p]
        o_ref[...] = acc

    return pl.pallas_call(body, grid=(R_ // T,), in_specs=[pl.BlockSpec((P_, T, W), lambda i: (0, i, 0))],
                          out_specs=pl.BlockSpec((T, W), lambda i: (i, 0)),
                          out_shape=jax.ShapeDtypeStruct((R_, W), F32),
                          compiler_params=_params(("parallel",)), name=name)(a)


_ANY = pl.BlockSpec(memory_space=pl.ANY)
_MESH = pl.DeviceIdType.MESH


DMA_CHUNK_BYTES = 2 * 1024 * 1024


def _row_chunks(shape, dtype):
    r = shape[-2]
    total = 1
    for s in shape:
        total *= s
    want = max(1, (total * jnp.dtype(dtype).itemsize) // DMA_CHUNK_BYTES)
    n = 1
    while n * 2 <= want and r % (n * 2 * 16) == 0 and n < 8:
        n *= 2
    return [(j * (r // n), r // n) for j in range(n)]


def _comm_call(plan, srcs, out_shapes, name):
    n = len(srcs)
    probe = plan(0, 0, 0, [_ShapeOnly(s.shape) for s in srcs], [_ShapeOnly(s.shape) for s in out_shapes])
    n_local, n_remote = len(probe[0]), len(probe[1])

    def body(*refs):
        src_refs, out_refs = refs[:n], refs[n:2 * n]
        send_sems, recv_sems, local_sems = refs[2 * n:]
        x, y, c = lax.axis_index("x"), lax.axis_index("y"), lax.axis_index("c")
        local, remote = plan(x, y, c, src_refs, out_refs)
        started = []
        for j, (s, d) in enumerate(local):
            cp = pltpu.make_async_copy(s, d, local_sems.at[j])
            cp.start()
            started.append(cp)
        sent = []
        for j, (s, d, peer) in enumerate(remote):
            cp = pltpu.make_async_remote_copy(src_ref=s, dst_ref=d, send_sem=send_sems.at[j], recv_sem=recv_sems.at[j],
                                              device_id=peer, device_id_type=_MESH)
            cp.start()
            sent.append(cp)
        for cp in sent:
            cp.wait()
        for cp in started:
            cp.wait()

    return pl.pallas_call(
        body, in_specs=[_ANY] * n, out_specs=[_ANY] * n,
        out_shape=[jax.ShapeDtypeStruct(s.shape, s.dtype) for s in out_shapes],
        scratch_shapes=[pltpu.SemaphoreType.DMA((n_remote,)), pltpu.SemaphoreType.DMA((n_remote,)),
                        pltpu.SemaphoreType.DMA((n_local,))], name=name)(*srcs)


class _ShapeOnly:
    def __init__(self, shape):
        self.shape = tuple(shape)

    @property
    def at(self):
        return self

    def __getitem__(self, idx):
        return self


def _other_places(x, y):
    return [(1 - x, y), (x, 1 - y), (1 - x, 1 - y)]


def _gather_places(shards, row_major, name):
    outs = []
    for s, rm in zip(shards, row_major):
        L_, r, c_ = s.shape
        outs.append(jax.ShapeDtypeStruct((L_, N_PLACES, r, c_) if rm else (N_PLACES, L_, r, c_), s.dtype))

    def plan(x, y, c, src_refs, out_refs):
        me = 2 * x + y
        local, remote = [], []
        for s_ref, o_ref, rm, s in zip(src_refs, out_refs, row_major, shards):
            for r0, rn in _row_chunks(s.shape, s.dtype):
                src = s_ref.at[:, pl.ds(r0, rn), :]
                dst = o_ref.at[:, me, pl.ds(r0, rn), :] if rm else o_ref.at[me, :, pl.ds(r0, rn), :]
                local.append((src, dst))
                for px, py in _other_places(x, y):
                    remote.append((src, dst, (px, py, c)))
        return local, remote

    return _comm_call(plan, shards, outs, name)


def _reduce_sibling(gs, name):
    outs = [jax.ShapeDtypeStruct((2, N_PLACES) + g.shape[2:], g.dtype) for g in gs]

    def plan(x, y, c, src_refs, out_refs):
        local, remote = [], []
        for g_ref, o_ref, g in zip(src_refs, out_refs, gs):
            for r0, rn in _row_chunks(g.shape[2:], g.dtype):
                for p in range(N_PLACES):
                    dst = o_ref.at[c, p, pl.ds(r0, rn), :]
                    local.append((g_ref.at[p, c, pl.ds(r0, rn), :], dst))
                    remote.append((g_ref.at[p, 1 - c, pl.ds(r0, rn), :], dst, (x, y, 1 - c)))
        return local, remote

    return _comm_call(plan, gs, outs, name)


def _reduce_places(hs, name):
    def plan(x, y, c, src_refs, out_refs):
        me = 2 * x + y
        local, remote = [], []
        for h_ref, o_ref, h in zip(src_refs, out_refs, hs):
            for r0, rn in _row_chunks(h.shape[1:], h.dtype):
                dst = o_ref.at[me, pl.ds(r0, rn), :]
                local.append((h_ref.at[me, pl.ds(r0, rn), :], dst))
                for px, py in _other_places(x, y):
                    remote.append((h_ref.at[2 * px + py, pl.ds(r0, rn), :], dst, (px, py, c)))
        return local, remote

    return _comm_call(plan, hs, hs, name)


def _gather_sibling(qs, name):
    outs = [jax.ShapeDtypeStruct((2,) + q.shape, q.dtype) for q in qs]

    def plan(x, y, c, src_refs, out_refs):
        local, remote = [], []
        for q_ref, o_ref, q in zip(src_refs, out_refs, qs):
            for r0, rn in _row_chunks(q.shape, q.dtype):
                src, dst = q_ref.at[pl.ds(r0, rn), :], o_ref.at[c, pl.ds(r0, rn), :]
                local.append((src, dst))
                remote.append((src, dst, (x, y, 1 - c)))
        return local, remote

    return _comm_call(plan, qs, outs, name)


def _to_heads(a, dtype):
    S = a.shape[0]
    return a.reshape(S, HEADS, HEAD_DIM).transpose(1, 0, 2).astype(dtype)


def _from_heads(a):
    return a.transpose(1, 0, 2).reshape(a.shape[1], HEADS * HEAD_DIM)


def _lane_tile(vec16):
    return jnp.concatenate([jnp.zeros((DT_LANE0,), F32), vec16,
                            jnp.zeros((LANE - DT_LANE0 - SSD_HEADS,), F32)])[None]


def _layer_consts(W, l):
    return dict(
        norm_mix=W['norm_mix'][l][None], mix=W['pool_mix'][l].astype(MXU_DTYPE), scale=W['pool_scale'][l][None],
        f_bias=W['f_bias'][l][:, None], cw=W['ssd_conv_w'][l], cb=W['ssd_conv_b'][l][None],
        dtb=_lane_tile(W['ssd_dt_bias'][l]), a_row=_lane_tile(-jnp.exp(W['ssd_a_log'][l])),
        d_full=jnp.repeat(W['ssd_d'][l], SSD_P)[None], ssd_norm=W['ssd_norm'][l][None],
        norm_ffn=W['norm_ffn'][l][None], fcw=W['ffn_conv_w'][l], fcb=W['ffn_conv_b'][l][None])


def _layer_fwd(x, W, l):
    n = f"l{l}_"
    cs = _layer_consts(W, l)
    win = {k: v[l] for k, v in W['w_in'].items()}
    u = _norm_fwd(x, cs['norm_mix'], n + "norm_mix")
    pqkv = _mm(u, win['p'], name=n + "in_p")
    z = _mm(u, win['z'], name=n + "in_z")
    xbc = _mm(u, win['x'], name=n + "in_x")
    gl = _mm(u, win['g'], name=n + "in_g")
    fdt = _mm(u, win['f'], name=n + "in_f")
    d, ypm = _pool_fwd(pqkv, cs['mix'], cs['scale'], n + "pool")
    yp = _mm(ypm, W['p_pool'][l], name=n + "p_pool")
    fT = fdt[:, :HEADS].T
    c = _logf_cumsum(fT, cs['f_bias'], n + "logf")
    c_row = c[:, None, :]
    qh = _to_heads(pqkv[:, ATTN_W:2 * ATTN_W] * ATTN_SCALE, MXU_DTYPE)
    kh, vh = (_to_heads(pqkv[:, (2 + j) * ATTN_W:(3 + j) * ATTN_W], MXU_DTYPE) for j in range(2))
    oh, lse = _attn_fwd(qh, kh, vh, c_row, n + "attn")
    o = _from_heads(oh)
    ya = _mm(o, W['p_attn'][l], name=n + "p_attn")
    xa, dtw = _ssd_pre_fwd(xbc, fdt, cs['cw'], cs['cb'], cs['dtb'], n + "ssd_pre")
    y, hprev = _ssd_chunk_fwd(xa, dtw, cs['a_row'], cs['d_full'], n + "ssd_scan")
    yn = _ssd_post_fwd(y, z, cs['ssd_norm'], n + "ssd_post")
    ys = _mm(yn, W['p_ssd'][l], name=n + "p_ssd")
    merged = _merge_fwd(gl, yp, ya, ys, n + "merge")
    x1 = _mm(merged, W['w_out'][l], acc=x, name=n + "w_out")
    u2 = _norm_fwd(x1, cs['norm_ffn'], n + "norm_ffn")
    hpre = _mm(u2, W['ffn_up'][l], name=n + "ffn_up")
    act = _ffn_act_fwd(hpre, cs['fcw'], cs['fcb'], n + "ffn_act")
    x2 = _mm(act, W['ffn_down'][l], acc=x1, name=n + "ffn_down")
    saved = dict(x=x, u=u, pqkv=pqkv, z=z, xbc=xbc, gl=gl, fdt=fdt, d=d, ypm=ypm, yp=yp, fT=fT,
                 c_row=c_row, qh=qh, kh=kh, vh=vh, oh=oh, o=o, lse=lse, ya=ya, xa=xa, dtw=dtw, y=y, hprev=hprev,
                 yn=yn, ys=ys, merged=merged, x1=x1, u2=u2, hpre=hpre, act=act, win=win, cs=cs)
    return x2, saved


def _layer_bwd(dx2, sv, W, l):
    n = f"l{l}_b_"
    cs, win = sv['cs'], sv['win']
    g = {}
    dact = _mm(dx2, W['ffn_down'][l], tb=True, name=n + "ffn_down_dx")
    g['ffn_down'] = _mm(sv['act'], dx2, ta=True, name=n + "ffn_down_dw")
    dhc, g['ffn_conv_w'], dfcb = _ffn_act_bwd_a(sv['hpre'], dact, cs['fcw'], cs['fcb'], n + "ffn_act_a")
    g['ffn_conv_b'] = dfcb[0]
    dhpre = _conv_bwd_b(dhc, cs['fcw'], n + "ffn_act_b")
    du2 = _mm(dhpre, W['ffn_up'][l], tb=True, name=n + "ffn_up_dx")
    g['ffn_up'] = _mm(sv['u2'], dhpre, ta=True, name=n + "ffn_up_dw")
    dx1, dnf = _norm_bwd(sv['x1'], cs['norm_ffn'], du2, dx2, n + "norm_ffn")
    g['norm_ffn'] = dnf[0]
    dm = _mm(dx1, W['w_out'][l], tb=True, name=n + "w_out_dx")
    g['w_out'] = _mm(sv['merged'], dx1, ta=True, name=n + "w_out_dw")
    dyp, dya, dys, dgl = _merge_bwd(sv['gl'], sv['yp'], sv['ya'], sv['ys'], dm, n + "merge")
    dypm = _mm(dyp, W['p_pool'][l], tb=True, name=n + "p_pool_dx")
    g['p_pool'] = _mm(sv['ypm'], dyp, ta=True, name=n + "p_pool_dw")
    dd, dscale, dmix = _pool_bwd_a(dypm, sv['d'], cs['mix'], cs['scale'], n + "pool_a")
    g['pool_scale'] = dscale[0]
    g['pool_mix'] = dmix.reshape(len(POOL_WINDOWS), LANE, LANE)
    dpool_v = _pool_bwd_b(dd, n + "pool_b")
    do = _mm(dya, W['p_attn'][l], tb=True, name=n + "p_attn_dx")
    g['p_attn'] = _mm(sv['o'], dya, ta=True, name=n + "p_attn_dw")
    dqh, dkh, dvh, dc, dcq = _attn_bwd(sv['qh'], sv['kh'], sv['vh'], _to_heads(do, F32), sv['oh'],
                                       sv['c_row'], sv['lse'], n + "attn")
    dfT, dfb = _logf_cumsum_bwd(sv['fT'], cs['f_bias'], dc[:, 0, :], dcq[:, :, 0], n + "logf")
    g['f_bias'] = dfb[:, 0]
    dpqkv = jnp.concatenate([dpool_v, (_from_heads(dqh) * ATTN_SCALE).astype(MXU_DTYPE)]
                            + [_from_heads(t).astype(MXU_DTYPE) for t in (dkh, dvh)], axis=1)
    dyn = _mm(dys, W['p_ssd'][l], tb=True, name=n + "p_ssd_dx")
    g['p_ssd'] = _mm(sv['yn'], dys, ta=True, name=n + "p_ssd_dw")
    dy, dz, dsn = _ssd_post_bwd(sv['y'], sv['z'], dyn, cs['ssd_norm'], n + "ssd_post")
    g['ssd_norm'] = dsn[0]
    dxa, ddtw, dA, dD = _ssd_chunk_bwd(sv['xa'], sv['dtw'], dy, sv['hprev'], cs['a_row'], cs['d_full'], n + "ssd_scan")
    heads = slice(DT_LANE0, DT_LANE0 + SSD_HEADS)
    g['ssd_a_log'] = dA[0, heads] * cs['a_row'][0, heads]
    g['ssd_d'] = dD[0, heads]
    dpre, ddt_raw, g['ssd_conv_w'], dcb, ddtb = _ssd_pre_bwd_a(sv['xbc'], sv['fdt'], dxa, ddtw, cs['cw'], cs['cb'],
                                                              cs['dtb'], n + "ssd_pre_a")
    g['ssd_conv_b'] = dcb[0]
    g['ssd_dt_bias'] = ddtb[0, heads]
    dxbc = _conv_bwd_b(dpre, cs['cw'], n + "ssd_pre_b")
    dfdt = jnp.concatenate([dfT.T, ddt_raw[:, HEADS:]], axis=1).astype(MXU_DTYPE)
    dsegs = dict(p=dpqkv, z=dz, x=dxbc, g=dgl, f=dfdt)
    du, dwin = None, {}
    for key in ('p', 'z', 'x', 'g', 'f'):
        du = _mm(dsegs[key], win[key], tb=True, acc=du, name=n + "in_dx_" + key)
        dwin[key] = _mm(sv['u'], dsegs[key], ta=True, name=n + "in_dw_" + key)
    g['w_in'] = dwin
    dx, dnm = _norm_bwd(sv['x'], cs['norm_mix'], du, dx1, n + "norm_mix")
    g['norm_mix'] = dnm[0]
    return dx, g


def _local_step(x, target, W):
    depth = W['norm_mix'].shape[0]
    saved = []
    h = x
    for l in range(depth):
        h, sv = _layer_fwd(h, W, l)
        saved.append(sv)
    dx, dwf, loss = _loss_head(h, W['norm_final'][None], target, "loss_head")
    grads = [None] * depth
    for l in reversed(range(depth)):
        dx, grads[l] = _layer_bwd(dx, saved[l], W, l)
    return loss[0, 0], dx, grads, dwf[0]


def _pack_rows(parts, row_align=1):
    flat = jnp.concatenate([p.reshape(-1) for p in parts])
    n = flat.shape[0]
    total = -(-n // (PACK_W * row_align)) * PACK_W * row_align
    if total > n:
        flat = jnp.concatenate([flat, jnp.zeros((total - n,), flat.dtype)])
    return flat.reshape(-1, PACK_W)


def _unpack_rows(buf, shapes):
    flat = buf.reshape(-1)
    out, pos = [], 0
    for shp in shapes:
        size = 1
        for s in shp:
            size *= s
        out.append(flat[pos:pos + size].reshape(shp))
        pos += size
    return out


def _to_place_major(gfull, name):
    R_, C = gfull.shape
    if name in COL_SHARDED:
        return gfull.reshape(R_, N_PLACES, C // N_PLACES).transpose(1, 0, 2)
    return gfull.reshape(N_PLACES, R_ // N_PLACES, C)


_W_IN_LAYOUT = (('p', 0, 0, 2048), ('f', 0, 2048, HEADS), ('z', 0, 2056, 1024), ('x', 0, 3080, 1536),
                ('f', DT_LANE0, 4616, SSD_HEADS), ('g', 0, 4632, 3072))


def _w_in_segments(slabs):
    starts = [0]
    for s in slabs:
        starts.append(starts[-1] + s.shape[-1])

    def cols(a, b):
        parts = []
        for s, s0 in zip(slabs, starts):
            lo, hi = max(a, s0), min(b, s0 + s.shape[-1])
            if lo < hi:
                parts.append(s[..., lo - s0:hi - s0])
        return parts[0] if len(parts) == 1 else jnp.concatenate(parts, axis=-1)

    pad = jnp.zeros(slabs[0].shape[:-1] + (LANE - DT_LANE0 - SSD_HEADS,), slabs[0].dtype)
    return dict(p=cols(0, 2048), z=cols(2056, 3080), x=cols(3080, 4616), g=cols(4632, 7704),
                f=jnp.concatenate([cols(2048, 2056), cols(4616, 4632), pad], axis=-1))


def _w_in_columns(segs, a, b):
    parts = []
    for key, s0, g0, w in _W_IN_LAYOUT:
        lo, hi = max(a, g0), min(b, g0 + w)
        if lo < hi:
            parts.append(segs[key][..., s0 + lo - g0:s0 + hi - g0])
    return parts[0] if len(parts) == 1 else jnp.concatenate(parts, axis=-1)


def kernel(x, norm_mix, w_in, pool_mix, pool_scale, f_bias, ssd_conv_w, ssd_conv_b, ssd_dt_bias, ssd_a_log, ssd_d, ssd_norm, p_pool, p_attn, p_ssd, w_out, norm_ffn, ffn_up, ffn_conv_w, ffn_conv_b, ffn_down, norm_final, loss_target, m_norm_mix, m_w_in, m_pool_mix, m_pool_scale, m_f_bias, m_ssd_conv_w, m_ssd_conv_b, m_ssd_dt_bias, m_ssd_a_log, m_ssd_d, m_ssd_norm, m_p_pool, m_p_attn, m_p_ssd, m_w_out, m_norm_ffn, m_ffn_up, m_ffn_conv_w, m_ffn_conv_b, m_ffn_down, m_norm_final, v_norm_mix, v_w_in, v_pool_mix, v_pool_scale, v_f_bias, v_ssd_conv_w, v_ssd_conv_b, v_ssd_dt_bias, v_ssd_a_log, v_ssd_d, v_ssd_norm, v_p_pool, v_p_attn, v_p_ssd, v_w_out, v_norm_ffn, v_ffn_up, v_ffn_conv_w, v_ffn_conv_b, v_ffn_down, v_norm_final):
    args = dict(locals())
    w_sh = {k: args[k] for k in WEIGHTS}
    m_sh = {k: args['m_' + k] for k in WEIGHTS}
    v_sh = {k: args['v_' + k] for k in WEIGHTS}
    depth = norm_mix.shape[0]
    place = 2 * lax.axis_index("x") + lax.axis_index("y")
    row_sharded = [k for k in BIG if k not in COL_SHARDED]

    gathered = _gather_places([w_sh[k].astype(MXU_DTYPE) for k in BIG] + [w_sh[k] for k in SMALL_SHARDED],
                              [k in row_sharded for k in BIG] + [False] * len(SMALL_SHARDED), "gather_weights")
    gathered = dict(zip(BIG + SMALL_SHARDED, gathered))
    W = {k: w_sh[k] for k in SMALL if k not in SMALL_SHARDED}
    for k in BIG + SMALL_SHARDED:
        gk = gathered[k]
        if k in row_sharded:
            W[k] = gk.reshape(gk.shape[0], -1, gk.shape[-1])
        elif k == 'w_in':
            W[k] = _w_in_segments([gk[p] for p in range(N_PLACES)])
        else:
            W[k] = jnp.concatenate([gk[p] for p in range(N_PLACES)], axis=-1)

    loss_local, grad_x, grads, g_final = _local_step(x[0], loss_target[0], W)
    loss = lax.psum(loss_local, ("x", "y", "c"))

    def place_major(k, l):
        if k == 'w_in':
            c = IN_TOTAL // N_PLACES
            return jnp.stack([_w_in_columns(grads[l][k], p * c, (p + 1) * c) for p in range(N_PLACES)])
        return _to_place_major(grads[l][k], k)

    g_big = [jnp.stack([place_major(k, l) for l in range(depth)], axis=1) for k in BIG]
    small_names = [k for k in SMALL if k != 'norm_final'] + ['norm_final']
    small_full = [jnp.stack([grads[l][k] for l in range(depth)]) for k in small_names[:-1]] + [g_final]
    small_full_shapes = [a.shape for a in small_full]
    small_packed = _pack_rows(small_full, 32)
    g_small = jnp.broadcast_to(small_packed.reshape(1, 2, -1, PACK_W),
                               (N_PLACES, 2, small_packed.shape[0] // 2, PACK_W))

    both = _reduce_sibling(g_big + [g_small], "reduce_sibling")
    halves = [_sum_leading(b.reshape(2, -1, b.shape[-1]), f"reduce_sibling_add{j}").reshape(b.shape[1:])
              for j, b in enumerate(both)]
    recv = _reduce_places(halves, "reduce_places")
    qs = [_sum_leading(r, f"reduce_places_add{j}") for j, r in enumerate(recv)]
    gsum = _gather_sibling(qs, "reduce_gather")

    def mine(k, a):
        if k in SMALL_SHARDED:
            c = a.shape[-1] // N_PLACES
            return lax.dynamic_slice_in_dim(a, place * c, c, axis=a.ndim - 1)
        return a

    outs = {}
    for j, k in enumerate(BIG):
        shp = w_sh[k].shape
        flat = lambda a, shp=shp: a.reshape(-1, shp[-1])
        d_, m_, v_ = _adamw(flat(gsum[j]), flat(w_sh[k]), flat(m_sh[k]), flat(v_sh[k]), "adamw_" + k)
        for prefix, a in (('grad_', gsum[j]), ('delta_', d_), ('new_m_', m_), ('new_v_', v_)):
            outs[prefix + k] = a.reshape(shp)
    g_small_list = [mine(k, a) for k, a in
                    zip(small_names, _unpack_rows(gsum[-1].reshape(-1, PACK_W), small_full_shapes))]
    shapes = [a.shape for a in g_small_list]
    gp = _pack_rows(g_small_list, 128)
    wp, mp, vp = (_pack_rows([d[k] for k in small_names], 128) for d in (w_sh, m_sh, v_sh))
    delta_p, m_p, v_p = _adamw(gp, wp, mp, vp, "adamw_small")
    for prefix, buf in (('grad_', gp), ('delta_', delta_p), ('new_m_', m_p), ('new_v_', v_p)):
        for k, a in zip(small_names, _unpack_rows(buf, shapes)):
            outs[prefix + k] = a
    result = [loss, grad_x[None]]
    for prefix in ('grad_', 'delta_', 'new_m_', 'new_v_'):
        result += [outs[prefix + k] for k in WEIGHTS]
    return tuple(result)
```

```python
import functools

import jax
import jax.numpy as jnp
from jax import lax
from jax.experimental import pallas as pl
from jax.experimental.pallas import tpu as pltpu

F32 = jnp.float32
MXU_DTYPE = jnp.bfloat16
NORM_EPS = 1e-6
HALO = 16
LANE = 128
NEG_BIG = -1e30
VMEM_LIMIT = 52 * 1024 * 1024

D_MODEL = 1024
POOL_WINDOWS = (2, 4, 8, 16)
POOL_W = 512
HEADS = 8
HEAD_DIM = 64
ATTN_W = 512
ATTN_SCALE = HEAD_DIM ** -0.5
SSD_W = 1024
SSD_HEADS = 16
SSD_P = 64
SSD_N = 128
SSD_CHUNK = 128
SSD_CONV_CH = 1536
FFN = 2816
DT_LANE0 = 8
IN_SPLITS = (512, 512, 512, 512, 8, 1024, 1536, 16, 3072)
IN_TOTAL = sum(IN_SPLITS)
N_PLACES = 4

ADAM_LR, ADAM_B1, ADAM_B2, ADAM_EPS, ADAM_WD, ADAM_STEP = 0.001, 0.9, 0.999, 1e-08, 0.01, 10

BIG = ('w_in', 'p_pool', 'p_attn', 'p_ssd', 'w_out', 'ffn_up', 'ffn_down')
COL_SHARDED = ('w_in', 'p_pool', 'p_attn', 'ffn_up')
SMALL = ('norm_mix', 'pool_mix', 'pool_scale', 'f_bias', 'ssd_conv_w', 'ssd_conv_b', 'ssd_dt_bias',
         'ssd_a_log', 'ssd_d', 'ssd_norm', 'norm_ffn', 'ffn_conv_w', 'ffn_conv_b', 'norm_final')
SMALL_SHARDED = ('ssd_conv_w', 'ffn_conv_w')
WEIGHTS = ('norm_mix', 'w_in', 'pool_mix', 'pool_scale', 'f_bias', 'ssd_conv_w', 'ssd_conv_b', 'ssd_dt_bias',
           'ssd_a_log', 'ssd_d', 'ssd_norm', 'p_pool', 'p_attn', 'p_ssd', 'w_out', 'norm_ffn', 'ffn_up',
           'ffn_conv_w', 'ffn_conv_b', 'ffn_down', 'norm_final')
PACK_W = 1024


def _params(sem):
    return pltpu.CompilerParams(dimension_semantics=sem, vmem_limit_bytes=VMEM_LIMIT)


def _tile(n, prefs=(512, 256, 128)):
    for t in prefs:
        if n % t == 0:
            return t
    return n


def _sigmoid(x):
    return 0.5 * jnp.tanh(0.5 * x) + 0.5


def _softplus(x):
    return jnp.maximum(x, 0.0) + jnp.log1p(jnp.exp(-jnp.abs(x)))


def _dot(a, b, dims=((1,), (0,))):
    return lax.dot_general(a.astype(MXU_DTYPE), b.astype(MXU_DTYPE), (dims, ((), ())),
                           preferred_element_type=F32)


NT = ((1,), (1,))


def _mm(a, b, *, ta=False, tb=False, acc=None, out_dtype=F32, name):
    M, K = (a.shape[1], a.shape[0]) if ta else a.shape
    N = b.shape[0] if tb else b.shape[1]
    big = (1024, 1408, 512, 256, 128)
    tm, tn = _tile(M, big), _tile(N, big)
    tk = K if K <= 1024 else _tile(K, (512, 256, 128) if ta else big)
    nk = K // tk
    a_spec = pl.BlockSpec((tk, tm), lambda i, j, k: (k, i)) if ta else pl.BlockSpec((tm, tk), lambda i, j, k: (i, k))
    b_spec = pl.BlockSpec((tn, tk), lambda i, j, k: (j, k)) if tb else pl.BlockSpec((tk, tn), lambda i, j, k: (k, j))
    in_specs = [a_spec, b_spec]
    args = [a, b]
    if acc is not None:
        in_specs.append(pl.BlockSpec((tm, tn), lambda i, j, k: (i, j)))
        args.append(acc)

    def body(*refs):
        if acc is not None:
            a_ref, b_ref, c_ref, o_ref, acc_ref = refs
        else:
            a_ref, b_ref, o_ref, acc_ref = refs
        k = pl.program_id(2)

        @pl.when(k == 0)
        def _():
            if acc is not None:
                acc_ref[...] = c_ref[...].astype(F32)
            else:
                acc_ref[...] = jnp.zeros_like(acc_ref)

        av = a_ref[...]
        if ta:
            av = av.astype(F32).T
        acc_ref[...] += _dot(av, b_ref[...], NT if tb else ((1,), (0,)))

        @pl.when(k == nk - 1)
        def _():
            o_ref[...] = acc_ref[...].astype(out_dtype)

    return pl.pallas_call(
        body, grid=(M // tm, N // tn, nk), in_specs=in_specs,
        out_specs=pl.BlockSpec((tm, tn), lambda i, j, k: (i, j)),
        out_shape=jax.ShapeDtypeStruct((M, N), out_dtype),
        scratch_shapes=[pltpu.VMEM((tm, tn), F32)],
        compiler_params=_params(("parallel", "parallel", "arbitrary")), name=name)(*args)


def _rows(body, S, T, *, rows=(), consts=(), prevs=(), nexts=(), out_rows=(), out_accs=(), scratch=(), name):
    n = S // T
    hb = T // HALO
    last_h = S // HALO - 1

    def norm(r):
        return r if isinstance(r, tuple) else (r, r.shape[1], 0)

    rows, prevs, nexts = [norm(r) for r in rows], [norm(r) for r in prevs], [norm(r) for r in nexts]
    in_specs, args = [], []
    for arr, W, cb in rows:
        in_specs.append(pl.BlockSpec((T, W), lambda i, cb=cb: (i, cb)))
        args.append(arr)
    for cst in consts:
        in_specs.append(pl.BlockSpec(cst.shape, lambda i, nd=cst.ndim: (0,) * nd))
        args.append(cst)
    for arr, W, cb in prevs:
        in_specs.append(pl.BlockSpec((HALO, W), lambda i, cb=cb: (jnp.maximum(i * hb - 1, 0), cb)))
        args.append(arr)
    for arr, W, cb in nexts:
        in_specs.append(pl.BlockSpec((HALO, W), lambda i, cb=cb: (jnp.minimum((i + 1) * hb, last_h), cb)))
        args.append(arr)
    out_specs = [pl.BlockSpec((T, W), lambda i: (i, 0)) for W, _ in out_rows]
    out_specs += [pl.BlockSpec(shp, lambda i, nd=len(shp): (0,) * nd) for shp, _ in out_accs]
    out_shape = [jax.ShapeDtypeStruct((S, W), dt) for W, dt in out_rows]
    out_shape += [jax.ShapeDtypeStruct(shp, dt) for shp, dt in out_accs]
    cuts = [len(rows), len(consts), len(prevs), len(nexts), len(out_rows), len(out_accs), len(scratch)]

    def kern(*refs):
        groups, pos = [], 0
        for c in cuts:
            groups.append(list(refs[pos:pos + c]))
            pos += c
        i = pl.program_id(0)

        @pl.when(i == 0)
        def _():
            for a_ref in groups[5]:
                a_ref[...] = jnp.zeros_like(a_ref)

        body(i, n, *groups)

    outs = pl.pallas_call(kern, grid=(n,), in_specs=in_specs, out_specs=out_specs, out_shape=out_shape,
                          scratch_shapes=list(scratch), compiler_params=_params(("arbitrary",)), name=name)(*args)
    return outs


def _fill_prev(ext, prev_ref, cur, i):
    ext[0:HALO, :] = jnp.where(i > 0, prev_ref[...].astype(F32), 0.0)
    ext[HALO:, :] = cur


def _fill_next(ext, cur, next_val, T):
    ext[0:T, :] = cur
    ext[T:, :] = next_val


def _row_ids(i, T, W=1):
    return i * T + lax.broadcasted_iota(jnp.int32, (T, W), 0)


def _norm_fwd(x, w, name):
    S, D = x.shape

    def body(i, n, R, C, P, N, O, A, Sc):
        xv = R[0][...]
        r = lax.rsqrt(jnp.mean(xv * xv, axis=-1, keepdims=True) + NORM_EPS)
        O[0][...] = (xv * r * C[0][...]).astype(MXU_DTYPE)

    return _rows(body, S, _tile(S), rows=[x], consts=[w], out_rows=[(D, MXU_DTYPE)], name=name)[0]


def _norm_bwd_math(xv, w, du):
    r = lax.rsqrt(jnp.mean(xv * xv, axis=-1, keepdims=True) + NORM_EPS)
    xh = xv * r
    g = du * w
    dx = r * (g - xh * jnp.mean(g * xh, axis=-1, keepdims=True))
    dw = jnp.sum(du * xh, axis=0, keepdims=True)
    return dx, dw


def _norm_bwd(x, w, du, dres, name):
    S, D = x.shape

    def body(i, n, R, C, P, N, O, A, Sc):
        dx, dw = _norm_bwd_math(R[0][...], C[0][...], R[1][...])
        O[0][...] = R[2][...] + dx
        A[0][...] += dw

    return _rows(body, S, _tile(S), rows=[x, du, dres], consts=[w], out_rows=[(D, F32)],
                 out_accs=[((1, D), F32)], name=name)


def _loss_head(x, w, target, name):
    S, D = x.shape

    def body(i, n, R, C, P, N, O, A, Sc):
        xv, w_, tg = R[0][...], C[0][...], R[1][...]
        r = lax.rsqrt(jnp.mean(xv * xv, axis=-1, keepdims=True) + NORM_EPS)
        e = xv * r * w_ - tg
        A[1][...] += jnp.broadcast_to(0.5 * jnp.sum(jnp.mean(e * e, axis=-1, keepdims=True)), (1, LANE))
        dx, dw = _norm_bwd_math(xv, w_, e / D)
        O[0][...] = dx
        A[0][...] += dw

    return _rows(body, S, _tile(S), rows=[x, target], consts=[w], out_rows=[(D, F32)],
                 out_accs=[((1, D), F32), ((1, LANE), F32)], name=name)


def _pool_fwd(pqkv, mix, scale, name):
    S = pqkv.shape[0]
    T = _tile(S, (256, 128))

    def body(i, n, R, C, P, N, O, A, Sc):
        ext = Sc[0]
        v = R[0][...]
        _fill_prev(ext, P[0], v, i)
        t1 = (_row_ids(i, T) + 1).astype(F32)
        for g, w in enumerate(POOL_WINDOWS):
            cols = slice(g * LANE, (g + 1) * LANE)
            acc = v[:, cols]
            for j in range(1, w):
                acc = acc + ext[pl.ds(HALO - j, T), cols]
            d = (acc / jnp.minimum(t1, float(w)) - v[:, cols]).astype(MXU_DTYPE)
            O[0][:, cols] = d
            O[1][:, cols] = (_dot(d, C[0][g]) * C[1][:, cols]).astype(MXU_DTYPE)

    return _rows(body, S, T, rows=[(pqkv, POOL_W, 0)], prevs=[(pqkv, POOL_W, 0)], consts=[mix, scale],
                 out_rows=[(POOL_W, MXU_DTYPE), (POOL_W, MXU_DTYPE)],
                 scratch=[pltpu.VMEM((HALO + T, POOL_W), F32)], name=name)


def _pool_bwd_a(dypm, d, mix, scale, name):
    S = d.shape[0]
    T = _tile(S, (256, 128))

    def body(i, n, R, C, P, N, O, A, Sc):
        for g in range(len(POOL_WINDOWS)):
            cols = slice(g * LANE, (g + 1) * LANE)
            dg = R[1][:, cols]
            dy = R[0][:, cols]
            yg = _dot(dg, C[0][g])
            A[0][:, cols] += jnp.sum(dy * yg, axis=0, keepdims=True)
            dys = dy * C[1][:, cols]
            A[1][cols, :] += _dot(dg.astype(F32).T, dys)
            O[0][:, cols] = _dot(dys, C[0][g], NT)

    return _rows(body, S, T, rows=[dypm, d], consts=[mix, scale], out_rows=[(POOL_W, F32)],
                 out_accs=[((1, POOL_W), F32), ((POOL_W, LANE), F32)], name=name)


def _pool_bwd_b(dd, name):
    S = dd.shape[0]
    T = _tile(S, (256, 128))

    def body(i, n, R, C, P, N, O, A, Sc):
        ext = Sc[0]
        ddv = R[0][...]
        t1 = (_row_ids(i, T) + 1).astype(F32)
        nxt = jnp.where(i < n - 1, N[0][...], 0.0)
        for g, w in enumerate(POOL_WINDOWS):
            cols = slice(g * LANE, (g + 1) * LANE)
            ext[0:T, cols] = ddv[:, cols] / jnp.minimum(t1, float(w))
            ext[T:, cols] = nxt[:, cols] / float(w)
        for g, w in enumerate(POOL_WINDOWS):
            cols = slice(g * LANE, (g + 1) * LANE)
            acc = ext[0:T, cols]
            for j in range(1, w):
                acc = acc + ext[pl.ds(j, T), cols]
            O[0][:, cols] = (acc - ddv[:, cols]).astype(MXU_DTYPE)

    return _rows(body, S, T, rows=[dd], nexts=[dd], out_rows=[(POOL_W, MXU_DTYPE)],
                 scratch=[pltpu.VMEM((T + HALO, POOL_W), F32)], name=name)[0]


def _lane_cumsum(seg, reverse=False):
    lane = lax.broadcasted_iota(jnp.int32, seg.shape, 1)
    sh = 1
    while sh < LANE:
        if reverse:
            seg = seg + jnp.where(lane < LANE - sh, pltpu.roll(seg, LANE - sh, axis=1), 0.0)
        else:
            seg = seg + jnp.where(lane >= sh, pltpu.roll(seg, sh, axis=1), 0.0)
        sh *= 2
    return seg


def _logf_cumsum(fT, bias, name):
    H, S = fT.shape
    TB = _tile(S)
    nb = S // TB

    def body(f_ref, b_ref, o_ref, carry):
        @pl.when(pl.program_id(0) == 0)
        def _():
            carry[...] = jnp.zeros_like(carry)

        x = f_ref[...] + b_ref[...]
        lf = jnp.minimum(x, 0.0) - jnp.log1p(jnp.exp(-jnp.abs(x)))
        c = carry[...]
        for j in range(TB // LANE):
            seg = _lane_cumsum(lf[:, j * LANE:(j + 1) * LANE]) + c
            o_ref[:, j * LANE:(j + 1) * LANE] = seg
            c = seg[:, LANE - 1:LANE]
        carry[...] = c

    return pl.pallas_call(
        body, grid=(nb,), in_specs=[pl.BlockSpec((H, TB), lambda i: (0, i)), pl.BlockSpec((H, 1), lambda i: (0, 0))],
        out_specs=pl.BlockSpec((H, TB), lambda i: (0, i)), out_shape=jax.ShapeDtypeStruct((H, S), F32),
        scratch_shapes=[pltpu.VMEM((H, 1), F32)], compiler_params=_params(("arbitrary",)), name=name)(fT, bias)


def _logf_cumsum_bwd(fT, bias, dc, dcq, name):
    H, S = fT.shape
    TB = _tile(S)
    nb = S // TB

    def body(f_ref, b_ref, dc_ref, dcq_ref, o_ref, db_ref, carry):
        @pl.when(pl.program_id(0) == 0)
        def _():
            carry[...] = jnp.zeros_like(carry)
            db_ref[...] = jnp.zeros_like(db_ref)

        x = f_ref[...] + b_ref[...]
        sg = _sigmoid(-x)
        dcv = dc_ref[...] + dcq_ref[...]
        c = carry[...]
        db = jnp.zeros((H, 1), F32)
        for j in reversed(range(TB // LANE)):
            seg = _lane_cumsum(dcv[:, j * LANE:(j + 1) * LANE], reverse=True) + c
            df = seg * sg[:, j * LANE:(j + 1) * LANE]
            o_ref[:, j * LANE:(j + 1) * LANE] = df
            db = db + jnp.sum(df, axis=1, keepdims=True)
            c = seg[:, 0:1]
        carry[...] = c
        db_ref[...] += db

    rev = lambda i: (0, nb - 1 - i)
    return pl.pallas_call(
        body, grid=(nb,),
        in_specs=[pl.BlockSpec((H, TB), rev), pl.BlockSpec((H, 1), lambda i: (0, 0)), pl.BlockSpec((H, TB), rev),
                  pl.BlockSpec((H, TB), rev)],
        out_specs=[pl.BlockSpec((H, TB), rev), pl.BlockSpec((H, 1), lambda i: (0, 0))],
        out_shape=[jax.ShapeDtypeStruct((H, S), F32), jax.ShapeDtypeStruct((H, 1), F32)],
        scratch_shapes=[pltpu.VMEM((H, 1), F32)], compiler_params=_params(("arbitrary",)), name=name)(
            fT, bias, dc, dcq)


def _attn_scores(q, k, ck, diagonal, T):
    s = _dot(q, k, NT) - ck
    if diagonal:
        tril = lax.broadcasted_iota(jnp.int32, (T, T), 1) <= lax.broadcasted_iota(jnp.int32, (T, T), 0)
        s = jnp.where(tril, s, NEG_BIG)
    return s


def _attn_fwd(q, k, v, c_row, name):
    H, S, Dh = q.shape
    T = _tile(S, (1024, 512, 256, 128))
    nq = S // T

    def body(q_ref, k_ref, v_ref, ck_ref, o_ref, lse_ref, m_s, l_s, acc_s):
        qi, ki = pl.program_id(1), pl.program_id(2)

        @pl.when(ki == 0)
        def _():
            m_s[...] = jnp.full_like(m_s, NEG_BIG)
            l_s[...] = jnp.zeros_like(l_s)
            acc_s[...] = jnp.zeros_like(acc_s)

        def step(diagonal):
            s = _attn_scores(q_ref[0], k_ref[0], ck_ref[0], diagonal, T)
            m_new = jnp.maximum(m_s[...], jnp.max(s, axis=1, keepdims=True))
            alpha = jnp.exp(m_s[...] - m_new)
            p = jnp.exp(s - m_new)
            l_s[...] = alpha * l_s[...] + jnp.sum(p, axis=1, keepdims=True)
            acc_s[...] = alpha * acc_s[...] + _dot(p, v_ref[0])
            m_s[...] = m_new

        @pl.when(ki < qi)
        def _():
            step(False)

        @pl.when(ki == qi)
        def _():
            step(True)
            o_ref[0] = acc_s[...] / l_s[...]
            lse_ref[0] = m_s[...] + jnp.log(l_s[...])

    qmap = lambda h, qi, ki: (h, qi, 0)
    kmap = lambda h, qi, ki: (h, jnp.minimum(ki, qi), 0)
    return pl.pallas_call(
        body, grid=(H, nq, nq),
        in_specs=[pl.BlockSpec((1, T, Dh), qmap), pl.BlockSpec((1, T, Dh), kmap), pl.BlockSpec((1, T, Dh), kmap),
                  pl.BlockSpec((1, 1, T), lambda h, qi, ki: (h, 0, jnp.minimum(ki, qi)))],
        out_specs=[pl.BlockSpec((1, T, Dh), qmap), pl.BlockSpec((1, T, 1), qmap)],
        out_shape=[jax.ShapeDtypeStruct((H, S, Dh), F32), jax.ShapeDtypeStruct((H, S, 1), F32)],
        scratch_shapes=[pltpu.VMEM((T, 1), F32), pltpu.VMEM((T, 1), F32), pltpu.VMEM((T, Dh), F32)],
        compiler_params=_params(("parallel", "parallel", "arbitrary")), name=name)(q, k, v, c_row)


def _attn_bwd(q, k, v, do, o, c_row, lse, name):
    H, S, Dh = q.shape
    T = _tile(S, (1024, 512, 256, 128))
    nq = S // T

    def body(q_ref, k_ref, v_ref, do_ref, o_ref, ck_ref, lse_ref, dq_ref, dk_ref, dv_ref, dc_ref, dcq_ref,
             dk_s, dv_s, dc_s):
        ki, qi = pl.program_id(1), pl.program_id(2)

        @pl.when((ki == 0) & (qi == 0))
        def _():
            dq_ref[...] = jnp.zeros_like(dq_ref)
            dcq_ref[...] = jnp.zeros_like(dcq_ref)

        @pl.when(qi == 0)
        def _():
            dk_s[...] = jnp.zeros_like(dk_s)
            dv_s[...] = jnp.zeros_like(dv_s)
            dc_s[...] = jnp.zeros_like(dc_s)

        def step(diagonal):
            qv, kv, vv = q_ref[0], k_ref[0], v_ref[0]
            s = _attn_scores(qv, kv, ck_ref[0], diagonal, T)
            p = jnp.exp(s - lse_ref[0])
            dov = do_ref[0]
            delta = jnp.sum(dov * o_ref[0], axis=1, keepdims=True)
            dv_s[...] += _dot(p.T, dov)
            dp = _dot(dov, vv, NT)
            ds = p * (dp - delta)
            dc_s[...] -= jnp.sum(ds, axis=0, keepdims=True)
            rows = pl.ds(pl.multiple_of(qi * T, T), T)
            dq_ref[0, rows, :] += _dot(ds, kv)
            dcq_ref[0, rows, :] += jnp.sum(ds, axis=1, keepdims=True)
            dk_s[...] += _dot(ds.T, qv)

        @pl.when(qi > ki)
        def _():
            step(False)

        @pl.when(qi == ki)
        def _():
            step(True)

        @pl.when(qi == nq - 1)
        def _():
            dk_ref[0] = dk_s[...]
            dv_ref[0] = dv_s[...]
            dc_ref[0] = dc_s[...]

    qmap = lambda h, ki, qi: (h, jnp.maximum(qi, ki), 0)
    kmap = lambda h, ki, qi: (h, ki, 0)
    return pl.pallas_call(
        body, grid=(H, nq, nq),
        in_specs=[pl.BlockSpec((1, T, Dh), qmap), pl.BlockSpec((1, T, Dh), kmap), pl.BlockSpec((1, T, Dh), kmap),
                  pl.BlockSpec((1, T, Dh), qmap), pl.BlockSpec((1, T, Dh), qmap),
                  pl.BlockSpec((1, 1, T), lambda h, ki, qi: (h, 0, ki)), pl.BlockSpec((1, T, 1), qmap)],
        out_specs=[pl.BlockSpec((1, S, Dh), lambda h, ki, qi: (h, 0, 0)), pl.BlockSpec((1, T, Dh), kmap),
                   pl.BlockSpec((1, T, Dh), kmap), pl.BlockSpec((1, 1, T), lambda h, ki, qi: (h, 0, ki)),
                   pl.BlockSpec((1, S, 1), lambda h, ki, qi: (h, 0, 0))],
        out_shape=[jax.ShapeDtypeStruct((H, S, Dh), F32), jax.ShapeDtypeStruct((H, S, Dh), F32),
                   jax.ShapeDtypeStruct((H, S, Dh), F32), jax.ShapeDtypeStruct((H, 1, S), F32),
                   jax.ShapeDtypeStruct((H, S, 1), F32)],
        scratch_shapes=[pltpu.VMEM((T, Dh), F32), pltpu.VMEM((T, Dh), F32), pltpu.VMEM((1, T), F32)],
        compiler_params=_params(("parallel", "arbitrary", "arbitrary")), name=name)(
            q, k, v, do, o, c_row, lse)


def _conv_taps(ext, w_ref, K, T):
    out = None
    for k in range(K):
        term = ext[pl.ds(HALO - (K - 1 - k), T), :] * w_ref[k:k + 1, :]
        out = term if out is None else out + term
    return out


def _conv_bwd_b(dpre, w, name):
    S, C = dpre.shape
    K = w.shape[0]
    T = _tile(S, (128,))

    def body(i, n, R, Cs, P, N, O, A, Sc):
        ext = Sc[0]
        _fill_next(ext, R[0][...], jnp.where(i < n - 1, N[0][...], 0.0), T)
        out = None
        for k in range(K):
            term = ext[pl.ds(K - 1 - k, T), :] * Cs[0][k:k + 1, :]
            out = term if out is None else out + term
        O[0][...] = out.astype(MXU_DTYPE)

    return _rows(body, S, T, rows=[dpre], nexts=[dpre], consts=[w], out_rows=[(C, MXU_DTYPE)],
                 scratch=[pltpu.VMEM((T + HALO, C), F32)], name=name)[0]


def _dt_mask():
    lane = lax.broadcasted_iota(jnp.int32, (1, LANE), 1)
    return ((lane >= DT_LANE0) & (lane < DT_LANE0 + SSD_HEADS)).astype(F32)


def _ssd_pre_fwd(xbc, fdt, cw, cb, dtb, name):
    S, C = xbc.shape
    T = _tile(S, (256, 128))
    K = cw.shape[0]

    def body(i, n, R, Cs, P, N, O, A, Sc):
        ext = Sc[0]
        _fill_prev(ext, P[0], R[0][...], i)
        pre = _conv_taps(ext, Cs[0], K, T) + Cs[1][...]
        O[0][...] = pre * _sigmoid(pre)
        O[1][...] = _softplus(R[1][...] + Cs[2][...]) * _dt_mask()

    return _rows(body, S, T, rows=[xbc, fdt], prevs=[xbc], consts=[cw, cb, dtb],
                 out_rows=[(C, F32), (LANE, F32)], scratch=[pltpu.VMEM((HALO + T, C), F32)], name=name)


def _silu_grad(pre):
    sg = _sigmoid(pre)
    return sg * (1.0 + pre * (1.0 - sg))


def _conv_wgrad(ext, dpre, K, T):
    return jnp.concatenate([jnp.sum(dpre * ext[pl.ds(HALO - (K - 1 - k), T), :], axis=0, keepdims=True)
                            for k in range(K)], axis=0)


def _ssd_pre_bwd_a(xbc, fdt, dxa, ddtw, cw, cb, dtb, name):
    S, C = xbc.shape
    T = _tile(S, (256, 128))
    K = cw.shape[0]

    def body(i, n, R, Cs, P, N, O, A, Sc):
        ext = Sc[0]
        _fill_prev(ext, P[0], R[0][...], i)
        pre = _conv_taps(ext, Cs[0], K, T) + Cs[1][...]
        dpre = R[2][...] * _silu_grad(pre)
        O[0][...] = dpre
        A[0][...] += _conv_wgrad(ext, dpre, K, T)
        A[1][...] += jnp.sum(dpre, axis=0, keepdims=True)
        ddt = R[3][...] * _sigmoid(R[1][...] + Cs[2][...]) * _dt_mask()
        O[1][...] = ddt
        A[2][...] += jnp.sum(ddt, axis=0, keepdims=True)

    return _rows(body, S, T, rows=[xbc, fdt, dxa, ddtw], prevs=[xbc], consts=[cw, cb, dtb],
                 out_rows=[(C, F32), (LANE, F32)],
                 out_accs=[((K, C), F32), ((1, C), F32), ((1, LANE), F32)],
                 scratch=[pltpu.VMEM((HALO + T, C), F32)], name=name)


def _split3(x):
    hi = x.astype(jnp.bfloat16)
    r1 = x - hi.astype(F32)
    mid = r1.astype(jnp.bfloat16)
    lo = (r1 - mid.astype(F32)).astype(jnp.bfloat16)
    return hi, mid, lo


def _expand_mat():
    r = lax.broadcasted_iota(jnp.int32, (LANE, SSD_W), 0)
    c = lax.broadcasted_iota(jnp.int32, (LANE, SSD_W), 1)
    return (r - DT_LANE0 == c // SSD_P).astype(jnp.bfloat16)


def _headsum_mat():
    r = lax.broadcasted_iota(jnp.int32, (SSD_W, LANE), 0)
    c = lax.broadcasted_iota(jnp.int32, (SSD_W, LANE), 1)
    return (c - DT_LANE0 == r // SSD_P).astype(jnp.bfloat16)


def _expand(tile, ex):
    return sum(lax.dot_general(part, ex, (((1,), (0,)), ((), ())), preferred_element_type=F32)
               for part in _split3(tile))


def _headsum(full, hs):
    return sum(lax.dot_general(part, hs, (((1,), (0,)), ((), ())), preferred_element_type=F32)
               for part in _split3(full))


def _sub_cumsum(a, reverse=False):
    n = a.shape[0]
    row = lax.broadcasted_iota(jnp.int32, a.shape, 0)
    sh = 1
    while sh < n:
        if reverse:
            a = a + jnp.where(row < n - sh, pltpu.roll(a, n - sh, axis=0), 0.0)
        else:
            a = a + jnp.where(row >= sh, pltpu.roll(a, sh, axis=0), 0.0)
        sh *= 2
    return a


def _chunk_common(xa_ref, dtw_ref, a_row, ex):
    L = SSD_CHUNK
    xs = xa_ref[:, 0:SSD_W]
    dtv = dtw_ref[...]
    acs = _sub_cumsum(dtv * a_row)
    last = acs[L - 1:L, :]
    dt_full = _expand(dtv, ex)
    xd = xs * dt_full
    dec_full = _expand(jnp.exp(last - acs), ex)
    e_full = _expand(jnp.exp(acs), ex)
    elast_full = _expand(jnp.broadcast_to(jnp.exp(last), (8, LANE)), ex)[0:1, :]
    return xs, dtv, acs, last, dt_full, xd, dec_full, e_full, elast_full


def _decay_mask(acs, acsT, col):
    L = SSD_CHUNK
    diff = acs[:, col:col + 1] - acsT[col:col + 1, :]
    tril = lax.broadcasted_iota(jnp.int32, (L, L), 0) >= lax.broadcasted_iota(jnp.int32, (L, L), 1)
    return jnp.where(tril, jnp.exp(jnp.minimum(diff, 0.0)), 0.0)


def _half_mask(h):
    lane = lax.broadcasted_iota(jnp.int32, (1, LANE), 1)
    return ((lane // SSD_P) == (h % 2)).astype(F32)


def _ssd_chunk_fwd(xa, dtw, a_row, d_full, name):
    S = xa.shape[0]
    L, G = SSD_CHUNK, 2
    nc = S // L
    GW = SSD_W // G

    def body(xa_ref, dtw_ref, a_ref, d_ref, y_ref, hp_ref, state):
        @pl.when(pl.program_id(0) == 0)
        def _():
            state[...] = jnp.zeros_like(state)

        ex = _expand_mat()
        xs, dtv, acs, last, dt_full, xd, dec_full, e_full, elast_full = _chunk_common(xa_ref, dtw_ref, a_ref[...], ex)
        acsT = acs.T
        hp_ref[0] = state[...]
        for g in range(G):
            gc = slice(g * GW, (g + 1) * GW)
            Bg = xa_ref[:, SSD_W + g * SSD_N: SSD_W + (g + 1) * SSD_N]
            Cg = xa_ref[:, SSD_W + G * SSD_N + g * SSD_N: SSD_W + G * SSD_N + (g + 1) * SSD_N]
            cb = _dot(Cg, Bg, NT)
            y_off = e_full[:, gc] * _dot(Cg, state[:, gc])
            for hp in range(GW // LANE):
                pc = slice(g * GW + hp * LANE, g * GW + (hp + 1) * LANE)
                xd_pair = xd[:, pc]
                yp = y_off[:, hp * LANE:(hp + 1) * LANE] + d_ref[:, pc] * xs[:, pc]
                for h2 in range(2):
                    h = (g * GW + hp * LANE) // SSD_P + h2
                    m = cb * _decay_mask(acs, acsT, DT_LANE0 + h)
                    yp = yp + _dot(m, xd_pair * _half_mask(h))
                y_ref[:, pc] = yp
            st_new = _dot(Bg.T, xd[:, gc] * dec_full[:, gc])
            state[:, gc] = elast_full[:, gc] * state[:, gc] + st_new

    return pl.pallas_call(
        body, grid=(nc,),
        in_specs=[pl.BlockSpec((L, SSD_CONV_CH), lambda c: (c, 0)), pl.BlockSpec((L, LANE), lambda c: (c, 0)),
                  pl.BlockSpec((1, LANE), lambda c: (0, 0)), pl.BlockSpec((1, SSD_W), lambda c: (0, 0))],
        out_specs=[pl.BlockSpec((L, SSD_W), lambda c: (c, 0)), pl.BlockSpec((1, SSD_N, SSD_W), lambda c: (c, 0, 0))],
        out_shape=[jax.ShapeDtypeStruct((S, SSD_W), F32), jax.ShapeDtypeStruct((nc, SSD_N, SSD_W), F32)],
        scratch_shapes=[pltpu.VMEM((SSD_N, SSD_W), F32)],
        compiler_params=_params(("arbitrary",)), name=name)(xa, dtw, a_row, d_full)


def _ssd_chunk_bwd(xa, dtw, dy, hprev, a_row, d_full, name):
    S = xa.shape[0]
    L, G = SSD_CHUNK, 2
    nc = S // L
    GW = SSD_W // G

    def body(xa_ref, dtw_ref, dy_ref, hp_ref, a_ref, d_ref, dxa_ref, ddt_ref, da_ref, dd_ref, dstate):
        @pl.when(pl.program_id(0) == 0)
        def _():
            dstate[...] = jnp.zeros_like(dstate)
            da_ref[...] = jnp.zeros_like(da_ref)
            dd_ref[...] = jnp.zeros_like(dd_ref)

        ex, hs = _expand_mat(), _headsum_mat()
        a_row = a_ref[...]
        xs, dtv, acs, last, dt_full, xd, dec_full, e_full, elast_full = _chunk_common(xa_ref, dtw_ref, a_row, ex)
        acsT = acs.T
        dyv = dy_ref[...]
        lane = lax.broadcasted_iota(jnp.int32, (L, LANE), 1)
        sub = lax.broadcasted_iota(jnp.int32, (LANE, L), 0)
        dacs_c = jnp.zeros((L, LANE), F32)
        dacs_r = jnp.zeros((LANE, L), F32)
        dd_ref[...] += jnp.sum(_headsum(dyv * xs, hs), axis=0, keepdims=True)
        dxd_parts, yoff_parts, dxdd_parts, hh_parts = [], [], [], []
        for g in range(G):
            gc = slice(g * GW, (g + 1) * GW)
            b0 = SSD_W + g * SSD_N
            c0 = SSD_W + G * SSD_N + g * SSD_N
            Bg = xa_ref[:, b0:b0 + SSD_N]
            Cg = xa_ref[:, c0:c0 + SSD_N]
            Hp = hp_ref[0, :, gc]
            dH = dstate[:, gc]
            cb = _dot(Cg, Bg, NT)
            Gm = _dot(Cg, Hp)
            yoff_parts.append(e_full[:, gc] * Gm)
            dG = e_full[:, gc] * dyv[:, gc]
            dC = _dot(dG, Hp, NT)
            dHp = _dot(Cg.T, dG)
            xdd = xd[:, gc] * dec_full[:, gc]
            dB = _dot(xdd, dH, NT)
            dxdd = _dot(Bg, dH)
            dxdd_parts.append(dxdd)
            hh_parts.append(dH * Hp)
            dstate[:, gc] = dHp + elast_full[:, gc] * dH
            dcb = jnp.zeros((L, L), F32)
            dxd_g = []
            for hp in range(GW // LANE):
                pc = slice(g * GW + hp * LANE, g * GW + (hp + 1) * LANE)
                xd_pair = xd[:, pc]
                dxd_pair = dxdd[:, hp * LANE:(hp + 1) * LANE] * dec_full[:, pc]
                for h2 in range(2):
                    h = (g * GW + hp * LANE) // SSD_P + h2
                    col = DT_LANE0 + h
                    lm = _decay_mask(acs, acsT, col)
                    m = cb * lm
                    dy_h = dyv[:, pc] * _half_mask(h)
                    dm = _dot(dy_h, xd_pair, NT)
                    dxd_pair = dxd_pair + _dot(m.T, dy_h)
                    wm = dm * m
                    dacs_c = dacs_c + jnp.where(lane == col, jnp.sum(wm, axis=1, keepdims=True), 0.0)
                    dacs_r = dacs_r - jnp.where(sub == col, jnp.sum(wm, axis=0, keepdims=True), 0.0)
                    dcb = dcb + dm * lm
                dxd_g.append(dxd_pair)
            dxd_parts.append(jnp.concatenate(dxd_g, axis=1))
            dxa_ref[:, c0:c0 + SSD_N] = dC + _dot(dcb, Bg)
            dxa_ref[:, b0:b0 + SSD_N] = dB + _dot(dcb.T, Cg)
        dxd = jnp.concatenate(dxd_parts, axis=1)
        y_off = jnp.concatenate(yoff_parts, axis=1)
        dxdd_full = jnp.concatenate(dxdd_parts, axis=1)
        hh = jnp.concatenate(hh_parts, axis=1)
        dxa_ref[:, 0:SSD_W] = d_ref[...] * dyv + dxd * dt_full
        ddt = _headsum(dxd * xs, hs)
        w_dec = _headsum(dxdd_full * xd, hs) * jnp.exp(last - acs)
        dlast = jnp.sum(w_dec, axis=0, keepdims=True) + jnp.exp(last) * jnp.sum(_headsum(hh, hs), axis=0, keepdims=True)
        dacs = dacs_c + dacs_r.T + _headsum(dyv * y_off, hs) - w_dec
        rowid = lax.broadcasted_iota(jnp.int32, (L, LANE), 0)
        dacs = dacs + jnp.where(rowid == L - 1, dlast, 0.0)
        da = _sub_cumsum(dacs, reverse=True)
        ddt_ref[...] = ddt + da * a_row
        da_ref[...] += jnp.sum(da * dtv, axis=0, keepdims=True)

    rev = lambda c: (nc - 1 - c, 0)
    return pl.pallas_call(
        body, grid=(nc,),
        in_specs=[pl.BlockSpec((L, SSD_CONV_CH), rev), pl.BlockSpec((L, LANE), rev), pl.BlockSpec((L, SSD_W), rev),
                  pl.BlockSpec((1, SSD_N, SSD_W), lambda c: (nc - 1 - c, 0, 0)),
                  pl.BlockSpec((1, LANE), lambda c: (0, 0)), pl.BlockSpec((1, SSD_W), lambda c: (0, 0))],
        out_specs=[pl.BlockSpec((L, SSD_CONV_CH), rev), pl.BlockSpec((L, LANE), rev),
                   pl.BlockSpec((1, LANE), lambda c: (0, 0)), pl.BlockSpec((1, LANE), lambda c: (0, 0))],
        out_shape=[jax.ShapeDtypeStruct((S, SSD_CONV_CH), F32), jax.ShapeDtypeStruct((S, LANE), F32),
                   jax.ShapeDtypeStruct((1, LANE), F32), jax.ShapeDtypeStruct((1, LANE), F32)],
        scratch_shapes=[pltpu.VMEM((SSD_N, SSD_W), F32)],
        compiler_params=_params(("arbitrary",)), name=name)(xa, dtw, dy, hprev, a_row, d_full)


def _ssd_post_fwd(y, z, w, name):
    S = y.shape[0]
    GW = SSD_W // 2

    def body(i, n, R, C, P, N, O, A, Sc):
        zv = R[1][...]
        v = R[0][...] * (zv * _sigmoid(zv))
        for g in range(2):
            gc = slice(g * GW, (g + 1) * GW)
            vg = v[:, gc]
            r = lax.rsqrt(jnp.mean(vg * vg, axis=-1, keepdims=True) + NORM_EPS)
            O[0][:, gc] = (vg * r * C[0][:, gc]).astype(MXU_DTYPE)

    return _rows(body, S, _tile(S, (256, 128)), rows=[y, z], consts=[w], out_rows=[(SSD_W, MXU_DTYPE)], name=name)[0]


def _ssd_post_bwd(y, z, dyn, w, name):
    S = y.shape[0]
    GW = SSD_W // 2

    def body(i, n, R, C, P, N, O, A, Sc):
        yv, zv, dn = R[0][...], R[1][...], R[2][...]
        sz = zv * _sigmoid(zv)
        v = yv * sz
        for g in range(2):
            gc = slice(g * GW, (g + 1) * GW)
            dv, dw = _norm_bwd_math(v[:, gc], C[0][:, gc], dn[:, gc])
            A[0][:, gc] += dw
            O[0][:, gc] = dv * sz[:, gc]
            O[1][:, gc] = (dv * yv[:, gc] * _silu_grad(zv[:, gc])).astype(MXU_DTYPE)

    return _rows(body, S, _tile(S, (256, 128)), rows=[y, z, dyn], consts=[w],
                 out_rows=[(SSD_W, F32), (SSD_W, MXU_DTYPE)], out_accs=[((1, SSD_W), F32)], name=name)


def _merge_fwd(gl, yp, ya, ys, name):
    S, D = yp.shape

    def body(i, n, R, C, P, N, O, A, Sc):
        acc = None
        for b in range(3):
            term = _sigmoid(R[0][:, b * D:(b + 1) * D]) * R[1 + b][...]
            acc = term if acc is None else acc + term
        O[0][...] = acc.astype(MXU_DTYPE)

    return _rows(body, S, _tile(S, (256, 128)), rows=[gl, yp, ya, ys], out_rows=[(D, MXU_DTYPE)], name=name)[0]


def _merge_bwd(gl, yp, ya, ys, dm, name):
    S, D = yp.shape

    def body(i, n, R, C, P, N, O, A, Sc):
        dmv = R[4][...]
        for b in range(3):
            gt = _sigmoid(R[0][:, b * D:(b + 1) * D])
            O[b][...] = (gt * dmv).astype(MXU_DTYPE)
            O[3][:, b * D:(b + 1) * D] = (dmv * R[1 + b][...] * gt * (1.0 - gt)).astype(MXU_DTYPE)

    return _rows(body, S, _tile(S, (256, 128)), rows=[gl, yp, ya, ys, dm],
                 out_rows=[(D, MXU_DTYPE)] * 3 + [(3 * D, MXU_DTYPE)], name=name)


def _ffn_act_fwd(hpre, cw, cb, name):
    S, C = hpre.shape
    K = cw.shape[0]
    T = _tile(S, (128,))
    Fd = C // 2

    def body(i, n, R, Cs, P, N, O, A, Sc):
        ext = Sc[0]
        _fill_prev(ext, P[0], R[0][...], i)
        hc = _conv_taps(ext, Cs[0], K, T) + Cs[1][...]
        gt = hc[:, :Fd]
        O[0][...] = (gt * _sigmoid(gt) * hc[:, Fd:]).astype(MXU_DTYPE)

    return _rows(body, S, T, rows=[hpre], prevs=[hpre], consts=[cw, cb], out_rows=[(Fd, MXU_DTYPE)],
                 scratch=[pltpu.VMEM((HALO + T, C), F32)], name=name)[0]


def _ffn_act_bwd_a(hpre, dact, cw, cb, name):
    S, C = hpre.shape
    K = cw.shape[0]
    T = _tile(S, (128,))
    Fd = C // 2

    def body(i, n, R, Cs, P, N, O, A, Sc):
        ext = Sc[0]
        _fill_prev(ext, P[0], R[0][...], i)
        hc = _conv_taps(ext, Cs[0], K, T) + Cs[1][...]
        gt, val, da = hc[:, :Fd], hc[:, Fd:], R[1][...]
        dhc = jnp.concatenate([da * val * _silu_grad(gt), da * gt * _sigmoid(gt)], axis=1)
        O[0][...] = dhc
        A[0][...] += _conv_wgrad(ext, dhc, K, T)
        A[1][...] += jnp.sum(dhc, axis=0, keepdims=True)

    return _rows(body, S, T, rows=[hpre, dact], prevs=[hpre], consts=[cw, cb], out_rows=[(C, F32)],
                 out_accs=[((K, C), F32), ((1, C), F32)], scratch=[pltpu.VMEM((HALO + T, C), F32)], name=name)


def _adamw_math(g, w, m, v):
    c1 = 1.0 - ADAM_B1 ** ADAM_STEP
    c2 = 1.0 - ADAM_B2 ** ADAM_STEP
    mn = ADAM_B1 * m + (1.0 - ADAM_B1) * g
    vn = ADAM_B2 * v + (1.0 - ADAM_B2) * (g * g)
    return -ADAM_LR * ((mn / c1) / (jnp.sqrt(vn / c2) + ADAM_EPS) + ADAM_WD * w), mn, vn


def _adamw(g, w, m, v, name):
    R_, W = g.shape

    def body(i, n, R, C, P, N, O, A, Sc):
        O[0][...], O[1][...], O[2][...] = _adamw_math(R[0][...], R[1][...], R[2][...], R[3][...])

    return _rows(body, R_, _row_tile(R_, W, 7), rows=[g, w, m, v], out_rows=[(W, F32)] * 3, name=name)


def _adamw_pair(q, other, w, m, v, pos, name):
    _, R_, W = w.shape
    T = _row_tile(R_, W, 9)

    def body(pos_ref, q_ref, o_ref, w_ref, m_ref, v_ref, g_out, d_out, m_out, v_out):
        g = jnp.where(pl.program_id(0) == pos_ref[1], q_ref[...], o_ref[...])
        g_out[0] = g
        d_out[0], m_out[0], v_out[0] = _adamw_math(g, w_ref[0], m_ref[0], v_ref[0])

    flat = pl.BlockSpec((T, W), lambda l, i, pos: (i, 0))
    full = pl.BlockSpec((1, T, W), lambda l, i, pos: (l, i, 0))
    return _scalar_call(body, pos, (2, R_ // T), [flat, flat, full, full, full], [full] * 4,
                        [jax.ShapeDtypeStruct(w.shape, F32)] * 4, (q, other, w, m, v), name)


def _row_tile(rows, width, n_blocks, budget=14 * 1024 * 1024):
    wpad = -(-width // LANE) * LANE
    for t in (512, 256, 128, 64, 32, 16, 8):
        if rows % t == 0 and n_blocks * t * wpad * 4 <= budget:
            return t
    return rows


_ANY = pl.BlockSpec(memory_space=pl.ANY)
_MESH = pl.DeviceIdType.MESH


DMA_CHUNK_BYTES = 2 * 1024 * 1024


def _row_chunks(shape, dtype):
    r = shape[-2]
    total = 1
    for s in shape:
        total *= s
    want = max(1, (total * jnp.dtype(dtype).itemsize) // DMA_CHUNK_BYTES)
    n = 1
    while n * 2 <= want and r % (n * 2 * 16) == 0 and n < 8:
        n *= 2
    return [(j * (r // n), r // n) for j in range(n)]


def _comm_call(plan, srcs, out_shapes, name):
    n = len(srcs)
    probe = plan(0, 0, 0, [_ShapeOnly(s.shape) for s in srcs], [_ShapeOnly(s.shape) for s in out_shapes])
    n_local, n_remote = len(probe[0]), len(probe[1])

    def body(*refs):
        src_refs, out_refs = refs[:n], refs[n:2 * n]
        send_sems, recv_sems, local_sems = refs[2 * n:]
        x, y, c = lax.axis_index("x"), lax.axis_index("y"), lax.axis_index("c")
        local, remote = plan(x, y, c, src_refs, out_refs)
        started = []
        for j, (s, d) in enumerate(local):
            cp = pltpu.make_async_copy(s, d, local_sems.at[j])
            cp.start()
            started.append(cp)
        sent = []
        for j, (s, d, peer) in enumerate(remote):
            cp = pltpu.make_async_remote_copy(src_ref=s, dst_ref=d, send_sem=send_sems.at[j], recv_sem=recv_sems.at[j],
                                              device_id=peer, device_id_type=_MESH)
            cp.start()
            sent.append(cp)
        for cp in sent:
            cp.wait()
        for cp in started:
            cp.wait()

    return pl.pallas_call(
        body, in_specs=[_ANY] * n, out_specs=[_ANY] * n,
        out_shape=[jax.ShapeDtypeStruct(s.shape, s.dtype) for s in out_shapes],
        scratch_shapes=[pltpu.SemaphoreType.DMA((n_remote,)), pltpu.SemaphoreType.DMA((n_remote,)),
                        pltpu.SemaphoreType.DMA((max(n_local, 1),))], name=name)(*srcs)


class _ShapeOnly:
    def __init__(self, shape):
        self.shape = tuple(shape)

    @property
    def at(self):
        return self

    def __getitem__(self, idx):
        return self


def _other_places(x, y):
    return [(1 - x, y), (x, 1 - y), (1 - x, 1 - y)]


def _gather_places(shards, row_major, name):
    outs = []
    for s, rm in zip(shards, row_major):
        L_, r, c_ = s.shape
        outs.append(jax.ShapeDtypeStruct((L_, N_PLACES, r, c_) if rm else (N_PLACES, L_, r, c_), s.dtype))

    def plan(x, y, c, src_refs, out_refs):
        me = 2 * x + y
        local, remote = [], []
        for s_ref, o_ref, rm, s in zip(src_refs, out_refs, row_major, shards):
            for r0, rn in _row_chunks(s.shape, s.dtype):
                src = s_ref.at[:, pl.ds(r0, rn), :]
                dst = o_ref.at[:, me, pl.ds(r0, rn), :] if rm else o_ref.at[me, :, pl.ds(r0, rn), :]
                local.append((src, dst))
                for px, py in _other_places(x, y):
                    remote.append((src, dst, (px, py, c)))
        return local, remote

    return _comm_call(plan, shards, outs, name)


def _reduce_sibling(gs, name):
    outs = [jax.ShapeDtypeStruct((N_PLACES,) + g.shape[2:], g.dtype) for g in gs]

    def plan(x, y, c, src_refs, out_refs):
        remote = []
        for g_ref, o_ref, g in zip(src_refs, out_refs, gs):
            for r0, rn in _row_chunks(g.shape[2:], g.dtype):
                for p in range(N_PLACES):
                    remote.append((g_ref.at[p, 1 - c, pl.ds(r0, rn), :], o_ref.at[p, pl.ds(r0, rn), :], (x, y, 1 - c)))
        return [], remote

    return _comm_call(plan, gs, outs, name)


def _reduce_places(hs, name):
    outs = [jax.ShapeDtypeStruct((3,) + h.shape[1:], h.dtype) for h in hs]

    def plan(x, y, c, src_refs, out_refs):
        remote = []
        for h_ref, o_ref, h in zip(src_refs, out_refs, hs):
            for r0, rn in _row_chunks(h.shape[1:], h.dtype):
                for j, (px, py) in enumerate(_other_places(x, y)):
                    remote.append((h_ref.at[2 * px + py, pl.ds(r0, rn), :], o_ref.at[j, pl.ds(r0, rn), :], (px, py, c)))
        return [], remote

    return _comm_call(plan, hs, outs, name)


def _swap_sibling(qs, name):
    def plan(x, y, c, src_refs, out_refs):
        remote = []
        for q_ref, o_ref, q in zip(src_refs, out_refs, qs):
            for r0, rn in _row_chunks(q.shape, q.dtype):
                remote.append((q_ref.at[pl.ds(r0, rn), :], o_ref.at[pl.ds(r0, rn), :], (x, y, 1 - c)))
        return [], remote

    return _comm_call(plan, qs, qs, name)


def _scalar_call(body, scalars, grid, in_specs, out_specs, out_shape, args, name):
    return pl.pallas_call(
        body, grid_spec=pltpu.PrefetchScalarGridSpec(num_scalar_prefetch=1, grid=grid, in_specs=in_specs,
                                                     out_specs=out_specs),
        out_shape=out_shape, compiler_params=_params(("arbitrary",) * len(grid)), name=name)(scalars, *args)


def _add_own_slot(g, r_, pos, name):
    P_, _, R_, W = g.shape
    T = _row_tile(R_, W, 3)

    def body(pos_ref, g_ref, r_ref, o_ref):
        o_ref[...] = g_ref[0] + r_ref[...]

    return _scalar_call(
        body, pos, (P_, R_ // T),
        [pl.BlockSpec((1, 1, T, W), lambda p, i, pos: (p, pos[1], i, 0)), pl.BlockSpec((1, T, W), lambda p, i, pos: (p, i, 0))],
        pl.BlockSpec((1, T, W), lambda p, i, pos: (p, i, 0)), jax.ShapeDtypeStruct((P_, R_, W), F32), (g, r_), name)


def _sum_places(h, recv, pos, name):
    _, R_, W = h.shape
    T = _row_tile(R_, W, 5)

    def body(pos_ref, h_ref, recv_ref, o_ref):
        for m in range(N_PLACES):
            @pl.when(pos_ref[0] == m)
            def _(m=m):
                acc = None
                for p in range(N_PLACES):
                    if p == m:
                        term = h_ref[0]
                    else:
                        dx, dy = (p >> 1) != (m >> 1), (p & 1) != (m & 1)
                        term = recv_ref[0 if (dx and not dy) else 1 if (dy and not dx) else 2]
                    acc = term if acc is None else acc + term
                o_ref[...] = acc

    return _scalar_call(
        body, pos, (R_ // T,),
        [pl.BlockSpec((1, T, W), lambda i, pos: (pos[0], i, 0)), pl.BlockSpec((3, T, W), lambda i, pos: (0, i, 0))],
        pl.BlockSpec((T, W), lambda i, pos: (i, 0)), jax.ShapeDtypeStruct((R_, W), F32), (h, recv), name)


def _to_heads(a, dtype):
    S = a.shape[0]
    return a.reshape(S, HEADS, HEAD_DIM).transpose(1, 0, 2).astype(dtype)


def _from_heads(a):
    return a.transpose(1, 0, 2).reshape(a.shape[1], HEADS * HEAD_DIM)


def _lane_tile(vec16):
    return jnp.concatenate([jnp.zeros((DT_LANE0,), F32), vec16,
                            jnp.zeros((LANE - DT_LANE0 - SSD_HEADS,), F32)])[None]


def _layer_consts(W, l):
    return dict(
        norm_mix=W['norm_mix'][l][None], mix=W['pool_mix'][l].astype(MXU_DTYPE), scale=W['pool_scale'][l][None],
        f_bias=W['f_bias'][l][:, None], cw=W['ssd_conv_w'][l], cb=W['ssd_conv_b'][l][None],
        dtb=_lane_tile(W['ssd_dt_bias'][l]), a_row=_lane_tile(-jnp.exp(W['ssd_a_log'][l])),
        d_full=jnp.repeat(W['ssd_d'][l], SSD_P)[None], ssd_norm=W['ssd_norm'][l][None],
        norm_ffn=W['norm_ffn'][l][None], fcw=W['ffn_conv_w'][l], fcb=W['ffn_conv_b'][l][None])


def _layer_fwd(x, W, l):
    n = f"l{l}_"
    cs = _layer_consts(W, l)
    win = {k: v[l] for k, v in W['w_in'].items()}
    u = _norm_fwd(x, cs['norm_mix'], n + "norm_mix")
    pqkv = _mm(u, win['p'], name=n + "in_p")
    z = _mm(u, win['z'], name=n + "in_z")
    xbc = _mm(u, win['x'], name=n + "in_x")
    gl = _mm(u, win['g'], name=n + "in_g")
    fdt = _mm(u, win['f'], name=n + "in_f")
    d, ypm = _pool_fwd(pqkv, cs['mix'], cs['scale'], n + "pool")
    yp = _mm(ypm, W['p_pool'][l], name=n + "p_pool")
    fT = fdt[:, :HEADS].T
    c = _logf_cumsum(fT, cs['f_bias'], n + "logf")
    c_row = c[:, None, :]
    qh = _to_heads(pqkv[:, ATTN_W:2 * ATTN_W] * ATTN_SCALE, MXU_DTYPE)
    kh, vh = (_to_heads(pqkv[:, (2 + j) * ATTN_W:(3 + j) * ATTN_W], MXU_DTYPE) for j in range(2))
    oh, lse = _attn_fwd(qh, kh, vh, c_row, n + "attn")
    o = _from_heads(oh)
    ya = _mm(o, W['p_attn'][l], name=n + "p_attn")
    xa, dtw = _ssd_pre_fwd(xbc, fdt, cs['cw'], cs['cb'], cs['dtb'], n + "ssd_pre")
    y, hprev = _ssd_chunk_fwd(xa, dtw, cs['a_row'], cs['d_full'], n + "ssd_scan")
    yn = _ssd_post_fwd(y, z, cs['ssd_norm'], n + "ssd_post")
    ys = _mm(yn, W['p_ssd'][l], name=n + "p_ssd")
    merged = _merge_fwd(gl, yp, ya, ys, n + "merge")
    x1 = _mm(merged, W['w_out'][l], acc=x, name=n + "w_out")
    u2 = _norm_fwd(x1, cs['norm_ffn'], n + "norm_ffn")
    hpre = _mm(u2, W['ffn_up'][l], name=n + "ffn_up")
    act = _ffn_act_fwd(hpre, cs['fcw'], cs['fcb'], n + "ffn_act")
    x2 = _mm(act, W['ffn_down'][l], acc=x1, name=n + "ffn_down")
    saved = dict(x=x, u=u, pqkv=pqkv, z=z, xbc=xbc, gl=gl, fdt=fdt, d=d, ypm=ypm, yp=yp, fT=fT,
                 c_row=c_row, qh=qh, kh=kh, vh=vh, oh=oh, o=o, lse=lse, ya=ya, xa=xa, dtw=dtw, y=y, hprev=hprev,
                 yn=yn, ys=ys, merged=merged, x1=x1, u2=u2, hpre=hpre, act=act, win=win, cs=cs)
    return x2, saved


def _layer_bwd(dx2, sv, W, l):
    n = f"l{l}_b_"
    cs, win = sv['cs'], sv['win']
    g = {}
    dact = _mm(dx2, W['ffn_down'][l], tb=True, name=n + "ffn_down_dx")
    g['ffn_down'] = _mm(sv['act'], dx2, ta=True, name=n + "ffn_down_dw")
    dhc, g['ffn_conv_w'], dfcb = _ffn_act_bwd_a(sv['hpre'], dact, cs['fcw'], cs['fcb'], n + "ffn_act_a")
    g['ffn_conv_b'] = dfcb[0]
    dhpre = _conv_bwd_b(dhc, cs['fcw'], n + "ffn_act_b")
    du2 = _mm(dhpre, W['ffn_up'][l], tb=True, name=n + "ffn_up_dx")
    g['ffn_up'] = _mm(sv['u2'], dhpre, ta=True, name=n + "ffn_up_dw")
    dx1, dnf = _norm_bwd(sv['x1'], cs['norm_ffn'], du2, dx2, n + "norm_ffn")
    g['norm_ffn'] = dnf[0]
    dm = _mm(dx1, W['w_out'][l], tb=True, name=n + "w_out_dx")
    g['w_out'] = _mm(sv['merged'], dx1, ta=True, name=n + "w_out_dw")
    dyp, dya, dys, dgl = _merge_bwd(sv['gl'], sv['yp'], sv['ya'], sv['ys'], dm, n + "merge")
    dypm = _mm(dyp, W['p_pool'][l], tb=True, name=n + "p_pool_dx")
    g['p_pool'] = _mm(sv['ypm'], dyp, ta=True, name=n + "p_pool_dw")
    dd, dscale, dmix = _pool_bwd_a(dypm, sv['d'], cs['mix'], cs['scale'], n + "pool_a")
    g['pool_scale'] = dscale[0]
    g['pool_mix'] = dmix.reshape(len(POOL_WINDOWS), LANE, LANE)
    dpool_v = _pool_bwd_b(dd, n + "pool_b")
    do = _mm(dya, W['p_attn'][l], tb=True, name=n + "p_attn_dx")
    g['p_attn'] = _mm(sv['o'], dya, ta=True, name=n + "p_attn_dw")
    dqh, dkh, dvh, dc, dcq = _attn_bwd(sv['qh'], sv['kh'], sv['vh'], _to_heads(do, F32), sv['oh'],
                                       sv['c_row'], sv['lse'], n + "attn")
    dfT, dfb = _logf_cumsum_bwd(sv['fT'], cs['f_bias'], dc[:, 0, :], dcq[:, :, 0], n + "logf")
    g['f_bias'] = dfb[:, 0]
    dpqkv = jnp.concatenate([dpool_v, (_from_heads(dqh) * ATTN_SCALE).astype(MXU_DTYPE)]
                            + [_from_heads(t).astype(MXU_DTYPE) for t in (dkh, dvh)], axis=1)
    dyn = _mm(dys, W['p_ssd'][l], tb=True, name=n + "p_ssd_dx")
    g['p_ssd'] = _mm(sv['yn'], dys, ta=True, name=n + "p_ssd_dw")
    dy, dz, dsn = _ssd_post_bwd(sv['y'], sv['z'], dyn, cs['ssd_norm'], n + "ssd_post")
    g['ssd_norm'] = dsn[0]
    dxa, ddtw, dA, dD = _ssd_chunk_bwd(sv['xa'], sv['dtw'], dy, sv['hprev'], cs['a_row'], cs['d_full'], n + "ssd_scan")
    heads = slice(DT_LANE0, DT_LANE0 + SSD_HEADS)
    g['ssd_a_log'] = dA[0, heads] * cs['a_row'][0, heads]
    g['ssd_d'] = dD[0, heads]
    dpre, ddt_raw, g['ssd_conv_w'], dcb, ddtb = _ssd_pre_bwd_a(sv['xbc'], sv['fdt'], dxa, ddtw, cs['cw'], cs['cb'],
                                                              cs['dtb'], n + "ssd_pre_a")
    g['ssd_conv_b'] = dcb[0]
    g['ssd_dt_bias'] = ddtb[0, heads]
    dxbc = _conv_bwd_b(dpre, cs['cw'], n + "ssd_pre_b")
    dfdt = jnp.concatenate([dfT.T, ddt_raw[:, HEADS:]], axis=1).astype(MXU_DTYPE)
    dsegs = dict(p=dpqkv, z=dz, x=dxbc, g=dgl, f=dfdt)
    du, dwin = None, {}
    for key in ('p', 'z', 'x', 'g', 'f'):
        du = _mm(dsegs[key], win[key], tb=True, acc=du, name=n + "in_dx_" + key)
        dwin[key] = _mm(sv['u'], dsegs[key], ta=True, name=n + "in_dw_" + key)
    g['w_in'] = dwin
    dx, dnm = _norm_bwd(sv['x'], cs['norm_mix'], du, dx1, n + "norm_mix")
    g['norm_mix'] = dnm[0]
    return dx, g


def _local_step(x, target, W):
    depth = W['norm_mix'].shape[0]
    saved = []
    h = x
    for l in range(depth):
        h, sv = _layer_fwd(h, W, l)
        saved.append(sv)
    dx, dwf, loss = _loss_head(h, W['norm_final'][None], target, "loss_head")
    grads = [None] * depth
    for l in reversed(range(depth)):
        dx, grads[l] = _layer_bwd(dx, saved[l], W, l)
    return loss[0, 0], dx, grads, dwf[0]


def _pack_rows(parts, row_align=1):
    flat = jnp.concatenate([p.reshape(-1) for p in parts])
    n = flat.shape[0]
    total = -(-n // (PACK_W * row_align)) * PACK_W * row_align
    if total > n:
        flat = jnp.concatenate([flat, jnp.zeros((total - n,), flat.dtype)])
    return flat.reshape(-1, PACK_W)


def _unpack_rows(buf, shapes):
    flat = buf.reshape(-1)
    out, pos = [], 0
    for shp in shapes:
        size = 1
        for s in shp:
            size *= s
        out.append(flat[pos:pos + size].reshape(shp))
        pos += size
    return out


def _to_place_major(gfull, name):
    R_, C = gfull.shape
    if name in COL_SHARDED:
        return gfull.reshape(R_, N_PLACES, C // N_PLACES).transpose(1, 0, 2)
    return gfull.reshape(N_PLACES, R_ // N_PLACES, C)


_W_IN_LAYOUT = (('p', 0, 0, 2048), ('f', 0, 2048, HEADS), ('z', 0, 2056, 1024), ('x', 0, 3080, 1536),
                ('f', DT_LANE0, 4616, SSD_HEADS), ('g', 0, 4632, 3072))


def _w_in_segments(slabs):
    starts = [0]
    for s in slabs:
        starts.append(starts[-1] + s.shape[-1])

    def cols(a, b):
        parts = []
        for s, s0 in zip(slabs, starts):
            lo, hi = max(a, s0), min(b, s0 + s.shape[-1])
            if lo < hi:
                parts.append(s[..., lo - s0:hi - s0])
        return parts[0] if len(parts) == 1 else jnp.concatenate(parts, axis=-1)

    pad = jnp.zeros(slabs[0].shape[:-1] + (LANE - DT_LANE0 - SSD_HEADS,), slabs[0].dtype)
    return dict(p=cols(0, 2048), z=cols(2056, 3080), x=cols(3080, 4616), g=cols(4632, 7704),
                f=jnp.concatenate([cols(2048, 2056), cols(4616, 4632), pad], axis=-1))


def _w_in_columns(segs, a, b):
    parts = []
    for key, s0, g0, w in _W_IN_LAYOUT:
        lo, hi = max(a, g0), min(b, g0 + w)
        if lo < hi:
            parts.append(segs[key][..., s0 + lo - g0:s0 + hi - g0])
    return parts[0] if len(parts) == 1 else jnp.concatenate(parts, axis=-1)


def kernel(x, norm_mix, w_in, pool_mix, pool_scale, f_bias, ssd_conv_w, ssd_conv_b, ssd_dt_bias, ssd_a_log, ssd_d, ssd_norm, p_pool, p_attn, p_ssd, w_out, norm_ffn, ffn_up, ffn_conv_w, ffn_conv_b, ffn_down, norm_final, loss_target, m_norm_mix, m_w_in, m_pool_mix, m_pool_scale, m_f_bias, m_ssd_conv_w, m_ssd_conv_b, m_ssd_dt_bias, m_ssd_a_log, m_ssd_d, m_ssd_norm, m_p_pool, m_p_attn, m_p_ssd, m_w_out, m_norm_ffn, m_ffn_up, m_ffn_conv_w, m_ffn_conv_b, m_ffn_down, m_norm_final, v_norm_mix, v_w_in, v_pool_mix, v_pool_scale, v_f_bias, v_ssd_conv_w, v_ssd_conv_b, v_ssd_dt_bias, v_ssd_a_log, v_ssd_d, v_ssd_norm, v_p_pool, v_p_attn, v_p_ssd, v_w_out, v_norm_ffn, v_ffn_up, v_ffn_conv_w, v_ffn_conv_b, v_ffn_down, v_norm_final):
    args = dict(locals())
    w_sh = {k: args[k] for k in WEIGHTS}
    m_sh = {k: args['m_' + k] for k in WEIGHTS}
    v_sh = {k: args['v_' + k] for k in WEIGHTS}
    depth = norm_mix.shape[0]
    place = 2 * lax.axis_index("x") + lax.axis_index("y")
    row_sharded = [k for k in BIG if k not in COL_SHARDED]

    gathered = _gather_places([w_sh[k].astype(MXU_DTYPE) for k in BIG] + [w_sh[k] for k in SMALL_SHARDED],
                              [k in row_sharded for k in BIG] + [False] * len(SMALL_SHARDED), "gather_weights")
    gathered = dict(zip(BIG + SMALL_SHARDED, gathered))
    W = {k: w_sh[k] for k in SMALL if k not in SMALL_SHARDED}
    for k in BIG + SMALL_SHARDED:
        gk = gathered[k]
        if k in row_sharded:
            W[k] = gk.reshape(gk.shape[0], -1, gk.shape[-1])
        elif k == 'w_in':
            W[k] = _w_in_segments([gk[p] for p in range(N_PLACES)])
        else:
            W[k] = jnp.concatenate([gk[p] for p in range(N_PLACES)], axis=-1)

    loss_local, grad_x, grads, g_final = _local_step(x[0], loss_target[0], W)
    loss = lax.psum(loss_local, ("x", "y", "c"))

    def place_major(k, l):
        if k == 'w_in':
            c = IN_TOTAL // N_PLACES
            return jnp.stack([_w_in_columns(grads[l][k], p * c, (p + 1) * c) for p in range(N_PLACES)])
        return _to_place_major(grads[l][k], k)

    g_big = [jnp.stack([place_major(k, l) for l in range(depth)], axis=1) for k in BIG]
    small_names = [k for k in SMALL if k != 'norm_final'] + ['norm_final']
    small_full = [jnp.stack([grads[l][k] for l in range(depth)]) for k in small_names[:-1]] + [g_final]
    small_full_shapes = [a.shape for a in small_full]
    small_packed = _pack_rows(small_full, 32)
    g_small = jnp.broadcast_to(small_packed.reshape(1, 2, -1, PACK_W),
                               (N_PLACES, 2, small_packed.shape[0] // 2, PACK_W))

    core = lax.axis_index("c")
    pos = jnp.stack([place, core]).astype(jnp.int32)
    g_all = g_big + [g_small]
    theirs = _reduce_sibling(g_all, "reduce_sibling")
    halves = [_add_own_slot(g, t, pos, f"reduce_sibling_add{j}") for j, (g, t) in enumerate(zip(g_all, theirs))]
    recv = _reduce_places(halves, "reduce_places")
    qs = [_sum_places(h, r, pos, f"reduce_places_add{j}") for j, (h, r) in enumerate(zip(halves, recv))]
    others = _swap_sibling(qs, "reduce_swap")

    def mine(k, a):
        if k in SMALL_SHARDED:
            c = a.shape[-1] // N_PLACES
            return lax.dynamic_slice_in_dim(a, place * c, c, axis=a.ndim - 1)
        return a

    outs = {}
    for j, k in enumerate(BIG):
        res = _adamw_pair(qs[j], others[j], w_sh[k], m_sh[k], v_sh[k], pos, "adamw_" + k)
        for prefix, a in zip(('grad_', 'delta_', 'new_m_', 'new_v_'), res):
            outs[prefix + k] = a
    small_sum = jnp.where(core == 0, jnp.concatenate([qs[-1], others[-1]]), jnp.concatenate([others[-1], qs[-1]]))
    g_small_list = [mine(k, a) for k, a in zip(small_names, _unpack_rows(small_sum, small_full_shapes))]
    shapes = [a.shape for a in g_small_list]
    gp = _pack_rows(g_small_list, 128)
    wp, mp, vp = (_pack_rows([d[k] for k in small_names], 128) for d in (w_sh, m_sh, v_sh))
    delta_p, m_p, v_p = _adamw(gp, wp, mp, vp, "adamw_small")
    for prefix, buf in (('grad_', gp), ('delta_', delta_p), ('new_m_', m_p), ('new_v_', v_p)):
        for k, a in zip(small_names, _unpack_rows(buf, shapes)):
            outs[prefix + k] = a
    result = [loss, grad_x[None]]
    for prefix in ('grad_', 'delta_', 'new_m_', 'new_v_'):
        result += [outs[prefix + k] for k in WEIGHTS]
    return tuple(result)
```

```python
import functools

import jax
import jax.numpy as jnp
from jax import lax
from jax.experimental import pallas as pl
from jax.experimental.pallas import tpu as pltpu

F32 = jnp.float32
MXU_DTYPE = jnp.bfloat16
WIRE_DTYPE = jnp.bfloat16
NORM_EPS = 1e-6
HALO = 16
LANE = 128
NEG_BIG = -1e30
VMEM_LIMIT = 52 * 1024 * 1024

D_MODEL = 1024
POOL_WINDOWS = (2, 4, 8, 16)
POOL_W = 512
HEADS = 8
HEAD_DIM = 64
ATTN_W = 512
ATTN_SCALE = HEAD_DIM ** -0.5
SSD_W = 1024
SSD_HEADS = 16
SSD_P = 64
SSD_N = 128
SSD_CHUNK = 128
SSD_CONV_CH = 1536
FFN = 2816
DT_LANE0 = 8
IN_SPLITS = (512, 512, 512, 512, 8, 1024, 1536, 16, 3072)
IN_TOTAL = sum(IN_SPLITS)
N_PLACES = 4

ADAM_LR, ADAM_B1, ADAM_B2, ADAM_EPS, ADAM_WD, ADAM_STEP = 0.001, 0.9, 0.999, 1e-08, 0.01, 10

BIG = ('w_in', 'p_pool', 'p_attn', 'p_ssd', 'w_out', 'ffn_up', 'ffn_down')
COL_SHARDED = ('w_in', 'p_pool', 'p_attn', 'ffn_up')
SMALL = ('norm_mix', 'pool_mix', 'pool_scale', 'f_bias', 'ssd_conv_w', 'ssd_conv_b', 'ssd_dt_bias',
         'ssd_a_log', 'ssd_d', 'ssd_norm', 'norm_ffn', 'ffn_conv_w', 'ffn_conv_b', 'norm_final')
SMALL_SHARDED = ('ssd_conv_w', 'ffn_conv_w')
WEIGHTS = ('norm_mix', 'w_in', 'pool_mix', 'pool_scale', 'f_bias', 'ssd_conv_w', 'ssd_conv_b', 'ssd_dt_bias',
           'ssd_a_log', 'ssd_d', 'ssd_norm', 'p_pool', 'p_attn', 'p_ssd', 'w_out', 'norm_ffn', 'ffn_up',
           'ffn_conv_w', 'ffn_conv_b', 'ffn_down', 'norm_final')
PACK_W = 1024


def _params(sem):
    return pltpu.CompilerParams(dimension_semantics=sem, vmem_limit_bytes=VMEM_LIMIT)


def _tile(n, prefs=(512, 256, 128)):
    for t in prefs:
        if n % t == 0:
            return t
    return n


def _sigmoid(x):
    return 0.5 * jnp.tanh(0.5 * x) + 0.5


def _softplus(x):
    return jnp.maximum(x, 0.0) + jnp.log1p(jnp.exp(-jnp.abs(x)))


def _dot(a, b, dims=((1,), (0,))):
    return lax.dot_general(a.astype(MXU_DTYPE), b.astype(MXU_DTYPE), (dims, ((), ())),
                           preferred_element_type=F32)


NT = ((1,), (1,))


def _mm(a, b, *, ta=False, tb=False, acc=None, out_dtype=F32, name):
    M, K = (a.shape[1], a.shape[0]) if ta else a.shape
    N = b.shape[0] if tb else b.shape[1]
    big = (1024, 1408, 512, 256, 128)
    tm, tn = _tile(M, big), _tile(N, big)
    tk = K if K <= 1024 else _tile(K, (512, 256, 128) if ta else big)
    nk = K // tk
    a_spec = pl.BlockSpec((tk, tm), lambda i, j, k: (k, i)) if ta else pl.BlockSpec((tm, tk), lambda i, j, k: (i, k))
    b_spec = pl.BlockSpec((tn, tk), lambda i, j, k: (j, k)) if tb else pl.BlockSpec((tk, tn), lambda i, j, k: (k, j))
    in_specs = [a_spec, b_spec]
    args = [a, b]
    if acc is not None:
        in_specs.append(pl.BlockSpec((tm, tn), lambda i, j, k: (i, j)))
        args.append(acc)

    def body(*refs):
        if acc is not None:
            a_ref, b_ref, c_ref, o_ref, acc_ref = refs
        else:
            a_ref, b_ref, o_ref, acc_ref = refs
        k = pl.program_id(2)

        @pl.when(k == 0)
        def _():
            if acc is not None:
                acc_ref[...] = c_ref[...].astype(F32)
            else:
                acc_ref[...] = jnp.zeros_like(acc_ref)

        av = a_ref[...]
        if ta:
            av = av.astype(F32).T
        acc_ref[...] += _dot(av, b_ref[...], NT if tb else ((1,), (0,)))

        @pl.when(k == nk - 1)
        def _():
            o_ref[...] = acc_ref[...].astype(out_dtype)

    return pl.pallas_call(
        body, grid=(M // tm, N // tn, nk), in_specs=in_specs,
        out_specs=pl.BlockSpec((tm, tn), lambda i, j, k: (i, j)),
        out_shape=jax.ShapeDtypeStruct((M, N), out_dtype),
        scratch_shapes=[pltpu.VMEM((tm, tn), F32)],
        compiler_params=_params(("parallel", "parallel", "arbitrary")), name=name)(*args)


def _rows(body, S, T, *, rows=(), consts=(), prevs=(), nexts=(), out_rows=(), out_accs=(), scratch=(), name):
    n = S // T
    hb = T // HALO
    last_h = S // HALO - 1

    def norm(r):
        return r if isinstance(r, tuple) else (r, r.shape[1], 0)

    rows, prevs, nexts = [norm(r) for r in rows], [norm(r) for r in prevs], [norm(r) for r in nexts]
    in_specs, args = [], []
    for arr, W, cb in rows:
        in_specs.append(pl.BlockSpec((T, W), lambda i, cb=cb: (i, cb)))
        args.append(arr)
    for cst in consts:
        in_specs.append(pl.BlockSpec(cst.shape, lambda i, nd=cst.ndim: (0,) * nd))
        args.append(cst)
    for arr, W, cb in prevs:
        in_specs.append(pl.BlockSpec((HALO, W), lambda i, cb=cb: (jnp.maximum(i * hb - 1, 0), cb)))
        args.append(arr)
    for arr, W, cb in nexts:
        in_specs.append(pl.BlockSpec((HALO, W), lambda i, cb=cb: (jnp.minimum((i + 1) * hb, last_h), cb)))
        args.append(arr)
    out_specs = [pl.BlockSpec((T, W), lambda i: (i, 0)) for W, _ in out_rows]
    out_specs += [pl.BlockSpec(shp, lambda i, nd=len(shp): (0,) * nd) for shp, _ in out_accs]
    out_shape = [jax.ShapeDtypeStruct((S, W), dt) for W, dt in out_rows]
    out_shape += [jax.ShapeDtypeStruct(shp, dt) for shp, dt in out_accs]
    cuts = [len(rows), len(consts), len(prevs), len(nexts), len(out_rows), len(out_accs), len(scratch)]

    def kern(*refs):
        groups, pos = [], 0
        for c in cuts:
            groups.append(list(refs[pos:pos + c]))
            pos += c
        i = pl.program_id(0)

        @pl.when(i == 0)
        def _():
            for a_ref in groups[5]:
                a_ref[...] = jnp.zeros_like(a_ref)

        body(i, n, *groups)

    outs = pl.pallas_call(kern, grid=(n,), in_specs=in_specs, out_specs=out_specs, out_shape=out_shape,
                          scratch_shapes=list(scratch), compiler_params=_params(("arbitrary",)), name=name)(*args)
    return outs


def _fill_prev(ext, prev_ref, cur, i):
    ext[0:HALO, :] = jnp.where(i > 0, prev_ref[...].astype(F32), 0.0)
    ext[HALO:, :] = cur


def _fill_next(ext, cur, next_val, T):
    ext[0:T, :] = cur
    ext[T:, :] = next_val


def _row_ids(i, T, W=1):
    return i * T + lax.broadcasted_iota(jnp.int32, (T, W), 0)


SUB_ROWS = 32
WIN_PAD = 8


def _tile_loop(T, fn):
    n = T // SUB_ROWS
    fn(0, True, n == 1)
    if n > 2:
        def body(rb, carry):
            fn(pl.multiple_of(rb * SUB_ROWS, SUB_ROWS), False, False)
            return carry
        lax.fori_loop(1, n - 1, body, 0)
    if n > 1:
        fn((n - 1) * SUB_ROWS, False, True)


def _win_prev(x_ref, prev_ref, i, r0, first, cols):
    if first:
        top = jnp.where(i > 0, prev_ref[HALO - WIN_PAD:HALO, cols].astype(F32), 0.0)
        return jnp.concatenate([top, x_ref[0:SUB_ROWS, cols].astype(F32)], axis=0)
    start = r0 - WIN_PAD if isinstance(r0, int) else pl.multiple_of(r0 - WIN_PAD, WIN_PAD)
    return x_ref[pl.ds(start, SUB_ROWS + WIN_PAD), cols].astype(F32)


def _behind(win, j):
    return win[WIN_PAD:, :] if j == 0 else pltpu.roll(win, j, axis=0)[WIN_PAD:, :]


def _win_next(x_ref, next_ref, i, n, r0, last, cols):
    if last:
        bot = jnp.where(i < n - 1, next_ref[0:WIN_PAD, cols].astype(F32), 0.0)
        return jnp.concatenate([x_ref[r0:r0 + SUB_ROWS, cols].astype(F32), bot], axis=0)
    return x_ref[pl.ds(r0, SUB_ROWS + WIN_PAD), cols].astype(F32)


def _ahead(win, j):
    return win[:SUB_ROWS, :] if j == 0 else pltpu.roll(win, SUB_ROWS + WIN_PAD - j, axis=0)[:SUB_ROWS, :]


def _conv_win(win, w_ref, b_ref, cols):
    K = w_ref.shape[0]
    out = b_ref[:, cols]
    for k in range(K):
        out = out + _behind(win, K - 1 - k) * w_ref[k:k + 1, cols]
    return out


def _conv_wgrad_win(acc_ref, win, d, cols):
    K = acc_ref.shape[0]
    for k in range(K):
        acc_ref[k:k + 1, cols] += jnp.sum(d * _behind(win, K - 1 - k), axis=0, keepdims=True)


def _norm_fwd(x, w, name):
    S, D = x.shape

    def body(i, n, R, C, P, N, O, A, Sc):
        xv = R[0][...]
        r = lax.rsqrt(jnp.mean(xv * xv, axis=-1, keepdims=True) + NORM_EPS)
        O[0][...] = (xv * r * C[0][...]).astype(MXU_DTYPE)

    return _rows(body, S, _tile(S), rows=[x], consts=[w], out_rows=[(D, MXU_DTYPE)], name=name)[0]


def _norm_bwd_math(xv, w, du):
    r = lax.rsqrt(jnp.mean(xv * xv, axis=-1, keepdims=True) + NORM_EPS)
    xh = xv * r
    g = du * w
    dx = r * (g - xh * jnp.mean(g * xh, axis=-1, keepdims=True))
    dw = jnp.sum(du * xh, axis=0, keepdims=True)
    return dx, dw


def _norm_bwd(x, w, du, dres, name):
    S, D = x.shape

    def body(i, n, R, C, P, N, O, A, Sc):
        dx, dw = _norm_bwd_math(R[0][...], C[0][...], R[1][...])
        O[0][...] = R[2][...] + dx
        A[0][...] += dw

    return _rows(body, S, _tile(S), rows=[x, du, dres], consts=[w], out_rows=[(D, F32)],
                 out_accs=[((1, D), F32)], name=name)


def _loss_head(x, w, target, name):
    S, D = x.shape

    def body(i, n, R, C, P, N, O, A, Sc):
        xv, w_, tg = R[0][...], C[0][...], R[1][...]
        r = lax.rsqrt(jnp.mean(xv * xv, axis=-1, keepdims=True) + NORM_EPS)
        e = xv * r * w_ - tg
        A[1][...] += jnp.broadcast_to(0.5 * jnp.sum(jnp.mean(e * e, axis=-1, keepdims=True)), (1, LANE))
        dx, dw = _norm_bwd_math(xv, w_, e / D)
        O[0][...] = dx
        A[0][...] += dw

    return _rows(body, S, _tile(S), rows=[x, target], consts=[w], out_rows=[(D, F32)],
                 out_accs=[((1, D), F32), ((1, LANE), F32)], name=name)


def _pool_fwd(pqkv, mix, scale, name):
    S = pqkv.shape[0]
    T = _tile(S, (256, 128))

    def body(i, n, R, C, P, N, O, A, Sc):
        ext = Sc[0]
        v = R[0][...]
        _fill_prev(ext, P[0], v, i)
        t1 = (_row_ids(i, T) + 1).astype(F32)
        for g, w in enumerate(POOL_WINDOWS):
            cols = slice(g * LANE, (g + 1) * LANE)
            acc = v[:, cols]
            for j in range(1, w):
                acc = acc + ext[pl.ds(HALO - j, T), cols]
            d = (acc / jnp.minimum(t1, float(w)) - v[:, cols]).astype(MXU_DTYPE)
            O[0][:, cols] = d
            O[1][:, cols] = (_dot(d, C[0][g]) * C[1][:, cols]).astype(MXU_DTYPE)

    return _rows(body, S, T, rows=[(pqkv, POOL_W, 0)], prevs=[(pqkv, POOL_W, 0)], consts=[mix, scale],
                 out_rows=[(POOL_W, MXU_DTYPE), (POOL_W, MXU_DTYPE)],
                 scratch=[pltpu.VMEM((HALO + T, POOL_W), F32)], name=name)


def _pool_bwd_a(dypm, d, mix, scale, name):
    S = d.shape[0]
    T = _tile(S, (256, 128))

    def body(i, n, R, C, P, N, O, A, Sc):
        for g in range(len(POOL_WINDOWS)):
            cols = slice(g * LANE, (g + 1) * LANE)
            dg = R[1][:, cols]
            dy = R[0][:, cols]
            yg = _dot(dg, C[0][g])
            A[0][:, cols] += jnp.sum(dy * yg, axis=0, keepdims=True)
            dys = dy * C[1][:, cols]
            A[1][cols, :] += _dot(dg.astype(F32).T, dys)
            O[0][:, cols] = _dot(dys, C[0][g], NT)

    return _rows(body, S, T, rows=[dypm, d], consts=[mix, scale], out_rows=[(POOL_W, F32)],
                 out_accs=[((1, POOL_W), F32), ((POOL_W, LANE), F32)], name=name)


def _pool_bwd_b(dd, name):
    S = dd.shape[0]
    T = _tile(S, (256, 128))

    def body(i, n, R, C, P, N, O, A, Sc):
        ext = Sc[0]
        ddv = R[0][...]
        t1 = (_row_ids(i, T) + 1).astype(F32)
        nxt = jnp.where(i < n - 1, N[0][...], 0.0)
        for g, w in enumerate(POOL_WINDOWS):
            cols = slice(g * LANE, (g + 1) * LANE)
            ext[0:T, cols] = ddv[:, cols] / jnp.minimum(t1, float(w))
            ext[T:, cols] = nxt[:, cols] / float(w)
        for g, w in enumerate(POOL_WINDOWS):
            cols = slice(g * LANE, (g + 1) * LANE)
            acc = ext[0:T, cols]
            for j in range(1, w):
                acc = acc + ext[pl.ds(j, T), cols]
            O[0][:, cols] = (acc - ddv[:, cols]).astype(MXU_DTYPE)

    return _rows(body, S, T, rows=[dd], nexts=[dd], out_rows=[(POOL_W, MXU_DTYPE)],
                 scratch=[pltpu.VMEM((T + HALO, POOL_W), F32)], name=name)[0]


def _lane_cumsum(seg, reverse=False):
    lane = lax.broadcasted_iota(jnp.int32, seg.shape, 1)
    sh = 1
    while sh < LANE:
        if reverse:
            seg = seg + jnp.where(lane < LANE - sh, pltpu.roll(seg, LANE - sh, axis=1), 0.0)
        else:
            seg = seg + jnp.where(lane >= sh, pltpu.roll(seg, sh, axis=1), 0.0)
        sh *= 2
    return seg


def _logf_cumsum(fT, bias, name):
    H, S = fT.shape
    TB = _tile(S)
    nb = S // TB

    def body(f_ref, b_ref, o_ref, carry):
        @pl.when(pl.program_id(0) == 0)
        def _():
            carry[...] = jnp.zeros_like(carry)

        x = f_ref[...] + b_ref[...]
        lf = jnp.minimum(x, 0.0) - jnp.log1p(jnp.exp(-jnp.abs(x)))
        c = carry[...]
        for j in range(TB // LANE):
            seg = _lane_cumsum(lf[:, j * LANE:(j + 1) * LANE]) + c
            o_ref[:, j * LANE:(j + 1) * LANE] = seg
            c = seg[:, LANE - 1:LANE]
        carry[...] = c

    return pl.pallas_call(
        body, grid=(nb,), in_specs=[pl.BlockSpec((H, TB), lambda i: (0, i)), pl.BlockSpec((H, 1), lambda i: (0, 0))],
        out_specs=pl.BlockSpec((H, TB), lambda i: (0, i)), out_shape=jax.ShapeDtypeStruct((H, S), F32),
        scratch_shapes=[pltpu.VMEM((H, 1), F32)], compiler_params=_params(("arbitrary",)), name=name)(fT, bias)


def _logf_cumsum_bwd(fT, bias, dc, dcq, name):
    H, S = fT.shape
    TB = _tile(S)
    nb = S // TB

    def body(f_ref, b_ref, dc_ref, dcq_ref, o_ref, db_ref, carry):
        @pl.when(pl.program_id(0) == 0)
        def _():
            carry[...] = jnp.zeros_like(carry)
            db_ref[...] = jnp.zeros_like(db_ref)

        x = f_ref[...] + b_ref[...]
        sg = _sigmoid(-x)
        dcv = dc_ref[...] + dcq_ref[...]
        c = carry[...]
        db = jnp.zeros((H, 1), F32)
        for j in reversed(range(TB // LANE)):
            seg = _lane_cumsum(dcv[:, j * LANE:(j + 1) * LANE], reverse=True) + c
            df = seg * sg[:, j * LANE:(j + 1) * LANE]
            o_ref[:, j * LANE:(j + 1) * LANE] = df
            db = db + jnp.sum(df, axis=1, keepdims=True)
            c = seg[:, 0:1]
        carry[...] = c
        db_ref[...] += db

    rev = lambda i: (0, nb - 1 - i)
    return pl.pallas_call(
        body, grid=(nb,),
        in_specs=[pl.BlockSpec((H, TB), rev), pl.BlockSpec((H, 1), lambda i: (0, 0)), pl.BlockSpec((H, TB), rev),
                  pl.BlockSpec((H, TB), rev)],
        out_specs=[pl.BlockSpec((H, TB), rev), pl.BlockSpec((H, 1), lambda i: (0, 0))],
        out_shape=[jax.ShapeDtypeStruct((H, S), F32), jax.ShapeDtypeStruct((H, 1), F32)],
        scratch_shapes=[pltpu.VMEM((H, 1), F32)], compiler_params=_params(("arbitrary",)), name=name)(
            fT, bias, dc, dcq)


def _attn_scores(q, k, ck, diagonal, T):
    s = _dot(q, k, NT) - ck
    if diagonal:
        tril = lax.broadcasted_iota(jnp.int32, (T, T), 1) <= lax.broadcasted_iota(jnp.int32, (T, T), 0)
        s = jnp.where(tril, s, NEG_BIG)
    return s


def _attn_fwd(q, k, v, c_row, name):
    H, S, Dh = q.shape
    T = _tile(S, (1024, 512, 256, 128))
    nq = S // T

    def body(q_ref, k_ref, v_ref, ck_ref, o_ref, lse_ref, m_s, l_s, acc_s):
        qi, ki = pl.program_id(1), pl.program_id(2)

        @pl.when(ki == 0)
        def _():
            m_s[...] = jnp.full_like(m_s, NEG_BIG)
            l_s[...] = jnp.zeros_like(l_s)
            acc_s[...] = jnp.zeros_like(acc_s)

        def step(diagonal):
            s = _attn_scores(q_ref[0], k_ref[0], ck_ref[0], diagonal, T)
            m_new = jnp.maximum(m_s[...], jnp.max(s, axis=1, keepdims=True))
            alpha = jnp.exp(m_s[...] - m_new)
            p = jnp.exp(s - m_new)
            l_s[...] = alpha * l_s[...] + jnp.sum(p, axis=1, keepdims=True)
            acc_s[...] = alpha * acc_s[...] + _dot(p, v_ref[0])
            m_s[...] = m_new

        @pl.when(ki < qi)
        def _():
            step(False)

        @pl.when(ki == qi)
        def _():
            step(True)
            o_ref[0] = acc_s[...] / l_s[...]
            lse_ref[0] = m_s[...] + jnp.log(l_s[...])

    qmap = lambda h, qi, ki: (h, qi, 0)
    kmap = lambda h, qi, ki: (h, jnp.minimum(ki, qi), 0)
    return pl.pallas_call(
        body, grid=(H, nq, nq),
        in_specs=[pl.BlockSpec((1, T, Dh), qmap), pl.BlockSpec((1, T, Dh), kmap), pl.BlockSpec((1, T, Dh), kmap),
                  pl.BlockSpec((1, 1, T), lambda h, qi, ki: (h, 0, jnp.minimum(ki, qi)))],
        out_specs=[pl.BlockSpec((1, T, Dh), qmap), pl.BlockSpec((1, T, 1), qmap)],
        out_shape=[jax.ShapeDtypeStruct((H, S, Dh), F32), jax.ShapeDtypeStruct((H, S, 1), F32)],
        scratch_shapes=[pltpu.VMEM((T, 1), F32), pltpu.VMEM((T, 1), F32), pltpu.VMEM((T, Dh), F32)],
        compiler_params=_params(("parallel", "parallel", "arbitrary")), name=name)(q, k, v, c_row)


def _attn_bwd(q, k, v, do, o, c_row, lse, name):
    H, S, Dh = q.shape
    T = _tile(S, (1024, 512, 256, 128))
    nq = S // T

    def body(q_ref, k_ref, v_ref, do_ref, o_ref, ck_ref, lse_ref, dq_ref, dk_ref, dv_ref, dc_ref, dcq_ref,
             dk_s, dv_s, dc_s):
        ki, qi = pl.program_id(1), pl.program_id(2)

        @pl.when((ki == 0) & (qi == 0))
        def _():
            dq_ref[...] = jnp.zeros_like(dq_ref)
            dcq_ref[...] = jnp.zeros_like(dcq_ref)

        @pl.when(qi == 0)
        def _():
            dk_s[...] = jnp.zeros_like(dk_s)
            dv_s[...] = jnp.zeros_like(dv_s)
            dc_s[...] = jnp.zeros_like(dc_s)

        def step(diagonal):
            qv, kv, vv = q_ref[0], k_ref[0], v_ref[0]
            s = _attn_scores(qv, kv, ck_ref[0], diagonal, T)
            p = jnp.exp(s - lse_ref[0])
            dov = do_ref[0]
            delta = jnp.sum(dov * o_ref[0], axis=1, keepdims=True)
            dv_s[...] += _dot(p.T, dov)
            dp = _dot(dov, vv, NT)
            ds = p * (dp - delta)
            dc_s[...] -= jnp.sum(ds, axis=0, keepdims=True)
            rows = pl.ds(pl.multiple_of(qi * T, T), T)
            dq_ref[0, rows, :] += _dot(ds, kv)
            dcq_ref[0, rows, :] += jnp.sum(ds, axis=1, keepdims=True)
            dk_s[...] += _dot(ds.T, qv)

        @pl.when(qi > ki)
        def _():
            step(False)

        @pl.when(qi == ki)
        def _():
            step(True)

        @pl.when(qi == nq - 1)
        def _():
            dk_ref[0] = dk_s[...]
            dv_ref[0] = dv_s[...]
            dc_ref[0] = dc_s[...]

    qmap = lambda h, ki, qi: (h, jnp.maximum(qi, ki), 0)
    kmap = lambda h, ki, qi: (h, ki, 0)
    return pl.pallas_call(
        body, grid=(H, nq, nq),
        in_specs=[pl.BlockSpec((1, T, Dh), qmap), pl.BlockSpec((1, T, Dh), kmap), pl.BlockSpec((1, T, Dh), kmap),
                  pl.BlockSpec((1, T, Dh), qmap), pl.BlockSpec((1, T, Dh), qmap),
                  pl.BlockSpec((1, 1, T), lambda h, ki, qi: (h, 0, ki)), pl.BlockSpec((1, T, 1), qmap)],
        out_specs=[pl.BlockSpec((1, S, Dh), lambda h, ki, qi: (h, 0, 0)), pl.BlockSpec((1, T, Dh), kmap),
                   pl.BlockSpec((1, T, Dh), kmap), pl.BlockSpec((1, 1, T), lambda h, ki, qi: (h, 0, ki)),
                   pl.BlockSpec((1, S, 1), lambda h, ki, qi: (h, 0, 0))],
        out_shape=[jax.ShapeDtypeStruct((H, S, Dh), F32), jax.ShapeDtypeStruct((H, S, Dh), F32),
                   jax.ShapeDtypeStruct((H, S, Dh), F32), jax.ShapeDtypeStruct((H, 1, S), F32),
                   jax.ShapeDtypeStruct((H, S, 1), F32)],
        scratch_shapes=[pltpu.VMEM((T, Dh), F32), pltpu.VMEM((T, Dh), F32), pltpu.VMEM((1, T), F32)],
        compiler_params=_params(("parallel", "arbitrary", "arbitrary")), name=name)(
            q, k, v, do, o, c_row, lse)


CONV_COLS = 512


def _conv_bwd_b(dpre, w, name):
    S, C = dpre.shape
    K = w.shape[0]
    T = _tile(S, (256, 128))

    def body(i, n, R, Cs, P, N, O, A, Sc):
        def tile(r0, first, last):
            for c0 in range(0, C, CONV_COLS):
                cols = slice(c0, c0 + CONV_COLS)
                win = _win_next(R[0], N[0], i, n, r0, last, cols)
                out = None
                for k in range(K):
                    term = _ahead(win, K - 1 - k) * Cs[0][k:k + 1, cols]
                    out = term if out is None else out + term
                O[0][pl.ds(r0, SUB_ROWS), cols] = out.astype(MXU_DTYPE)

        _tile_loop(T, tile)

    return _rows(body, S, T, rows=[dpre], nexts=[dpre], consts=[w], out_rows=[(C, MXU_DTYPE)], name=name)[0]


def _dt_mask():
    lane = lax.broadcasted_iota(jnp.int32, (1, LANE), 1)
    return ((lane >= DT_LANE0) & (lane < DT_LANE0 + SSD_HEADS)).astype(F32)


def _ssd_pre_fwd(xbc, fdt, cw, cb, dtb, name):
    S, C = xbc.shape
    T = _tile(S, (256, 128))
    K = cw.shape[0]

    def body(i, n, R, Cs, P, N, O, A, Sc):
        def tile(r0, first, last):
            rows = pl.ds(r0, SUB_ROWS)
            for c0 in range(0, C, CONV_COLS):
                cols = slice(c0, c0 + CONV_COLS)
                pre = _conv_win(_win_prev(R[0], P[0], i, r0, first, cols), Cs[0], Cs[1], cols)
                O[0][rows, cols] = pre * _sigmoid(pre)
            O[1][rows, :] = _softplus(R[1][rows, :] + Cs[2][...]) * _dt_mask()

        _tile_loop(T, tile)

    return _rows(body, S, T, rows=[xbc, fdt], prevs=[xbc], consts=[cw, cb, dtb],
                 out_rows=[(C, F32), (LANE, F32)], name=name)


def _silu_grad(pre):
    sg = _sigmoid(pre)
    return sg * (1.0 + pre * (1.0 - sg))


def _ssd_pre_bwd_a(xbc, fdt, dxa, ddtw, cw, cb, dtb, name):
    S, C = xbc.shape
    T = _tile(S, (256, 128))
    K = cw.shape[0]

    def body(i, n, R, Cs, P, N, O, A, Sc):
        def tile(r0, first, last):
            rows = pl.ds(r0, SUB_ROWS)
            for c0 in range(0, C, CONV_COLS):
                cols = slice(c0, c0 + CONV_COLS)
                win = _win_prev(R[0], P[0], i, r0, first, cols)
                dpre = R[2][rows, cols] * _silu_grad(_conv_win(win, Cs[0], Cs[1], cols))
                O[0][rows, cols] = dpre
                _conv_wgrad_win(A[0], win, dpre, cols)
                A[1][:, cols] += jnp.sum(dpre, axis=0, keepdims=True)
            ddt = R[3][rows, :] * _sigmoid(R[1][rows, :] + Cs[2][...]) * _dt_mask()
            O[1][rows, :] = ddt
            A[2][...] += jnp.sum(ddt, axis=0, keepdims=True)

        _tile_loop(T, tile)

    return _rows(body, S, T, rows=[xbc, fdt, dxa, ddtw], prevs=[xbc], consts=[cw, cb, dtb],
                 out_rows=[(C, F32), (LANE, F32)],
                 out_accs=[((K, C), F32), ((1, C), F32), ((1, LANE), F32)], name=name)


def _split3(x):
    hi = x.astype(jnp.bfloat16)
    r1 = x - hi.astype(F32)
    mid = r1.astype(jnp.bfloat16)
    lo = (r1 - mid.astype(F32)).astype(jnp.bfloat16)
    return hi, mid, lo


def _expand_mat():
    r = lax.broadcasted_iota(jnp.int32, (LANE, SSD_W), 0)
    c = lax.broadcasted_iota(jnp.int32, (LANE, SSD_W), 1)
    return (r - DT_LANE0 == c // SSD_P).astype(jnp.bfloat16)


def _headsum_mat():
    r = lax.broadcasted_iota(jnp.int32, (SSD_W, LANE), 0)
    c = lax.broadcasted_iota(jnp.int32, (SSD_W, LANE), 1)
    return (c - DT_LANE0 == r // SSD_P).astype(jnp.bfloat16)


def _expand(tile, ex):
    return sum(lax.dot_general(part, ex, (((1,), (0,)), ((), ())), preferred_element_type=F32)
               for part in _split3(tile))


def _headsum(full, hs):
    return sum(lax.dot_general(part, hs, (((1,), (0,)), ((), ())), preferred_element_type=F32)
               for part in _split3(full))


def _sub_cumsum(a, reverse=False):
    n = a.shape[0]
    row = lax.broadcasted_iota(jnp.int32, a.shape, 0)
    sh = 1
    while sh < n:
        if reverse:
            a = a + jnp.where(row < n - sh, pltpu.roll(a, n - sh, axis=0), 0.0)
        else:
            a = a + jnp.where(row >= sh, pltpu.roll(a, sh, axis=0), 0.0)
        sh *= 2
    return a


def _chunk_common(xa_ref, dtw_ref, a_row, ex):
    L = SSD_CHUNK
    xs = xa_ref[:, 0:SSD_W]
    dtv = dtw_ref[...]
    acs = _sub_cumsum(dtv * a_row)
    last = acs[L - 1:L, :]
    dt_full = _expand(dtv, ex)
    xd = xs * dt_full
    dec_full = _expand(jnp.exp(last - acs), ex)
    e_full = _expand(jnp.exp(acs), ex)
    elast_full = _expand(jnp.broadcast_to(jnp.exp(last), (8, LANE)), ex)[0:1, :]
    return xs, dtv, acs, last, dt_full, xd, dec_full, e_full, elast_full


def _decay_mask(acs, acsT, col):
    L = SSD_CHUNK
    diff = acs[:, col:col + 1] - acsT[col:col + 1, :]
    tril = lax.broadcasted_iota(jnp.int32, (L, L), 0) >= lax.broadcasted_iota(jnp.int32, (L, L), 1)
    return jnp.where(tril, jnp.exp(jnp.minimum(diff, 0.0)), 0.0)


def _half_mask(h):
    lane = lax.broadcasted_iota(jnp.int32, (1, LANE), 1)
    return ((lane // SSD_P) == (h % 2)).astype(F32)


def _ssd_chunk_fwd(xa, dtw, a_row, d_full, name):
    S = xa.shape[0]
    L, G = SSD_CHUNK, 2
    nc = S // L
    GW = SSD_W // G

    def body(xa_ref, dtw_ref, a_ref, d_ref, y_ref, hp_ref, state):
        @pl.when(pl.program_id(0) == 0)
        def _():
            state[...] = jnp.zeros_like(state)

        ex = _expand_mat()
        xs, dtv, acs, last, dt_full, xd, dec_full, e_full, elast_full = _chunk_common(xa_ref, dtw_ref, a_ref[...], ex)
        acsT = acs.T
        hp_ref[0] = state[...]
        for g in range(G):
            gc = slice(g * GW, (g + 1) * GW)
            Bg = xa_ref[:, SSD_W + g * SSD_N: SSD_W + (g + 1) * SSD_N]
            Cg = xa_ref[:, SSD_W + G * SSD_N + g * SSD_N: SSD_W + G * SSD_N + (g + 1) * SSD_N]
            cb = _dot(Cg, Bg, NT)
            y_off = e_full[:, gc] * _dot(Cg, state[:, gc])
            for hp in range(GW // LANE):
                pc = slice(g * GW + hp * LANE, g * GW + (hp + 1) * LANE)
                xd_pair = xd[:, pc]
                yp = y_off[:, hp * LANE:(hp + 1) * LANE] + d_ref[:, pc] * xs[:, pc]
                for h2 in range(2):
                    h = (g * GW + hp * LANE) // SSD_P + h2
                    m = cb * _decay_mask(acs, acsT, DT_LANE0 + h)
                    yp = yp + _dot(m, xd_pair * _half_mask(h))
                y_ref[:, pc] = yp
            st_new = _dot(Bg.T, xd[:, gc] * dec_full[:, gc])
            state[:, gc] = elast_full[:, gc] * state[:, gc] + st_new

    return pl.pallas_call(
        body, grid=(nc,),
        in_specs=[pl.BlockSpec((L, SSD_CONV_CH), lambda c: (c, 0)), pl.BlockSpec((L, LANE), lambda c: (c, 0)),
                  pl.BlockSpec((1, LANE), lambda c: (0, 0)), pl.BlockSpec((1, SSD_W), lambda c: (0, 0))],
        out_specs=[pl.BlockSpec((L, SSD_W), lambda c: (c, 0)), pl.BlockSpec((1, SSD_N, SSD_W), lambda c: (c, 0, 0))],
        out_shape=[jax.ShapeDtypeStruct((S, SSD_W), F32), jax.ShapeDtypeStruct((nc, SSD_N, SSD_W), F32)],
        scratch_shapes=[pltpu.VMEM((SSD_N, SSD_W), F32)],
        compiler_params=_params(("arbitrary",)), name=name)(xa, dtw, a_row, d_full)


def _ssd_chunk_bwd(xa, dtw, dy, hprev, a_row, d_full, name):
    S = xa.shape[0]
    L, G = SSD_CHUNK, 2
    nc = S // L
    GW = SSD_W // G

    def body(xa_ref, dtw_ref, dy_ref, hp_ref, a_ref, d_ref, dxa_ref, ddt_ref, da_ref, dd_ref, dstate):
        @pl.when(pl.program_id(0) == 0)
        def _():
            dstate[...] = jnp.zeros_like(dstate)
            da_ref[...] = jnp.zeros_like(da_ref)
            dd_ref[...] = jnp.zeros_like(dd_ref)

        ex, hs = _expand_mat(), _headsum_mat()
        a_row = a_ref[...]
        xs, dtv, acs, last, dt_full, xd, dec_full, e_full, elast_full = _chunk_common(xa_ref, dtw_ref, a_row, ex)
        acsT = acs.T
        dyv = dy_ref[...]
        lane = lax.broadcasted_iota(jnp.int32, (L, LANE), 1)
        sub = lax.broadcasted_iota(jnp.int32, (LANE, L), 0)
        dacs_c = jnp.zeros((L, LANE), F32)
        dacs_r = jnp.zeros((LANE, L), F32)
        dd_ref[...] += jnp.sum(_headsum(dyv * xs, hs), axis=0, keepdims=True)
        dxd_parts, yoff_parts, dxdd_parts, hh_parts = [], [], [], []
        for g in range(G):
            gc = slice(g * GW, (g + 1) * GW)
            b0 = SSD_W + g * SSD_N
            c0 = SSD_W + G * SSD_N + g * SSD_N
            Bg = xa_ref[:, b0:b0 + SSD_N]
            Cg = xa_ref[:, c0:c0 + SSD_N]
            Hp = hp_ref[0, :, gc]
            dH = dstate[:, gc]
            cb = _dot(Cg, Bg, NT)
            Gm = _dot(Cg, Hp)
            yoff_parts.append(e_full[:, gc] * Gm)
            dG = e_full[:, gc] * dyv[:, gc]
            dC = _dot(dG, Hp, NT)
            dHp = _dot(Cg.T, dG)
            xdd = xd[:, gc] * dec_full[:, gc]
            dB = _dot(xdd, dH, NT)
            dxdd = _dot(Bg, dH)
            dxdd_parts.append(dxdd)
            hh_parts.append(dH * Hp)
            dstate[:, gc] = dHp + elast_full[:, gc] * dH
            dcb = jnp.zeros((L, L), F32)
            dxd_g = []
            for hp in range(GW // LANE):
                pc = slice(g * GW + hp * LANE, g * GW + (hp + 1) * LANE)
                xd_pair = xd[:, pc]
                dxd_pair = dxdd[:, hp * LANE:(hp + 1) * LANE] * dec_full[:, pc]
                for h2 in range(2):
                    h = (g * GW + hp * LANE) // SSD_P + h2
                    col = DT_LANE0 + h
                    lm = _decay_mask(acs, acsT, col)
                    m = cb * lm
                    dy_h = dyv[:, pc] * _half_mask(h)
                    dm = _dot(dy_h, xd_pair, NT)
                    dxd_pair = dxd_pair + _dot(m.T, dy_h)
                    wm = dm * m
                    dacs_c = dacs_c + jnp.where(lane == col, jnp.sum(wm, axis=1, keepdims=True), 0.0)
                    dacs_r = dacs_r - jnp.where(sub == col, jnp.sum(wm, axis=0, keepdims=True), 0.0)
                    dcb = dcb + dm * lm
                dxd_g.append(dxd_pair)
            dxd_parts.append(jnp.concatenate(dxd_g, axis=1))
            dxa_ref[:, c0:c0 + SSD_N] = dC + _dot(dcb, Bg)
            dxa_ref[:, b0:b0 + SSD_N] = dB + _dot(dcb.T, Cg)
        dxd = jnp.concatenate(dxd_parts, axis=1)
        y_off = jnp.concatenate(yoff_parts, axis=1)
        dxdd_full = jnp.concatenate(dxdd_parts, axis=1)
        hh = jnp.concatenate(hh_parts, axis=1)
        dxa_ref[:, 0:SSD_W] = d_ref[...] * dyv + dxd * dt_full
        ddt = _headsum(dxd * xs, hs)
        w_dec = _headsum(dxdd_full * xd, hs) * jnp.exp(last - acs)
        dlast = jnp.sum(w_dec, axis=0, keepdims=True) + jnp.exp(last) * jnp.sum(_headsum(hh, hs), axis=0, keepdims=True)
        dacs = dacs_c + dacs_r.T + _headsum(dyv * y_off, hs) - w_dec
        rowid = lax.broadcasted_iota(jnp.int32, (L, LANE), 0)
        dacs = dacs + jnp.where(rowid == L - 1, dlast, 0.0)
        da = _sub_cumsum(dacs, reverse=True)
        ddt_ref[...] = ddt + da * a_row
        da_ref[...] += jnp.sum(da * dtv, axis=0, keepdims=True)

    rev = lambda c: (nc - 1 - c, 0)
    return pl.pallas_call(
        body, grid=(nc,),
        in_specs=[pl.BlockSpec((L, SSD_CONV_CH), rev), pl.BlockSpec((L, LANE), rev), pl.BlockSpec((L, SSD_W), rev),
                  pl.BlockSpec((1, SSD_N, SSD_W), lambda c: (nc - 1 - c, 0, 0)),
                  pl.BlockSpec((1, LANE), lambda c: (0, 0)), pl.BlockSpec((1, SSD_W), lambda c: (0, 0))],
        out_specs=[pl.BlockSpec((L, SSD_CONV_CH), rev), pl.BlockSpec((L, LANE), rev),
                   pl.BlockSpec((1, LANE), lambda c: (0, 0)), pl.BlockSpec((1, LANE), lambda c: (0, 0))],
        out_shape=[jax.ShapeDtypeStruct((S, SSD_CONV_CH), F32), jax.ShapeDtypeStruct((S, LANE), F32),
                   jax.ShapeDtypeStruct((1, LANE), F32), jax.ShapeDtypeStruct((1, LANE), F32)],
        scratch_shapes=[pltpu.VMEM((SSD_N, SSD_W), F32)],
        compiler_params=_params(("arbitrary",)), name=name)(xa, dtw, dy, hprev, a_row, d_full)


def _ssd_post_fwd(y, z, w, name):
    S = y.shape[0]
    GW = SSD_W // 2

    def body(i, n, R, C, P, N, O, A, Sc):
        zv = R[1][...]
        v = R[0][...] * (zv * _sigmoid(zv))
        for g in range(2):
            gc = slice(g * GW, (g + 1) * GW)
            vg = v[:, gc]
            r = lax.rsqrt(jnp.mean(vg * vg, axis=-1, keepdims=True) + NORM_EPS)
            O[0][:, gc] = (vg * r * C[0][:, gc]).astype(MXU_DTYPE)

    return _rows(body, S, _tile(S, (256, 128)), rows=[y, z], consts=[w], out_rows=[(SSD_W, MXU_DTYPE)], name=name)[0]


def _ssd_post_bwd(y, z, dyn, w, name):
    S = y.shape[0]
    GW = SSD_W // 2

    def body(i, n, R, C, P, N, O, A, Sc):
        yv, zv, dn = R[0][...], R[1][...], R[2][...]
        sz = zv * _sigmoid(zv)
        v = yv * sz
        for g in range(2):
            gc = slice(g * GW, (g + 1) * GW)
            dv, dw = _norm_bwd_math(v[:, gc], C[0][:, gc], dn[:, gc])
            A[0][:, gc] += dw
            O[0][:, gc] = dv * sz[:, gc]
            O[1][:, gc] = (dv * yv[:, gc] * _silu_grad(zv[:, gc])).astype(MXU_DTYPE)

    return _rows(body, S, _tile(S, (256, 128)), rows=[y, z, dyn], consts=[w],
                 out_rows=[(SSD_W, F32), (SSD_W, MXU_DTYPE)], out_accs=[((1, SSD_W), F32)], name=name)


def _merge_fwd(gl, yp, ya, ys, name):
    S, D = yp.shape

    def body(i, n, R, C, P, N, O, A, Sc):
        def tile(r0, first, last):
            rows = pl.ds(r0, SUB_ROWS)
            for c0 in range(0, D, CONV_COLS):
                cols = slice(c0, c0 + CONV_COLS)
                acc = None
                for b in range(3):
                    term = _sigmoid(R[0][rows, b * D + c0:b * D + c0 + CONV_COLS]) * R[1 + b][rows, cols]
                    acc = term if acc is None else acc + term
                O[0][rows, cols] = acc.astype(MXU_DTYPE)

        _tile_loop(T, tile)

    T = _tile(S, (256, 128))
    return _rows(body, S, T, rows=[gl, yp, ya, ys], out_rows=[(D, MXU_DTYPE)], name=name)[0]


def _merge_bwd(gl, yp, ya, ys, dm, name):
    S, D = yp.shape
    T = _tile(S, (256, 128))

    def body(i, n, R, C, P, N, O, A, Sc):
        def tile(r0, first, last):
            rows = pl.ds(r0, SUB_ROWS)
            for c0 in range(0, D, CONV_COLS):
                cols = slice(c0, c0 + CONV_COLS)
                dmv = R[4][rows, cols]
                for b in range(3):
                    gcols = slice(b * D + c0, b * D + c0 + CONV_COLS)
                    gt = _sigmoid(R[0][rows, gcols])
                    O[b][rows, cols] = (gt * dmv).astype(MXU_DTYPE)
                    O[3][rows, gcols] = (dmv * R[1 + b][rows, cols] * gt * (1.0 - gt)).astype(MXU_DTYPE)

        _tile_loop(T, tile)

    return _rows(body, S, T, rows=[gl, yp, ya, ys, dm],
                 out_rows=[(D, MXU_DTYPE)] * 3 + [(3 * D, MXU_DTYPE)], name=name)


FFN_COLS = 256


def _ffn_act_fwd(hpre, cw, cb, name):
    S, C = hpre.shape
    T = _tile(S, (256, 128))
    Fd = C // 2

    def body(i, n, R, Cs, P, N, O, A, Sc):
        def tile(r0, first, last):
            for c0 in range(0, Fd, FFN_COLS):
                gcols, vcols = slice(c0, c0 + FFN_COLS), slice(Fd + c0, Fd + c0 + FFN_COLS)
                gt = _conv_win(_win_prev(R[0], P[0], i, r0, first, gcols), Cs[0], Cs[1], gcols)
                val = _conv_win(_win_prev(R[0], P[0], i, r0, first, vcols), Cs[0], Cs[1], vcols)
                O[0][pl.ds(r0, SUB_ROWS), gcols] = (gt * _sigmoid(gt) * val).astype(MXU_DTYPE)

        _tile_loop(T, tile)

    return _rows(body, S, T, rows=[hpre], prevs=[hpre], consts=[cw, cb], out_rows=[(Fd, MXU_DTYPE)], name=name)[0]


def _ffn_act_bwd_a(hpre, dact, cw, cb, name):
    S, C = hpre.shape
    K = cw.shape[0]
    T = _tile(S, (256, 128))
    Fd = C // 2

    def body(i, n, R, Cs, P, N, O, A, Sc):
        def tile(r0, first, last):
            rows = pl.ds(r0, SUB_ROWS)
            for c0 in range(0, Fd, FFN_COLS):
                gcols, vcols = slice(c0, c0 + FFN_COLS), slice(Fd + c0, Fd + c0 + FFN_COLS)
                gwin = _win_prev(R[0], P[0], i, r0, first, gcols)
                vwin = _win_prev(R[0], P[0], i, r0, first, vcols)
                gt = _conv_win(gwin, Cs[0], Cs[1], gcols)
                val = _conv_win(vwin, Cs[0], Cs[1], vcols)
                da = R[1][rows, gcols]
                for cols, win, d in ((gcols, gwin, da * val * _silu_grad(gt)), (vcols, vwin, da * gt * _sigmoid(gt))):
                    O[0][rows, cols] = d
                    _conv_wgrad_win(A[0], win, d, cols)
                    A[1][:, cols] += jnp.sum(d, axis=0, keepdims=True)

        _tile_loop(T, tile)

    return _rows(body, S, T, rows=[hpre, dact], prevs=[hpre], consts=[cw, cb], out_rows=[(C, F32)],
                 out_accs=[((K, C), F32), ((1, C), F32)], name=name)


def _adamw_math(g, w, m, v):
    c1 = 1.0 - ADAM_B1 ** ADAM_STEP
    c2 = 1.0 - ADAM_B2 ** ADAM_STEP
    mn = ADAM_B1 * m + (1.0 - ADAM_B1) * g
    vn = ADAM_B2 * v + (1.0 - ADAM_B2) * (g * g)
    return -ADAM_LR * ((mn / c1) / (jnp.sqrt(vn / c2) + ADAM_EPS) + ADAM_WD * w), mn, vn


def _adamw(g, w, m, v, name):
    R_, W = g.shape

    def body(i, n, R, C, P, N, O, A, Sc):
        O[0][...], O[1][...], O[2][...] = _adamw_math(R[0][...], R[1][...], R[2][...], R[3][...])

    return _rows(body, R_, _row_tile(R_, W, 7), rows=[g, w, m, v], out_rows=[(W, F32)] * 3, name=name)


def _adamw_pair(q, other, w, m, v, pos, name):
    _, R_, W = w.shape
    T = _row_tile(R_, W, 9)

    def body(pos_ref, q_ref, o_ref, w_ref, m_ref, v_ref, g_out, d_out, m_out, v_out):
        g = jnp.where(pl.program_id(0) == pos_ref[1], q_ref[...], o_ref[...])
        g_out[0] = g
        d_out[0], m_out[0], v_out[0] = _adamw_math(g, w_ref[0], m_ref[0], v_ref[0])

    flat = pl.BlockSpec((T, W), lambda l, i, pos: (i, 0))
    full = pl.BlockSpec((1, T, W), lambda l, i, pos: (l, i, 0))
    return _scalar_call(body, pos, (2, R_ // T), [flat, flat, full, full, full], [full] * 4,
                        [jax.ShapeDtypeStruct(w.shape, F32)] * 4, (q, other, w, m, v), name)


def _row_tile(rows, width, n_blocks, budget=14 * 1024 * 1024):
    wpad = -(-width // LANE) * LANE
    for t in (512, 256, 128, 64, 32, 16, 8):
        if rows % t == 0 and n_blocks * t * wpad * 4 <= budget:
            return t
    return rows


_ANY = pl.BlockSpec(memory_space=pl.ANY)
_MESH = pl.DeviceIdType.MESH


DMA_CHUNK_BYTES = 2 * 1024 * 1024


def _row_chunks(shape, dtype):
    r = shape[-2]
    total = 1
    for s in shape:
        total *= s
    want = max(1, (total * jnp.dtype(dtype).itemsize) // DMA_CHUNK_BYTES)
    n = 1
    while n * 2 <= want and r % (n * 2 * 16) == 0 and n < 8:
        n *= 2
    return [(j * (r // n), r // n) for j in range(n)]


def _comm_call(plan, srcs, out_shapes, name):
    n = len(srcs)
    probe = plan(0, 0, 0, [_ShapeOnly(s.shape) for s in srcs], [_ShapeOnly(s.shape) for s in out_shapes])
    n_local, n_remote = len(probe[0]), len(probe[1])

    def body(*refs):
        src_refs, out_refs = refs[:n], refs[n:2 * n]
        send_sems, recv_sems, local_sems = refs[2 * n:]
        x, y, c = lax.axis_index("x"), lax.axis_index("y"), lax.axis_index("c")
        local, remote = plan(x, y, c, src_refs, out_refs)
        started = []
        for j, (s, d) in enumerate(local):
            cp = pltpu.make_async_copy(s, d, local_sems.at[j])
            cp.start()
            started.append(cp)
        sent = []
        for j, (s, d, peer) in enumerate(remote):
            cp = pltpu.make_async_remote_copy(src_ref=s, dst_ref=d, send_sem=send_sems.at[j], recv_sem=recv_sems.at[j],
                                              device_id=peer, device_id_type=_MESH)
            cp.start()
            sent.append(cp)
        for cp in sent:
            cp.wait()
        for cp in started:
            cp.wait()

    return pl.pallas_call(
        body, in_specs=[_ANY] * n, out_specs=[_ANY] * n,
        out_shape=[jax.ShapeDtypeStruct(s.shape, s.dtype) for s in out_shapes],
        scratch_shapes=[pltpu.SemaphoreType.DMA((n_remote,)), pltpu.SemaphoreType.DMA((n_remote,)),
                        pltpu.SemaphoreType.DMA((max(n_local, 1),))], name=name)(*srcs)


class _ShapeOnly:
    def __init__(self, shape):
        self.shape = tuple(shape)

    @property
    def at(self):
        return self

    def __getitem__(self, idx):
        return self


def _other_places(x, y):
    return [(1 - x, y), (x, 1 - y), (1 - x, 1 - y)]


def _gather_places(shards, row_major, name):
    outs = []
    for s, rm in zip(shards, row_major):
        L_, r, c_ = s.shape
        outs.append(jax.ShapeDtypeStruct((L_, N_PLACES, r, c_) if rm else (N_PLACES, L_, r, c_), s.dtype))

    def plan(x, y, c, src_refs, out_refs):
        me = 2 * x + y
        local, remote = [], []
        for s_ref, o_ref, rm, s in zip(src_refs, out_refs, row_major, shards):
            for r0, rn in _row_chunks(s.shape, s.dtype):
                src = s_ref.at[:, pl.ds(r0, rn), :]
                dst = o_ref.at[:, me, pl.ds(r0, rn), :] if rm else o_ref.at[me, :, pl.ds(r0, rn), :]
                local.append((src, dst))
                for px, py in _other_places(x, y):
                    remote.append((src, dst, (px, py, c)))
        return local, remote

    return _comm_call(plan, shards, outs, name)


def _reduce_sibling(gs, name):
    outs = [jax.ShapeDtypeStruct((N_PLACES,) + g.shape[2:], g.dtype) for g in gs]

    def plan(x, y, c, src_refs, out_refs):
        remote = []
        for g_ref, o_ref, g in zip(src_refs, out_refs, gs):
            for r0, rn in _row_chunks(g.shape[2:], g.dtype):
                for p in range(N_PLACES):
                    remote.append((g_ref.at[p, 1 - c, pl.ds(r0, rn), :], o_ref.at[p, pl.ds(r0, rn), :], (x, y, 1 - c)))
        return [], remote

    return _comm_call(plan, gs, outs, name)


def _reduce_places(hs, name):
    outs = [jax.ShapeDtypeStruct((3,) + h.shape[1:], h.dtype) for h in hs]

    def plan(x, y, c, src_refs, out_refs):
        remote = []
        for h_ref, o_ref, h in zip(src_refs, out_refs, hs):
            for r0, rn in _row_chunks(h.shape[1:], h.dtype):
                for j, (px, py) in enumerate(_other_places(x, y)):
                    remote.append((h_ref.at[2 * px + py, pl.ds(r0, rn), :], o_ref.at[j, pl.ds(r0, rn), :], (px, py, c)))
        return [], remote

    return _comm_call(plan, hs, outs, name)


def _swap_sibling(qs, name):
    def plan(x, y, c, src_refs, out_refs):
        remote = []
        for q_ref, o_ref, q in zip(src_refs, out_refs, qs):
            for r0, rn in _row_chunks(q.shape, q.dtype):
                remote.append((q_ref.at[pl.ds(r0, rn), :], o_ref.at[pl.ds(r0, rn), :], (x, y, 1 - c)))
        return [], remote

    return _comm_call(plan, qs, qs, name)


def _scalar_call(body, scalars, grid, in_specs, out_specs, out_shape, args, name):
    return pl.pallas_call(
        body, grid_spec=pltpu.PrefetchScalarGridSpec(num_scalar_prefetch=1, grid=grid, in_specs=in_specs,
                                                     out_specs=out_specs),
        out_shape=out_shape, compiler_params=_params(("arbitrary",) * len(grid)), name=name)(scalars, *args)


def _add_own_slot(g, r_, pos, out_dtype, name):
    P_, _, R_, W = g.shape
    T = _row_tile(R_, W, 3)

    def body(pos_ref, g_ref, r_ref, o_ref):
        o_ref[...] = (g_ref[0] + r_ref[...]).astype(out_dtype)

    return _scalar_call(
        body, pos, (P_, R_ // T),
        [pl.BlockSpec((1, 1, T, W), lambda p, i, pos: (p, pos[1], i, 0)), pl.BlockSpec((1, T, W), lambda p, i, pos: (p, i, 0))],
        pl.BlockSpec((1, T, W), lambda p, i, pos: (p, i, 0)), jax.ShapeDtypeStruct((P_, R_, W), out_dtype), (g, r_), name)


def _sum_places(g, r_, recv, pos, name):
    _, _, R_, W = g.shape
    T = _row_tile(R_, W, 5)

    def body(pos_ref, g_ref, r_ref, recv_ref, o_ref):
        for m in range(N_PLACES):
            @pl.when(pos_ref[0] == m)
            def _(m=m):
                acc = None
                for p in range(N_PLACES):
                    if p == m:
                        term = g_ref[0, 0] + r_ref[0]
                    else:
                        dx, dy = (p >> 1) != (m >> 1), (p & 1) != (m & 1)
                        term = recv_ref[0 if (dx and not dy) else 1 if (dy and not dx) else 2].astype(F32)
                    acc = term if acc is None else acc + term
                o_ref[...] = acc

    return _scalar_call(
        body, pos, (R_ // T,),
        [pl.BlockSpec((1, 1, T, W), lambda i, pos: (pos[0], pos[1], i, 0)),
         pl.BlockSpec((1, T, W), lambda i, pos: (pos[0], i, 0)), pl.BlockSpec((3, T, W), lambda i, pos: (0, i, 0))],
        pl.BlockSpec((T, W), lambda i, pos: (i, 0)), jax.ShapeDtypeStruct((R_, W), F32), (g, r_, recv), name)


def _to_heads(a, dtype):
    S = a.shape[0]
    return a.reshape(S, HEADS, HEAD_DIM).transpose(1, 0, 2).astype(dtype)


def _from_heads(a):
    return a.transpose(1, 0, 2).reshape(a.shape[1], HEADS * HEAD_DIM)


def _lane_tile(vec16):
    return jnp.concatenate([jnp.zeros((DT_LANE0,), F32), vec16,
                            jnp.zeros((LANE - DT_LANE0 - SSD_HEADS,), F32)])[None]


def _layer_consts(W, l):
    return dict(
        norm_mix=W['norm_mix'][l][None], mix=W['pool_mix'][l].astype(MXU_DTYPE), scale=W['pool_scale'][l][None],
        f_bias=W['f_bias'][l][:, None], cw=W['ssd_conv_w'][l], cb=W['ssd_conv_b'][l][None],
        dtb=_lane_tile(W['ssd_dt_bias'][l]), a_row=_lane_tile(-jnp.exp(W['ssd_a_log'][l])),
        d_full=jnp.repeat(W['ssd_d'][l], SSD_P)[None], ssd_norm=W['ssd_norm'][l][None],
        norm_ffn=W['norm_ffn'][l][None], fcw=W['ffn_conv_w'][l], fcb=W['ffn_conv_b'][l][None])


def _layer_fwd(x, W, l):
    n = f"l{l}_"
    cs = _layer_consts(W, l)
    win = {k: v[l] for k, v in W['w_in'].items()}
    u = _norm_fwd(x, cs['norm_mix'], n + "norm_mix")
    pqkv = _mm(u, win['p'], name=n + "in_p")
    z = _mm(u, win['z'], name=n + "in_z")
    xbc = _mm(u, win['x'], name=n + "in_x")
    gl = _mm(u, win['g'], name=n + "in_g")
    fdt = _mm(u, win['f'], name=n + "in_f")
    d, ypm = _pool_fwd(pqkv, cs['mix'], cs['scale'], n + "pool")
    yp = _mm(ypm, W['p_pool'][l], name=n + "p_pool")
    fT = fdt[:, :HEADS].T
    c = _logf_cumsum(fT, cs['f_bias'], n + "logf")
    c_row = c[:, None, :]
    qh = _to_heads(pqkv[:, ATTN_W:2 * ATTN_W] * ATTN_SCALE, MXU_DTYPE)
    kh, vh = (_to_heads(pqkv[:, (2 + j) * ATTN_W:(3 + j) * ATTN_W], MXU_DTYPE) for j in range(2))
    oh, lse = _attn_fwd(qh, kh, vh, c_row, n + "attn")
    o = _from_heads(oh)
    ya = _mm(o, W['p_attn'][l], name=n + "p_attn")
    xa, dtw = _ssd_pre_fwd(xbc, fdt, cs['cw'], cs['cb'], cs['dtb'], n + "ssd_pre")
    y, hprev = _ssd_chunk_fwd(xa, dtw, cs['a_row'], cs['d_full'], n + "ssd_scan")
    yn = _ssd_post_fwd(y, z, cs['ssd_norm'], n + "ssd_post")
    ys = _mm(yn, W['p_ssd'][l], name=n + "p_ssd")
    merged = _merge_fwd(gl, yp, ya, ys, n + "merge")
    x1 = _mm(merged, W['w_out'][l], acc=x, name=n + "w_out")
    u2 = _norm_fwd(x1, cs['norm_ffn'], n + "norm_ffn")
    hpre = _mm(u2, W['ffn_up'][l], name=n + "ffn_up")
    act = _ffn_act_fwd(hpre, cs['fcw'], cs['fcb'], n + "ffn_act")
    x2 = _mm(act, W['ffn_down'][l], acc=x1, name=n + "ffn_down")
    saved = dict(x=x, u=u, pqkv=pqkv, z=z, xbc=xbc, gl=gl, fdt=fdt, d=d, ypm=ypm, yp=yp, fT=fT,
                 c_row=c_row, qh=qh, kh=kh, vh=vh, oh=oh, o=o, lse=lse, ya=ya, xa=xa, dtw=dtw, y=y, hprev=hprev,
                 yn=yn, ys=ys, merged=merged, x1=x1, u2=u2, hpre=hpre, act=act, win=win, cs=cs)
    return x2, saved


def _layer_bwd(dx2, sv, W, l):
    n = f"l{l}_b_"
    cs, win = sv['cs'], sv['win']
    g = {}
    dact = _mm(dx2, W['ffn_down'][l], tb=True, name=n + "ffn_down_dx")
    g['ffn_down'] = _mm(sv['act'], dx2, ta=True, name=n + "ffn_down_dw")
    dhc, g['ffn_conv_w'], dfcb = _ffn_act_bwd_a(sv['hpre'], dact, cs['fcw'], cs['fcb'], n + "ffn_act_a")
    g['ffn_conv_b'] = dfcb[0]
    dhpre = _conv_bwd_b(dhc, cs['fcw'], n + "ffn_act_b")
    du2 = _mm(dhpre, W['ffn_up'][l], tb=True, name=n + "ffn_up_dx")
    g['ffn_up'] = _mm(sv['u2'], dhpre, ta=True, name=n + "ffn_up_dw")
    dx1, dnf = _norm_bwd(sv['x1'], cs['norm_ffn'], du2, dx2, n + "norm_ffn")
    g['norm_ffn'] = dnf[0]
    dm = _mm(dx1, W['w_out'][l], tb=True, name=n + "w_out_dx")
    g['w_out'] = _mm(sv['merged'], dx1, ta=True, name=n + "w_out_dw")
    dyp, dya, dys, dgl = _merge_bwd(sv['gl'], sv['yp'], sv['ya'], sv['ys'], dm, n + "merge")
    dypm = _mm(dyp, W['p_pool'][l], tb=True, name=n + "p_pool_dx")
    g['p_pool'] = _mm(sv['ypm'], dyp, ta=True, name=n + "p_pool_dw")
    dd, dscale, dmix = _pool_bwd_a(dypm, sv['d'], cs['mix'], cs['scale'], n + "pool_a")
    g['pool_scale'] = dscale[0]
    g['pool_mix'] = dmix.reshape(len(POOL_WINDOWS), LANE, LANE)
    dpool_v = _pool_bwd_b(dd, n + "pool_b")
    do = _mm(dya, W['p_attn'][l], tb=True, name=n + "p_attn_dx")
    g['p_attn'] = _mm(sv['o'], dya, ta=True, name=n + "p_attn_dw")
    dqh, dkh, dvh, dc, dcq = _attn_bwd(sv['qh'], sv['kh'], sv['vh'], _to_heads(do, F32), sv['oh'],
                                       sv['c_row'], sv['lse'], n + "attn")
    dfT, dfb = _logf_cumsum_bwd(sv['fT'], cs['f_bias'], dc[:, 0, :], dcq[:, :, 0], n + "logf")
    g['f_bias'] = dfb[:, 0]
    dpqkv = jnp.concatenate([dpool_v, (_from_heads(dqh) * ATTN_SCALE).astype(MXU_DTYPE)]
                            + [_from_heads(t).astype(MXU_DTYPE) for t in (dkh, dvh)], axis=1)
    dyn = _mm(dys, W['p_ssd'][l], tb=True, name=n + "p_ssd_dx")
    g['p_ssd'] = _mm(sv['yn'], dys, ta=True, name=n + "p_ssd_dw")
    dy, dz, dsn = _ssd_post_bwd(sv['y'], sv['z'], dyn, cs['ssd_norm'], n + "ssd_post")
    g['ssd_norm'] = dsn[0]
    dxa, ddtw, dA, dD = _ssd_chunk_bwd(sv['xa'], sv['dtw'], dy, sv['hprev'], cs['a_row'], cs['d_full'], n + "ssd_scan")
    heads = slice(DT_LANE0, DT_LANE0 + SSD_HEADS)
    g['ssd_a_log'] = dA[0, heads] * cs['a_row'][0, heads]
    g['ssd_d'] = dD[0, heads]
    dpre, ddt_raw, g['ssd_conv_w'], dcb, ddtb = _ssd_pre_bwd_a(sv['xbc'], sv['fdt'], dxa, ddtw, cs['cw'], cs['cb'],
                                                              cs['dtb'], n + "ssd_pre_a")
    g['ssd_conv_b'] = dcb[0]
    g['ssd_dt_bias'] = ddtb[0, heads]
    dxbc = _conv_bwd_b(dpre, cs['cw'], n + "ssd_pre_b")
    dfdt = jnp.concatenate([dfT.T, ddt_raw[:, HEADS:]], axis=1).astype(MXU_DTYPE)
    dsegs = dict(p=dpqkv, z=dz, x=dxbc, g=dgl, f=dfdt)
    du, dwin = None, {}
    for key in ('p', 'z', 'x', 'g', 'f'):
        du = _mm(dsegs[key], win[key], tb=True, acc=du, name=n + "in_dx_" + key)
        dwin[key] = _mm(sv['u'], dsegs[key], ta=True, name=n + "in_dw_" + key)
    g['w_in'] = dwin
    dx, dnm = _norm_bwd(sv['x'], cs['norm_mix'], du, dx1, n + "norm_mix")
    g['norm_mix'] = dnm[0]
    return dx, g


def _local_step(x, target, W):
    depth = W['norm_mix'].shape[0]
    saved = []
    h = x
    for l in range(depth):
        h, sv = _layer_fwd(h, W, l)
        saved.append(sv)
    dx, dwf, loss = _loss_head(h, W['norm_final'][None], target, "loss_head")
    grads = [None] * depth
    for l in reversed(range(depth)):
        dx, grads[l] = _layer_bwd(dx, saved[l], W, l)
    return loss[0, 0], dx, grads, dwf[0]


def _pack_rows(parts, row_align=1):
    flat = jnp.concatenate([p.reshape(-1) for p in parts])
    n = flat.shape[0]
    total = -(-n // (PACK_W * row_align)) * PACK_W * row_align
    if total > n:
        flat = jnp.concatenate([flat, jnp.zeros((total - n,), flat.dtype)])
    return flat.reshape(-1, PACK_W)


def _unpack_rows(buf, shapes):
    flat = buf.reshape(-1)
    out, pos = [], 0
    for shp in shapes:
        size = 1
        for s in shp:
            size *= s
        out.append(flat[pos:pos + size].reshape(shp))
        pos += size
    return out


def _to_place_major(gfull, name):
    R_, C = gfull.shape
    if name in COL_SHARDED:
        return gfull.reshape(R_, N_PLACES, C // N_PLACES).transpose(1, 0, 2)
    return gfull.reshape(N_PLACES, R_ // N_PLACES, C)


_W_IN_LAYOUT = (('p', 0, 0, 2048), ('f', 0, 2048, HEADS), ('z', 0, 2056, 1024), ('x', 0, 3080, 1536),
                ('f', DT_LANE0, 4616, SSD_HEADS), ('g', 0, 4632, 3072))


def _w_in_segments(slabs):
    starts = [0]
    for s in slabs:
        starts.append(starts[-1] + s.shape[-1])

    def cols(a, b):
        parts = []
        for s, s0 in zip(slabs, starts):
            lo, hi = max(a, s0), min(b, s0 + s.shape[-1])
            if lo < hi:
                parts.append(s[..., lo - s0:hi - s0])
        return parts[0] if len(parts) == 1 else jnp.concatenate(parts, axis=-1)

    pad = jnp.zeros(slabs[0].shape[:-1] + (LANE - DT_LANE0 - SSD_HEADS,), slabs[0].dtype)
    return dict(p=cols(0, 2048), z=cols(2056, 3080), x=cols(3080, 4616), g=cols(4632, 7704),
                f=jnp.concatenate([cols(2048, 2056), cols(4616, 4632), pad], axis=-1))


def _w_in_columns(segs, a, b):
    parts = []
    for key, s0, g0, w in _W_IN_LAYOUT:
        lo, hi = max(a, g0), min(b, g0 + w)
        if lo < hi:
            parts.append(segs[key][..., s0 + lo - g0:s0 + hi - g0])
    return parts[0] if len(parts) == 1 else jnp.concatenate(parts, axis=-1)


def kernel(x, norm_mix, w_in, pool_mix, pool_scale, f_bias, ssd_conv_w, ssd_conv_b, ssd_dt_bias, ssd_a_log, ssd_d, ssd_norm, p_pool, p_attn, p_ssd, w_out, norm_ffn, ffn_up, ffn_conv_w, ffn_conv_b, ffn_down, norm_final, loss_target, m_norm_mix, m_w_in, m_pool_mix, m_pool_scale, m_f_bias, m_ssd_conv_w, m_ssd_conv_b, m_ssd_dt_bias, m_ssd_a_log, m_ssd_d, m_ssd_norm, m_p_pool, m_p_attn, m_p_ssd, m_w_out, m_norm_ffn, m_ffn_up, m_ffn_conv_w, m_ffn_conv_b, m_ffn_down, m_norm_final, v_norm_mix, v_w_in, v_pool_mix, v_pool_scale, v_f_bias, v_ssd_conv_w, v_ssd_conv_b, v_ssd_dt_bias, v_ssd_a_log, v_ssd_d, v_ssd_norm, v_p_pool, v_p_attn, v_p_ssd, v_w_out, v_norm_ffn, v_ffn_up, v_ffn_conv_w, v_ffn_conv_b, v_ffn_down, v_norm_final):
    args = dict(locals())
    w_sh = {k: args[k] for k in WEIGHTS}
    m_sh = {k: args['m_' + k] for k in WEIGHTS}
    v_sh = {k: args['v_' + k] for k in WEIGHTS}
    depth = norm_mix.shape[0]
    place = 2 * lax.axis_index("x") + lax.axis_index("y")
    row_sharded = [k for k in BIG if k not in COL_SHARDED]

    gathered = _gather_places([w_sh[k].astype(MXU_DTYPE) for k in BIG] + [w_sh[k] for k in SMALL_SHARDED],
                              [k in row_sharded for k in BIG] + [False] * len(SMALL_SHARDED), "gather_weights")
    gathered = dict(zip(BIG + SMALL_SHARDED, gathered))
    W = {k: w_sh[k] for k in SMALL if k not in SMALL_SHARDED}
    for k in BIG + SMALL_SHARDED:
        gk = gathered[k]
        if k in row_sharded:
            W[k] = gk.reshape(gk.shape[0], -1, gk.shape[-1])
        elif k == 'w_in':
            W[k] = _w_in_segments([gk[p] for p in range(N_PLACES)])
        else:
            W[k] = jnp.concatenate([gk[p] for p in range(N_PLACES)], axis=-1)

    loss_local, grad_x, grads, g_final = _local_step(x[0], loss_target[0], W)
    loss = lax.psum(loss_local, ("x", "y", "c"))

    def place_major(k, l):
        if k == 'w_in':
            c = IN_TOTAL // N_PLACES
            return jnp.stack([_w_in_columns(grads[l][k], p * c, (p + 1) * c) for p in range(N_PLACES)])
        return _to_place_major(grads[l][k], k)

    g_big = [jnp.stack([place_major(k, l) for l in range(depth)], axis=1) for k in BIG]
    small_names = [k for k in SMALL if k != 'norm_final'] + ['norm_final']
    small_full = [jnp.stack([grads[l][k] for l in range(depth)]) for k in small_names[:-1]] + [g_final]
    small_full_shapes = [a.shape for a in small_full]
    small_packed = _pack_rows(small_full, 32)
    g_small = jnp.broadcast_to(small_packed.reshape(1, 2, -1, PACK_W),
                               (N_PLACES, 2, small_packed.shape[0] // 2, PACK_W))

    core = lax.axis_index("c")
    pos = jnp.stack([place, core]).astype(jnp.int32)
    g_all = g_big + [g_small]
    theirs = _reduce_sibling(g_all, "reduce_sibling")
    wire = [WIRE_DTYPE] * len(BIG) + [F32]
    halves = [_add_own_slot(g, t, pos, dt, f"reduce_sibling_add{j}")
              for j, (g, t, dt) in enumerate(zip(g_all, theirs, wire))]
    recv = _reduce_places(halves, "reduce_places")
    qs = [_sum_places(g, t, r, pos, f"reduce_places_add{j}") for j, (g, t, r) in enumerate(zip(g_all, theirs, recv))]
    others = _swap_sibling(qs, "reduce_swap")

    def mine(k, a):
        if k in SMALL_SHARDED:
            c = a.shape[-1] // N_PLACES
            return lax.dynamic_slice_in_dim(a, place * c, c, axis=a.ndim - 1)
        return a

    outs = {}
    for j, k in enumerate(BIG):
        res = _adamw_pair(qs[j], others[j], w_sh[k], m_sh[k], v_sh[k], pos, "adamw_" + k)
        for prefix, a in zip(('grad_', 'delta_', 'new_m_', 'new_v_'), res):
            outs[prefix + k] = a
    small_sum = jnp.where(core == 0, jnp.concatenate([qs[-1], others[-1]]), jnp.concatenate([others[-1], qs[-1]]))
    g_small_list = [mine(k, a) for k, a in zip(small_names, _unpack_rows(small_sum, small_full_shapes))]
    shapes = [a.shape for a in g_small_list]
    gp = _pack_rows(g_small_list, 128)
    wp, mp, vp = (_pack_rows([d[k] for k in small_names], 128) for d in (w_sh, m_sh, v_sh))
    delta_p, m_p, v_p = _adamw(gp, wp, mp, vp, "adamw_small")
    for prefix, buf in (('grad_', gp), ('delta_', delta_p), ('new_m_', m_p), ('new_v_', v_p)):
        for k, a in zip(small_names, _unpack_rows(buf, shapes)):
            outs[prefix + k] = a
    result = [loss, grad_x[None]]
    for prefix in ('grad_', 'delta_', 'new_m_', 'new_v_'):
        result += [outs[prefix + k] for k in WEIGHTS]
    return tuple(result)
```

```python
import functools

import jax
import jax.numpy as jnp
from jax import lax
from jax.experimental import pallas as pl
from jax.experimental.pallas import tpu as pltpu

F32 = jnp.float32
MXU_DTYPE = jnp.bfloat16
WIRE_DTYPE = jnp.bfloat16
NORM_EPS = 1e-6
HALO = 16
LANE = 128
NEG_BIG = -1e30
VMEM_LIMIT = 52 * 1024 * 1024

D_MODEL = 1024
POOL_WINDOWS = (2, 4, 8, 16)
POOL_W = 512
HEADS = 8
HEAD_DIM = 64
ATTN_W = 512
ATTN_SCALE = HEAD_DIM ** -0.5
SSD_W = 1024
SSD_HEADS = 16
SSD_P = 64
SSD_N = 128
SSD_CHUNK = 128
SSD_CONV_CH = 1536
FFN = 2816
DT_LANE0 = 8
IN_SPLITS = (512, 512, 512, 512, 8, 1024, 1536, 16, 3072)
IN_TOTAL = sum(IN_SPLITS)
N_PLACES = 4

ADAM_LR, ADAM_B1, ADAM_B2, ADAM_EPS, ADAM_WD, ADAM_STEP = 0.001, 0.9, 0.999, 1e-08, 0.01, 10

BIG = ('w_in', 'p_pool', 'p_attn', 'p_ssd', 'w_out', 'ffn_up', 'ffn_down')
COL_SHARDED = ('w_in', 'p_pool', 'p_attn', 'ffn_up')
SMALL = ('norm_mix', 'pool_mix', 'pool_scale', 'f_bias', 'ssd_conv_w', 'ssd_conv_b', 'ssd_dt_bias',
         'ssd_a_log', 'ssd_d', 'ssd_norm', 'norm_ffn', 'ffn_conv_w', 'ffn_conv_b', 'norm_final')
SMALL_SHARDED = ('ssd_conv_w', 'ffn_conv_w')
WEIGHTS = ('norm_mix', 'w_in', 'pool_mix', 'pool_scale', 'f_bias', 'ssd_conv_w', 'ssd_conv_b', 'ssd_dt_bias',
           'ssd_a_log', 'ssd_d', 'ssd_norm', 'p_pool', 'p_attn', 'p_ssd', 'w_out', 'norm_ffn', 'ffn_up',
           'ffn_conv_w', 'ffn_conv_b', 'ffn_down', 'norm_final')
PACK_W = 1024


def _params(sem):
    return pltpu.CompilerParams(dimension_semantics=sem, vmem_limit_bytes=VMEM_LIMIT)


def _tile(n, prefs=(512, 256, 128)):
    for t in prefs:
        if n % t == 0:
            return t
    return n


def _sigmoid(x):
    return 0.5 * jnp.tanh(0.5 * x) + 0.5


def _softplus(x):
    return jnp.maximum(x, 0.0) + jnp.log1p(jnp.exp(-jnp.abs(x)))


def _dot(a, b, dims=((1,), (0,))):
    return lax.dot_general(a.astype(MXU_DTYPE), b.astype(MXU_DTYPE), (dims, ((), ())),
                           preferred_element_type=F32)


NT = ((1,), (1,))


def _mm(a, b, *, ta=False, tb=False, acc=None, out_dtype=F32, name):
    M, K = (a.shape[1], a.shape[0]) if ta else a.shape
    N = b.shape[0] if tb else b.shape[1]
    big = (1024, 1408, 512, 256, 128)
    tm, tn = _tile(M, big), _tile(N, big)
    tk = K if K <= 1024 else _tile(K, (512, 256, 128) if ta else big)
    nk = K // tk
    a_spec = pl.BlockSpec((tk, tm), lambda i, j, k: (k, i)) if ta else pl.BlockSpec((tm, tk), lambda i, j, k: (i, k))
    b_spec = pl.BlockSpec((tn, tk), lambda i, j, k: (j, k)) if tb else pl.BlockSpec((tk, tn), lambda i, j, k: (k, j))
    in_specs = [a_spec, b_spec]
    args = [a, b]
    if acc is not None:
        in_specs.append(pl.BlockSpec((tm, tn), lambda i, j, k: (i, j)))
        args.append(acc)

    def body(*refs):
        if acc is not None:
            a_ref, b_ref, c_ref, o_ref, acc_ref = refs
        else:
            a_ref, b_ref, o_ref, acc_ref = refs
        k = pl.program_id(2)

        @pl.when(k == 0)
        def _():
            if acc is not None:
                acc_ref[...] = c_ref[...].astype(F32)
            else:
                acc_ref[...] = jnp.zeros_like(acc_ref)

        av = a_ref[...]
        if ta:
            av = av.astype(F32).T
        acc_ref[...] += _dot(av, b_ref[...], NT if tb else ((1,), (0,)))

        @pl.when(k == nk - 1)
        def _():
            o_ref[...] = acc_ref[...].astype(out_dtype)

    return pl.pallas_call(
        body, grid=(M // tm, N // tn, nk), in_specs=in_specs,
        out_specs=pl.BlockSpec((tm, tn), lambda i, j, k: (i, j)),
        out_shape=jax.ShapeDtypeStruct((M, N), out_dtype),
        scratch_shapes=[pltpu.VMEM((tm, tn), F32)],
        compiler_params=_params(("parallel", "parallel", "arbitrary")), name=name)(*args)


def _rows(body, S, T, *, rows=(), consts=(), prevs=(), nexts=(), out_rows=(), out_accs=(), scratch=(), name):
    n = S // T
    hb = T // HALO
    last_h = S // HALO - 1

    def norm(r):
        return r if isinstance(r, tuple) else (r, r.shape[1], 0)

    rows, prevs, nexts = [norm(r) for r in rows], [norm(r) for r in prevs], [norm(r) for r in nexts]
    in_specs, args = [], []
    for arr, W, cb in rows:
        in_specs.append(pl.BlockSpec((T, W), lambda i, cb=cb: (i, cb)))
        args.append(arr)
    for cst in consts:
        in_specs.append(pl.BlockSpec(cst.shape, lambda i, nd=cst.ndim: (0,) * nd))
        args.append(cst)
    for arr, W, cb in prevs:
        in_specs.append(pl.BlockSpec((HALO, W), lambda i, cb=cb: (jnp.maximum(i * hb - 1, 0), cb)))
        args.append(arr)
    for arr, W, cb in nexts:
        in_specs.append(pl.BlockSpec((HALO, W), lambda i, cb=cb: (jnp.minimum((i + 1) * hb, last_h), cb)))
        args.append(arr)
    out_specs = [pl.BlockSpec((T, W), lambda i: (i, 0)) for W, _ in out_rows]
    out_specs += [pl.BlockSpec(shp, lambda i, nd=len(shp): (0,) * nd) for shp, _ in out_accs]
    out_shape = [jax.ShapeDtypeStruct((S, W), dt) for W, dt in out_rows]
    out_shape += [jax.ShapeDtypeStruct(shp, dt) for shp, dt in out_accs]
    cuts = [len(rows), len(consts), len(prevs), len(nexts), len(out_rows), len(out_accs), len(scratch)]

    def kern(*refs):
        groups, pos = [], 0
        for c in cuts:
            groups.append(list(refs[pos:pos + c]))
            pos += c
        i = pl.program_id(0)

        @pl.when(i == 0)
        def _():
            for a_ref in groups[5]:
                a_ref[...] = jnp.zeros_like(a_ref)

        body(i, n, *groups)

    outs = pl.pallas_call(kern, grid=(n,), in_specs=in_specs, out_specs=out_specs, out_shape=out_shape,
                          scratch_shapes=list(scratch), compiler_params=_params(("arbitrary",)), name=name)(*args)
    return outs


def _fill_prev(ext, prev_ref, cur, i):
    ext[0:HALO, :] = jnp.where(i > 0, prev_ref[...].astype(F32), 0.0)
    ext[HALO:, :] = cur


def _fill_next(ext, cur, next_val, T):
    ext[0:T, :] = cur
    ext[T:, :] = next_val


def _row_ids(i, T, W=1):
    return i * T + lax.broadcasted_iota(jnp.int32, (T, W), 0)


SUB_ROWS = 32
WIN_PAD = 8


def _tile_loop(T, fn):
    n = T // SUB_ROWS
    fn(0, True, n == 1)
    if n > 2:
        def body(rb, carry):
            fn(pl.multiple_of(rb * SUB_ROWS, SUB_ROWS), False, False)
            return carry
        lax.fori_loop(1, n - 1, body, 0)
    if n > 1:
        fn((n - 1) * SUB_ROWS, False, True)


def _win_prev(x_ref, prev_ref, i, r0, first, cols):
    if first:
        top = jnp.where(i > 0, prev_ref[HALO - WIN_PAD:HALO, cols].astype(F32), 0.0)
        return jnp.concatenate([top, x_ref[0:SUB_ROWS, cols].astype(F32)], axis=0)
    start = r0 - WIN_PAD if isinstance(r0, int) else pl.multiple_of(r0 - WIN_PAD, WIN_PAD)
    return x_ref[pl.ds(start, SUB_ROWS + WIN_PAD), cols].astype(F32)


def _behind(win, j):
    return win[WIN_PAD:, :] if j == 0 else pltpu.roll(win, j, axis=0)[WIN_PAD:, :]


def _win_next(x_ref, next_ref, i, n, r0, last, cols):
    if last:
        bot = jnp.where(i < n - 1, next_ref[0:WIN_PAD, cols].astype(F32), 0.0)
        return jnp.concatenate([x_ref[r0:r0 + SUB_ROWS, cols].astype(F32), bot], axis=0)
    return x_ref[pl.ds(r0, SUB_ROWS + WIN_PAD), cols].astype(F32)


def _ahead(win, j):
    return win[:SUB_ROWS, :] if j == 0 else pltpu.roll(win, SUB_ROWS + WIN_PAD - j, axis=0)[:SUB_ROWS, :]


def _conv_win(win, w_ref, b_ref, cols):
    K = w_ref.shape[0]
    out = b_ref[:, cols]
    for k in range(K):
        out = out + _behind(win, K - 1 - k) * w_ref[k:k + 1, cols]
    return out


def _conv_wgrad_win(acc_ref, win, d, cols):
    K = acc_ref.shape[0]
    for k in range(K):
        acc_ref[k:k + 1, cols] += jnp.sum(d * _behind(win, K - 1 - k), axis=0, keepdims=True)


def _norm_fwd(x, w, name):
    S, D = x.shape

    def body(i, n, R, C, P, N, O, A, Sc):
        def tile(r0, first, last):
            rows = pl.ds(r0, SUB_ROWS)
            xv = R[0][rows, :]
            r = lax.rsqrt(jnp.mean(xv * xv, axis=-1, keepdims=True) + NORM_EPS)
            O[0][rows, :] = (xv * r * C[0][...]).astype(MXU_DTYPE)

        _tile_loop(T, tile)

    T = _tile(S)
    return _rows(body, S, T, rows=[x], consts=[w], out_rows=[(D, MXU_DTYPE)], name=name)[0]


def _norm_bwd_math(xv, w, du):
    r = lax.rsqrt(jnp.mean(xv * xv, axis=-1, keepdims=True) + NORM_EPS)
    xh = xv * r
    g = du * w
    dx = r * (g - xh * jnp.mean(g * xh, axis=-1, keepdims=True))
    dw = jnp.sum(du * xh, axis=0, keepdims=True)
    return dx, dw


def _norm_bwd(x, w, du, dres, name):
    S, D = x.shape

    def body(i, n, R, C, P, N, O, A, Sc):
        def tile(r0, first, last):
            rows = pl.ds(r0, SUB_ROWS)
            dx, dw = _norm_bwd_math(R[0][rows, :], C[0][...], R[1][rows, :])
            O[0][rows, :] = R[2][rows, :] + dx
            A[0][...] += dw

        _tile_loop(T, tile)

    T = _tile(S)
    return _rows(body, S, T, rows=[x, du, dres], consts=[w], out_rows=[(D, F32)],
                 out_accs=[((1, D), F32)], name=name)


def _loss_head(x, w, target, name):
    S, D = x.shape

    def body(i, n, R, C, P, N, O, A, Sc):
        def tile(r0, first, last):
            rows = pl.ds(r0, SUB_ROWS)
            xv, w_, tg = R[0][rows, :], C[0][...], R[1][rows, :]
            r = lax.rsqrt(jnp.mean(xv * xv, axis=-1, keepdims=True) + NORM_EPS)
            e = xv * r * w_ - tg
            A[1][...] += jnp.broadcast_to(0.5 * jnp.sum(jnp.mean(e * e, axis=-1, keepdims=True)), (1, LANE))
            dx, dw = _norm_bwd_math(xv, w_, e / D)
            O[0][rows, :] = dx
            A[0][...] += dw

        _tile_loop(T, tile)

    T = _tile(S)
    return _rows(body, S, T, rows=[x, target], consts=[w], out_rows=[(D, F32)],
                 out_accs=[((1, D), F32), ((1, LANE), F32)], name=name)


def _pool_fwd(pqkv, mix, scale, name):
    S = pqkv.shape[0]
    T = _tile(S, (256, 128))

    def body(i, n, R, C, P, N, O, A, Sc):
        ext = Sc[0]
        v = R[0][...]
        _fill_prev(ext, P[0], v, i)
        t1 = (_row_ids(i, T) + 1).astype(F32)
        for g, w in enumerate(POOL_WINDOWS):
            cols = slice(g * LANE, (g + 1) * LANE)
            acc = v[:, cols]
            for j in range(1, w):
                acc = acc + ext[pl.ds(HALO - j, T), cols]
            d = (acc / jnp.minimum(t1, float(w)) - v[:, cols]).astype(MXU_DTYPE)
            O[0][:, cols] = d
            O[1][:, cols] = (_dot(d, C[0][g]) * C[1][:, cols]).astype(MXU_DTYPE)

    return _rows(body, S, T, rows=[(pqkv, POOL_W, 0)], prevs=[(pqkv, POOL_W, 0)], consts=[mix, scale],
                 out_rows=[(POOL_W, MXU_DTYPE), (POOL_W, MXU_DTYPE)],
                 scratch=[pltpu.VMEM((HALO + T, POOL_W), F32)], name=name)


def _pool_bwd_a(dypm, d, mix, scale, name):
    S = d.shape[0]
    T = _tile(S, (256, 128))

    def body(i, n, R, C, P, N, O, A, Sc):
        for g in range(len(POOL_WINDOWS)):
            cols = slice(g * LANE, (g + 1) * LANE)
            dg = R[1][:, cols]
            dy = R[0][:, cols]
            yg = _dot(dg, C[0][g])
            A[0][:, cols] += jnp.sum(dy * yg, axis=0, keepdims=True)
            dys = dy * C[1][:, cols]
            A[1][cols, :] += _dot(dg.astype(F32).T, dys)
            O[0][:, cols] = _dot(dys, C[0][g], NT)

    return _rows(body, S, T, rows=[dypm, d], consts=[mix, scale], out_rows=[(POOL_W, F32)],
                 out_accs=[((1, POOL_W), F32), ((POOL_W, LANE), F32)], name=name)


def _pool_bwd_b(dd, name):
    S = dd.shape[0]
    T = _tile(S, (256, 128))

    def body(i, n, R, C, P, N, O, A, Sc):
        ext = Sc[0]
        ddv = R[0][...]
        t1 = (_row_ids(i, T) + 1).astype(F32)
        nxt = jnp.where(i < n - 1, N[0][...], 0.0)
        for g, w in enumerate(POOL_WINDOWS):
            cols = slice(g * LANE, (g + 1) * LANE)
            ext[0:T, cols] = ddv[:, cols] / jnp.minimum(t1, float(w))
            ext[T:, cols] = nxt[:, cols] / float(w)
        for g, w in enumerate(POOL_WINDOWS):
            cols = slice(g * LANE, (g + 1) * LANE)
            acc = ext[0:T, cols]
            for j in range(1, w):
                acc = acc + ext[pl.ds(j, T), cols]
            O[0][:, cols] = (acc - ddv[:, cols]).astype(MXU_DTYPE)

    return _rows(body, S, T, rows=[dd], nexts=[dd], out_rows=[(POOL_W, MXU_DTYPE)],
                 scratch=[pltpu.VMEM((T + HALO, POOL_W), F32)], name=name)[0]


def _lane_cumsum(seg, reverse=False):
    lane = lax.broadcasted_iota(jnp.int32, seg.shape, 1)
    sh = 1
    while sh < LANE:
        if reverse:
            seg = seg + jnp.where(lane < LANE - sh, pltpu.roll(seg, LANE - sh, axis=1), 0.0)
        else:
            seg = seg + jnp.where(lane >= sh, pltpu.roll(seg, sh, axis=1), 0.0)
        sh *= 2
    return seg


def _logf_cumsum(fT, bias, name):
    H, S = fT.shape
    TB = _tile(S)
    nb = S // TB

    def body(f_ref, b_ref, o_ref, carry):
        @pl.when(pl.program_id(0) == 0)
        def _():
            carry[...] = jnp.zeros_like(carry)

        x = f_ref[...] + b_ref[...]
        lf = jnp.minimum(x, 0.0) - jnp.log1p(jnp.exp(-jnp.abs(x)))
        c = carry[...]
        for j in range(TB // LANE):
            seg = _lane_cumsum(lf[:, j * LANE:(j + 1) * LANE]) + c
            o_ref[:, j * LANE:(j + 1) * LANE] = seg
            c = seg[:, LANE - 1:LANE]
        carry[...] = c

    return pl.pallas_call(
        body, grid=(nb,), in_specs=[pl.BlockSpec((H, TB), lambda i: (0, i)), pl.BlockSpec((H, 1), lambda i: (0, 0))],
        out_specs=pl.BlockSpec((H, TB), lambda i: (0, i)), out_shape=jax.ShapeDtypeStruct((H, S), F32),
        scratch_shapes=[pltpu.VMEM((H, 1), F32)], compiler_params=_params(("arbitrary",)), name=name)(fT, bias)


def _logf_cumsum_bwd(fT, bias, dc, dcq, name):
    H, S = fT.shape
    TB = _tile(S)
    nb = S // TB

    def body(f_ref, b_ref, dc_ref, dcq_ref, o_ref, db_ref, carry):
        @pl.when(pl.program_id(0) == 0)
        def _():
            carry[...] = jnp.zeros_like(carry)
            db_ref[...] = jnp.zeros_like(db_ref)

        x = f_ref[...] + b_ref[...]
        sg = _sigmoid(-x)
        dcv = dc_ref[...] + dcq_ref[...]
        c = carry[...]
        db = jnp.zeros((H, 1), F32)
        for j in reversed(range(TB // LANE)):
            seg = _lane_cumsum(dcv[:, j * LANE:(j + 1) * LANE], reverse=True) + c
            df = seg * sg[:, j * LANE:(j + 1) * LANE]
            o_ref[:, j * LANE:(j + 1) * LANE] = df
            db = db + jnp.sum(df, axis=1, keepdims=True)
            c = seg[:, 0:1]
        carry[...] = c
        db_ref[...] += db

    rev = lambda i: (0, nb - 1 - i)
    return pl.pallas_call(
        body, grid=(nb,),
        in_specs=[pl.BlockSpec((H, TB), rev), pl.BlockSpec((H, 1), lambda i: (0, 0)), pl.BlockSpec((H, TB), rev),
                  pl.BlockSpec((H, TB), rev)],
        out_specs=[pl.BlockSpec((H, TB), rev), pl.BlockSpec((H, 1), lambda i: (0, 0))],
        out_shape=[jax.ShapeDtypeStruct((H, S), F32), jax.ShapeDtypeStruct((H, 1), F32)],
        scratch_shapes=[pltpu.VMEM((H, 1), F32)], compiler_params=_params(("arbitrary",)), name=name)(
            fT, bias, dc, dcq)


def _attn_scores(q, k, ck, diagonal, T):
    s = _dot(q, k, NT) - ck
    if diagonal:
        tril = lax.broadcasted_iota(jnp.int32, (T, T), 1) <= lax.broadcasted_iota(jnp.int32, (T, T), 0)
        s = jnp.where(tril, s, NEG_BIG)
    return s


def _attn_fwd(q, k, v, c_row, name):
    H, S, Dh = q.shape
    T = _tile(S, (1024, 512, 256, 128))
    nq = S // T

    steps = jnp.asarray([[qi for qi in range(nq) for ki in range(qi + 1)],
                         [ki for qi in range(nq) for ki in range(qi + 1)]], jnp.int32)

    def body(st_ref, q_ref, k_ref, v_ref, ck_ref, o_ref, lse_ref, m_s, l_s, acc_s):
        qi, ki = st_ref[0, pl.program_id(1)], st_ref[1, pl.program_id(1)]

        @pl.when(ki == 0)
        def _():
            m_s[...] = jnp.full_like(m_s, NEG_BIG)
            l_s[...] = jnp.zeros_like(l_s)
            acc_s[...] = jnp.zeros_like(acc_s)

        def step(diagonal):
            s = _attn_scores(q_ref[0], k_ref[0], ck_ref[0], diagonal, T)
            m_new = jnp.maximum(m_s[...], jnp.max(s, axis=1, keepdims=True))
            alpha = jnp.exp(m_s[...] - m_new)
            p = jnp.exp(s - m_new)
            l_s[...] = alpha * l_s[...] + jnp.sum(p, axis=1, keepdims=True)
            acc_s[...] = alpha * acc_s[...] + _dot(p, v_ref[0])
            m_s[...] = m_new

        @pl.when(ki < qi)
        def _():
            step(False)

        @pl.when(ki == qi)
        def _():
            step(True)
            o_ref[0] = acc_s[...] / l_s[...]
            lse_ref[0] = m_s[...] + jnp.log(l_s[...])

    qmap = lambda h, s, st: (h, st[0, s], 0)
    kmap = lambda h, s, st: (h, st[1, s], 0)
    return pl.pallas_call(
        body, grid_spec=pltpu.PrefetchScalarGridSpec(
            num_scalar_prefetch=1, grid=(H, steps.shape[1]),
            in_specs=[pl.BlockSpec((1, T, Dh), qmap), pl.BlockSpec((1, T, Dh), kmap), pl.BlockSpec((1, T, Dh), kmap),
                      pl.BlockSpec((1, 1, T), lambda h, s, st: (h, 0, st[1, s]))],
            out_specs=[pl.BlockSpec((1, T, Dh), qmap), pl.BlockSpec((1, T, 1), qmap)],
            scratch_shapes=[pltpu.VMEM((T, 1), F32), pltpu.VMEM((T, 1), F32), pltpu.VMEM((T, Dh), F32)]),
        out_shape=[jax.ShapeDtypeStruct((H, S, Dh), F32), jax.ShapeDtypeStruct((H, S, 1), F32)],
        compiler_params=_params(("arbitrary", "arbitrary")), name=name)(steps, q, k, v, c_row)


def _attn_bwd(q, k, v, do, o, c_row, lse, name):
    H, S, Dh = q.shape
    T = _tile(S, (1024, 512, 256, 128))
    nq = S // T

    steps = jnp.asarray([[ki for ki in range(nq) for qi in range(ki, nq)],
                         [qi for ki in range(nq) for qi in range(ki, nq)]], jnp.int32)

    def body(st_ref, q_ref, k_ref, v_ref, do_ref, o_ref, ck_ref, lse_ref, dq_ref, dk_ref, dv_ref, dc_ref, dcq_ref,
             dk_s, dv_s, dc_s):
        ki, qi = st_ref[0, pl.program_id(1)], st_ref[1, pl.program_id(1)]

        @pl.when(pl.program_id(1) == 0)
        def _():
            dq_ref[...] = jnp.zeros_like(dq_ref)
            dcq_ref[...] = jnp.zeros_like(dcq_ref)

        @pl.when(qi == ki)
        def _():
            dk_s[...] = jnp.zeros_like(dk_s)
            dv_s[...] = jnp.zeros_like(dv_s)
            dc_s[...] = jnp.zeros_like(dc_s)

        def step(diagonal):
            qv, kv, vv = q_ref[0], k_ref[0], v_ref[0]
            s = _attn_scores(qv, kv, ck_ref[0], diagonal, T)
            p = jnp.exp(s - lse_ref[0])
            dov = do_ref[0]
            delta = jnp.sum(dov * o_ref[0], axis=1, keepdims=True)
            dv_s[...] += _dot(p.T, dov)
            dp = _dot(dov, vv, NT)
            ds = p * (dp - delta)
            dc_s[...] -= jnp.sum(ds, axis=0, keepdims=True)
            rows = pl.ds(pl.multiple_of(qi * T, T), T)
            dq_ref[0, rows, :] += _dot(ds, kv)
            dcq_ref[0, rows, :] += jnp.sum(ds, axis=1, keepdims=True)
            dk_s[...] += _dot(ds.T, qv)

        @pl.when(qi > ki)
        def _():
            step(False)

        @pl.when(qi == ki)
        def _():
            step(True)

        @pl.when(qi == nq - 1)
        def _():
            dk_ref[0] = dk_s[...]
            dv_ref[0] = dv_s[...]
            dc_ref[0] = dc_s[...]

    qmap = lambda h, s, st: (h, st[1, s], 0)
    kmap = lambda h, s, st: (h, st[0, s], 0)
    whole = lambda h, s, st: (h, 0, 0)
    return pl.pallas_call(
        body, grid_spec=pltpu.PrefetchScalarGridSpec(
            num_scalar_prefetch=1, grid=(H, steps.shape[1]),
            in_specs=[pl.BlockSpec((1, T, Dh), qmap), pl.BlockSpec((1, T, Dh), kmap), pl.BlockSpec((1, T, Dh), kmap),
                      pl.BlockSpec((1, T, Dh), qmap), pl.BlockSpec((1, T, Dh), qmap),
                      pl.BlockSpec((1, 1, T), lambda h, s, st: (h, 0, st[0, s])), pl.BlockSpec((1, T, 1), qmap)],
            out_specs=[pl.BlockSpec((1, S, Dh), whole), pl.BlockSpec((1, T, Dh), kmap), pl.BlockSpec((1, T, Dh), kmap),
                       pl.BlockSpec((1, 1, T), lambda h, s, st: (h, 0, st[0, s])), pl.BlockSpec((1, S, 1), whole)],
            scratch_shapes=[pltpu.VMEM((T, Dh), F32), pltpu.VMEM((T, Dh), F32), pltpu.VMEM((1, T), F32)]),
        out_shape=[jax.ShapeDtypeStruct((H, S, Dh), F32), jax.ShapeDtypeStruct((H, S, Dh), F32),
                   jax.ShapeDtypeStruct((H, S, Dh), F32), jax.ShapeDtypeStruct((H, 1, S), F32),
                   jax.ShapeDtypeStruct((H, S, 1), F32)],
        compiler_params=_params(("arbitrary", "arbitrary")), name=name)(steps, q, k, v, do, o, c_row, lse)


CONV_COLS = 512


def _conv_bwd_b(dpre, w, name):
    S, C = dpre.shape
    K = w.shape[0]
    T = _tile(S, (256, 128))

    def body(i, n, R, Cs, P, N, O, A, Sc):
        def tile(r0, first, last):
            for c0 in range(0, C, CONV_COLS):
                cols = slice(c0, c0 + CONV_COLS)
                win = _win_next(R[0], N[0], i, n, r0, last, cols)
                out = None
                for k in range(K):
                    term = _ahead(win, K - 1 - k) * Cs[0][k:k + 1, cols]
                    out = term if out is None else out + term
                O[0][pl.ds(r0, SUB_ROWS), cols] = out.astype(MXU_DTYPE)

        _tile_loop(T, tile)

    return _rows(body, S, T, rows=[dpre], nexts=[dpre], consts=[w], out_rows=[(C, MXU_DTYPE)], name=name)[0]


def _dt_mask():
    lane = lax.broadcasted_iota(jnp.int32, (1, LANE), 1)
    return ((lane >= DT_LANE0) & (lane < DT_LANE0 + SSD_HEADS)).astype(F32)


def _ssd_pre_fwd(xbc, fdt, cw, cb, dtb, name):
    S, C = xbc.shape
    T = _tile(S, (256, 128))
    K = cw.shape[0]

    def body(i, n, R, Cs, P, N, O, A, Sc):
        def tile(r0, first, last):
            rows = pl.ds(r0, SUB_ROWS)
            for c0 in range(0, C, CONV_COLS):
                cols = slice(c0, c0 + CONV_COLS)
                pre = _conv_win(_win_prev(R[0], P[0], i, r0, first, cols), Cs[0], Cs[1], cols)
                O[0][rows, cols] = pre * _sigmoid(pre)
            O[1][rows, :] = _softplus(R[1][rows, :] + Cs[2][...]) * _dt_mask()

        _tile_loop(T, tile)

    return _rows(body, S, T, rows=[xbc, fdt], prevs=[xbc], consts=[cw, cb, dtb],
                 out_rows=[(C, F32), (LANE, F32)], name=name)


def _silu_grad(pre):
    sg = _sigmoid(pre)
    return sg * (1.0 + pre * (1.0 - sg))


def _ssd_pre_bwd_a(xbc, fdt, dxa, ddtw, cw, cb, dtb, name):
    S, C = xbc.shape
    T = _tile(S, (256, 128))
    K = cw.shape[0]

    def body(i, n, R, Cs, P, N, O, A, Sc):
        def tile(r0, first, last):
            rows = pl.ds(r0, SUB_ROWS)
            for c0 in range(0, C, CONV_COLS):
                cols = slice(c0, c0 + CONV_COLS)
                win = _win_prev(R[0], P[0], i, r0, first, cols)
                dpre = R[2][rows, cols] * _silu_grad(_conv_win(win, Cs[0], Cs[1], cols))
                O[0][rows, cols] = dpre
                _conv_wgrad_win(A[0], win, dpre, cols)
                A[1][:, cols] += jnp.sum(dpre, axis=0, keepdims=True)
            ddt = R[3][rows, :] * _sigmoid(R[1][rows, :] + Cs[2][...]) * _dt_mask()
            O[1][rows, :] = ddt
            A[2][...] += jnp.sum(ddt, axis=0, keepdims=True)

        _tile_loop(T, tile)

    return _rows(body, S, T, rows=[xbc, fdt, dxa, ddtw], prevs=[xbc], consts=[cw, cb, dtb],
                 out_rows=[(C, F32), (LANE, F32)],
                 out_accs=[((K, C), F32), ((1, C), F32), ((1, LANE), F32)], name=name)


def _split3(x):
    hi = x.astype(jnp.bfloat16)
    r1 = x - hi.astype(F32)
    mid = r1.astype(jnp.bfloat16)
    lo = (r1 - mid.astype(F32)).astype(jnp.bfloat16)
    return hi, mid, lo


def _expand_mat():
    r = lax.broadcasted_iota(jnp.int32, (LANE, SSD_W), 0)
    c = lax.broadcasted_iota(jnp.int32, (LANE, SSD_W), 1)
    return (r - DT_LANE0 == c // SSD_P).astype(jnp.bfloat16)


def _headsum_mat():
    r = lax.broadcasted_iota(jnp.int32, (SSD_W, LANE), 0)
    c = lax.broadcasted_iota(jnp.int32, (SSD_W, LANE), 1)
    return (c - DT_LANE0 == r // SSD_P).astype(jnp.bfloat16)


def _expand(tile, ex):
    return sum(lax.dot_general(part, ex, (((1,), (0,)), ((), ())), preferred_element_type=F32)
               for part in _split3(tile))


def _headsum(full, hs):
    return sum(lax.dot_general(part, hs, (((1,), (0,)), ((), ())), preferred_element_type=F32)
               for part in _split3(full))


def _sub_cumsum(a, reverse=False):
    n = a.shape[0]
    row = lax.broadcasted_iota(jnp.int32, a.shape, 0)
    sh = 1
    while sh < n:
        if reverse:
            a = a + jnp.where(row < n - sh, pltpu.roll(a, n - sh, axis=0), 0.0)
        else:
            a = a + jnp.where(row >= sh, pltpu.roll(a, sh, axis=0), 0.0)
        sh *= 2
    return a


def _chunk_common(xa_ref, dtw_ref, a_row, ex):
    L = SSD_CHUNK
    xs = xa_ref[:, 0:SSD_W]
    dtv = dtw_ref[...]
    acs = _sub_cumsum(dtv * a_row)
    last = acs[L - 1:L, :]
    dt_full = _expand(dtv, ex)
    xd = xs * dt_full
    dec_full = _expand(jnp.exp(last - acs), ex)
    e_full = _expand(jnp.exp(acs), ex)
    elast_full = _expand(jnp.broadcast_to(jnp.exp(last), (8, LANE)), ex)[0:1, :]
    return xs, dtv, acs, last, dt_full, xd, dec_full, e_full, elast_full


def _decay_mask(acs, acsT, col):
    L = SSD_CHUNK
    diff = acs[:, col:col + 1] - acsT[col:col + 1, :]
    tril = lax.broadcasted_iota(jnp.int32, (L, L), 0) >= lax.broadcasted_iota(jnp.int32, (L, L), 1)
    return jnp.where(tril, jnp.exp(jnp.minimum(diff, 0.0)), 0.0)


def _half_mask(h):
    lane = lax.broadcasted_iota(jnp.int32, (1, LANE), 1)
    return ((lane // SSD_P) == (h % 2)).astype(F32)


def _ssd_chunk_fwd(xa, dtw, a_row, d_full, name):
    S = xa.shape[0]
    L, G = SSD_CHUNK, 2
    nc = S // L
    GW = SSD_W // G

    def body(xa_ref, dtw_ref, a_ref, d_ref, y_ref, hp_ref, state):
        @pl.when(pl.program_id(0) == 0)
        def _():
            state[...] = jnp.zeros_like(state)

        ex = _expand_mat()
        xs, dtv, acs, last, dt_full, xd, dec_full, e_full, elast_full = _chunk_common(xa_ref, dtw_ref, a_ref[...], ex)
        acsT = acs.T
        hp_ref[0] = state[...]
        for g in range(G):
            gc = slice(g * GW, (g + 1) * GW)
            Bg = xa_ref[:, SSD_W + g * SSD_N: SSD_W + (g + 1) * SSD_N]
            Cg = xa_ref[:, SSD_W + G * SSD_N + g * SSD_N: SSD_W + G * SSD_N + (g + 1) * SSD_N]
            cb = _dot(Cg, Bg, NT)
            y_off = e_full[:, gc] * _dot(Cg, state[:, gc])
            for hp in range(GW // LANE):
                pc = slice(g * GW + hp * LANE, g * GW + (hp + 1) * LANE)
                xd_pair = xd[:, pc]
                yp = y_off[:, hp * LANE:(hp + 1) * LANE] + d_ref[:, pc] * xs[:, pc]
                for h2 in range(2):
                    h = (g * GW + hp * LANE) // SSD_P + h2
                    m = cb * _decay_mask(acs, acsT, DT_LANE0 + h)
                    yp = yp + _dot(m, xd_pair * _half_mask(h))
                y_ref[:, pc] = yp
            st_new = _dot(Bg.T, xd[:, gc] * dec_full[:, gc])
            state[:, gc] = elast_full[:, gc] * state[:, gc] + st_new

    return pl.pallas_call(
        body, grid=(nc,),
        in_specs=[pl.BlockSpec((L, SSD_CONV_CH), lambda c: (c, 0)), pl.BlockSpec((L, LANE), lambda c: (c, 0)),
                  pl.BlockSpec((1, LANE), lambda c: (0, 0)), pl.BlockSpec((1, SSD_W), lambda c: (0, 0))],
        out_specs=[pl.BlockSpec((L, SSD_W), lambda c: (c, 0)), pl.BlockSpec((1, SSD_N, SSD_W), lambda c: (c, 0, 0))],
        out_shape=[jax.ShapeDtypeStruct((S, SSD_W), F32), jax.ShapeDtypeStruct((nc, SSD_N, SSD_W), F32)],
        scratch_shapes=[pltpu.VMEM((SSD_N, SSD_W), F32)],
        compiler_params=_params(("arbitrary",)), name=name)(xa, dtw, a_row, d_full)


def _ssd_chunk_bwd(xa, dtw, dy, hprev, a_row, d_full, name):
    S = xa.shape[0]
    L, G = SSD_CHUNK, 2
    nc = S // L
    GW = SSD_W // G

    def body(xa_ref, dtw_ref, dy_ref, hp_ref, a_ref, d_ref, dxa_ref, ddt_ref, da_ref, dd_ref, dstate):
        @pl.when(pl.program_id(0) == 0)
        def _():
            dstate[...] = jnp.zeros_like(dstate)
            da_ref[...] = jnp.zeros_like(da_ref)
            dd_ref[...] = jnp.zeros_like(dd_ref)

        ex, hs = _expand_mat(), _headsum_mat()
        a_row = a_ref[...]
        xs, dtv, acs, last, dt_full, xd, dec_full, e_full, elast_full = _chunk_common(xa_ref, dtw_ref, a_row, ex)
        acsT = acs.T
        dyv = dy_ref[...]
        lane = lax.broadcasted_iota(jnp.int32, (L, LANE), 1)
        sub = lax.broadcasted_iota(jnp.int32, (LANE, L), 0)
        dacs_c = jnp.zeros((L, LANE), F32)
        dacs_r = jnp.zeros((LANE, L), F32)
        dd_ref[...] += jnp.sum(_headsum(dyv * xs, hs), axis=0, keepdims=True)
        dxd_parts, yoff_parts, dxdd_parts, hh_parts = [], [], [], []
        for g in range(G):
            gc = slice(g * GW, (g + 1) * GW)
            b0 = SSD_W + g * SSD_N
            c0 = SSD_W + G * SSD_N + g * SSD_N
            Bg = xa_ref[:, b0:b0 + SSD_N]
            Cg = xa_ref[:, c0:c0 + SSD_N]
            Hp = hp_ref[0, :, gc]
            dH = dstate[:, gc]
            cb = _dot(Cg, Bg, NT)
            Gm = _dot(Cg, Hp)
            yoff_parts.append(e_full[:, gc] * Gm)
            dG = e_full[:, gc] * dyv[:, gc]
            dC = _dot(dG, Hp, NT)
            dHp = _dot(Cg.T, dG)
            xdd = xd[:, gc] * dec_full[:, gc]
            dB = _dot(xdd, dH, NT)
            dxdd = _dot(Bg, dH)
            dxdd_parts.append(dxdd)
            hh_parts.append(dH * Hp)
            dstate[:, gc] = dHp + elast_full[:, gc] * dH
            dcb = jnp.zeros((L, L), F32)
            dxd_g = []
            for hp in range(GW // LANE):
                pc = slice(g * GW + hp * LANE, g * GW + (hp + 1) * LANE)
                xd_pair = xd[:, pc]
                dxd_pair = dxdd[:, hp * LANE:(hp + 1) * LANE] * dec_full[:, pc]
                for h2 in range(2):
                    h = (g * GW + hp * LANE) // SSD_P + h2
                    col = DT_LANE0 + h
                    lm = _decay_mask(acs, acsT, col)
                    m = cb * lm
                    dy_h = dyv[:, pc] * _half_mask(h)
                    dm = _dot(dy_h, xd_pair, NT)
                    dxd_pair = dxd_pair + _dot(m.T, dy_h)
                    wm = dm * m
                    dacs_c = dacs_c + jnp.where(lane == col, jnp.sum(wm, axis=1, keepdims=True), 0.0)
                    dacs_r = dacs_r - jnp.where(sub == col, jnp.sum(wm, axis=0, keepdims=True), 0.0)
                    dcb = dcb + dm * lm
                dxd_g.append(dxd_pair)
            dxd_parts.append(jnp.concatenate(dxd_g, axis=1))
            dxa_ref[:, c0:c0 + SSD_N] = dC + _dot(dcb, Bg)
            dxa_ref[:, b0:b0 + SSD_N] = dB + _dot(dcb.T, Cg)
        dxd = jnp.concatenate(dxd_parts, axis=1)
        y_off = jnp.concatenate(yoff_parts, axis=1)
        dxdd_full = jnp.concatenate(dxdd_parts, axis=1)
        hh = jnp.concatenate(hh_parts, axis=1)
        dxa_ref[:, 0:SSD_W] = d_ref[...] * dyv + dxd * dt_full
        ddt = _headsum(dxd * xs, hs)
        w_dec = _headsum(dxdd_full * xd, hs) * jnp.exp(last - acs)
        dlast = jnp.sum(w_dec, axis=0, keepdims=True) + jnp.exp(last) * jnp.sum(_headsum(hh, hs), axis=0, keepdims=True)
        dacs = dacs_c + dacs_r.T + _headsum(dyv * y_off, hs) - w_dec
        rowid = lax.broadcasted_iota(jnp.int32, (L, LANE), 0)
        dacs = dacs + jnp.where(rowid == L - 1, dlast, 0.0)
        da = _sub_cumsum(dacs, reverse=True)
        ddt_ref[...] = ddt + da * a_row
        da_ref[...] += jnp.sum(da * dtv, axis=0, keepdims=True)

    rev = lambda c: (nc - 1 - c, 0)
    return pl.pallas_call(
        body, grid=(nc,),
        in_specs=[pl.BlockSpec((L, SSD_CONV_CH), rev), pl.BlockSpec((L, LANE), rev), pl.BlockSpec((L, SSD_W), rev),
                  pl.BlockSpec((1, SSD_N, SSD_W), lambda c: (nc - 1 - c, 0, 0)),
                  pl.BlockSpec((1, LANE), lambda c: (0, 0)), pl.BlockSpec((1, SSD_W), lambda c: (0, 0))],
        out_specs=[pl.BlockSpec((L, SSD_CONV_CH), rev), pl.BlockSpec((L, LANE), rev),
                   pl.BlockSpec((1, LANE), lambda c: (0, 0)), pl.BlockSpec((1, LANE), lambda c: (0, 0))],
        out_shape=[jax.ShapeDtypeStruct((S, SSD_CONV_CH), F32), jax.ShapeDtypeStruct((S, LANE), F32),
                   jax.ShapeDtypeStruct((1, LANE), F32), jax.ShapeDtypeStruct((1, LANE), F32)],
        scratch_shapes=[pltpu.VMEM((SSD_N, SSD_W), F32)],
        compiler_params=_params(("arbitrary",)), name=name)(xa, dtw, dy, hprev, a_row, d_full)


def _ssd_post_fwd(y, z, w, name):
    S = y.shape[0]
    GW = SSD_W // 2

    def body(i, n, R, C, P, N, O, A, Sc):
        def tile(r0, first, last):
            rows = pl.ds(r0, SUB_ROWS)
            for g in range(2):
                gc = slice(g * GW, (g + 1) * GW)
                zv = R[1][rows, gc]
                vg = R[0][rows, gc] * (zv * _sigmoid(zv))
                r = lax.rsqrt(jnp.mean(vg * vg, axis=-1, keepdims=True) + NORM_EPS)
                O[0][rows, gc] = (vg * r * C[0][:, gc]).astype(MXU_DTYPE)

        _tile_loop(T, tile)

    T = _tile(S, (256, 128))
    return _rows(body, S, T, rows=[y, z], consts=[w], out_rows=[(SSD_W, MXU_DTYPE)], name=name)[0]


def _ssd_post_bwd(y, z, dyn, w, name):
    S = y.shape[0]
    GW = SSD_W // 2

    def body(i, n, R, C, P, N, O, A, Sc):
        def tile(r0, first, last):
            rows = pl.ds(r0, SUB_ROWS)
            for g in range(2):
                gc = slice(g * GW, (g + 1) * GW)
                yv, zv = R[0][rows, gc], R[1][rows, gc]
                sz = zv * _sigmoid(zv)
                dv, dw = _norm_bwd_math(yv * sz, C[0][:, gc], R[2][rows, gc])
                A[0][:, gc] += dw
                O[0][rows, gc] = dv * sz
                O[1][rows, gc] = (dv * yv * _silu_grad(zv)).astype(MXU_DTYPE)

        _tile_loop(T, tile)

    T = _tile(S, (256, 128))
    return _rows(body, S, T, rows=[y, z, dyn], consts=[w],
                 out_rows=[(SSD_W, F32), (SSD_W, MXU_DTYPE)], out_accs=[((1, SSD_W), F32)], name=name)


def _merge_fwd(gl, yp, ya, ys, name):
    S, D = yp.shape

    def body(i, n, R, C, P, N, O, A, Sc):
        def tile(r0, first, last):
            rows = pl.ds(r0, SUB_ROWS)
            for c0 in range(0, D, CONV_COLS):
                cols = slice(c0, c0 + CONV_COLS)
                acc = None
                for b in range(3):
                    term = _sigmoid(R[0][rows, b * D + c0:b * D + c0 + CONV_COLS]) * R[1 + b][rows, cols]
                    acc = term if acc is None else acc + term
                O[0][rows, cols] = acc.astype(MXU_DTYPE)

        _tile_loop(T, tile)

    T = _tile(S, (256, 128))
    return _rows(body, S, T, rows=[gl, yp, ya, ys], out_rows=[(D, MXU_DTYPE)], name=name)[0]


def _merge_bwd(gl, yp, ya, ys, dm, name):
    S, D = yp.shape
    T = _tile(S, (256, 128))

    def body(i, n, R, C, P, N, O, A, Sc):
        def tile(r0, first, last):
            rows = pl.ds(r0, SUB_ROWS)
            for c0 in range(0, D, CONV_COLS):
                cols = slice(c0, c0 + CONV_COLS)
                dmv = R[4][rows, cols]
                for b in range(3):
                    gcols = slice(b * D + c0, b * D + c0 + CONV_COLS)
                    gt = _sigmoid(R[0][rows, gcols])
                    O[b][rows, cols] = (gt * dmv).astype(MXU_DTYPE)
                    O[3][rows, gcols] = (dmv * R[1 + b][rows, cols] * gt * (1.0 - gt)).astype(MXU_DTYPE)

        _tile_loop(T, tile)

    return _rows(body, S, T, rows=[gl, yp, ya, ys, dm],
                 out_rows=[(D, MXU_DTYPE)] * 3 + [(3 * D, MXU_DTYPE)], name=name)


FFN_COLS = 256


def _ffn_act_fwd(hpre, cw, cb, name):
    S, C = hpre.shape
    T = _tile(S, (256, 128))
    Fd = C // 2

    def body(i, n, R, Cs, P, N, O, A, Sc):
        def tile(r0, first, last):
            for c0 in range(0, Fd, FFN_COLS):
                gcols, vcols = slice(c0, c0 + FFN_COLS), slice(Fd + c0, Fd + c0 + FFN_COLS)
                gt = _conv_win(_win_prev(R[0], P[0], i, r0, first, gcols), Cs[0], Cs[1], gcols)
                val = _conv_win(_win_prev(R[0], P[0], i, r0, first, vcols), Cs[0], Cs[1], vcols)
                O[0][pl.ds(r0, SUB_ROWS), gcols] = (gt * _sigmoid(gt) * val).astype(MXU_DTYPE)

        _tile_loop(T, tile)

    return _rows(body, S, T, rows=[hpre], prevs=[hpre], consts=[cw, cb], out_rows=[(Fd, MXU_DTYPE)], name=name)[0]


def _ffn_act_bwd_a(hpre, dact, cw, cb, name):
    S, C = hpre.shape
    K = cw.shape[0]
    T = _tile(S, (256, 128))
    Fd = C // 2

    def body(i, n, R, Cs, P, N, O, A, Sc):
        def tile(r0, first, last):
            rows = pl.ds(r0, SUB_ROWS)
            for c0 in range(0, Fd, FFN_COLS):
                gcols, vcols = slice(c0, c0 + FFN_COLS), slice(Fd + c0, Fd + c0 + FFN_COLS)
                gwin = _win_prev(R[0], P[0], i, r0, first, gcols)
                vwin = _win_prev(R[0], P[0], i, r0, first, vcols)
                gt = _conv_win(gwin, Cs[0], Cs[1], gcols)
                val = _conv_win(vwin, Cs[0], Cs[1], vcols)
                da = R[1][rows, gcols]
                for cols, win, d in ((gcols, gwin, da * val * _silu_grad(gt)), (vcols, vwin, da * gt * _sigmoid(gt))):
                    O[0][rows, cols] = d
                    _conv_wgrad_win(A[0], win, d, cols)
                    A[1][:, cols] += jnp.sum(d, axis=0, keepdims=True)

        _tile_loop(T, tile)

    return _rows(body, S, T, rows=[hpre, dact], prevs=[hpre], consts=[cw, cb], out_rows=[(C, F32)],
                 out_accs=[((K, C), F32), ((1, C), F32)], name=name)


def _adamw_math(g, w, m, v):
    c1 = 1.0 - ADAM_B1 ** ADAM_STEP
    c2 = 1.0 - ADAM_B2 ** ADAM_STEP
    mn = ADAM_B1 * m + (1.0 - ADAM_B1) * g
    vn = ADAM_B2 * v + (1.0 - ADAM_B2) * (g * g)
    return -ADAM_LR * ((mn / c1) / (jnp.sqrt(vn / c2) + ADAM_EPS) + ADAM_WD * w), mn, vn


def _adamw(g, w, m, v, name):
    R_, W = g.shape

    def body(i, n, R, C, P, N, O, A, Sc):
        O[0][...], O[1][...], O[2][...] = _adamw_math(R[0][...], R[1][...], R[2][...], R[3][...])

    return _rows(body, R_, _row_tile(R_, W, 7), rows=[g, w, m, v], out_rows=[(W, F32)] * 3, name=name)


def _adamw_pair(q, other, w, m, v, pos, name):
    _, R_, W = w.shape
    T = _row_tile(R_, W, 9)

    def body(pos_ref, q_ref, o_ref, w_ref, m_ref, v_ref, g_out, d_out, m_out, v_out):
        g = jnp.where(pl.program_id(0) == pos_ref[1], q_ref[...], o_ref[...])
        g_out[0] = g
        d_out[0], m_out[0], v_out[0] = _adamw_math(g, w_ref[0], m_ref[0], v_ref[0])

    flat = pl.BlockSpec((T, W), lambda l, i, pos: (i, 0))
    full = pl.BlockSpec((1, T, W), lambda l, i, pos: (l, i, 0))
    return _scalar_call(body, pos, (2, R_ // T), [flat, flat, full, full, full], [full] * 4,
                        [jax.ShapeDtypeStruct(w.shape, F32)] * 4, (q, other, w, m, v), name)


def _row_tile(rows, width, n_blocks, budget=14 * 1024 * 1024):
    wpad = -(-width // LANE) * LANE
    for t in (512, 256, 128, 64, 32, 16, 8):
        if rows % t == 0 and n_blocks * t * wpad * 4 <= budget:
            return t
    return rows


_ANY = pl.BlockSpec(memory_space=pl.ANY)
_MESH = pl.DeviceIdType.MESH


DMA_CHUNK_BYTES = 2 * 1024 * 1024


def _row_chunks(shape, dtype):
    r = shape[-2]
    total = 1
    for s in shape:
        total *= s
    want = max(1, (total * jnp.dtype(dtype).itemsize) // DMA_CHUNK_BYTES)
    n = 1
    while n * 2 <= want and r % (n * 2 * 16) == 0 and n < 8:
        n *= 2
    return [(j * (r // n), r // n) for j in range(n)]


def _comm_call(plan, srcs, out_shapes, name):
    n = len(srcs)
    probe = plan(0, 0, 0, [_ShapeOnly(s.shape) for s in srcs], [_ShapeOnly(s.shape) for s in out_shapes])
    n_local, n_remote = len(probe[0]), len(probe[1])

    def body(*refs):
        src_refs, out_refs = refs[:n], refs[n:2 * n]
        send_sems, recv_sems, local_sems = refs[2 * n:]
        x, y, c = lax.axis_index("x"), lax.axis_index("y"), lax.axis_index("c")
        local, remote = plan(x, y, c, src_refs, out_refs)
        started = []
        for j, (s, d) in enumerate(local):
            cp = pltpu.make_async_copy(s, d, local_sems.at[j])
            cp.start()
            started.append(cp)
        sent = []
        for j, (s, d, peer) in enumerate(remote):
            cp = pltpu.make_async_remote_copy(src_ref=s, dst_ref=d, send_sem=send_sems.at[j], recv_sem=recv_sems.at[j],
                                              device_id=peer, device_id_type=_MESH)
            cp.start()
            sent.append(cp)
        for cp in sent:
            cp.wait()
        for cp in started:
            cp.wait()

    return pl.pallas_call(
        body, in_specs=[_ANY] * n, out_specs=[_ANY] * n,
        out_shape=[jax.ShapeDtypeStruct(s.shape, s.dtype) for s in out_shapes],
        scratch_shapes=[pltpu.SemaphoreType.DMA((n_remote,)), pltpu.SemaphoreType.DMA((n_remote,)),
                        pltpu.SemaphoreType.DMA((max(n_local, 1),))], name=name)(*srcs)


class _ShapeOnly:
    def __init__(self, shape):
        self.shape = tuple(shape)

    @property
    def at(self):
        return self

    def __getitem__(self, idx):
        return self


def _other_places(x, y):
    return [(1 - x, y), (x, 1 - y), (1 - x, 1 - y)]


def _gather_places(shards, row_major, name):
    n = len(shards)
    outs = []
    for s, rm in zip(shards, row_major):
        L_, r, c_ = s.shape
        assert L_ == 2
        outs.append(jax.ShapeDtypeStruct((L_, N_PLACES, r, c_) if rm else (N_PLACES, L_, r, c_), s.dtype))
    n_copies = 3 * sum(len(_row_chunks(s.shape[1:], s.dtype)) for s in shards)

    def body(*refs):
        src_refs, out_refs = refs[:n], refs[n:2 * n]
        ici_send, ici_recv, d2d_send, d2d_recv = refs[2 * n:]
        x, y, c = lax.axis_index("x"), lax.axis_index("y"), lax.axis_index("c")
        me = 2 * x + y

        def slot(o_ref, rm, place, layer, r0, rn):
            return o_ref.at[layer, place, pl.ds(r0, rn), :] if rm else o_ref.at[place, layer, pl.ds(r0, rn), :]

        over_ici, landed = [], []
        for s_ref, o_ref, rm, s in zip(src_refs, out_refs, row_major, shards):
            for r0, rn in _row_chunks(s.shape[1:], s.dtype):
                for px, py in _other_places(x, y):
                    j = len(over_ici)
                    cp = pltpu.make_async_remote_copy(
                        src_ref=s_ref.at[c, pl.ds(r0, rn), :], dst_ref=slot(o_ref, rm, me, c, r0, rn),
                        send_sem=ici_send.at[j], recv_sem=ici_recv.at[j], device_id=(px, py, c), device_id_type=_MESH)
                    cp.start()
                    over_ici.append(cp)
                    landed.append((o_ref, rm, 2 * px + py, r0, rn))
        passed = []
        for j, (o_ref, rm, place, r0, rn) in enumerate(landed):
            pltpu.make_async_remote_copy(
                src_ref=slot(o_ref, rm, place, c, r0, rn), dst_ref=slot(o_ref, rm, place, c, r0, rn),
                send_sem=ici_send.at[j], recv_sem=ici_recv.at[j], device_id=(x, y, c), device_id_type=_MESH).wait_recv()
            cp = pltpu.make_async_remote_copy(
                src_ref=slot(o_ref, rm, place, c, r0, rn), dst_ref=slot(o_ref, rm, place, c, r0, rn),
                send_sem=d2d_send.at[j], recv_sem=d2d_recv.at[j], device_id=(x, y, 1 - c), device_id_type=_MESH)
            cp.start()
            passed.append(cp)
        for j, (o_ref, rm, place, r0, rn) in enumerate(landed):
            pltpu.make_async_remote_copy(
                src_ref=slot(o_ref, rm, place, 1 - c, r0, rn), dst_ref=slot(o_ref, rm, place, 1 - c, r0, rn),
                send_sem=d2d_send.at[j], recv_sem=d2d_recv.at[j], device_id=(x, y, 1 - c), device_id_type=_MESH).wait_recv()
        for cp in over_ici + passed:
            cp.wait_send()

    return pl.pallas_call(
        body, in_specs=[_ANY] * n, out_specs=[_ANY] * n, out_shape=outs,
        scratch_shapes=[pltpu.SemaphoreType.DMA((n_copies,))] * 4, name=name)(*shards)


def _reduce_sibling(gs, name):
    outs = [jax.ShapeDtypeStruct((N_PLACES,) + g.shape[2:], g.dtype) for g in gs]

    def plan(x, y, c, src_refs, out_refs):
        remote = []
        for g_ref, o_ref, g in zip(src_refs, out_refs, gs):
            for r0, rn in _row_chunks(g.shape[2:], g.dtype):
                for p in range(N_PLACES):
                    remote.append((g_ref.at[p, 1 - c, pl.ds(r0, rn), :], o_ref.at[p, pl.ds(r0, rn), :], (x, y, 1 - c)))
        return [], remote

    return _comm_call(plan, gs, outs, name)


def _reduce_places(hs, name):
    outs = [jax.ShapeDtypeStruct((3,) + h.shape[1:], h.dtype) for h in hs]

    def plan(x, y, c, src_refs, out_refs):
        remote = []
        for h_ref, o_ref, h in zip(src_refs, out_refs, hs):
            for r0, rn in _row_chunks(h.shape[1:], h.dtype):
                for j, (px, py) in enumerate(_other_places(x, y)):
                    remote.append((h_ref.at[2 * px + py, pl.ds(r0, rn), :], o_ref.at[j, pl.ds(r0, rn), :], (px, py, c)))
        return [], remote

    return _comm_call(plan, hs, outs, name)


def _swap_sibling(qs, name):
    def plan(x, y, c, src_refs, out_refs):
        remote = []
        for q_ref, o_ref, q in zip(src_refs, out_refs, qs):
            for r0, rn in _row_chunks(q.shape, q.dtype):
                remote.append((q_ref.at[pl.ds(r0, rn), :], o_ref.at[pl.ds(r0, rn), :], (x, y, 1 - c)))
        return [], remote

    return _comm_call(plan, qs, qs, name)


def _scalar_call(body, scalars, grid, in_specs, out_specs, out_shape, args, name):
    return pl.pallas_call(
        body, grid_spec=pltpu.PrefetchScalarGridSpec(num_scalar_prefetch=1, grid=grid, in_specs=in_specs,
                                                     out_specs=out_specs),
        out_shape=out_shape, compiler_params=_params(("arbitrary",) * len(grid)), name=name)(scalars, *args)


def _add_own_slot(g, r_, pos, out_dtype, name):
    P_, _, R_, W = g.shape
    T = _row_tile(R_, W, 3)

    def body(pos_ref, g_ref, r_ref, o_ref):
        o_ref[...] = (g_ref[0] + r_ref[...]).astype(out_dtype)

    return _scalar_call(
        body, pos, (P_, R_ // T),
        [pl.BlockSpec((1, 1, T, W), lambda p, i, pos: (p, pos[1], i, 0)), pl.BlockSpec((1, T, W), lambda p, i, pos: (p, i, 0))],
        pl.BlockSpec((1, T, W), lambda p, i, pos: (p, i, 0)), jax.ShapeDtypeStruct((P_, R_, W), out_dtype), (g, r_), name)


def _sum_places(g, r_, recv, pos, name):
    _, _, R_, W = g.shape
    T = _row_tile(R_, W, 5)

    def body(pos_ref, g_ref, r_ref, recv_ref, o_ref):
        for m in range(N_PLACES):
            @pl.when(pos_ref[0] == m)
            def _(m=m):
                acc = None
                for p in range(N_PLACES):
                    if p == m:
                        term = g_ref[0, 0] + r_ref[0]
                    else:
                        dx, dy = (p >> 1) != (m >> 1), (p & 1) != (m & 1)
                        term = recv_ref[0 if (dx and not dy) else 1 if (dy and not dx) else 2].astype(F32)
                    acc = term if acc is None else acc + term
                o_ref[...] = acc

    return _scalar_call(
        body, pos, (R_ // T,),
        [pl.BlockSpec((1, 1, T, W), lambda i, pos: (pos[0], pos[1], i, 0)),
         pl.BlockSpec((1, T, W), lambda i, pos: (pos[0], i, 0)), pl.BlockSpec((3, T, W), lambda i, pos: (0, i, 0))],
        pl.BlockSpec((T, W), lambda i, pos: (i, 0)), jax.ShapeDtypeStruct((R_, W), F32), (g, r_, recv), name)


def _to_heads(a, dtype):
    S = a.shape[0]
    return a.reshape(S, HEADS, HEAD_DIM).transpose(1, 0, 2).astype(dtype)


def _from_heads(a):
    return a.transpose(1, 0, 2).reshape(a.shape[1], HEADS * HEAD_DIM)


def _lane_tile(vec16):
    return jnp.concatenate([jnp.zeros((DT_LANE0,), F32), vec16,
                            jnp.zeros((LANE - DT_LANE0 - SSD_HEADS,), F32)])[None]


def _layer_consts(W, l):
    return dict(
        norm_mix=W['norm_mix'][l][None], mix=W['pool_mix'][l].astype(MXU_DTYPE), scale=W['pool_scale'][l][None],
        f_bias=W['f_bias'][l][:, None], cw=W['ssd_conv_w'][l], cb=W['ssd_conv_b'][l][None],
        dtb=_lane_tile(W['ssd_dt_bias'][l]), a_row=_lane_tile(-jnp.exp(W['ssd_a_log'][l])),
        d_full=jnp.repeat(W['ssd_d'][l], SSD_P)[None], ssd_norm=W['ssd_norm'][l][None],
        norm_ffn=W['norm_ffn'][l][None], fcw=W['ffn_conv_w'][l], fcb=W['ffn_conv_b'][l][None])


def _layer_fwd(x, W, l):
    n = f"l{l}_"
    cs = _layer_consts(W, l)
    win = {k: v[l] for k, v in W['w_in'].items()}
    u = _norm_fwd(x, cs['norm_mix'], n + "norm_mix")
    pqkv = _mm(u, win['p'], name=n + "in_p")
    z = _mm(u, win['z'], name=n + "in_z")
    xbc = _mm(u, win['x'], name=n + "in_x")
    gl = _mm(u, win['g'], name=n + "in_g")
    fdt = _mm(u, win['f'], name=n + "in_f")
    d, ypm = _pool_fwd(pqkv, cs['mix'], cs['scale'], n + "pool")
    yp = _mm(ypm, W['p_pool'][l], name=n + "p_pool")
    fT = fdt[:, :HEADS].T
    c = _logf_cumsum(fT, cs['f_bias'], n + "logf")
    c_row = c[:, None, :]
    qh = _to_heads(pqkv[:, ATTN_W:2 * ATTN_W] * ATTN_SCALE, MXU_DTYPE)
    kh, vh = (_to_heads(pqkv[:, (2 + j) * ATTN_W:(3 + j) * ATTN_W], MXU_DTYPE) for j in range(2))
    oh, lse = _attn_fwd(qh, kh, vh, c_row, n + "attn")
    o = _from_heads(oh)
    ya = _mm(o, W['p_attn'][l], name=n + "p_attn")
    xa, dtw = _ssd_pre_fwd(xbc, fdt, cs['cw'], cs['cb'], cs['dtb'], n + "ssd_pre")
    y, hprev = _ssd_chunk_fwd(xa, dtw, cs['a_row'], cs['d_full'], n + "ssd_scan")
    yn = _ssd_post_fwd(y, z, cs['ssd_norm'], n + "ssd_post")
    ys = _mm(yn, W['p_ssd'][l], name=n + "p_ssd")
    merged = _merge_fwd(gl, yp, ya, ys, n + "merge")
    x1 = _mm(merged, W['w_out'][l], acc=x, name=n + "w_out")
    u2 = _norm_fwd(x1, cs['norm_ffn'], n + "norm_ffn")
    hpre = _mm(u2, W['ffn_up'][l], name=n + "ffn_up")
    act = _ffn_act_fwd(hpre, cs['fcw'], cs['fcb'], n + "ffn_act")
    x2 = _mm(act, W['ffn_down'][l], acc=x1, name=n + "ffn_down")
    saved = dict(x=x, u=u, pqkv=pqkv, z=z, xbc=xbc, gl=gl, fdt=fdt, d=d, ypm=ypm, yp=yp, fT=fT,
                 c_row=c_row, qh=qh, kh=kh, vh=vh, oh=oh, o=o, lse=lse, ya=ya, xa=xa, dtw=dtw, y=y, hprev=hprev,
                 yn=yn, ys=ys, merged=merged, x1=x1, u2=u2, hpre=hpre, act=act, win=win, cs=cs)
    return x2, saved


def _layer_bwd(dx2, sv, W, l):
    n = f"l{l}_b_"
    cs, win = sv['cs'], sv['win']
    g = {}
    dact = _mm(dx2, W['ffn_down'][l], tb=True, name=n + "ffn_down_dx")
    g['ffn_down'] = _mm(sv['act'], dx2, ta=True, name=n + "ffn_down_dw")
    dhc, g['ffn_conv_w'], dfcb = _ffn_act_bwd_a(sv['hpre'], dact, cs['fcw'], cs['fcb'], n + "ffn_act_a")
    g['ffn_conv_b'] = dfcb[0]
    dhpre = _conv_bwd_b(dhc, cs['fcw'], n + "ffn_act_b")
    du2 = _mm(dhpre, W['ffn_up'][l], tb=True, name=n + "ffn_up_dx")
    g['ffn_up'] = _mm(sv['u2'], dhpre, ta=True, name=n + "ffn_up_dw")
    dx1, dnf = _norm_bwd(sv['x1'], cs['norm_ffn'], du2, dx2, n + "norm_ffn")
    g['norm_ffn'] = dnf[0]
    dm = _mm(dx1, W['w_out'][l], tb=True, name=n + "w_out_dx")
    g['w_out'] = _mm(sv['merged'], dx1, ta=True, name=n + "w_out_dw")
    dyp, dya, dys, dgl = _merge_bwd(sv['gl'], sv['yp'], sv['ya'], sv['ys'], dm, n + "merge")
    dypm = _mm(dyp, W['p_pool'][l], tb=True, name=n + "p_pool_dx")
    g['p_pool'] = _mm(sv['ypm'], dyp, ta=True, name=n + "p_pool_dw")
    dd, dscale, dmix = _pool_bwd_a(dypm, sv['d'], cs['mix'], cs['scale'], n + "pool_a")
    g['pool_scale'] = dscale[0]
    g['pool_mix'] = dmix.reshape(len(POOL_WINDOWS), LANE, LANE)
    dpool_v = _pool_bwd_b(dd, n + "pool_b")
    do = _mm(dya, W['p_attn'][l], tb=True, name=n + "p_attn_dx")
    g['p_attn'] = _mm(sv['o'], dya, ta=True, name=n + "p_attn_dw")
    dqh, dkh, dvh, dc, dcq = _attn_bwd(sv['qh'], sv['kh'], sv['vh'], _to_heads(do, F32), sv['oh'],
                                       sv['c_row'], sv['lse'], n + "attn")
    dfT, dfb = _logf_cumsum_bwd(sv['fT'], cs['f_bias'], dc[:, 0, :], dcq[:, :, 0], n + "logf")
    g['f_bias'] = dfb[:, 0]
    dpqkv = jnp.concatenate([dpool_v, (_from_heads(dqh) * ATTN_SCALE).astype(MXU_DTYPE)]
                            + [_from_heads(t).astype(MXU_DTYPE) for t in (dkh, dvh)], axis=1)
    dyn = _mm(dys, W['p_ssd'][l], tb=True, name=n + "p_ssd_dx")
    g['p_ssd'] = _mm(sv['yn'], dys, ta=True, name=n + "p_ssd_dw")
    dy, dz, dsn = _ssd_post_bwd(sv['y'], sv['z'], dyn, cs['ssd_norm'], n + "ssd_post")
    g['ssd_norm'] = dsn[0]
    dxa, ddtw, dA, dD = _ssd_chunk_bwd(sv['xa'], sv['dtw'], dy, sv['hprev'], cs['a_row'], cs['d_full'], n + "ssd_scan")
    heads = slice(DT_LANE0, DT_LANE0 + SSD_HEADS)
    g['ssd_a_log'] = dA[0, heads] * cs['a_row'][0, heads]
    g['ssd_d'] = dD[0, heads]
    dpre, ddt_raw, g['ssd_conv_w'], dcb, ddtb = _ssd_pre_bwd_a(sv['xbc'], sv['fdt'], dxa, ddtw, cs['cw'], cs['cb'],
                                                              cs['dtb'], n + "ssd_pre_a")
    g['ssd_conv_b'] = dcb[0]
    g['ssd_dt_bias'] = ddtb[0, heads]
    dxbc = _conv_bwd_b(dpre, cs['cw'], n + "ssd_pre_b")
    dfdt = jnp.concatenate([dfT.T, ddt_raw[:, HEADS:]], axis=1).astype(MXU_DTYPE)
    dsegs = dict(p=dpqkv, z=dz, x=dxbc, g=dgl, f=dfdt)
    du, dwin = None, {}
    for key in ('p', 'z', 'x', 'g', 'f'):
        du = _mm(dsegs[key], win[key], tb=True, acc=du, name=n + "in_dx_" + key)
        dwin[key] = _mm(sv['u'], dsegs[key], ta=True, name=n + "in_dw_" + key)
    g['w_in'] = dwin
    dx, dnm = _norm_bwd(sv['x'], cs['norm_mix'], du, dx1, n + "norm_mix")
    g['norm_mix'] = dnm[0]
    return dx, g


def _local_step(x, target, W):
    depth = W['norm_mix'].shape[0]
    saved = []
    h = x
    for l in range(depth):
        h, sv = _layer_fwd(h, W, l)
        saved.append(sv)
    dx, dwf, loss = _loss_head(h, W['norm_final'][None], target, "loss_head")
    grads = [None] * depth
    for l in reversed(range(depth)):
        dx, grads[l] = _layer_bwd(dx, saved[l], W, l)
    return loss[0, 0], dx, grads, dwf[0]


def _pack_rows(parts, row_align=1):
    flat = jnp.concatenate([p.reshape(-1) for p in parts])
    n = flat.shape[0]
    total = -(-n // (PACK_W * row_align)) * PACK_W * row_align
    if total > n:
        flat = jnp.concatenate([flat, jnp.zeros((total - n,), flat.dtype)])
    return flat.reshape(-1, PACK_W)


def _unpack_rows(buf, shapes):
    flat = buf.reshape(-1)
    out, pos = [], 0
    for shp in shapes:
        size = 1
        for s in shp:
            size *= s
        out.append(flat[pos:pos + size].reshape(shp))
        pos += size
    return out


def _to_place_major(gfull, name):
    R_, C = gfull.shape
    if name in COL_SHARDED:
        return gfull.reshape(R_, N_PLACES, C // N_PLACES).transpose(1, 0, 2)
    return gfull.reshape(N_PLACES, R_ // N_PLACES, C)


_W_IN_LAYOUT = (('p', 0, 0, 2048), ('f', 0, 2048, HEADS), ('z', 0, 2056, 1024), ('x', 0, 3080, 1536),
                ('f', DT_LANE0, 4616, SSD_HEADS), ('g', 0, 4632, 3072))


def _w_in_segments(slabs):
    starts = [0]
    for s in slabs:
        starts.append(starts[-1] + s.shape[-1])

    def cols(a, b):
        parts = []
        for s, s0 in zip(slabs, starts):
            lo, hi = max(a, s0), min(b, s0 + s.shape[-1])
            if lo < hi:
                parts.append(s[..., lo - s0:hi - s0])
        return parts[0] if len(parts) == 1 else jnp.concatenate(parts, axis=-1)

    pad = jnp.zeros(slabs[0].shape[:-1] + (LANE - DT_LANE0 - SSD_HEADS,), slabs[0].dtype)
    return dict(p=cols(0, 2048), z=cols(2056, 3080), x=cols(3080, 4616), g=cols(4632, 7704),
                f=jnp.concatenate([cols(2048, 2056), cols(4616, 4632), pad], axis=-1))


def _w_in_columns(segs, a, b):
    parts = []
    for key, s0, g0, w in _W_IN_LAYOUT:
        lo, hi = max(a, g0), min(b, g0 + w)
        if lo < hi:
            parts.append(segs[key][..., s0 + lo - g0:s0 + hi - g0])
    return parts[0] if len(parts) == 1 else jnp.concatenate(parts, axis=-1)


def kernel(x, norm_mix, w_in, pool_mix, pool_scale, f_bias, ssd_conv_w, ssd_conv_b, ssd_dt_bias, ssd_a_log, ssd_d, ssd_norm, p_pool, p_attn, p_ssd, w_out, norm_ffn, ffn_up, ffn_conv_w, ffn_conv_b, ffn_down, norm_final, loss_target, m_norm_mix, m_w_in, m_pool_mix, m_pool_scale, m_f_bias, m_ssd_conv_w, m_ssd_conv_b, m_ssd_dt_bias, m_ssd_a_log, m_ssd_d, m_ssd_norm, m_p_pool, m_p_attn, m_p_ssd, m_w_out, m_norm_ffn, m_ffn_up, m_ffn_conv_w, m_ffn_conv_b, m_ffn_down, m_norm_final, v_norm_mix, v_w_in, v_pool_mix, v_pool_scale, v_f_bias, v_ssd_conv_w, v_ssd_conv_b, v_ssd_dt_bias, v_ssd_a_log, v_ssd_d, v_ssd_norm, v_p_pool, v_p_attn, v_p_ssd, v_w_out, v_norm_ffn, v_ffn_up, v_ffn_conv_w, v_ffn_conv_b, v_ffn_down, v_norm_final):
    args = dict(locals())
    w_sh = {k: args[k] for k in WEIGHTS}
    m_sh = {k: args['m_' + k] for k in WEIGHTS}
    v_sh = {k: args['v_' + k] for k in WEIGHTS}
    depth = norm_mix.shape[0]
    place = 2 * lax.axis_index("x") + lax.axis_index("y")
    row_sharded = [k for k in BIG if k not in COL_SHARDED]

    sent = {k: w_sh[k].astype(MXU_DTYPE) for k in BIG}
    sent.update({k: w_sh[k] for k in SMALL_SHARDED})
    gathered = _gather_places([sent[k] for k in BIG + SMALL_SHARDED],
                              [k in row_sharded for k in BIG + SMALL_SHARDED], "gather_weights")
    gathered = dict(zip(BIG + SMALL_SHARDED, gathered))
    W = {k: w_sh[k] for k in SMALL if k not in SMALL_SHARDED}
    zero = jnp.zeros((), jnp.int32)
    for k in BIG + SMALL_SHARDED:
        if k in row_sharded:
            gk = lax.dynamic_update_slice(gathered[k], sent[k][:, None], (zero, place, zero, zero))
            W[k] = gk.reshape(gk.shape[0], -1, gk.shape[-1])
            continue
        gk = lax.dynamic_update_slice(gathered[k], sent[k][None], (place, zero, zero, zero))
        if k == 'w_in':
            W[k] = _w_in_segments([gk[p] for p in range(N_PLACES)])
        else:
            W[k] = jnp.concatenate([gk[p] for p in range(N_PLACES)], axis=-1)

    loss_local, grad_x, grads, g_final = _local_step(x[0], loss_target[0], W)
    loss = lax.psum(loss_local, ("x", "y", "c"))

    def place_major(k, l):
        if k == 'w_in':
            c = IN_TOTAL // N_PLACES
            return jnp.stack([_w_in_columns(grads[l][k], p * c, (p + 1) * c) for p in range(N_PLACES)])
        return _to_place_major(grads[l][k], k)

    g_big = [jnp.stack([place_major(k, l) for l in range(depth)], axis=1) for k in BIG]
    small_names = [k for k in SMALL if k != 'norm_final'] + ['norm_final']
    small_full = [jnp.stack([grads[l][k] for l in range(depth)]) for k in small_names[:-1]] + [g_final]
    small_full_shapes = [a.shape for a in small_full]
    small_packed = _pack_rows(small_full, 32)
    g_small = jnp.broadcast_to(small_packed.reshape(1, 2, -1, PACK_W),
                               (N_PLACES, 2, small_packed.shape[0] // 2, PACK_W))

    core = lax.axis_index("c")
    pos = jnp.stack([place, core]).astype(jnp.int32)
    g_all = g_big + [g_small]
    theirs = _reduce_sibling(g_all, "reduce_sibling")
    wire = [WIRE_DTYPE] * len(BIG) + [F32]
    halves = [_add_own_slot(g, t, pos, dt, f"reduce_sibling_add{j}")
              for j, (g, t, dt) in enumerate(zip(g_all, theirs, wire))]
    recv = _reduce_places(halves, "reduce_places")
    qs = [_sum_places(g, t, r, pos, f"reduce_places_add{j}") for j, (g, t, r) in enumerate(zip(g_all, theirs, recv))]
    others = _swap_sibling(qs, "reduce_swap")

    def mine(k, a):
        if k in SMALL_SHARDED:
            c = a.shape[-1] // N_PLACES
            return lax.dynamic_slice_in_dim(a, place * c, c, axis=a.ndim - 1)
        return a

    outs = {}
    for j, k in enumerate(BIG):
        res = _adamw_pair(qs[j], others[j], w_sh[k], m_sh[k], v_sh[k], pos, "adamw_" + k)
        for prefix, a in zip(('grad_', 'delta_', 'new_m_', 'new_v_'), res):
            outs[prefix + k] = a
    small_sum = jnp.where(core == 0, jnp.concatenate([qs[-1], others[-1]]), jnp.concatenate([others[-1], qs[-1]]))
    g_small_list = [mine(k, a) for k, a in zip(small_names, _unpack_rows(small_sum, small_full_shapes))]
    shapes = [a.shape for a in g_small_list]
    gp = _pack_rows(g_small_list, 128)
    wp, mp, vp = (_pack_rows([d[k] for k in small_names], 128) for d in (w_sh, m_sh, v_sh))
    delta_p, m_p, v_p = _adamw(gp, wp, mp, vp, "adamw_small")
    for prefix, buf in (('grad_', gp), ('delta_', delta_p), ('new_m_', m_p), ('new_v_', v_p)):
        for k, a in zip(small_names, _unpack_rows(buf, shapes)):
            outs[prefix + k] = a
    result = [loss, grad_x[None]]
    for prefix in ('grad_', 'delta_', 'new_m_', 'new_v_'):
        result += [outs[prefix + k] for k in WEIGHTS]
    return tuple(result)
```

```python
import functools

import jax
import jax.numpy as jnp
from jax import lax
from jax.experimental import pallas as pl
from jax.experimental.pallas import tpu as pltpu

F32 = jnp.float32
MXU_DTYPE = jnp.bfloat16
WIRE_DTYPE = jnp.bfloat16
NORM_EPS = 1e-6
HALO = 16
LANE = 128
NEG_BIG = -1e30
VMEM_LIMIT = 52 * 1024 * 1024

D_MODEL = 1024
POOL_WINDOWS = (2, 4, 8, 16)
POOL_W = 512
HEADS = 8
HEAD_DIM = 64
ATTN_W = 512
ATTN_SCALE = HEAD_DIM ** -0.5
SSD_W = 1024
SSD_HEADS = 16
SSD_P = 64
SSD_N = 128
SSD_CHUNK = 128
SSD_CONV_CH = 1536
FFN = 2816
DT_LANE0 = 8
IN_SPLITS = (512, 512, 512, 512, 8, 1024, 1536, 16, 3072)
IN_TOTAL = sum(IN_SPLITS)
N_PLACES = 4

ADAM_LR, ADAM_B1, ADAM_B2, ADAM_EPS, ADAM_WD, ADAM_STEP = 0.001, 0.9, 0.999, 1e-08, 0.01, 10

BIG = ('w_in', 'p_pool', 'p_attn', 'p_ssd', 'w_out', 'ffn_up', 'ffn_down')
COL_SHARDED = ('w_in', 'p_pool', 'p_attn', 'ffn_up')
SMALL = ('norm_mix', 'pool_mix', 'pool_scale', 'f_bias', 'ssd_conv_w', 'ssd_conv_b', 'ssd_dt_bias',
         'ssd_a_log', 'ssd_d', 'ssd_norm', 'norm_ffn', 'ffn_conv_w', 'ffn_conv_b', 'norm_final')
SMALL_SHARDED = ('ssd_conv_w', 'ffn_conv_w')
WEIGHTS = ('norm_mix', 'w_in', 'pool_mix', 'pool_scale', 'f_bias', 'ssd_conv_w', 'ssd_conv_b', 'ssd_dt_bias',
           'ssd_a_log', 'ssd_d', 'ssd_norm', 'p_pool', 'p_attn', 'p_ssd', 'w_out', 'norm_ffn', 'ffn_up',
           'ffn_conv_w', 'ffn_conv_b', 'ffn_down', 'norm_final')
PACK_W = 1024


def _params(sem):
    return pltpu.CompilerParams(dimension_semantics=sem, vmem_limit_bytes=VMEM_LIMIT)


def _tile(n, prefs=(512, 256, 128)):
    for t in prefs:
        if n % t == 0:
            return t
    return n


def _sigmoid(x):
    return 0.5 * jnp.tanh(0.5 * x) + 0.5


def _softplus(x):
    return jnp.maximum(x, 0.0) + jnp.log1p(jnp.exp(-jnp.abs(x)))


def _dot(a, b, dims=((1,), (0,))):
    return lax.dot_general(a.astype(MXU_DTYPE), b.astype(MXU_DTYPE), (dims, ((), ())),
                           preferred_element_type=F32)


NT = ((1,), (1,))


def _mm(a, b, *, ta=False, tb=False, acc=None, out_dtype=F32, name):
    M, K = (a.shape[1], a.shape[0]) if ta else a.shape
    N = b.shape[0] if tb else b.shape[1]
    big = (1024, 1408, 512, 256, 128)
    tm, tn = _tile(M, big), _tile(N, big)
    tk = K if K <= 1024 else _tile(K, (512, 256, 128) if ta else big)
    nk = K // tk
    a_spec = pl.BlockSpec((tk, tm), lambda i, j, k: (k, i)) if ta else pl.BlockSpec((tm, tk), lambda i, j, k: (i, k))
    b_spec = pl.BlockSpec((tn, tk), lambda i, j, k: (j, k)) if tb else pl.BlockSpec((tk, tn), lambda i, j, k: (k, j))
    in_specs = [a_spec, b_spec]
    args = [a, b]
    if acc is not None:
        in_specs.append(pl.BlockSpec((tm, tn), lambda i, j, k: (i, j)))
        args.append(acc)

    def body(*refs):
        if acc is not None:
            a_ref, b_ref, c_ref, o_ref, acc_ref = refs
        else:
            a_ref, b_ref, o_ref, acc_ref = refs
        k = pl.program_id(2)

        @pl.when(k == 0)
        def _():
            if acc is not None:
                acc_ref[...] = c_ref[...].astype(F32)
            else:
                acc_ref[...] = jnp.zeros_like(acc_ref)

        av = a_ref[...]
        if ta:
            av = av.astype(F32).T
        acc_ref[...] += _dot(av, b_ref[...], NT if tb else ((1,), (0,)))

        @pl.when(k == nk - 1)
        def _():
            o_ref[...] = acc_ref[...].astype(out_dtype)

    return pl.pallas_call(
        body, grid=(M // tm, N // tn, nk), in_specs=in_specs,
        out_specs=pl.BlockSpec((tm, tn), lambda i, j, k: (i, j)),
        out_shape=jax.ShapeDtypeStruct((M, N), out_dtype),
        scratch_shapes=[pltpu.VMEM((tm, tn), F32)],
        compiler_params=_params(("parallel", "parallel", "arbitrary")), name=name)(*args)


def _rows(body, S, T, *, rows=(), consts=(), prevs=(), nexts=(), out_rows=(), out_accs=(), scratch=(), name):
    n = S // T
    hb = T // HALO
    last_h = S // HALO - 1

    def norm(r):
        return r if isinstance(r, tuple) else (r, r.shape[1], 0)

    rows, prevs, nexts = [norm(r) for r in rows], [norm(r) for r in prevs], [norm(r) for r in nexts]
    in_specs, args = [], []
    for arr, W, cb in rows:
        in_specs.append(pl.BlockSpec((T, W), lambda i, cb=cb: (i, cb)))
        args.append(arr)
    for cst in consts:
        in_specs.append(pl.BlockSpec(cst.shape, lambda i, nd=cst.ndim: (0,) * nd))
        args.append(cst)
    for arr, W, cb in prevs:
        in_specs.append(pl.BlockSpec((HALO, W), lambda i, cb=cb: (jnp.maximum(i * hb - 1, 0), cb)))
        args.append(arr)
    for arr, W, cb in nexts:
        in_specs.append(pl.BlockSpec((HALO, W), lambda i, cb=cb: (jnp.minimum((i + 1) * hb, last_h), cb)))
        args.append(arr)
    out_specs = [pl.BlockSpec((T, W), lambda i: (i, 0)) for W, _ in out_rows]
    out_specs += [pl.BlockSpec(shp, lambda i, nd=len(shp): (0,) * nd) for shp, _ in out_accs]
    out_shape = [jax.ShapeDtypeStruct((S, W), dt) for W, dt in out_rows]
    out_shape += [jax.ShapeDtypeStruct(shp, dt) for shp, dt in out_accs]
    cuts = [len(rows), len(consts), len(prevs), len(nexts), len(out_rows), len(out_accs), len(scratch)]

    def kern(*refs):
        groups, pos = [], 0
        for c in cuts:
            groups.append(list(refs[pos:pos + c]))
            pos += c
        i = pl.program_id(0)

        @pl.when(i == 0)
        def _():
            for a_ref in groups[5]:
                a_ref[...] = jnp.zeros_like(a_ref)

        body(i, n, *groups)

    outs = pl.pallas_call(kern, grid=(n,), in_specs=in_specs, out_specs=out_specs, out_shape=out_shape,
                          scratch_shapes=list(scratch), compiler_params=_params(("arbitrary",)), name=name)(*args)
    return outs


def _fill_prev(ext, prev_ref, cur, i):
    ext[0:HALO, :] = jnp.where(i > 0, prev_ref[...].astype(F32), 0.0)
    ext[HALO:, :] = cur


def _fill_next(ext, cur, next_val, T):
    ext[0:T, :] = cur
    ext[T:, :] = next_val


def _row_ids(i, T, W=1):
    return i * T + lax.broadcasted_iota(jnp.int32, (T, W), 0)


SUB_ROWS = 32
WIN_PAD = 8


def _tile_loop(T, fn):
    n = T // SUB_ROWS
    fn(0, True, n == 1)
    if n > 2:
        def body(rb, carry):
            fn(pl.multiple_of(rb * SUB_ROWS, SUB_ROWS), False, False)
            return carry
        lax.fori_loop(1, n - 1, body, 0)
    if n > 1:
        fn((n - 1) * SUB_ROWS, False, True)


def _win_prev(x_ref, prev_ref, i, r0, first, cols):
    if first:
        top = jnp.where(i > 0, prev_ref[HALO - WIN_PAD:HALO, cols].astype(F32), 0.0)
        return jnp.concatenate([top, x_ref[0:SUB_ROWS, cols].astype(F32)], axis=0)
    start = r0 - WIN_PAD if isinstance(r0, int) else pl.multiple_of(r0 - WIN_PAD, WIN_PAD)
    return x_ref[pl.ds(start, SUB_ROWS + WIN_PAD), cols].astype(F32)


def _behind(win, j):
    return win[WIN_PAD:, :] if j == 0 else pltpu.roll(win, j, axis=0)[WIN_PAD:, :]


def _win_next(x_ref, next_ref, i, n, r0, last, cols):
    if last:
        bot = jnp.where(i < n - 1, next_ref[0:WIN_PAD, cols].astype(F32), 0.0)
        return jnp.concatenate([x_ref[r0:r0 + SUB_ROWS, cols].astype(F32), bot], axis=0)
    return x_ref[pl.ds(r0, SUB_ROWS + WIN_PAD), cols].astype(F32)


def _ahead(win, j):
    return win[:SUB_ROWS, :] if j == 0 else pltpu.roll(win, SUB_ROWS + WIN_PAD - j, axis=0)[:SUB_ROWS, :]


def _taps(win, K):
    return [_behind(win, K - 1 - k) for k in range(K)]


def _conv_win(taps, w_ref, b_ref, cols):
    out = b_ref[:, cols]
    for k, tap in enumerate(taps):
        out = out + tap * w_ref[k:k + 1, cols]
    return out


def _conv_wgrad_win(acc_ref, taps, d, cols):
    for k, tap in enumerate(taps):
        acc_ref[k:k + 1, cols] += jnp.sum(d * tap, axis=0, keepdims=True)


def _norm_fwd(x, w, name):
    S, D = x.shape

    def body(i, n, R, C, P, N, O, A, Sc):
        xv = R[0][...]
        r = lax.rsqrt(jnp.mean(xv * xv, axis=-1, keepdims=True) + NORM_EPS)
        O[0][...] = (xv * r * C[0][...]).astype(MXU_DTYPE)

    return _rows(body, S, _tile(S), rows=[x], consts=[w], out_rows=[(D, MXU_DTYPE)], name=name)[0]


def _norm_bwd_math(xv, w, du):
    r = lax.rsqrt(jnp.mean(xv * xv, axis=-1, keepdims=True) + NORM_EPS)
    xh = xv * r
    g = du * w
    dx = r * (g - xh * jnp.mean(g * xh, axis=-1, keepdims=True))
    dw = jnp.sum(du * xh, axis=0, keepdims=True)
    return dx, dw


def _norm_bwd(x, w, du, dres, name):
    S, D = x.shape

    def body(i, n, R, C, P, N, O, A, Sc):
        dx, dw = _norm_bwd_math(R[0][...], C[0][...], R[1][...])
        O[0][...] = R[2][...] + dx
        A[0][...] += dw

    return _rows(body, S, _tile(S), rows=[x, du, dres], consts=[w], out_rows=[(D, F32)],
                 out_accs=[((1, D), F32)], name=name)


def _loss_head(x, w, target, name):
    S, D = x.shape

    def body(i, n, R, C, P, N, O, A, Sc):
        xv, w_, tg = R[0][...], C[0][...], R[1][...]
        r = lax.rsqrt(jnp.mean(xv * xv, axis=-1, keepdims=True) + NORM_EPS)
        e = xv * r * w_ - tg
        A[1][...] += jnp.broadcast_to(0.5 * jnp.sum(jnp.mean(e * e, axis=-1, keepdims=True)), (1, LANE))
        dx, dw = _norm_bwd_math(xv, w_, e / D)
        O[0][...] = dx
        A[0][...] += dw

    return _rows(body, S, _tile(S), rows=[x, target], consts=[w], out_rows=[(D, F32)],
                 out_accs=[((1, D), F32), ((1, LANE), F32)], name=name)


def _pool_fwd(pqkv, mix, scale, name):
    S = pqkv.shape[0]
    T = _tile(S, (256, 128))

    def body(i, n, R, C, P, N, O, A, Sc):
        ext = Sc[0]
        v = R[0][...]
        _fill_prev(ext, P[0], v, i)
        t1 = (_row_ids(i, T) + 1).astype(F32)
        for g, w in enumerate(POOL_WINDOWS):
            cols = slice(g * LANE, (g + 1) * LANE)
            acc = v[:, cols]
            for j in range(1, w):
                acc = acc + ext[pl.ds(HALO - j, T), cols]
            d = (acc / jnp.minimum(t1, float(w)) - v[:, cols]).astype(MXU_DTYPE)
            O[0][:, cols] = d
            O[1][:, cols] = (_dot(d, C[0][g]) * C[1][:, cols]).astype(MXU_DTYPE)

    return _rows(body, S, T, rows=[(pqkv, POOL_W, 0)], prevs=[(pqkv, POOL_W, 0)], consts=[mix, scale],
                 out_rows=[(POOL_W, MXU_DTYPE), (POOL_W, MXU_DTYPE)],
                 scratch=[pltpu.VMEM((HALO + T, POOL_W), F32)], name=name)


def _pool_bwd_a(dypm, d, mix, scale, name):
    S = d.shape[0]
    T = _tile(S, (256, 128))

    def body(i, n, R, C, P, N, O, A, Sc):
        for g in range(len(POOL_WINDOWS)):
            cols = slice(g * LANE, (g + 1) * LANE)
            dg = R[1][:, cols]
            dy = R[0][:, cols]
            yg = _dot(dg, C[0][g])
            A[0][:, cols] += jnp.sum(dy * yg, axis=0, keepdims=True)
            dys = dy * C[1][:, cols]
            A[1][cols, :] += _dot(dg.astype(F32).T, dys)
            O[0][:, cols] = _dot(dys, C[0][g], NT)

    return _rows(body, S, T, rows=[dypm, d], consts=[mix, scale], out_rows=[(POOL_W, F32)],
                 out_accs=[((1, POOL_W), F32), ((POOL_W, LANE), F32)], name=name)


def _pool_bwd_b(dd, name):
    S = dd.shape[0]
    T = _tile(S, (256, 128))

    def body(i, n, R, C, P, N, O, A, Sc):
        ext = Sc[0]
        ddv = R[0][...]
        t1 = (_row_ids(i, T) + 1).astype(F32)
        nxt = jnp.where(i < n - 1, N[0][...], 0.0)
        for g, w in enumerate(POOL_WINDOWS):
            cols = slice(g * LANE, (g + 1) * LANE)
            ext[0:T, cols] = ddv[:, cols] / jnp.minimum(t1, float(w))
            ext[T:, cols] = nxt[:, cols] / float(w)
        for g, w in enumerate(POOL_WINDOWS):
            cols = slice(g * LANE, (g + 1) * LANE)
            acc = ext[0:T, cols]
            for j in range(1, w):
                acc = acc + ext[pl.ds(j, T), cols]
            O[0][:, cols] = (acc - ddv[:, cols]).astype(MXU_DTYPE)

    return _rows(body, S, T, rows=[dd], nexts=[dd], out_rows=[(POOL_W, MXU_DTYPE)],
                 scratch=[pltpu.VMEM((T + HALO, POOL_W), F32)], name=name)[0]


def _lane_cumsum(seg, reverse=False):
    lane = lax.broadcasted_iota(jnp.int32, seg.shape, 1)
    sh = 1
    while sh < LANE:
        if reverse:
            seg = seg + jnp.where(lane < LANE - sh, pltpu.roll(seg, LANE - sh, axis=1), 0.0)
        else:
            seg = seg + jnp.where(lane >= sh, pltpu.roll(seg, sh, axis=1), 0.0)
        sh *= 2
    return seg


def _logf_cumsum(fT, bias, name):
    H, S = fT.shape
    TB = _tile(S)
    nb = S // TB

    def body(f_ref, b_ref, o_ref, carry):
        @pl.when(pl.program_id(0) == 0)
        def _():
            carry[...] = jnp.zeros_like(carry)

        x = f_ref[...] + b_ref[...]
        lf = jnp.minimum(x, 0.0) - jnp.log1p(jnp.exp(-jnp.abs(x)))
        c = carry[...]
        for j in range(TB // LANE):
            seg = _lane_cumsum(lf[:, j * LANE:(j + 1) * LANE]) + c
            o_ref[:, j * LANE:(j + 1) * LANE] = seg
            c = seg[:, LANE - 1:LANE]
        carry[...] = c

    return pl.pallas_call(
        body, grid=(nb,), in_specs=[pl.BlockSpec((H, TB), lambda i: (0, i)), pl.BlockSpec((H, 1), lambda i: (0, 0))],
        out_specs=pl.BlockSpec((H, TB), lambda i: (0, i)), out_shape=jax.ShapeDtypeStruct((H, S), F32),
        scratch_shapes=[pltpu.VMEM((H, 1), F32)], compiler_params=_params(("arbitrary",)), name=name)(fT, bias)


def _logf_cumsum_bwd(fT, bias, dc, dcq, name):
    H, S = fT.shape
    TB = _tile(S)
    nb = S // TB

    def body(f_ref, b_ref, dc_ref, dcq_ref, o_ref, db_ref, carry):
        @pl.when(pl.program_id(0) == 0)
        def _():
            carry[...] = jnp.zeros_like(carry)
            db_ref[...] = jnp.zeros_like(db_ref)

        x = f_ref[...] + b_ref[...]
        sg = _sigmoid(-x)
        dcv = dc_ref[...] + dcq_ref[...]
        c = carry[...]
        db = jnp.zeros((H, 1), F32)
        for j in reversed(range(TB // LANE)):
            seg = _lane_cumsum(dcv[:, j * LANE:(j + 1) * LANE], reverse=True) + c
            df = seg * sg[:, j * LANE:(j + 1) * LANE]
            o_ref[:, j * LANE:(j + 1) * LANE] = df
            db = db + jnp.sum(df, axis=1, keepdims=True)
            c = seg[:, 0:1]
        carry[...] = c
        db_ref[...] += db

    rev = lambda i: (0, nb - 1 - i)
    return pl.pallas_call(
        body, grid=(nb,),
        in_specs=[pl.BlockSpec((H, TB), rev), pl.BlockSpec((H, 1), lambda i: (0, 0)), pl.BlockSpec((H, TB), rev),
                  pl.BlockSpec((H, TB), rev)],
        out_specs=[pl.BlockSpec((H, TB), rev), pl.BlockSpec((H, 1), lambda i: (0, 0))],
        out_shape=[jax.ShapeDtypeStruct((H, S), F32), jax.ShapeDtypeStruct((H, 1), F32)],
        scratch_shapes=[pltpu.VMEM((H, 1), F32)], compiler_params=_params(("arbitrary",)), name=name)(
            fT, bias, dc, dcq)


def _attn_scores(q, k, ck, diagonal, T):
    s = _dot(q, k, NT) - ck
    if diagonal:
        tril = lax.broadcasted_iota(jnp.int32, (T, T), 1) <= lax.broadcasted_iota(jnp.int32, (T, T), 0)
        s = jnp.where(tril, s, NEG_BIG)
    return s


def _attn_fwd(q, k, v, c_row, name):
    H, S, Dh = q.shape
    T = _tile(S, (1024, 512, 256, 128))
    nq = S // T

    steps = jnp.asarray([[qi for qi in range(nq) for ki in range(qi + 1)],
                         [ki for qi in range(nq) for ki in range(qi + 1)]], jnp.int32)

    def body(st_ref, q_ref, k_ref, v_ref, ck_ref, o_ref, lse_ref, m_s, l_s, acc_s):
        qi, ki = st_ref[0, pl.program_id(1)], st_ref[1, pl.program_id(1)]

        @pl.when(ki == 0)
        def _():
            m_s[...] = jnp.full_like(m_s, NEG_BIG)
            l_s[...] = jnp.zeros_like(l_s)
            acc_s[...] = jnp.zeros_like(acc_s)

        def step(diagonal):
            s = _attn_scores(q_ref[0], k_ref[0], ck_ref[0], diagonal, T)
            m_new = jnp.maximum(m_s[...], jnp.max(s, axis=1, keepdims=True))
            alpha = jnp.exp(m_s[...] - m_new)
            p = jnp.exp(s - m_new)
            l_s[...] = alpha * l_s[...] + jnp.sum(p, axis=1, keepdims=True)
            acc_s[...] = alpha * acc_s[...] + _dot(p, v_ref[0])
            m_s[...] = m_new

        @pl.when(ki < qi)
        def _():
            step(False)

        @pl.when(ki == qi)
        def _():
            step(True)
            o_ref[0] = acc_s[...] / l_s[...]
            lse_ref[0] = m_s[...] + jnp.log(l_s[...])

    qmap = lambda h, s, st: (h, st[0, s], 0)
    kmap = lambda h, s, st: (h, st[1, s], 0)
    return pl.pallas_call(
        body, grid_spec=pltpu.PrefetchScalarGridSpec(
            num_scalar_prefetch=1, grid=(H, steps.shape[1]),
            in_specs=[pl.BlockSpec((1, T, Dh), qmap), pl.BlockSpec((1, T, Dh), kmap), pl.BlockSpec((1, T, Dh), kmap),
                      pl.BlockSpec((1, 1, T), lambda h, s, st: (h, 0, st[1, s]))],
            out_specs=[pl.BlockSpec((1, T, Dh), qmap), pl.BlockSpec((1, T, 1), qmap)],
            scratch_shapes=[pltpu.VMEM((T, 1), F32), pltpu.VMEM((T, 1), F32), pltpu.VMEM((T, Dh), F32)]),
        out_shape=[jax.ShapeDtypeStruct((H, S, Dh), F32), jax.ShapeDtypeStruct((H, S, 1), F32)],
        compiler_params=_params(("arbitrary", "arbitrary")), name=name)(steps, q, k, v, c_row)


def _attn_bwd(q, k, v, do, o, c_row, lse, name):
    H, S, Dh = q.shape
    T = _tile(S, (1024, 512, 256, 128))
    nq = S // T

    steps = jnp.asarray([[ki for ki in range(nq) for qi in range(ki, nq)],
                         [qi for ki in range(nq) for qi in range(ki, nq)]], jnp.int32)

    def body(st_ref, q_ref, k_ref, v_ref, do_ref, o_ref, ck_ref, lse_ref, dq_ref, dk_ref, dv_ref, dc_ref, dcq_ref,
             dk_s, dv_s, dc_s):
        ki, qi = st_ref[0, pl.program_id(1)], st_ref[1, pl.program_id(1)]

        @pl.when(pl.program_id(1) == 0)
        def _():
            dq_ref[...] = jnp.zeros_like(dq_ref)
            dcq_ref[...] = jnp.zeros_like(dcq_ref)

        @pl.when(qi == ki)
        def _():
            dk_s[...] = jnp.zeros_like(dk_s)
            dv_s[...] = jnp.zeros_like(dv_s)
            dc_s[...] = jnp.zeros_like(dc_s)

        def step(diagonal):
            qv, kv, vv = q_ref[0], k_ref[0], v_ref[0]
            s = _attn_scores(qv, kv, ck_ref[0], diagonal, T)
            p = jnp.exp(s - lse_ref[0])
            dov = do_ref[0]
            delta = jnp.sum(dov * o_ref[0], axis=1, keepdims=True)
            dv_s[...] += _dot(p.T, dov)
            dp = _dot(dov, vv, NT)
            ds = p * (dp - delta)
            dc_s[...] -= jnp.sum(ds, axis=0, keepdims=True)
            rows = pl.ds(pl.multiple_of(qi * T, T), T)
            dq_ref[0, rows, :] += _dot(ds, kv)
            dcq_ref[0, rows, :] += jnp.sum(ds, axis=1, keepdims=True)
            dk_s[...] += _dot(ds.T, qv)

        @pl.when(qi > ki)
        def _():
            step(False)

        @pl.when(qi == ki)
        def _():
            step(True)

        @pl.when(qi == nq - 1)
        def _():
            dk_ref[0] = dk_s[...]
            dv_ref[0] = dv_s[...]
            dc_ref[0] = dc_s[...]

    qmap = lambda h, s, st: (h, st[1, s], 0)
    kmap = lambda h, s, st: (h, st[0, s], 0)
    whole = lambda h, s, st: (h, 0, 0)
    return pl.pallas_call(
        body, grid_spec=pltpu.PrefetchScalarGridSpec(
            num_scalar_prefetch=1, grid=(H, steps.shape[1]),
            in_specs=[pl.BlockSpec((1, T, Dh), qmap), pl.BlockSpec((1, T, Dh), kmap), pl.BlockSpec((1, T, Dh), kmap),
                      pl.BlockSpec((1, T, Dh), qmap), pl.BlockSpec((1, T, Dh), qmap),
                      pl.BlockSpec((1, 1, T), lambda h, s, st: (h, 0, st[0, s])), pl.BlockSpec((1, T, 1), qmap)],
            out_specs=[pl.BlockSpec((1, S, Dh), whole), pl.BlockSpec((1, T, Dh), kmap), pl.BlockSpec((1, T, Dh), kmap),
                       pl.BlockSpec((1, 1, T), lambda h, s, st: (h, 0, st[0, s])), pl.BlockSpec((1, S, 1), whole)],
            scratch_shapes=[pltpu.VMEM((T, Dh), F32), pltpu.VMEM((T, Dh), F32), pltpu.VMEM((1, T), F32)]),
        out_shape=[jax.ShapeDtypeStruct((H, S, Dh), F32), jax.ShapeDtypeStruct((H, S, Dh), F32),
                   jax.ShapeDtypeStruct((H, S, Dh), F32), jax.ShapeDtypeStruct((H, 1, S), F32),
                   jax.ShapeDtypeStruct((H, S, 1), F32)],
        compiler_params=_params(("arbitrary", "arbitrary")), name=name)(steps, q, k, v, do, o, c_row, lse)


CONV_COLS = 512


def _conv_bwd_b(dpre, w, name):
    S, C = dpre.shape
    K = w.shape[0]
    T = _tile(S, (256, 128))

    def body(i, n, R, Cs, P, N, O, A, Sc):
        def tile(r0, first, last):
            for c0 in range(0, C, CONV_COLS):
                cols = slice(c0, c0 + CONV_COLS)
                win = _win_next(R[0], N[0], i, n, r0, last, cols)
                out = None
                for k in range(K):
                    term = _ahead(win, K - 1 - k) * Cs[0][k:k + 1, cols]
                    out = term if out is None else out + term
                O[0][pl.ds(r0, SUB_ROWS), cols] = out.astype(MXU_DTYPE)

        _tile_loop(T, tile)

    return _rows(body, S, T, rows=[dpre], nexts=[dpre], consts=[w], out_rows=[(C, MXU_DTYPE)], name=name)[0]


def _dt_mask():
    lane = lax.broadcasted_iota(jnp.int32, (1, LANE), 1)
    return ((lane >= DT_LANE0) & (lane < DT_LANE0 + SSD_HEADS)).astype(F32)


def _ssd_pre_fwd(xbc, fdt, cw, cb, dtb, name):
    S, C = xbc.shape
    T = _tile(S, (256, 128))
    K = cw.shape[0]

    def body(i, n, R, Cs, P, N, O, A, Sc):
        def tile(r0, first, last):
            rows = pl.ds(r0, SUB_ROWS)
            for c0 in range(0, C, CONV_COLS):
                cols = slice(c0, c0 + CONV_COLS)
                pre = _conv_win(_taps(_win_prev(R[0], P[0], i, r0, first, cols), K), Cs[0], Cs[1], cols)
                O[0][rows, cols] = pre * _sigmoid(pre)
            O[1][rows, :] = _softplus(R[1][rows, :] + Cs[2][...]) * _dt_mask()

        _tile_loop(T, tile)

    return _rows(body, S, T, rows=[xbc, fdt], prevs=[xbc], consts=[cw, cb, dtb],
                 out_rows=[(C, F32), (LANE, F32)], name=name)


def _silu_grad(pre):
    sg = _sigmoid(pre)
    return sg * (1.0 + pre * (1.0 - sg))


def _ssd_pre_bwd_a(xbc, fdt, dxa, ddtw, cw, cb, dtb, name):
    S, C = xbc.shape
    T = _tile(S, (256, 128))
    K = cw.shape[0]

    def body(i, n, R, Cs, P, N, O, A, Sc):
        def tile(r0, first, last):
            rows = pl.ds(r0, SUB_ROWS)
            for c0 in range(0, C, CONV_COLS):
                cols = slice(c0, c0 + CONV_COLS)
                win = _taps(_win_prev(R[0], P[0], i, r0, first, cols), K)
                dpre = R[2][rows, cols] * _silu_grad(_conv_win(win, Cs[0], Cs[1], cols))
                O[0][rows, cols] = dpre
                _conv_wgrad_win(A[0], win, dpre, cols)
                A[1][:, cols] += jnp.sum(dpre, axis=0, keepdims=True)
            ddt = R[3][rows, :] * _sigmoid(R[1][rows, :] + Cs[2][...]) * _dt_mask()
            O[1][rows, :] = ddt
            A[2][...] += jnp.sum(ddt, axis=0, keepdims=True)

        _tile_loop(T, tile)

    return _rows(body, S, T, rows=[xbc, fdt, dxa, ddtw], prevs=[xbc], consts=[cw, cb, dtb],
                 out_rows=[(C, F32), (LANE, F32)],
                 out_accs=[((K, C), F32), ((1, C), F32), ((1, LANE), F32)], name=name)


def _split3(x):
    hi = x.astype(jnp.bfloat16)
    r1 = x - hi.astype(F32)
    mid = r1.astype(jnp.bfloat16)
    lo = (r1 - mid.astype(F32)).astype(jnp.bfloat16)
    return hi, mid, lo


def _expand_mat():
    r = lax.broadcasted_iota(jnp.int32, (LANE, SSD_W), 0)
    c = lax.broadcasted_iota(jnp.int32, (LANE, SSD_W), 1)
    return (r - DT_LANE0 == c // SSD_P).astype(jnp.bfloat16)


def _headsum_mat():
    r = lax.broadcasted_iota(jnp.int32, (SSD_W, LANE), 0)
    c = lax.broadcasted_iota(jnp.int32, (SSD_W, LANE), 1)
    return (c - DT_LANE0 == r // SSD_P).astype(jnp.bfloat16)


def _expand(tile, ex):
    return sum(lax.dot_general(part, ex, (((1,), (0,)), ((), ())), preferred_element_type=F32)
               for part in _split3(tile))


def _headsum(full, hs):
    return sum(lax.dot_general(part, hs, (((1,), (0,)), ((), ())), preferred_element_type=F32)
               for part in _split3(full))


def _sub_cumsum(a, reverse=False):
    n = a.shape[0]
    row = lax.broadcasted_iota(jnp.int32, a.shape, 0)
    sh = 1
    while sh < n:
        if reverse:
            a = a + jnp.where(row < n - sh, pltpu.roll(a, n - sh, axis=0), 0.0)
        else:
            a = a + jnp.where(row >= sh, pltpu.roll(a, sh, axis=0), 0.0)
        sh *= 2
    return a


def _chunk_common(xa_ref, dtw_ref, a_row, ex):
    L = SSD_CHUNK
    xs = xa_ref[:, 0:SSD_W]
    dtv = dtw_ref[...]
    acs = _sub_cumsum(dtv * a_row)
    last = acs[L - 1:L, :]
    full = _expand(jnp.concatenate([dtv, jnp.exp(last - acs), jnp.exp(acs),
                                    jnp.broadcast_to(jnp.exp(last), (8, LANE))], axis=0), ex)
    dt_full, dec_full, e_full, elast_full = full[0:L], full[L:2 * L], full[2 * L:3 * L], full[3 * L:3 * L + 1]
    xd = xs * dt_full
    return xs, dtv, acs, last, dt_full, xd, dec_full, e_full, elast_full


def _decay_mask(acs, acsT, col):
    L = SSD_CHUNK
    diff = acs[:, col:col + 1] - acsT[col:col + 1, :]
    tril = lax.broadcasted_iota(jnp.int32, (L, L), 0) >= lax.broadcasted_iota(jnp.int32, (L, L), 1)
    return jnp.where(tril, jnp.exp(jnp.minimum(diff, 0.0)), 0.0)


def _half_mask(h):
    lane = lax.broadcasted_iota(jnp.int32, (1, LANE), 1)
    return ((lane // SSD_P) == (h % 2)).astype(F32)


def _ssd_chunk_fwd(xa, dtw, a_row, d_full, name):
    S = xa.shape[0]
    L, G = SSD_CHUNK, 2
    nc = S // L
    GW = SSD_W // G

    def body(xa_ref, dtw_ref, a_ref, d_ref, y_ref, hp_ref, state):
        @pl.when(pl.program_id(0) == 0)
        def _():
            state[...] = jnp.zeros_like(state)

        ex = _expand_mat()
        xs, dtv, acs, last, dt_full, xd, dec_full, e_full, elast_full = _chunk_common(xa_ref, dtw_ref, a_ref[...], ex)
        acsT = acs.T
        hp_ref[0] = state[...]
        for g in range(G):
            gc = slice(g * GW, (g + 1) * GW)
            Bg = xa_ref[:, SSD_W + g * SSD_N: SSD_W + (g + 1) * SSD_N]
            Cg = xa_ref[:, SSD_W + G * SSD_N + g * SSD_N: SSD_W + G * SSD_N + (g + 1) * SSD_N]
            cb = _dot(Cg, Bg, NT)
            y_off = e_full[:, gc] * _dot(Cg, state[:, gc])
            for hp in range(GW // LANE):
                pc = slice(g * GW + hp * LANE, g * GW + (hp + 1) * LANE)
                xd_pair = xd[:, pc]
                yp = y_off[:, hp * LANE:(hp + 1) * LANE] + d_ref[:, pc] * xs[:, pc]
                for h2 in range(2):
                    h = (g * GW + hp * LANE) // SSD_P + h2
                    m = cb * _decay_mask(acs, acsT, DT_LANE0 + h)
                    yp = yp + _dot(m, xd_pair * _half_mask(h))
                y_ref[:, pc] = yp
            st_new = _dot(Bg.T, xd[:, gc] * dec_full[:, gc])
            state[:, gc] = elast_full[:, gc] * state[:, gc] + st_new

    return pl.pallas_call(
        body, grid=(nc,),
        in_specs=[pl.BlockSpec((L, SSD_CONV_CH), lambda c: (c, 0)), pl.BlockSpec((L, LANE), lambda c: (c, 0)),
                  pl.BlockSpec((1, LANE), lambda c: (0, 0)), pl.BlockSpec((1, SSD_W), lambda c: (0, 0))],
        out_specs=[pl.BlockSpec((L, SSD_W), lambda c: (c, 0)), pl.BlockSpec((1, SSD_N, SSD_W), lambda c: (c, 0, 0))],
        out_shape=[jax.ShapeDtypeStruct((S, SSD_W), F32), jax.ShapeDtypeStruct((nc, SSD_N, SSD_W), F32)],
        scratch_shapes=[pltpu.VMEM((SSD_N, SSD_W), F32)],
        compiler_params=_params(("arbitrary",)), name=name)(xa, dtw, a_row, d_full)


def _ssd_chunk_bwd(xa, dtw, dy, hprev, a_row, d_full, name):
    S = xa.shape[0]
    L, G = SSD_CHUNK, 2
    nc = S // L
    GW = SSD_W // G

    def body(xa_ref, dtw_ref, dy_ref, hp_ref, a_ref, d_ref, dxa_ref, ddt_ref, da_ref, dd_ref, dstate):
        @pl.when(pl.program_id(0) == 0)
        def _():
            dstate[...] = jnp.zeros_like(dstate)
            da_ref[...] = jnp.zeros_like(da_ref)
            dd_ref[...] = jnp.zeros_like(dd_ref)

        ex, hs = _expand_mat(), _headsum_mat()
        a_row = a_ref[...]
        xs, dtv, acs, last, dt_full, xd, dec_full, e_full, elast_full = _chunk_common(xa_ref, dtw_ref, a_row, ex)
        acsT = acs.T
        dyv = dy_ref[...]
        lane = lax.broadcasted_iota(jnp.int32, (L, LANE), 1)
        sub = lax.broadcasted_iota(jnp.int32, (LANE, L), 0)
        dacs_c = jnp.zeros((L, LANE), F32)
        dacs_r = jnp.zeros((LANE, L), F32)
        dxd_parts, yoff_parts, dxdd_parts, hh_parts = [], [], [], []
        for g in range(G):
            gc = slice(g * GW, (g + 1) * GW)
            b0 = SSD_W + g * SSD_N
            c0 = SSD_W + G * SSD_N + g * SSD_N
            Bg = xa_ref[:, b0:b0 + SSD_N]
            Cg = xa_ref[:, c0:c0 + SSD_N]
            Hp = hp_ref[0, :, gc]
            dH = dstate[:, gc]
            cb = _dot(Cg, Bg, NT)
            Gm = _dot(Cg, Hp)
            yoff_parts.append(e_full[:, gc] * Gm)
            dG = e_full[:, gc] * dyv[:, gc]
            dC = _dot(dG, Hp, NT)
            dHp = _dot(Cg.T, dG)
            xdd = xd[:, gc] * dec_full[:, gc]
            dB = _dot(xdd, dH, NT)
            dxdd = _dot(Bg, dH)
            dxdd_parts.append(dxdd)
            hh_parts.append(dH * Hp)
            dstate[:, gc] = dHp + elast_full[:, gc] * dH
            dcb = jnp.zeros((L, L), F32)
            dxd_g = []
            for hp in range(GW // LANE):
                pc = slice(g * GW + hp * LANE, g * GW + (hp + 1) * LANE)
                xd_pair = xd[:, pc]
                dxd_pair = dxdd[:, hp * LANE:(hp + 1) * LANE] * dec_full[:, pc]
                for h2 in range(2):
                    h = (g * GW + hp * LANE) // SSD_P + h2
                    col = DT_LANE0 + h
                    lm = _decay_mask(acs, acsT, col)
                    m = cb * lm
                    dy_h = dyv[:, pc] * _half_mask(h)
                    dm = _dot(dy_h, xd_pair, NT)
                    dxd_pair = dxd_pair + _dot(m.T, dy_h)
                    wm = dm * m
                    dacs_c = dacs_c + jnp.where(lane == col, jnp.sum(wm, axis=1, keepdims=True), 0.0)
                    dacs_r = dacs_r - jnp.where(sub == col, jnp.sum(wm, axis=0, keepdims=True), 0.0)
                    dcb = dcb + dm * lm
                dxd_g.append(dxd_pair)
            dxd_parts.append(jnp.concatenate(dxd_g, axis=1))
            dxa_ref[:, c0:c0 + SSD_N] = dC + _dot(dcb, Bg)
            dxa_ref[:, b0:b0 + SSD_N] = dB + _dot(dcb.T, Cg)
        dxd = jnp.concatenate(dxd_parts, axis=1)
        y_off = jnp.concatenate(yoff_parts, axis=1)
        dxdd_full = jnp.concatenate(dxdd_parts, axis=1)
        hh = jnp.concatenate(hh_parts, axis=1)
        dxa_ref[:, 0:SSD_W] = d_ref[...] * dyv + dxd * dt_full
        sums = _headsum(jnp.concatenate([dyv * xs, dxd * xs, dxdd_full * xd, hh, dyv * y_off], axis=0), hs)
        hs_skip, ddt, hs_dec, hs_state, hs_off = (sums[j * L:(j + 1) * L] for j in range(5))
        dd_ref[...] += jnp.sum(hs_skip, axis=0, keepdims=True)
        w_dec = hs_dec * jnp.exp(last - acs)
        dlast = jnp.sum(w_dec, axis=0, keepdims=True) + jnp.exp(last) * jnp.sum(hs_state, axis=0, keepdims=True)
        dacs = dacs_c + dacs_r.T + hs_off - w_dec
        rowid = lax.broadcasted_iota(jnp.int32, (L, LANE), 0)
        dacs = dacs + jnp.where(rowid == L - 1, dlast, 0.0)
        da = _sub_cumsum(dacs, reverse=True)
        ddt_ref[...] = ddt + da * a_row
        da_ref[...] += jnp.sum(da * dtv, axis=0, keepdims=True)

    rev = lambda c: (nc - 1 - c, 0)
    return pl.pallas_call(
        body, grid=(nc,),
        in_specs=[pl.BlockSpec((L, SSD_CONV_CH), rev), pl.BlockSpec((L, LANE), rev), pl.BlockSpec((L, SSD_W), rev),
                  pl.BlockSpec((1, SSD_N, SSD_W), lambda c: (nc - 1 - c, 0, 0)),
                  pl.BlockSpec((1, LANE), lambda c: (0, 0)), pl.BlockSpec((1, SSD_W), lambda c: (0, 0))],
        out_specs=[pl.BlockSpec((L, SSD_CONV_CH), rev), pl.BlockSpec((L, LANE), rev),
                   pl.BlockSpec((1, LANE), lambda c: (0, 0)), pl.BlockSpec((1, LANE), lambda c: (0, 0))],
        out_shape=[jax.ShapeDtypeStruct((S, SSD_CONV_CH), F32), jax.ShapeDtypeStruct((S, LANE), F32),
                   jax.ShapeDtypeStruct((1, LANE), F32), jax.ShapeDtypeStruct((1, LANE), F32)],
        scratch_shapes=[pltpu.VMEM((SSD_N, SSD_W), F32)],
        compiler_params=_params(("arbitrary",)), name=name)(xa, dtw, dy, hprev, a_row, d_full)


def _ssd_post_fwd(y, z, w, name):
    S = y.shape[0]
    GW = SSD_W // 2

    def body(i, n, R, C, P, N, O, A, Sc):
        zv = R[1][...]
        v = R[0][...] * (zv * _sigmoid(zv))
        for g in range(2):
            gc = slice(g * GW, (g + 1) * GW)
            vg = v[:, gc]
            r = lax.rsqrt(jnp.mean(vg * vg, axis=-1, keepdims=True) + NORM_EPS)
            O[0][:, gc] = (vg * r * C[0][:, gc]).astype(MXU_DTYPE)

    return _rows(body, S, _tile(S, (256, 128)), rows=[y, z], consts=[w], out_rows=[(SSD_W, MXU_DTYPE)], name=name)[0]


def _ssd_post_bwd(y, z, dyn, w, name):
    S = y.shape[0]
    GW = SSD_W // 2

    def body(i, n, R, C, P, N, O, A, Sc):
        yv, zv, dn = R[0][...], R[1][...], R[2][...]
        sz = zv * _sigmoid(zv)
        v = yv * sz
        for g in range(2):
            gc = slice(g * GW, (g + 1) * GW)
            dv, dw = _norm_bwd_math(v[:, gc], C[0][:, gc], dn[:, gc])
            A[0][:, gc] += dw
            O[0][:, gc] = dv * sz[:, gc]
            O[1][:, gc] = (dv * yv[:, gc] * _silu_grad(zv[:, gc])).astype(MXU_DTYPE)

    return _rows(body, S, _tile(S, (256, 128)), rows=[y, z, dyn], consts=[w],
                 out_rows=[(SSD_W, F32), (SSD_W, MXU_DTYPE)], out_accs=[((1, SSD_W), F32)], name=name)


def _merge_fwd(gl, yp, ya, ys, name):
    S, D = yp.shape

    def body(i, n, R, C, P, N, O, A, Sc):
        def tile(r0, first, last):
            rows = pl.ds(r0, SUB_ROWS)
            for c0 in range(0, D, CONV_COLS):
                cols = slice(c0, c0 + CONV_COLS)
                acc = None
                for b in range(3):
                    term = _sigmoid(R[0][rows, b * D + c0:b * D + c0 + CONV_COLS]) * R[1 + b][rows, cols]
                    acc = term if acc is None else acc + term
                O[0][rows, cols] = acc.astype(MXU_DTYPE)

        _tile_loop(T, tile)

    T = _tile(S, (256, 128))
    return _rows(body, S, T, rows=[gl, yp, ya, ys], out_rows=[(D, MXU_DTYPE)], name=name)[0]


def _merge_bwd(gl, yp, ya, ys, dm, name):
    S, D = yp.shape
    T = _tile(S, (256, 128))

    def body(i, n, R, C, P, N, O, A, Sc):
        def tile(r0, first, last):
            rows = pl.ds(r0, SUB_ROWS)
            for c0 in range(0, D, CONV_COLS):
                cols = slice(c0, c0 + CONV_COLS)
                dmv = R[4][rows, cols]
                for b in range(3):
                    gcols = slice(b * D + c0, b * D + c0 + CONV_COLS)
                    gt = _sigmoid(R[0][rows, gcols])
                    O[b][rows, cols] = (gt * dmv).astype(MXU_DTYPE)
                    O[3][rows, gcols] = (dmv * R[1 + b][rows, cols] * gt * (1.0 - gt)).astype(MXU_DTYPE)

        _tile_loop(T, tile)

    return _rows(body, S, T, rows=[gl, yp, ya, ys, dm],
                 out_rows=[(D, MXU_DTYPE)] * 3 + [(3 * D, MXU_DTYPE)], name=name)


FFN_COLS = 256


def _ffn_act_fwd(hpre, cw, cb, name):
    S, C = hpre.shape
    K = cw.shape[0]
    T = _tile(S, (256, 128))
    Fd = C // 2

    def body(i, n, R, Cs, P, N, O, A, Sc):
        def tile(r0, first, last):
            for c0 in range(0, Fd, FFN_COLS):
                gcols, vcols = slice(c0, c0 + FFN_COLS), slice(Fd + c0, Fd + c0 + FFN_COLS)
                gt = _conv_win(_taps(_win_prev(R[0], P[0], i, r0, first, gcols), K), Cs[0], Cs[1], gcols)
                val = _conv_win(_taps(_win_prev(R[0], P[0], i, r0, first, vcols), K), Cs[0], Cs[1], vcols)
                O[0][pl.ds(r0, SUB_ROWS), gcols] = (gt * _sigmoid(gt) * val).astype(MXU_DTYPE)

        _tile_loop(T, tile)

    return _rows(body, S, T, rows=[hpre], prevs=[hpre], consts=[cw, cb], out_rows=[(Fd, MXU_DTYPE)], name=name)[0]


def _ffn_act_bwd_a(hpre, dact, cw, cb, name):
    S, C = hpre.shape
    K = cw.shape[0]
    T = _tile(S, (256, 128))
    Fd = C // 2

    def body(i, n, R, Cs, P, N, O, A, Sc):
        def tile(r0, first, last):
            rows = pl.ds(r0, SUB_ROWS)
            for c0 in range(0, Fd, FFN_COLS):
                gcols, vcols = slice(c0, c0 + FFN_COLS), slice(Fd + c0, Fd + c0 + FFN_COLS)
                gwin = _taps(_win_prev(R[0], P[0], i, r0, first, gcols), K)
                vwin = _taps(_win_prev(R[0], P[0], i, r0, first, vcols), K)
                gt = _conv_win(gwin, Cs[0], Cs[1], gcols)
                val = _conv_win(vwin, Cs[0], Cs[1], vcols)
                da = R[1][rows, gcols]
                for cols, win, d in ((gcols, gwin, da * val * _silu_grad(gt)), (vcols, vwin, da * gt * _sigmoid(gt))):
                    O[0][rows, cols] = d
                    _conv_wgrad_win(A[0], win, d, cols)
                    A[1][:, cols] += jnp.sum(d, axis=0, keepdims=True)

        _tile_loop(T, tile)

    return _rows(body, S, T, rows=[hpre, dact], prevs=[hpre], consts=[cw, cb], out_rows=[(C, F32)],
                 out_accs=[((K, C), F32), ((1, C), F32)], name=name)


def _adamw_math(g, w, m, v):
    c1 = 1.0 - ADAM_B1 ** ADAM_STEP
    c2 = 1.0 - ADAM_B2 ** ADAM_STEP
    mn = ADAM_B1 * m + (1.0 - ADAM_B1) * g
    vn = ADAM_B2 * v + (1.0 - ADAM_B2) * (g * g)
    return -ADAM_LR * ((mn / c1) / (jnp.sqrt(vn / c2) + ADAM_EPS) + ADAM_WD * w), mn, vn


def _adamw(g, w, m, v, name):
    R_, W = g.shape

    def body(i, n, R, C, P, N, O, A, Sc):
        O[0][...], O[1][...], O[2][...] = _adamw_math(R[0][...], R[1][...], R[2][...], R[3][...])

    return _rows(body, R_, _row_tile(R_, W, 7), rows=[g, w, m, v], out_rows=[(W, F32)] * 3, name=name)


def _adamw_pair(q, other, w, m, v, pos, name):
    _, R_, W = w.shape
    T = _row_tile(R_, W, 9)

    def body(pos_ref, q_ref, o_ref, w_ref, m_ref, v_ref, g_out, d_out, m_out, v_out):
        g = jnp.where(pl.program_id(0) == pos_ref[1], q_ref[...], o_ref[...])
        g_out[0] = g
        d_out[0], m_out[0], v_out[0] = _adamw_math(g, w_ref[0], m_ref[0], v_ref[0])

    flat = pl.BlockSpec((T, W), lambda l, i, pos: (i, 0))
    full = pl.BlockSpec((1, T, W), lambda l, i, pos: (l, i, 0))
    return _scalar_call(body, pos, (2, R_ // T), [flat, flat, full, full, full], [full] * 4,
                        [jax.ShapeDtypeStruct(w.shape, F32)] * 4, (q, other, w, m, v), name)


def _row_tile(rows, width, n_blocks, budget=14 * 1024 * 1024):
    wpad = -(-width // LANE) * LANE
    for t in (512, 256, 128, 64, 32, 16, 8):
        if rows % t == 0 and n_blocks * t * wpad * 4 <= budget:
            return t
    return rows


_ANY = pl.BlockSpec(memory_space=pl.ANY)
_MESH = pl.DeviceIdType.MESH


DMA_CHUNK_BYTES = 2 * 1024 * 1024


def _row_chunks(shape, dtype):
    r = shape[-2]
    total = 1
    for s in shape:
        total *= s
    want = max(1, (total * jnp.dtype(dtype).itemsize) // DMA_CHUNK_BYTES)
    n = 1
    while n * 2 <= want and r % (n * 2 * 16) == 0 and n < 8:
        n *= 2
    return [(j * (r // n), r // n) for j in range(n)]


def _comm_call(plan, srcs, out_shapes, name):
    n = len(srcs)
    probe = plan(0, 0, 0, [_ShapeOnly(s.shape) for s in srcs], [_ShapeOnly(s.shape) for s in out_shapes])
    n_local, n_remote = len(probe[0]), len(probe[1])

    def body(*refs):
        src_refs, out_refs = refs[:n], refs[n:2 * n]
        send_sems, recv_sems, local_sems = refs[2 * n:]
        x, y, c = lax.axis_index("x"), lax.axis_index("y"), lax.axis_index("c")
        local, remote = plan(x, y, c, src_refs, out_refs)
        started = []
        for j, (s, d) in enumerate(local):
            cp = pltpu.make_async_copy(s, d, local_sems.at[j])
            cp.start()
            started.append(cp)
        sent = []
        for j, (s, d, peer) in enumerate(remote):
            cp = pltpu.make_async_remote_copy(src_ref=s, dst_ref=d, send_sem=send_sems.at[j], recv_sem=recv_sems.at[j],
                                              device_id=peer, device_id_type=_MESH)
            cp.start()
            sent.append(cp)
        for cp in sent:
            cp.wait()
        for cp in started:
            cp.wait()

    return pl.pallas_call(
        body, in_specs=[_ANY] * n, out_specs=[_ANY] * n,
        out_shape=[jax.ShapeDtypeStruct(s.shape, s.dtype) for s in out_shapes],
        scratch_shapes=[pltpu.SemaphoreType.DMA((n_remote,)), pltpu.SemaphoreType.DMA((n_remote,)),
                        pltpu.SemaphoreType.DMA((max(n_local, 1),))], name=name)(*srcs)


class _ShapeOnly:
    def __init__(self, shape):
        self.shape = tuple(shape)

    @property
    def at(self):
        return self

    def __getitem__(self, idx):
        return self


def _other_places(x, y):
    return [(1 - x, y), (x, 1 - y), (1 - x, 1 - y)]


def _gather_places(shards, row_major, name):
    n = len(shards)
    outs = []
    for s, rm in zip(shards, row_major):
        L_, r, c_ = s.shape
        assert L_ == 2
        outs.append(jax.ShapeDtypeStruct((L_, N_PLACES, r, c_) if rm else (N_PLACES, L_, r, c_), s.dtype))
    n_copies = 3 * sum(len(_row_chunks(s.shape[1:], s.dtype)) for s in shards)

    def body(*refs):
        src_refs, out_refs = refs[:n], refs[n:2 * n]
        ici_send, ici_recv, d2d_send, d2d_recv = refs[2 * n:]
        x, y, c = lax.axis_index("x"), lax.axis_index("y"), lax.axis_index("c")
        me = 2 * x + y

        def slot(o_ref, rm, place, layer, r0, rn):
            return o_ref.at[layer, place, pl.ds(r0, rn), :] if rm else o_ref.at[place, layer, pl.ds(r0, rn), :]

        over_ici, landed = [], []
        for s_ref, o_ref, rm, s in zip(src_refs, out_refs, row_major, shards):
            for r0, rn in _row_chunks(s.shape[1:], s.dtype):
                for px, py in _other_places(x, y):
                    j = len(over_ici)
                    cp = pltpu.make_async_remote_copy(
                        src_ref=s_ref.at[c, pl.ds(r0, rn), :], dst_ref=slot(o_ref, rm, me, c, r0, rn),
                        send_sem=ici_send.at[j], recv_sem=ici_recv.at[j], device_id=(px, py, c), device_id_type=_MESH)
                    cp.start()
                    over_ici.append(cp)
                    landed.append((o_ref, rm, 2 * px + py, r0, rn))
        passed = []
        for j, (o_ref, rm, place, r0, rn) in enumerate(landed):
            pltpu.make_async_remote_copy(
                src_ref=slot(o_ref, rm, place, c, r0, rn), dst_ref=slot(o_ref, rm, place, c, r0, rn),
                send_sem=ici_send.at[j], recv_sem=ici_recv.at[j], device_id=(x, y, c), device_id_type=_MESH).wait_recv()
            cp = pltpu.make_async_remote_copy(
                src_ref=slot(o_ref, rm, place, c, r0, rn), dst_ref=slot(o_ref, rm, place, c, r0, rn),
                send_sem=d2d_send.at[j], recv_sem=d2d_recv.at[j], device_id=(x, y, 1 - c), device_id_type=_MESH)
            cp.start()
            passed.append(cp)
        for j, (o_ref, rm, place, r0, rn) in enumerate(landed):
            pltpu.make_async_remote_copy(
                src_ref=slot(o_ref, rm, place, 1 - c, r0, rn), dst_ref=slot(o_ref, rm, place, 1 - c, r0, rn),
                send_sem=d2d_send.at[j], recv_sem=d2d_recv.at[j], device_id=(x, y, 1 - c), device_id_type=_MESH).wait_recv()
        for cp in over_ici + passed:
            cp.wait_send()

    return pl.pallas_call(
        body, in_specs=[_ANY] * n, out_specs=[_ANY] * n, out_shape=outs,
        scratch_shapes=[pltpu.SemaphoreType.DMA((n_copies,))] * 4, name=name)(*shards)


def _reduce_sibling(gs, name):
    outs = [jax.ShapeDtypeStruct((N_PLACES,) + g.shape[2:], g.dtype) for g in gs]

    def plan(x, y, c, src_refs, out_refs):
        remote = []
        for g_ref, o_ref, g in zip(src_refs, out_refs, gs):
            for r0, rn in _row_chunks(g.shape[2:], g.dtype):
                for p in range(N_PLACES):
                    remote.append((g_ref.at[p, 1 - c, pl.ds(r0, rn), :], o_ref.at[p, pl.ds(r0, rn), :], (x, y, 1 - c)))
        return [], remote

    return _comm_call(plan, gs, outs, name)


def _reduce_places(hs, name):
    outs = [jax.ShapeDtypeStruct((3,) + h.shape[1:], h.dtype) for h in hs]

    def plan(x, y, c, src_refs, out_refs):
        remote = []
        for h_ref, o_ref, h in zip(src_refs, out_refs, hs):
            for r0, rn in _row_chunks(h.shape[1:], h.dtype):
                for j, (px, py) in enumerate(_other_places(x, y)):
                    remote.append((h_ref.at[2 * px + py, pl.ds(r0, rn), :], o_ref.at[j, pl.ds(r0, rn), :], (px, py, c)))
        return [], remote

    return _comm_call(plan, hs, outs, name)


def _swap_sibling(qs, name):
    def plan(x, y, c, src_refs, out_refs):
        remote = []
        for q_ref, o_ref, q in zip(src_refs, out_refs, qs):
            for r0, rn in _row_chunks(q.shape, q.dtype):
                remote.append((q_ref.at[pl.ds(r0, rn), :], o_ref.at[pl.ds(r0, rn), :], (x, y, 1 - c)))
        return [], remote

    return _comm_call(plan, qs, qs, name)


def _scalar_call(body, scalars, grid, in_specs, out_specs, out_shape, args, name):
    return pl.pallas_call(
        body, grid_spec=pltpu.PrefetchScalarGridSpec(num_scalar_prefetch=1, grid=grid, in_specs=in_specs,
                                                     out_specs=out_specs),
        out_shape=out_shape, compiler_params=_params(("arbitrary",) * len(grid)), name=name)(scalars, *args)


def _add_own_slot(g, r_, pos, out_dtype, name):
    P_, _, R_, W = g.shape
    T = _row_tile(R_, W, 3)

    def body(pos_ref, g_ref, r_ref, o_ref):
        o_ref[...] = (g_ref[0] + r_ref[...]).astype(out_dtype)

    return _scalar_call(
        body, pos, (P_, R_ // T),
        [pl.BlockSpec((1, 1, T, W), lambda p, i, pos: (p, pos[1], i, 0)), pl.BlockSpec((1, T, W), lambda p, i, pos: (p, i, 0))],
        pl.BlockSpec((1, T, W), lambda p, i, pos: (p, i, 0)), jax.ShapeDtypeStruct((P_, R_, W), out_dtype), (g, r_), name)


def _sum_places(g, r_, recv, pos, name):
    _, _, R_, W = g.shape
    T = _row_tile(R_, W, 5)

    def body(pos_ref, g_ref, r_ref, recv_ref, o_ref):
        for m in range(N_PLACES):
            @pl.when(pos_ref[0] == m)
            def _(m=m):
                acc = None
                for p in range(N_PLACES):
                    if p == m:
                        term = g_ref[0, 0] + r_ref[0]
                    else:
                        dx, dy = (p >> 1) != (m >> 1), (p & 1) != (m & 1)
                        term = recv_ref[0 if (dx and not dy) else 1 if (dy and not dx) else 2].astype(F32)
                    acc = term if acc is None else acc + term
                o_ref[...] = acc

    return _scalar_call(
        body, pos, (R_ // T,),
        [pl.BlockSpec((1, 1, T, W), lambda i, pos: (pos[0], pos[1], i, 0)),
         pl.BlockSpec((1, T, W), lambda i, pos: (pos[0], i, 0)), pl.BlockSpec((3, T, W), lambda i, pos: (0, i, 0))],
        pl.BlockSpec((T, W), lambda i, pos: (i, 0)), jax.ShapeDtypeStruct((R_, W), F32), (g, r_, recv), name)


def _to_heads(a, dtype):
    S = a.shape[0]
    return a.reshape(S, HEADS, HEAD_DIM).transpose(1, 0, 2).astype(dtype)


def _from_heads(a):
    return a.transpose(1, 0, 2).reshape(a.shape[1], HEADS * HEAD_DIM)


def _lane_tile(vec16):
    return jnp.concatenate([jnp.zeros((DT_LANE0,), F32), vec16,
                            jnp.zeros((LANE - DT_LANE0 - SSD_HEADS,), F32)])[None]


def _layer_consts(W, l):
    return dict(
        norm_mix=W['norm_mix'][l][None], mix=W['pool_mix'][l].astype(MXU_DTYPE), scale=W['pool_scale'][l][None],
        f_bias=W['f_bias'][l][:, None], cw=W['ssd_conv_w'][l], cb=W['ssd_conv_b'][l][None],
        dtb=_lane_tile(W['ssd_dt_bias'][l]), a_row=_lane_tile(-jnp.exp(W['ssd_a_log'][l])),
        d_full=jnp.repeat(W['ssd_d'][l], SSD_P)[None], ssd_norm=W['ssd_norm'][l][None],
        norm_ffn=W['norm_ffn'][l][None], fcw=W['ffn_conv_w'][l], fcb=W['ffn_conv_b'][l][None])


def _layer_fwd(x, W, l):
    n = f"l{l}_"
    cs = _layer_consts(W, l)
    win = {k: v[l] for k, v in W['w_in'].items()}
    u = _norm_fwd(x, cs['norm_mix'], n + "norm_mix")
    pqkv = _mm(u, win['p'], name=n + "in_p")
    z = _mm(u, win['z'], name=n + "in_z")
    xbc = _mm(u, win['x'], name=n + "in_x")
    gl = _mm(u, win['g'], name=n + "in_g")
    fdt = _mm(u, win['f'], name=n + "in_f")
    d, ypm = _pool_fwd(pqkv, cs['mix'], cs['scale'], n + "pool")
    yp = _mm(ypm, W['p_pool'][l], name=n + "p_pool")
    fT = fdt[:, :HEADS].T
    c = _logf_cumsum(fT, cs['f_bias'], n + "logf")
    c_row = c[:, None, :]
    qh = _to_heads(pqkv[:, ATTN_W:2 * ATTN_W] * ATTN_SCALE, MXU_DTYPE)
    kh, vh = (_to_heads(pqkv[:, (2 + j) * ATTN_W:(3 + j) * ATTN_W], MXU_DTYPE) for j in range(2))
    oh, lse = _attn_fwd(qh, kh, vh, c_row, n + "attn")
    o = _from_heads(oh)
    ya = _mm(o, W['p_attn'][l], name=n + "p_attn")
    xa, dtw = _ssd_pre_fwd(xbc, fdt, cs['cw'], cs['cb'], cs['dtb'], n + "ssd_pre")
    y, hprev = _ssd_chunk_fwd(xa, dtw, cs['a_row'], cs['d_full'], n + "ssd_scan")
    yn = _ssd_post_fwd(y, z, cs['ssd_norm'], n + "ssd_post")
    ys = _mm(yn, W['p_ssd'][l], name=n + "p_ssd")
    merged = _merge_fwd(gl, yp, ya, ys, n + "merge")
    x1 = _mm(merged, W['w_out'][l], acc=x, name=n + "w_out")
    u2 = _norm_fwd(x1, cs['norm_ffn'], n + "norm_ffn")
    hpre = _mm(u2, W['ffn_up'][l], name=n + "ffn_up")
    act = _ffn_act_fwd(hpre, cs['fcw'], cs['fcb'], n + "ffn_act")
    x2 = _mm(act, W['ffn_down'][l], acc=x1, name=n + "ffn_down")
    saved = dict(x=x, u=u, pqkv=pqkv, z=z, xbc=xbc, gl=gl, fdt=fdt, d=d, ypm=ypm, yp=yp, fT=fT,
                 c_row=c_row, qh=qh, kh=kh, vh=vh, oh=oh, o=o, lse=lse, ya=ya, xa=xa, dtw=dtw, y=y, hprev=hprev,
                 yn=yn, ys=ys, merged=merged, x1=x1, u2=u2, hpre=hpre, act=act, win=win, cs=cs)
    return x2, saved


def _layer_bwd(dx2, sv, W, l):
    n = f"l{l}_b_"
    cs, win = sv['cs'], sv['win']
    g = {}
    dact = _mm(dx2, W['ffn_down'][l], tb=True, name=n + "ffn_down_dx")
    g['ffn_down'] = _mm(sv['act'], dx2, ta=True, name=n + "ffn_down_dw")
    dhc, g['ffn_conv_w'], dfcb = _ffn_act_bwd_a(sv['hpre'], dact, cs['fcw'], cs['fcb'], n + "ffn_act_a")
    g['ffn_conv_b'] = dfcb[0]
    dhpre = _conv_bwd_b(dhc, cs['fcw'], n + "ffn_act_b")
    du2 = _mm(dhpre, W['ffn_up'][l], tb=True, name=n + "ffn_up_dx")
    g['ffn_up'] = _mm(sv['u2'], dhpre, ta=True, name=n + "ffn_up_dw")
    dx1, dnf = _norm_bwd(sv['x1'], cs['norm_ffn'], du2, dx2, n + "norm_ffn")
    g['norm_ffn'] = dnf[0]
    dm = _mm(dx1, W['w_out'][l], tb=True, name=n + "w_out_dx")
    g['w_out'] = _mm(sv['merged'], dx1, ta=True, name=n + "w_out_dw")
    dyp, dya, dys, dgl = _merge_bwd(sv['gl'], sv['yp'], sv['ya'], sv['ys'], dm, n + "merge")
    dypm = _mm(dyp, W['p_pool'][l], tb=True, name=n + "p_pool_dx")
    g['p_pool'] = _mm(sv['ypm'], dyp, ta=True, name=n + "p_pool_dw")
    dd, dscale, dmix = _pool_bwd_a(dypm, sv['d'], cs['mix'], cs['scale'], n + "pool_a")
    g['pool_scale'] = dscale[0]
    g['pool_mix'] = dmix.reshape(len(POOL_WINDOWS), LANE, LANE)
    dpool_v = _pool_bwd_b(dd, n + "pool_b")
    do = _mm(dya, W['p_attn'][l], tb=True, name=n + "p_attn_dx")
    g['p_attn'] = _mm(sv['o'], dya, ta=True, name=n + "p_attn_dw")
    dqh, dkh, dvh, dc, dcq = _attn_bwd(sv['qh'], sv['kh'], sv['vh'], _to_heads(do, F32), sv['oh'],
                                       sv['c_row'], sv['lse'], n + "attn")
    dfT, dfb = _logf_cumsum_bwd(sv['fT'], cs['f_bias'], dc[:, 0, :], dcq[:, :, 0], n + "logf")
    g['f_bias'] = dfb[:, 0]
    dpqkv = jnp.concatenate([dpool_v, (_from_heads(dqh) * ATTN_SCALE).astype(MXU_DTYPE)]
                            + [_from_heads(t).astype(MXU_DTYPE) for t in (dkh, dvh)], axis=1)
    dyn = _mm(dys, W['p_ssd'][l], tb=True, name=n + "p_ssd_dx")
    g['p_ssd'] = _mm(sv['yn'], dys, ta=True, name=n + "p_ssd_dw")
    dy, dz, dsn = _ssd_post_bwd(sv['y'], sv['z'], dyn, cs['ssd_norm'], n + "ssd_post")
    g['ssd_norm'] = dsn[0]
    dxa, ddtw, dA, dD = _ssd_chunk_bwd(sv['xa'], sv['dtw'], dy, sv['hprev'], cs['a_row'], cs['d_full'], n + "ssd_scan")
    heads = slice(DT_LANE0, DT_LANE0 + SSD_HEADS)
    g['ssd_a_log'] = dA[0, heads] * cs['a_row'][0, heads]
    g['ssd_d'] = dD[0, heads]
    dpre, ddt_raw, g['ssd_conv_w'], dcb, ddtb = _ssd_pre_bwd_a(sv['xbc'], sv['fdt'], dxa, ddtw, cs['cw'], cs['cb'],
                                                              cs['dtb'], n + "ssd_pre_a")
    g['ssd_conv_b'] = dcb[0]
    g['ssd_dt_bias'] = ddtb[0, heads]
    dxbc = _conv_bwd_b(dpre, cs['cw'], n + "ssd_pre_b")
    dfdt = jnp.concatenate([dfT.T, ddt_raw[:, HEADS:]], axis=1).astype(MXU_DTYPE)
    dsegs = dict(p=dpqkv, z=dz, x=dxbc, g=dgl, f=dfdt)
    du, dwin = None, {}
    for key in ('p', 'z', 'x', 'g', 'f'):
        du = _mm(dsegs[key], win[key], tb=True, acc=du, name=n + "in_dx_" + key)
        dwin[key] = _mm(sv['u'], dsegs[key], ta=True, name=n + "in_dw_" + key)
    g['w_in'] = dwin
    dx, dnm = _norm_bwd(sv['x'], cs['norm_mix'], du, dx1, n + "norm_mix")
    g['norm_mix'] = dnm[0]
    return dx, g


def _local_step(x, target, W):
    depth = W['norm_mix'].shape[0]
    saved = []
    h = x
    for l in range(depth):
        h, sv = _layer_fwd(h, W, l)
        saved.append(sv)
    dx, dwf, loss = _loss_head(h, W['norm_final'][None], target, "loss_head")
    grads = [None] * depth
    for l in reversed(range(depth)):
        dx, grads[l] = _layer_bwd(dx, saved[l], W, l)
    return loss[0, 0], dx, grads, dwf[0]


def _pack_rows(parts, row_align=1):
    flat = jnp.concatenate([p.reshape(-1) for p in parts])
    n = flat.shape[0]
    total = -(-n // (PACK_W * row_align)) * PACK_W * row_align
    if total > n:
        flat = jnp.concatenate([flat, jnp.zeros((total - n,), flat.dtype)])
    return flat.reshape(-1, PACK_W)


def _unpack_rows(buf, shapes):
    flat = buf.reshape(-1)
    out, pos = [], 0
    for shp in shapes:
        size = 1
        for s in shp:
            size *= s
        out.append(flat[pos:pos + size].reshape(shp))
        pos += size
    return out


def _to_place_major(gfull, name):
    R_, C = gfull.shape
    if name in COL_SHARDED:
        return gfull.reshape(R_, N_PLACES, C // N_PLACES).transpose(1, 0, 2)
    return gfull.reshape(N_PLACES, R_ // N_PLACES, C)


_W_IN_LAYOUT = (('p', 0, 0, 2048), ('f', 0, 2048, HEADS), ('z', 0, 2056, 1024), ('x', 0, 3080, 1536),
                ('f', DT_LANE0, 4616, SSD_HEADS), ('g', 0, 4632, 3072))


def _w_in_segments(slabs):
    starts = [0]
    for s in slabs:
        starts.append(starts[-1] + s.shape[-1])

    def cols(a, b):
        parts = []
        for s, s0 in zip(slabs, starts):
            lo, hi = max(a, s0), min(b, s0 + s.shape[-1])
            if lo < hi:
                parts.append(s[..., lo - s0:hi - s0])
        return parts[0] if len(parts) == 1 else jnp.concatenate(parts, axis=-1)

    pad = jnp.zeros(slabs[0].shape[:-1] + (LANE - DT_LANE0 - SSD_HEADS,), slabs[0].dtype)
    return dict(p=cols(0, 2048), z=cols(2056, 3080), x=cols(3080, 4616), g=cols(4632, 7704),
                f=jnp.concatenate([cols(2048, 2056), cols(4616, 4632), pad], axis=-1))


def _w_in_columns(segs, a, b):
    parts = []
    for key, s0, g0, w in _W_IN_LAYOUT:
        lo, hi = max(a, g0), min(b, g0 + w)
        if lo < hi:
            parts.append(segs[key][..., s0 + lo - g0:s0 + hi - g0])
    return parts[0] if len(parts) == 1 else jnp.concatenate(parts, axis=-1)


def kernel(x, norm_mix, w_in, pool_mix, pool_scale, f_bias, ssd_conv_w, ssd_conv_b, ssd_dt_bias, ssd_a_log, ssd_d, ssd_norm, p_pool, p_attn, p_ssd, w_out, norm_ffn, ffn_up, ffn_conv_w, ffn_conv_b, ffn_down, norm_final, loss_target, m_norm_mix, m_w_in, m_pool_mix, m_pool_scale, m_f_bias, m_ssd_conv_w, m_ssd_conv_b, m_ssd_dt_bias, m_ssd_a_log, m_ssd_d, m_ssd_norm, m_p_pool, m_p_attn, m_p_ssd, m_w_out, m_norm_ffn, m_ffn_up, m_ffn_conv_w, m_ffn_conv_b, m_ffn_down, m_norm_final, v_norm_mix, v_w_in, v_pool_mix, v_pool_scale, v_f_bias, v_ssd_conv_w, v_ssd_conv_b, v_ssd_dt_bias, v_ssd_a_log, v_ssd_d, v_ssd_norm, v_p_pool, v_p_attn, v_p_ssd, v_w_out, v_norm_ffn, v_ffn_up, v_ffn_conv_w, v_ffn_conv_b, v_ffn_down, v_norm_final):
    args = dict(locals())
    w_sh = {k: args[k] for k in WEIGHTS}
    m_sh = {k: args['m_' + k] for k in WEIGHTS}
    v_sh = {k: args['v_' + k] for k in WEIGHTS}
    depth = norm_mix.shape[0]
    place = 2 * lax.axis_index("x") + lax.axis_index("y")
    row_sharded = [k for k in BIG if k not in COL_SHARDED]

    sent = {k: w_sh[k].astype(MXU_DTYPE) for k in BIG}
    sent.update({k: w_sh[k] for k in SMALL_SHARDED})
    gathered = _gather_places([sent[k] for k in BIG + SMALL_SHARDED],
                              [k in row_sharded for k in BIG + SMALL_SHARDED], "gather_weights")
    gathered = dict(zip(BIG + SMALL_SHARDED, gathered))
    W = {k: w_sh[k] for k in SMALL if k not in SMALL_SHARDED}
    zero = jnp.zeros((), jnp.int32)
    for k in BIG + SMALL_SHARDED:
        if k in row_sharded:
            gk = lax.dynamic_update_slice(gathered[k], sent[k][:, None], (zero, place, zero, zero))
            W[k] = gk.reshape(gk.shape[0], -1, gk.shape[-1])
            continue
        gk = lax.dynamic_update_slice(gathered[k], sent[k][None], (place, zero, zero, zero))
        if k == 'w_in':
            W[k] = _w_in_segments([gk[p] for p in range(N_PLACES)])
        else:
            W[k] = jnp.concatenate([gk[p] for p in range(N_PLACES)], axis=-1)

    loss_local, grad_x, grads, g_final = _local_step(x[0], loss_target[0], W)
    loss = lax.psum(loss_local, ("x", "y", "c"))

    def place_major(k, l):
        if k == 'w_in':
            c = IN_TOTAL // N_PLACES
            return jnp.stack([_w_in_columns(grads[l][k], p * c, (p + 1) * c) for p in range(N_PLACES)])
        return _to_place_major(grads[l][k], k)

    g_big = [jnp.stack([place_major(k, l) for l in range(depth)], axis=1) for k in BIG]
    small_names = [k for k in SMALL if k != 'norm_final'] + ['norm_final']
    small_full = [jnp.stack([grads[l][k] for l in range(depth)]) for k in small_names[:-1]] + [g_final]
    small_full_shapes = [a.shape for a in small_full]
    small_packed = _pack_rows(small_full, 32)
    g_small = jnp.broadcast_to(small_packed.reshape(1, 2, -1, PACK_W),
                               (N_PLACES, 2, small_packed.shape[0] // 2, PACK_W))

    core = lax.axis_index("c")
    pos = jnp.stack([place, core]).astype(jnp.int32)
    g_all = g_big + [g_small]
    theirs = _reduce_sibling(g_all, "reduce_sibling")
    wire = [WIRE_DTYPE] * len(BIG) + [F32]
    halves = [_add_own_slot(g, t, pos, dt, f"reduce_sibling_add{j}")
              for j, (g, t, dt) in enumerate(zip(g_all, theirs, wire))]
    recv = _reduce_places(halves, "reduce_places")
    qs = [_sum_places(g, t, r, pos, f"reduce_places_add{j}") for j, (g, t, r) in enumerate(zip(g_all, theirs, recv))]
    others = _swap_sibling(qs, "reduce_swap")

    def mine(k, a):
        if k in SMALL_SHARDED:
            c = a.shape[-1] // N_PLACES
            return lax.dynamic_slice_in_dim(a, place * c, c, axis=a.ndim - 1)
        return a

    outs = {}
    for j, k in enumerate(BIG):
        res = _adamw_pair(qs[j], others[j], w_sh[k], m_sh[k], v_sh[k], pos, "adamw_" + k)
        for prefix, a in zip(('grad_', 'delta_', 'new_m_', 'new_v_'), res):
            outs[prefix + k] = a
    small_sum = jnp.where(core == 0, jnp.concatenate([qs[-1], others[-1]]), jnp.concatenate([others[-1], qs[-1]]))
    g_small_list = [mine(k, a) for k, a in zip(small_names, _unpack_rows(small_sum, small_full_shapes))]
    shapes = [a.shape for a in g_small_list]
    gp = _pack_rows(g_small_list, 128)
    wp, mp, vp = (_pack_rows([d[k] for k in small_names], 128) for d in (w_sh, m_sh, v_sh))
    delta_p, m_p, v_p = _adamw(gp, wp, mp, vp, "adamw_small")
    for prefix, buf in (('grad_', gp), ('delta_', delta_p), ('new_m_', m_p), ('new_v_', v_p)):
        for k, a in zip(small_names, _unpack_rows(buf, shapes)):
            outs[prefix + k] = a
    result = [loss, grad_x[None]]
    for prefix in ('grad_', 'delta_', 'new_m_', 'new_v_'):
        result += [outs[prefix + k] for k in WEIGHTS]
    return tuple(result)
```

```python
import functools

import jax
import jax.numpy as jnp
from jax import lax
from jax.experimental import pallas as pl
from jax.experimental.pallas import tpu as pltpu

F32 = jnp.float32
MXU_DTYPE = jnp.bfloat16
WIRE_DTYPE = jnp.bfloat16
NORM_EPS = 1e-6
HALO = 16
LANE = 128
NEG_BIG = -1e30
VMEM_LIMIT = 52 * 1024 * 1024

D_MODEL = 1024
POOL_WINDOWS = (2, 4, 8, 16)
POOL_W = 512
HEADS = 8
HEAD_DIM = 64
ATTN_W = 512
ATTN_SCALE = HEAD_DIM ** -0.5
SSD_W = 1024
SSD_HEADS = 16
SSD_P = 64
SSD_N = 128
SSD_CHUNK = 128
SSD_CONV_CH = 1536
FFN = 2816
DT_LANE0 = 8
IN_SPLITS = (512, 512, 512, 512, 8, 1024, 1536, 16, 3072)
IN_TOTAL = sum(IN_SPLITS)
N_PLACES = 4

ADAM_LR, ADAM_B1, ADAM_B2, ADAM_EPS, ADAM_WD, ADAM_STEP = 0.001, 0.9, 0.999, 1e-08, 0.01, 10

BIG = ('w_in', 'p_pool', 'p_attn', 'p_ssd', 'w_out', 'ffn_up', 'ffn_down')
COL_SHARDED = ('w_in', 'p_pool', 'p_attn', 'ffn_up')
SMALL = ('norm_mix', 'pool_mix', 'pool_scale', 'f_bias', 'ssd_conv_w', 'ssd_conv_b', 'ssd_dt_bias',
         'ssd_a_log', 'ssd_d', 'ssd_norm', 'norm_ffn', 'ffn_conv_w', 'ffn_conv_b', 'norm_final')
SMALL_SHARDED = ('ssd_conv_w', 'ffn_conv_w')
WEIGHTS = ('norm_mix', 'w_in', 'pool_mix', 'pool_scale', 'f_bias', 'ssd_conv_w', 'ssd_conv_b', 'ssd_dt_bias',
           'ssd_a_log', 'ssd_d', 'ssd_norm', 'p_pool', 'p_attn', 'p_ssd', 'w_out', 'norm_ffn', 'ffn_up',
           'ffn_conv_w', 'ffn_conv_b', 'ffn_down', 'norm_final')
PACK_W = 1024


def _params(sem):
    return pltpu.CompilerParams(dimension_semantics=sem, vmem_limit_bytes=VMEM_LIMIT)


def _tile(n, prefs=(512, 256, 128)):
    for t in prefs:
        if n % t == 0:
            return t
    return n


def _sigmoid(x):
    return 0.5 * jnp.tanh(0.5 * x) + 0.5


def _softplus(x):
    return jnp.maximum(x, 0.0) + jnp.log1p(jnp.exp(-jnp.abs(x)))


def _dot(a, b, dims=((1,), (0,))):
    return lax.dot_general(a.astype(MXU_DTYPE), b.astype(MXU_DTYPE), (dims, ((), ())),
                           preferred_element_type=F32)


NT = ((1,), (1,))


def _mm(a, b, *, ta=False, tb=False, acc=None, out_dtype=F32, name):
    M, K = (a.shape[1], a.shape[0]) if ta else a.shape
    N = b.shape[0] if tb else b.shape[1]
    big = (1024, 1408, 512, 256, 128)
    tm, tn = _tile(M, big), _tile(N, big)
    tk = K if K <= 1024 else _tile(K, (512, 256, 128) if ta else big)
    nk = K // tk
    a_spec = pl.BlockSpec((tk, tm), lambda i, j, k: (k, i)) if ta else pl.BlockSpec((tm, tk), lambda i, j, k: (i, k))
    b_spec = pl.BlockSpec((tn, tk), lambda i, j, k: (j, k)) if tb else pl.BlockSpec((tk, tn), lambda i, j, k: (k, j))
    in_specs = [a_spec, b_spec]
    args = [a, b]
    if acc is not None:
        in_specs.append(pl.BlockSpec((tm, tn), lambda i, j, k: (i, j)))
        args.append(acc)

    def body(*refs):
        if acc is not None:
            a_ref, b_ref, c_ref, o_ref, acc_ref = refs
        else:
            a_ref, b_ref, o_ref, acc_ref = refs
        k = pl.program_id(2)

        @pl.when(k == 0)
        def _():
            if acc is not None:
                acc_ref[...] = c_ref[...].astype(F32)
            else:
                acc_ref[...] = jnp.zeros_like(acc_ref)

        av = a_ref[...]
        if ta:
            av = av.astype(F32).T
        acc_ref[...] += _dot(av, b_ref[...], NT if tb else ((1,), (0,)))

        @pl.when(k == nk - 1)
        def _():
            o_ref[...] = acc_ref[...].astype(out_dtype)

    return pl.pallas_call(
        body, grid=(M // tm, N // tn, nk), in_specs=in_specs,
        out_specs=pl.BlockSpec((tm, tn), lambda i, j, k: (i, j)),
        out_shape=jax.ShapeDtypeStruct((M, N), out_dtype),
        scratch_shapes=[pltpu.VMEM((tm, tn), F32)],
        compiler_params=_params(("parallel", "parallel", "arbitrary")), name=name)(*args)


def _rows(body, S, T, *, rows=(), consts=(), prevs=(), nexts=(), out_rows=(), out_accs=(), scratch=(), name):
    n = S // T
    hb = T // HALO
    last_h = S // HALO - 1

    def norm(r):
        return r if isinstance(r, tuple) else (r, r.shape[1], 0)

    rows, prevs, nexts = [norm(r) for r in rows], [norm(r) for r in prevs], [norm(r) for r in nexts]
    in_specs, args = [], []
    for arr, W, cb in rows:
        in_specs.append(pl.BlockSpec((T, W), lambda i, cb=cb: (i, cb)))
        args.append(arr)
    for cst in consts:
        in_specs.append(pl.BlockSpec(cst.shape, lambda i, nd=cst.ndim: (0,) * nd))
        args.append(cst)
    for arr, W, cb in prevs:
        in_specs.append(pl.BlockSpec((HALO, W), lambda i, cb=cb: (jnp.maximum(i * hb - 1, 0), cb)))
        args.append(arr)
    for arr, W, cb in nexts:
        in_specs.append(pl.BlockSpec((HALO, W), lambda i, cb=cb: (jnp.minimum((i + 1) * hb, last_h), cb)))
        args.append(arr)
    out_specs = [pl.BlockSpec((T, W), lambda i: (i, 0)) for W, _ in out_rows]
    out_specs += [pl.BlockSpec(shp, lambda i, nd=len(shp): (0,) * nd) for shp, _ in out_accs]
    out_shape = [jax.ShapeDtypeStruct((S, W), dt) for W, dt in out_rows]
    out_shape += [jax.ShapeDtypeStruct(shp, dt) for shp, dt in out_accs]
    cuts = [len(rows), len(consts), len(prevs), len(nexts), len(out_rows), len(out_accs), len(scratch)]

    def kern(*refs):
        groups, pos = [], 0
        for c in cuts:
            groups.append(list(refs[pos:pos + c]))
            pos += c
        i = pl.program_id(0)

        @pl.when(i == 0)
        def _():
            for a_ref in groups[5]:
                a_ref[...] = jnp.zeros_like(a_ref)

        body(i, n, *groups)

    outs = pl.pallas_call(kern, grid=(n,), in_specs=in_specs, out_specs=out_specs, out_shape=out_shape,
                          scratch_shapes=list(scratch), compiler_params=_params(("arbitrary",)), name=name)(*args)
    return outs


def _fill_prev(ext, prev_ref, cur, i):
    ext[0:HALO, :] = jnp.where(i > 0, prev_ref[...].astype(F32), 0.0)
    ext[HALO:, :] = cur


def _fill_next(ext, cur, next_val, T):
    ext[0:T, :] = cur
    ext[T:, :] = next_val


def _row_ids(i, T, W=1):
    return i * T + lax.broadcasted_iota(jnp.int32, (T, W), 0)


SUB_ROWS = 32
WIN_PAD = 8


def _tile_loop(T, fn):
    n = T // SUB_ROWS
    fn(0, True, n == 1)
    if n > 2:
        def body(rb, carry):
            fn(pl.multiple_of(rb * SUB_ROWS, SUB_ROWS), False, False)
            return carry
        lax.fori_loop(1, n - 1, body, 0)
    if n > 1:
        fn((n - 1) * SUB_ROWS, False, True)


def _win_prev(x_ref, prev_ref, i, r0, first, cols):
    if first:
        top = jnp.where(i > 0, prev_ref[HALO - WIN_PAD:HALO, cols].astype(F32), 0.0)
        return jnp.concatenate([top, x_ref[0:SUB_ROWS, cols].astype(F32)], axis=0)
    start = r0 - WIN_PAD if isinstance(r0, int) else pl.multiple_of(r0 - WIN_PAD, WIN_PAD)
    return x_ref[pl.ds(start, SUB_ROWS + WIN_PAD), cols].astype(F32)


def _behind(win, j):
    return win[WIN_PAD:, :] if j == 0 else pltpu.roll(win, j, axis=0)[WIN_PAD:, :]


def _win_next(x_ref, next_ref, i, n, r0, last, cols):
    if last:
        bot = jnp.where(i < n - 1, next_ref[0:WIN_PAD, cols].astype(F32), 0.0)
        return jnp.concatenate([x_ref[r0:r0 + SUB_ROWS, cols].astype(F32), bot], axis=0)
    return x_ref[pl.ds(r0, SUB_ROWS + WIN_PAD), cols].astype(F32)


def _ahead(win, j):
    return win[:SUB_ROWS, :] if j == 0 else pltpu.roll(win, SUB_ROWS + WIN_PAD - j, axis=0)[:SUB_ROWS, :]


def _taps(win, K):
    return [_behind(win, K - 1 - k) for k in range(K)]


def _conv_win(taps, w_ref, b_ref, cols):
    out = b_ref[:, cols]
    for k, tap in enumerate(taps):
        out = out + tap * w_ref[k:k + 1, cols]
    return out


def _conv_wgrad_win(acc_ref, taps, d, cols):
    for k, tap in enumerate(taps):
        acc_ref[k:k + 1, cols] += jnp.sum(d * tap, axis=0, keepdims=True)


def _norm_fwd(x, w, name):
    S, D = x.shape

    def body(i, n, R, C, P, N, O, A, Sc):
        xv = R[0][...]
        r = lax.rsqrt(jnp.mean(xv * xv, axis=-1, keepdims=True) + NORM_EPS)
        O[0][...] = (xv * r * C[0][...]).astype(MXU_DTYPE)

    return _rows(body, S, _tile(S), rows=[x], consts=[w], out_rows=[(D, MXU_DTYPE)], name=name)[0]


def _norm_bwd_math(xv, w, du):
    r = lax.rsqrt(jnp.mean(xv * xv, axis=-1, keepdims=True) + NORM_EPS)
    xh = xv * r
    g = du * w
    dx = r * (g - xh * jnp.mean(g * xh, axis=-1, keepdims=True))
    dw = jnp.sum(du * xh, axis=0, keepdims=True)
    return dx, dw


def _norm_bwd(x, w, du, dres, name):
    S, D = x.shape

    def body(i, n, R, C, P, N, O, A, Sc):
        dx, dw = _norm_bwd_math(R[0][...], C[0][...], R[1][...])
        O[0][...] = R[2][...] + dx
        A[0][...] += dw

    return _rows(body, S, _tile(S), rows=[x, du, dres], consts=[w], out_rows=[(D, F32)],
                 out_accs=[((1, D), F32)], name=name)


def _loss_head(x, w, target, name):
    S, D = x.shape

    def body(i, n, R, C, P, N, O, A, Sc):
        xv, w_, tg = R[0][...], C[0][...], R[1][...]
        r = lax.rsqrt(jnp.mean(xv * xv, axis=-1, keepdims=True) + NORM_EPS)
        e = xv * r * w_ - tg
        A[1][...] += jnp.broadcast_to(0.5 * jnp.sum(jnp.mean(e * e, axis=-1, keepdims=True)), (1, LANE))
        dx, dw = _norm_bwd_math(xv, w_, e / D)
        O[0][...] = dx
        A[0][...] += dw

    return _rows(body, S, _tile(S), rows=[x, target], consts=[w], out_rows=[(D, F32)],
                 out_accs=[((1, D), F32), ((1, LANE), F32)], name=name)


def _pool_fwd(pqkv, mix, scale, name):
    S = pqkv.shape[0]
    T = _tile(S, (256, 128))

    def body(i, n, R, C, P, N, O, A, Sc):
        ext = Sc[0]
        v = R[0][...]
        _fill_prev(ext, P[0], v, i)
        t1 = (_row_ids(i, T) + 1).astype(F32)
        for g, w in enumerate(POOL_WINDOWS):
            cols = slice(g * LANE, (g + 1) * LANE)
            acc = v[:, cols]
            for j in range(1, w):
                acc = acc + ext[pl.ds(HALO - j, T), cols]
            d = (acc / jnp.minimum(t1, float(w)) - v[:, cols]).astype(MXU_DTYPE)
            O[0][:, cols] = d
            O[1][:, cols] = (_dot(d, C[0][g]) * C[1][:, cols]).astype(MXU_DTYPE)

    return _rows(body, S, T, rows=[(pqkv, POOL_W, 0)], prevs=[(pqkv, POOL_W, 0)], consts=[mix, scale],
                 out_rows=[(POOL_W, MXU_DTYPE), (POOL_W, MXU_DTYPE)],
                 scratch=[pltpu.VMEM((HALO + T, POOL_W), F32)], name=name)


def _pool_bwd_a(dypm, d, mix, scale, name):
    S = d.shape[0]
    T = _tile(S, (256, 128))

    def body(i, n, R, C, P, N, O, A, Sc):
        for g in range(len(POOL_WINDOWS)):
            cols = slice(g * LANE, (g + 1) * LANE)
            dg = R[1][:, cols]
            dy = R[0][:, cols]
            yg = _dot(dg, C[0][g])
            A[0][:, cols] += jnp.sum(dy * yg, axis=0, keepdims=True)
            dys = dy * C[1][:, cols]
            A[1][cols, :] += _dot(dg.astype(F32).T, dys)
            O[0][:, cols] = _dot(dys, C[0][g], NT)

    return _rows(body, S, T, rows=[dypm, d], consts=[mix, scale], out_rows=[(POOL_W, F32)],
                 out_accs=[((1, POOL_W), F32), ((POOL_W, LANE), F32)], name=name)


def _pool_bwd_b(dd, name):
    S = dd.shape[0]
    T = _tile(S, (256, 128))

    def body(i, n, R, C, P, N, O, A, Sc):
        ext = Sc[0]
        ddv = R[0][...]
        t1 = (_row_ids(i, T) + 1).astype(F32)
        nxt = jnp.where(i < n - 1, N[0][...], 0.0)
        for g, w in enumerate(POOL_WINDOWS):
            cols = slice(g * LANE, (g + 1) * LANE)
            ext[0:T, cols] = ddv[:, cols] / jnp.minimum(t1, float(w))
            ext[T:, cols] = nxt[:, cols] / float(w)
        for g, w in enumerate(POOL_WINDOWS):
            cols = slice(g * LANE, (g + 1) * LANE)
            acc = ext[0:T, cols]
            for j in range(1, w):
                acc = acc + ext[pl.ds(j, T), cols]
            O[0][:, cols] = (acc - ddv[:, cols]).astype(MXU_DTYPE)

    return _rows(body, S, T, rows=[dd], nexts=[dd], out_rows=[(POOL_W, MXU_DTYPE)],
                 scratch=[pltpu.VMEM((T + HALO, POOL_W), F32)], name=name)[0]


def _lane_cumsum(seg, reverse=False):
    lane = lax.broadcasted_iota(jnp.int32, seg.shape, 1)
    sh = 1
    while sh < LANE:
        if reverse:
            seg = seg + jnp.where(lane < LANE - sh, pltpu.roll(seg, LANE - sh, axis=1), 0.0)
        else:
            seg = seg + jnp.where(lane >= sh, pltpu.roll(seg, sh, axis=1), 0.0)
        sh *= 2
    return seg


def _logf_cumsum(fT, bias, name):
    H, S = fT.shape
    TB = _tile(S)
    nb = S // TB

    def body(f_ref, b_ref, o_ref, carry):
        @pl.when(pl.program_id(0) == 0)
        def _():
            carry[...] = jnp.zeros_like(carry)

        x = f_ref[...] + b_ref[...]
        lf = jnp.minimum(x, 0.0) - jnp.log1p(jnp.exp(-jnp.abs(x)))
        c = carry[...]
        for j in range(TB // LANE):
            seg = _lane_cumsum(lf[:, j * LANE:(j + 1) * LANE]) + c
            o_ref[:, j * LANE:(j + 1) * LANE] = seg
            c = seg[:, LANE - 1:LANE]
        carry[...] = c

    return pl.pallas_call(
        body, grid=(nb,), in_specs=[pl.BlockSpec((H, TB), lambda i: (0, i)), pl.BlockSpec((H, 1), lambda i: (0, 0))],
        out_specs=pl.BlockSpec((H, TB), lambda i: (0, i)), out_shape=jax.ShapeDtypeStruct((H, S), F32),
        scratch_shapes=[pltpu.VMEM((H, 1), F32)], compiler_params=_params(("arbitrary",)), name=name)(fT, bias)


def _logf_cumsum_bwd(fT, bias, dc, dcq, name):
    H, S = fT.shape
    TB = _tile(S)
    nb = S // TB

    def body(f_ref, b_ref, dc_ref, dcq_ref, o_ref, db_ref, carry):
        @pl.when(pl.program_id(0) == 0)
        def _():
            carry[...] = jnp.zeros_like(carry)
            db_ref[...] = jnp.zeros_like(db_ref)

        x = f_ref[...] + b_ref[...]
        sg = _sigmoid(-x)
        dcv = dc_ref[...] + dcq_ref[...]
        c = carry[...]
        db = jnp.zeros((H, 1), F32)
        for j in reversed(range(TB // LANE)):
            seg = _lane_cumsum(dcv[:, j * LANE:(j + 1) * LANE], reverse=True) + c
            df = seg * sg[:, j * LANE:(j + 1) * LANE]
            o_ref[:, j * LANE:(j + 1) * LANE] = df
            db = db + jnp.sum(df, axis=1, keepdims=True)
            c = seg[:, 0:1]
        carry[...] = c
        db_ref[...] += db

    rev = lambda i: (0, nb - 1 - i)
    return pl.pallas_call(
        body, grid=(nb,),
        in_specs=[pl.BlockSpec((H, TB), rev), pl.BlockSpec((H, 1), lambda i: (0, 0)), pl.BlockSpec((H, TB), rev),
                  pl.BlockSpec((H, TB), rev)],
        out_specs=[pl.BlockSpec((H, TB), rev), pl.BlockSpec((H, 1), lambda i: (0, 0))],
        out_shape=[jax.ShapeDtypeStruct((H, S), F32), jax.ShapeDtypeStruct((H, 1), F32)],
        scratch_shapes=[pltpu.VMEM((H, 1), F32)], compiler_params=_params(("arbitrary",)), name=name)(
            fT, bias, dc, dcq)


def _attn_scores(q, k, ck, diagonal, T):
    s = _dot(q, k, NT) - ck
    if diagonal:
        tril = lax.broadcasted_iota(jnp.int32, (T, T), 1) <= lax.broadcasted_iota(jnp.int32, (T, T), 0)
        s = jnp.where(tril, s, NEG_BIG)
    return s


def _attn_fwd(pqkv, c_row, name):
    S = pqkv.shape[0]
    T = _tile(S, (1024, 512, 256, 128))
    nq = S // T
    qb, kb, vb = (ATTN_W * j // LANE for j in (1, 2, 3))

    steps = jnp.asarray([[qi for qi in range(nq) for ki in range(qi + 1)],
                         [ki for qi in range(nq) for ki in range(qi + 1)]], jnp.int32)

    def body(st_ref, q_ref, k_ref, v_ref, ck_ref, o_ref, lse_ref, m_s, l_s, acc_s):
        qi, ki = st_ref[0, pl.program_id(1)], st_ref[1, pl.program_id(1)]

        @pl.when(ki == 0)
        def _():
            m_s[...] = jnp.full_like(m_s, NEG_BIG)
            l_s[...] = jnp.zeros_like(l_s)
            acc_s[...] = jnp.zeros_like(acc_s)

        def step(diagonal):
            qf, kv, vv = q_ref[...] * ATTN_SCALE, k_ref[...].astype(MXU_DTYPE), v_ref[...].astype(MXU_DTYPE)
            acc = acc_s[...]
            for a in range(2):
                mk = _half_mask(a)
                s = _attn_scores(qf * mk, kv, ck_ref[a], diagonal, T)
                m_new = jnp.maximum(m_s[a], jnp.max(s, axis=1, keepdims=True))
                alpha = jnp.exp(m_s[a] - m_new)
                p = jnp.exp(s - m_new)
                l_s[a] = alpha * l_s[a] + jnp.sum(p, axis=1, keepdims=True)
                acc = acc * (1.0 + mk * (alpha - 1.0)) + _dot(p, vv) * mk
                m_s[a] = m_new
            acc_s[...] = acc

        @pl.when(ki < qi)
        def _():
            step(False)

        @pl.when(ki == qi)
        def _():
            step(True)
            o_ref[...] = acc_s[...] / (l_s[0] * _half_mask(0) + l_s[1] * _half_mask(1))
            for a in range(2):
                lse_ref[a] = m_s[a] + jnp.log(l_s[a])

    return pl.pallas_call(
        body, grid_spec=pltpu.PrefetchScalarGridSpec(
            num_scalar_prefetch=1, grid=(HEADS // 2, steps.shape[1]),
            in_specs=[pl.BlockSpec((T, LANE), lambda hp, s, st: (st[0, s], qb + hp)),
                      pl.BlockSpec((T, LANE), lambda hp, s, st: (st[1, s], kb + hp)),
                      pl.BlockSpec((T, LANE), lambda hp, s, st: (st[1, s], vb + hp)),
                      pl.BlockSpec((2, 1, T), lambda hp, s, st: (hp, 0, st[1, s]))],
            out_specs=[pl.BlockSpec((T, LANE), lambda hp, s, st: (st[0, s], hp)),
                       pl.BlockSpec((2, T, 1), lambda hp, s, st: (hp, st[0, s], 0))],
            scratch_shapes=[pltpu.VMEM((2, T, 1), F32), pltpu.VMEM((2, T, 1), F32), pltpu.VMEM((T, LANE), F32)]),
        out_shape=[jax.ShapeDtypeStruct((S, ATTN_W), F32), jax.ShapeDtypeStruct((HEADS, S, 1), F32)],
        compiler_params=_params(("arbitrary", "arbitrary")), name=name)(steps, pqkv, pqkv, pqkv, c_row)


def _attn_bwd(pqkv, do, o, c_row, lse, name):
    S = pqkv.shape[0]
    T = _tile(S, (1024, 512, 256, 128))
    nq = S // T
    qb, kb, vb = (ATTN_W * j // LANE for j in (1, 2, 3))

    steps = jnp.asarray([[ki for ki in range(nq) for qi in range(ki, nq)],
                         [qi for ki in range(nq) for qi in range(ki, nq)]], jnp.int32)

    def body(st_ref, q_ref, k_ref, v_ref, do_ref, o_ref, ck_ref, lse_ref, dq_ref, dk_ref, dv_ref, dc_ref, dcq_ref,
             dk_s, dv_s, dc_s):
        ki, qi = st_ref[0, pl.program_id(1)], st_ref[1, pl.program_id(1)]

        @pl.when(pl.program_id(1) == 0)
        def _():
            dq_ref[...] = jnp.zeros_like(dq_ref)
            dcq_ref[...] = jnp.zeros_like(dcq_ref)

        @pl.when(qi == ki)
        def _():
            dk_s[...] = jnp.zeros_like(dk_s)
            dv_s[...] = jnp.zeros_like(dv_s)
            dc_s[...] = jnp.zeros_like(dc_s)

        def step(diagonal):
            qf, kf, vv = q_ref[...] * ATTN_SCALE, k_ref[...], v_ref[...].astype(MXU_DTYPE)
            dof, ov = do_ref[...], o_ref[...]
            rows = pl.ds(pl.multiple_of(qi * T, T), T)
            lane = lax.broadcasted_iota(jnp.int32, (1, LANE), 1)
            for a in range(2):
                mk = _half_mask(a)
                qa, ka, doa = (qf * mk).astype(MXU_DTYPE), (kf * mk).astype(MXU_DTYPE), dof * mk
                s = _attn_scores(qa, ka, ck_ref[a], diagonal, T)
                p = jnp.exp(s - lse_ref[a])
                delta = jnp.sum(doa * ov, axis=1, keepdims=True)
                dv_s[...] += _dot(p.T, doa)
                dp = _dot(doa, vv, NT)
                ds = p * (dp - delta)
                dc_s[a] -= jnp.sum(ds, axis=0, keepdims=True)
                dq_ref[rows, :] += _dot(ds, ka)
                dcq_ref[rows, :] += jnp.sum(ds, axis=1, keepdims=True) * (lane == a).astype(F32)
                dk_s[...] += _dot(ds.T, qa)

        @pl.when(qi > ki)
        def _():
            step(False)

        @pl.when(qi == ki)
        def _():
            step(True)

        @pl.when(qi == nq - 1)
        def _():
            dk_ref[...] = dk_s[...]
            dv_ref[...] = dv_s[...]
            dc_ref[...] = dc_s[...]

    qrow = lambda hp, s, st: (st[1, s], hp)
    krow = lambda hp, s, st: (st[0, s], hp)
    whole = lambda hp, s, st: (0, hp)
    return pl.pallas_call(
        body, grid_spec=pltpu.PrefetchScalarGridSpec(
            num_scalar_prefetch=1, grid=(HEADS // 2, steps.shape[1]),
            in_specs=[pl.BlockSpec((T, LANE), lambda hp, s, st: (st[1, s], qb + hp)),
                      pl.BlockSpec((T, LANE), lambda hp, s, st: (st[0, s], kb + hp)),
                      pl.BlockSpec((T, LANE), lambda hp, s, st: (st[0, s], vb + hp)),
                      pl.BlockSpec((T, LANE), qrow), pl.BlockSpec((T, LANE), qrow),
                      pl.BlockSpec((2, 1, T), lambda hp, s, st: (hp, 0, st[0, s])),
                      pl.BlockSpec((2, T, 1), lambda hp, s, st: (hp, st[1, s], 0))],
            out_specs=[pl.BlockSpec((S, LANE), whole), pl.BlockSpec((T, LANE), krow), pl.BlockSpec((T, LANE), krow),
                       pl.BlockSpec((2, 1, T), lambda hp, s, st: (hp, 0, st[0, s])), pl.BlockSpec((S, LANE), whole)],
            scratch_shapes=[pltpu.VMEM((T, LANE), F32), pltpu.VMEM((T, LANE), F32), pltpu.VMEM((2, 1, T), F32)]),
        out_shape=[jax.ShapeDtypeStruct((S, ATTN_W), F32), jax.ShapeDtypeStruct((S, ATTN_W), F32),
                   jax.ShapeDtypeStruct((S, ATTN_W), F32), jax.ShapeDtypeStruct((HEADS, 1, S), F32),
                   jax.ShapeDtypeStruct((S, ATTN_W), F32)],
        compiler_params=_params(("arbitrary", "arbitrary")), name=name)(
            steps, pqkv, pqkv, pqkv, do, o, c_row, lse)


CONV_COLS = 512


def _conv_bwd_b(dpre, w, name):
    S, C = dpre.shape
    K = w.shape[0]
    T = _tile(S, (256, 128))

    def body(i, n, R, Cs, P, N, O, A, Sc):
        def tile(r0, first, last):
            for c0 in range(0, C, CONV_COLS):
                cols = slice(c0, c0 + CONV_COLS)
                win = _win_next(R[0], N[0], i, n, r0, last, cols)
                out = None
                for k in range(K):
                    term = _ahead(win, K - 1 - k) * Cs[0][k:k + 1, cols]
                    out = term if out is None else out + term
                O[0][pl.ds(r0, SUB_ROWS), cols] = out.astype(MXU_DTYPE)

        _tile_loop(T, tile)

    return _rows(body, S, T, rows=[dpre], nexts=[dpre], consts=[w], out_rows=[(C, MXU_DTYPE)], name=name)[0]


def _dt_mask():
    lane = lax.broadcasted_iota(jnp.int32, (1, LANE), 1)
    return ((lane >= DT_LANE0) & (lane < DT_LANE0 + SSD_HEADS)).astype(F32)


def _ssd_pre_fwd(xbc, fdt, cw, cb, dtb, name):
    S, C = xbc.shape
    T = _tile(S, (256, 128))
    K = cw.shape[0]

    def body(i, n, R, Cs, P, N, O, A, Sc):
        def tile(r0, first, last):
            rows = pl.ds(r0, SUB_ROWS)
            for c0 in range(0, C, CONV_COLS):
                cols = slice(c0, c0 + CONV_COLS)
                pre = _conv_win(_taps(_win_prev(R[0], P[0], i, r0, first, cols), K), Cs[0], Cs[1], cols)
                O[0][rows, cols] = pre * _sigmoid(pre)
            O[1][rows, :] = _softplus(R[1][rows, :] + Cs[2][...]) * _dt_mask()

        _tile_loop(T, tile)

    return _rows(body, S, T, rows=[xbc, fdt], prevs=[xbc], consts=[cw, cb, dtb],
                 out_rows=[(C, F32), (LANE, F32)], name=name)


def _silu_grad(pre):
    sg = _sigmoid(pre)
    return sg * (1.0 + pre * (1.0 - sg))


def _ssd_pre_bwd_a(xbc, fdt, dxa, ddtw, cw, cb, dtb, name):
    S, C = xbc.shape
    T = _tile(S, (256, 128))
    K = cw.shape[0]

    def body(i, n, R, Cs, P, N, O, A, Sc):
        def tile(r0, first, last):
            rows = pl.ds(r0, SUB_ROWS)
            for c0 in range(0, C, CONV_COLS):
                cols = slice(c0, c0 + CONV_COLS)
                win = _taps(_win_prev(R[0], P[0], i, r0, first, cols), K)
                dpre = R[2][rows, cols] * _silu_grad(_conv_win(win, Cs[0], Cs[1], cols))
                O[0][rows, cols] = dpre
                _conv_wgrad_win(A[0], win, dpre, cols)
                A[1][:, cols] += jnp.sum(dpre, axis=0, keepdims=True)
            ddt = R[3][rows, :] * _sigmoid(R[1][rows, :] + Cs[2][...]) * _dt_mask()
            O[1][rows, :] = ddt
            A[2][...] += jnp.sum(ddt, axis=0, keepdims=True)

        _tile_loop(T, tile)

    return _rows(body, S, T, rows=[xbc, fdt, dxa, ddtw], prevs=[xbc], consts=[cw, cb, dtb],
                 out_rows=[(C, F32), (LANE, F32)],
                 out_accs=[((K, C), F32), ((1, C), F32), ((1, LANE), F32)], name=name)


def _split3(x):
    hi = x.astype(jnp.bfloat16)
    r1 = x - hi.astype(F32)
    mid = r1.astype(jnp.bfloat16)
    lo = (r1 - mid.astype(F32)).astype(jnp.bfloat16)
    return hi, mid, lo


def _expand_mat():
    r = lax.broadcasted_iota(jnp.int32, (LANE, SSD_W), 0)
    c = lax.broadcasted_iota(jnp.int32, (LANE, SSD_W), 1)
    return (r - DT_LANE0 == c // SSD_P).astype(jnp.bfloat16)


def _headsum_mat():
    r = lax.broadcasted_iota(jnp.int32, (SSD_W, LANE), 0)
    c = lax.broadcasted_iota(jnp.int32, (SSD_W, LANE), 1)
    return (c - DT_LANE0 == r // SSD_P).astype(jnp.bfloat16)


def _expand(tile, ex):
    return sum(lax.dot_general(part, ex, (((1,), (0,)), ((), ())), preferred_element_type=F32)
               for part in _split3(tile))


def _headsum(full, hs):
    return sum(lax.dot_general(part, hs, (((1,), (0,)), ((), ())), preferred_element_type=F32)
               for part in _split3(full))


def _sub_cumsum(a, reverse=False):
    n = a.shape[0]
    row = lax.broadcasted_iota(jnp.int32, a.shape, 0)
    sh = 1
    while sh < n:
        if reverse:
            a = a + jnp.where(row < n - sh, pltpu.roll(a, n - sh, axis=0), 0.0)
        else:
            a = a + jnp.where(row >= sh, pltpu.roll(a, sh, axis=0), 0.0)
        sh *= 2
    return a


def _chunk_common(xa_ref, dtw_ref, a_row, ex):
    L = SSD_CHUNK
    xs = xa_ref[:, 0:SSD_W]
    dtv = dtw_ref[...]
    acs = _sub_cumsum(dtv * a_row)
    last = acs[L - 1:L, :]
    full = _expand(jnp.concatenate([dtv, jnp.exp(last - acs), jnp.exp(acs),
                                    jnp.broadcast_to(jnp.exp(last), (8, LANE))], axis=0), ex)
    dt_full, dec_full, e_full, elast_full = full[0:L], full[L:2 * L], full[2 * L:3 * L], full[3 * L:3 * L + 1]
    xd = xs * dt_full
    return xs, dtv, acs, last, dt_full, xd, dec_full, e_full, elast_full


def _decay_mask(acs, acsT, col):
    L = SSD_CHUNK
    diff = acs[:, col:col + 1] - acsT[col:col + 1, :]
    tril = lax.broadcasted_iota(jnp.int32, (L, L), 0) >= lax.broadcasted_iota(jnp.int32, (L, L), 1)
    return jnp.where(tril, jnp.exp(jnp.minimum(diff, 0.0)), 0.0)


def _half_mask(h):
    lane = lax.broadcasted_iota(jnp.int32, (1, LANE), 1)
    return ((lane // SSD_P) == (h % 2)).astype(F32)


def _ssd_chunk_fwd(xa, dtw, a_row, d_full, name):
    S = xa.shape[0]
    L, G = SSD_CHUNK, 2
    nc = S // L
    GW = SSD_W // G

    def body(xa_ref, dtw_ref, a_ref, d_ref, y_ref, hp_ref, state):
        @pl.when(pl.program_id(0) == 0)
        def _():
            state[...] = jnp.zeros_like(state)

        ex = _expand_mat()
        xs, dtv, acs, last, dt_full, xd, dec_full, e_full, elast_full = _chunk_common(xa_ref, dtw_ref, a_ref[...], ex)
        acsT = acs.T
        hp_ref[0] = state[...]
        for g in range(G):
            gc = slice(g * GW, (g + 1) * GW)
            Bg = xa_ref[:, SSD_W + g * SSD_N: SSD_W + (g + 1) * SSD_N]
            Cg = xa_ref[:, SSD_W + G * SSD_N + g * SSD_N: SSD_W + G * SSD_N + (g + 1) * SSD_N]
            cb = _dot(Cg, Bg, NT)
            y_off = e_full[:, gc] * _dot(Cg, state[:, gc])
            for hp in range(GW // LANE):
                pc = slice(g * GW + hp * LANE, g * GW + (hp + 1) * LANE)
                xd_pair = xd[:, pc]
                yp = y_off[:, hp * LANE:(hp + 1) * LANE] + d_ref[:, pc] * xs[:, pc]
                for h2 in range(2):
                    h = (g * GW + hp * LANE) // SSD_P + h2
                    m = cb * _decay_mask(acs, acsT, DT_LANE0 + h)
                    yp = yp + _dot(m, xd_pair * _half_mask(h))
                y_ref[:, pc] = yp
            st_new = _dot(Bg.T, xd[:, gc] * dec_full[:, gc])
            state[:, gc] = elast_full[:, gc] * state[:, gc] + st_new

    return pl.pallas_call(
        body, grid=(nc,),
        in_specs=[pl.BlockSpec((L, SSD_CONV_CH), lambda c: (c, 0)), pl.BlockSpec((L, LANE), lambda c: (c, 0)),
                  pl.BlockSpec((1, LANE), lambda c: (0, 0)), pl.BlockSpec((1, SSD_W), lambda c: (0, 0))],
        out_specs=[pl.BlockSpec((L, SSD_W), lambda c: (c, 0)), pl.BlockSpec((1, SSD_N, SSD_W), lambda c: (c, 0, 0))],
        out_shape=[jax.ShapeDtypeStruct((S, SSD_W), F32), jax.ShapeDtypeStruct((nc, SSD_N, SSD_W), F32)],
        scratch_shapes=[pltpu.VMEM((SSD_N, SSD_W), F32)],
        compiler_params=_params(("arbitrary",)), name=name)(xa, dtw, a_row, d_full)


def _ssd_chunk_bwd(xa, dtw, dy, hprev, a_row, d_full, name):
    S = xa.shape[0]
    L, G = SSD_CHUNK, 2
    nc = S // L
    GW = SSD_W // G

    def body(xa_ref, dtw_ref, dy_ref, hp_ref, a_ref, d_ref, dxa_ref, ddt_ref, da_ref, dd_ref, dstate):
        @pl.when(pl.program_id(0) == 0)
        def _():
            dstate[...] = jnp.zeros_like(dstate)
            da_ref[...] = jnp.zeros_like(da_ref)
            dd_ref[...] = jnp.zeros_like(dd_ref)

        ex, hs = _expand_mat(), _headsum_mat()
        a_row = a_ref[...]
        xs, dtv, acs, last, dt_full, xd, dec_full, e_full, elast_full = _chunk_common(xa_ref, dtw_ref, a_row, ex)
        acsT = acs.T
        dyv = dy_ref[...]
        lane = lax.broadcasted_iota(jnp.int32, (L, LANE), 1)
        sub = lax.broadcasted_iota(jnp.int32, (LANE, L), 0)
        dacs_c = jnp.zeros((L, LANE), F32)
        dacs_r = jnp.zeros((LANE, L), F32)
        dxd_parts, yoff_parts, dxdd_parts, hh_parts = [], [], [], []
        for g in range(G):
            gc = slice(g * GW, (g + 1) * GW)
            b0 = SSD_W + g * SSD_N
            c0 = SSD_W + G * SSD_N + g * SSD_N
            Bg = xa_ref[:, b0:b0 + SSD_N]
            Cg = xa_ref[:, c0:c0 + SSD_N]
            Hp = hp_ref[0, :, gc]
            dH = dstate[:, gc]
            cb = _dot(Cg, Bg, NT)
            Gm = _dot(Cg, Hp)
            yoff_parts.append(e_full[:, gc] * Gm)
            dG = e_full[:, gc] * dyv[:, gc]
            dC = _dot(dG, Hp, NT)
            dHp = _dot(Cg.T, dG)
            xdd = xd[:, gc] * dec_full[:, gc]
            dB = _dot(xdd, dH, NT)
            dxdd = _dot(Bg, dH)
            dxdd_parts.append(dxdd)
            hh_parts.append(dH * Hp)
            dstate[:, gc] = dHp + elast_full[:, gc] * dH
            dcb = jnp.zeros((L, L), F32)
            dxd_g = []
            for hp in range(GW // LANE):
                pc = slice(g * GW + hp * LANE, g * GW + (hp + 1) * LANE)
                xd_pair = xd[:, pc]
                dxd_pair = dxdd[:, hp * LANE:(hp + 1) * LANE] * dec_full[:, pc]
                for h2 in range(2):
                    h = (g * GW + hp * LANE) // SSD_P + h2
                    col = DT_LANE0 + h
                    lm = _decay_mask(acs, acsT, col)
                    m = cb * lm
                    dy_h = dyv[:, pc] * _half_mask(h)
                    dm = _dot(dy_h, xd_pair, NT)
                    dxd_pair = dxd_pair + _dot(m.T, dy_h)
                    wm = dm * m
                    dacs_c = dacs_c + jnp.where(lane == col, jnp.sum(wm, axis=1, keepdims=True), 0.0)
                    dacs_r = dacs_r - jnp.where(sub == col, jnp.sum(wm, axis=0, keepdims=True), 0.0)
                    dcb = dcb + dm * lm
                dxd_g.append(dxd_pair)
            dxd_parts.append(jnp.concatenate(dxd_g, axis=1))
            dxa_ref[:, c0:c0 + SSD_N] = dC + _dot(dcb, Bg)
            dxa_ref[:, b0:b0 + SSD_N] = dB + _dot(dcb.T, Cg)
        dxd = jnp.concatenate(dxd_parts, axis=1)
        y_off = jnp.concatenate(yoff_parts, axis=1)
        dxdd_full = jnp.concatenate(dxdd_parts, axis=1)
        hh = jnp.concatenate(hh_parts, axis=1)
        dxa_ref[:, 0:SSD_W] = d_ref[...] * dyv + dxd * dt_full
        sums = _headsum(jnp.concatenate([dyv * xs, dxd * xs, dxdd_full * xd, hh, dyv * y_off], axis=0), hs)
        hs_skip, ddt, hs_dec, hs_state, hs_off = (sums[j * L:(j + 1) * L] for j in range(5))
        dd_ref[...] += jnp.sum(hs_skip, axis=0, keepdims=True)
        w_dec = hs_dec * jnp.exp(last - acs)
        dlast = jnp.sum(w_dec, axis=0, keepdims=True) + jnp.exp(last) * jnp.sum(hs_state, axis=0, keepdims=True)
        dacs = dacs_c + dacs_r.T + hs_off - w_dec
        rowid = lax.broadcasted_iota(jnp.int32, (L, LANE), 0)
        dacs = dacs + jnp.where(rowid == L - 1, dlast, 0.0)
        da = _sub_cumsum(dacs, reverse=True)
        ddt_ref[...] = ddt + da * a_row
        da_ref[...] += jnp.sum(da * dtv, axis=0, keepdims=True)

    rev = lambda c: (nc - 1 - c, 0)
    return pl.pallas_call(
        body, grid=(nc,),
        in_specs=[pl.BlockSpec((L, SSD_CONV_CH), rev), pl.BlockSpec((L, LANE), rev), pl.BlockSpec((L, SSD_W), rev),
                  pl.BlockSpec((1, SSD_N, SSD_W), lambda c: (nc - 1 - c, 0, 0)),
                  pl.BlockSpec((1, LANE), lambda c: (0, 0)), pl.BlockSpec((1, SSD_W), lambda c: (0, 0))],
        out_specs=[pl.BlockSpec((L, SSD_CONV_CH), rev), pl.BlockSpec((L, LANE), rev),
                   pl.BlockSpec((1, LANE), lambda c: (0, 0)), pl.BlockSpec((1, LANE), lambda c: (0, 0))],
        out_shape=[jax.ShapeDtypeStruct((S, SSD_CONV_CH), F32), jax.ShapeDtypeStruct((S, LANE), F32),
                   jax.ShapeDtypeStruct((1, LANE), F32), jax.ShapeDtypeStruct((1, LANE), F32)],
        scratch_shapes=[pltpu.VMEM((SSD_N, SSD_W), F32)],
        compiler_params=_params(("arbitrary",)), name=name)(xa, dtw, dy, hprev, a_row, d_full)


def _ssd_post_fwd(y, z, w, name):
    S = y.shape[0]
    GW = SSD_W // 2

    def body(i, n, R, C, P, N, O, A, Sc):
        zv = R[1][...]
        v = R[0][...] * (zv * _sigmoid(zv))
        for g in range(2):
            gc = slice(g * GW, (g + 1) * GW)
            vg = v[:, gc]
            r = lax.rsqrt(jnp.mean(vg * vg, axis=-1, keepdims=True) + NORM_EPS)
            O[0][:, gc] = (vg * r * C[0][:, gc]).astype(MXU_DTYPE)

    return _rows(body, S, _tile(S, (256, 128)), rows=[y, z], consts=[w], out_rows=[(SSD_W, MXU_DTYPE)], name=name)[0]


def _ssd_post_bwd(y, z, dyn, w, name):
    S = y.shape[0]
    GW = SSD_W // 2

    def body(i, n, R, C, P, N, O, A, Sc):
        yv, zv, dn = R[0][...], R[1][...], R[2][...]
        sz = zv * _sigmoid(zv)
        v = yv * sz
        for g in range(2):
            gc = slice(g * GW, (g + 1) * GW)
            dv, dw = _norm_bwd_math(v[:, gc], C[0][:, gc], dn[:, gc])
            A[0][:, gc] += dw
            O[0][:, gc] = dv * sz[:, gc]
            O[1][:, gc] = (dv * yv[:, gc] * _silu_grad(zv[:, gc])).astype(MXU_DTYPE)

    return _rows(body, S, _tile(S, (256, 128)), rows=[y, z, dyn], consts=[w],
                 out_rows=[(SSD_W, F32), (SSD_W, MXU_DTYPE)], out_accs=[((1, SSD_W), F32)], name=name)


def _merge_fwd(gl, yp, ya, ys, name):
    S, D = yp.shape

    def body(i, n, R, C, P, N, O, A, Sc):
        def tile(r0, first, last):
            rows = pl.ds(r0, SUB_ROWS)
            for c0 in range(0, D, CONV_COLS):
                cols = slice(c0, c0 + CONV_COLS)
                acc = None
                for b in range(3):
                    term = _sigmoid(R[0][rows, b * D + c0:b * D + c0 + CONV_COLS]) * R[1 + b][rows, cols]
                    acc = term if acc is None else acc + term
                O[0][rows, cols] = acc.astype(MXU_DTYPE)

        _tile_loop(T, tile)

    T = _tile(S, (256, 128))
    return _rows(body, S, T, rows=[gl, yp, ya, ys], out_rows=[(D, MXU_DTYPE)], name=name)[0]


def _merge_bwd(gl, yp, ya, ys, dm, name):
    S, D = yp.shape
    T = _tile(S, (256, 128))

    def body(i, n, R, C, P, N, O, A, Sc):
        def tile(r0, first, last):
            rows = pl.ds(r0, SUB_ROWS)
            for c0 in range(0, D, CONV_COLS):
                cols = slice(c0, c0 + CONV_COLS)
                dmv = R[4][rows, cols]
                for b in range(3):
                    gcols = slice(b * D + c0, b * D + c0 + CONV_COLS)
                    gt = _sigmoid(R[0][rows, gcols])
                    O[b][rows, cols] = (gt * dmv).astype(MXU_DTYPE)
                    O[3][rows, gcols] = (dmv * R[1 + b][rows, cols] * gt * (1.0 - gt)).astype(MXU_DTYPE)

        _tile_loop(T, tile)

    return _rows(body, S, T, rows=[gl, yp, ya, ys, dm],
                 out_rows=[(D, MXU_DTYPE)] * 3 + [(3 * D, MXU_DTYPE)], name=name)


FFN_COLS = 256


def _ffn_act_fwd(hpre, cw, cb, name):
    S, C = hpre.shape
    K = cw.shape[0]
    T = _tile(S, (256, 128))
    Fd = C // 2

    def body(i, n, R, Cs, P, N, O, A, Sc):
        def tile(r0, first, last):
            for c0 in range(0, Fd, FFN_COLS):
                gcols, vcols = slice(c0, c0 + FFN_COLS), slice(Fd + c0, Fd + c0 + FFN_COLS)
                gt = _conv_win(_taps(_win_prev(R[0], P[0], i, r0, first, gcols), K), Cs[0], Cs[1], gcols)
                val = _conv_win(_taps(_win_prev(R[0], P[0], i, r0, first, vcols), K), Cs[0], Cs[1], vcols)
                O[0][pl.ds(r0, SUB_ROWS), gcols] = (gt * _sigmoid(gt) * val).astype(MXU_DTYPE)

        _tile_loop(T, tile)

    return _rows(body, S, T, rows=[hpre], prevs=[hpre], consts=[cw, cb], out_rows=[(Fd, MXU_DTYPE)], name=name)[0]


def _ffn_act_bwd_a(hpre, dact, cw, cb, name):
    S, C = hpre.shape
    K = cw.shape[0]
    T = _tile(S, (256, 128))
    Fd = C // 2

    def body(i, n, R, Cs, P, N, O, A, Sc):
        def tile(r0, first, last):
            rows = pl.ds(r0, SUB_ROWS)
            for c0 in range(0, Fd, FFN_COLS):
                gcols, vcols = slice(c0, c0 + FFN_COLS), slice(Fd + c0, Fd + c0 + FFN_COLS)
                gwin = _taps(_win_prev(R[0], P[0], i, r0, first, gcols), K)
                vwin = _taps(_win_prev(R[0], P[0], i, r0, first, vcols), K)
                gt = _conv_win(gwin, Cs[0], Cs[1], gcols)
                val = _conv_win(vwin, Cs[0], Cs[1], vcols)
                da = R[1][rows, gcols]
                for cols, win, d in ((gcols, gwin, da * val * _silu_grad(gt)), (vcols, vwin, da * gt * _sigmoid(gt))):
                    O[0][rows, cols] = d
                    _conv_wgrad_win(A[0], win, d, cols)
                    A[1][:, cols] += jnp.sum(d, axis=0, keepdims=True)

        _tile_loop(T, tile)

    return _rows(body, S, T, rows=[hpre, dact], prevs=[hpre], consts=[cw, cb], out_rows=[(C, F32)],
                 out_accs=[((K, C), F32), ((1, C), F32)], name=name)


def _adamw_math(g, w, m, v):
    c1 = 1.0 - ADAM_B1 ** ADAM_STEP
    c2 = 1.0 - ADAM_B2 ** ADAM_STEP
    mn = ADAM_B1 * m + (1.0 - ADAM_B1) * g
    vn = ADAM_B2 * v + (1.0 - ADAM_B2) * (g * g)
    return -ADAM_LR * ((mn / c1) / (jnp.sqrt(vn / c2) + ADAM_EPS) + ADAM_WD * w), mn, vn


def _adamw(g, w, m, v, name):
    R_, W = g.shape

    def body(i, n, R, C, P, N, O, A, Sc):
        O[0][...], O[1][...], O[2][...] = _adamw_math(R[0][...], R[1][...], R[2][...], R[3][...])

    return _rows(body, R_, _row_tile(R_, W, 7), rows=[g, w, m, v], out_rows=[(W, F32)] * 3, name=name)


def _adamw_pair(q, other, w, m, v, pos, name):
    _, R_, W = w.shape
    T = _row_tile(R_, W, 9)

    def body(pos_ref, q_ref, o_ref, w_ref, m_ref, v_ref, g_out, d_out, m_out, v_out):
        g = jnp.where(pl.program_id(0) == pos_ref[1], q_ref[...], o_ref[...])
        g_out[0] = g
        d_out[0], m_out[0], v_out[0] = _adamw_math(g, w_ref[0], m_ref[0], v_ref[0])

    flat = pl.BlockSpec((T, W), lambda l, i, pos: (i, 0))
    full = pl.BlockSpec((1, T, W), lambda l, i, pos: (l, i, 0))
    return _scalar_call(body, pos, (2, R_ // T), [flat, flat, full, full, full], [full] * 4,
                        [jax.ShapeDtypeStruct(w.shape, F32)] * 4, (q, other, w, m, v), name)


def _row_tile(rows, width, n_blocks, budget=14 * 1024 * 1024):
    wpad = -(-width // LANE) * LANE
    for t in (512, 256, 128, 64, 32, 16, 8):
        if rows % t == 0 and n_blocks * t * wpad * 4 <= budget:
            return t
    return rows


_ANY = pl.BlockSpec(memory_space=pl.ANY)
_MESH = pl.DeviceIdType.MESH


DMA_CHUNK_BYTES = 2 * 1024 * 1024


def _row_chunks(shape, dtype):
    r = shape[-2]
    total = 1
    for s in shape:
        total *= s
    want = max(1, (total * jnp.dtype(dtype).itemsize) // DMA_CHUNK_BYTES)
    n = 1
    while n * 2 <= want and r % (n * 2 * 16) == 0 and n < 8:
        n *= 2
    return [(j * (r // n), r // n) for j in range(n)]


def _comm_call(plan, srcs, out_shapes, name):
    n = len(srcs)
    probe = plan(0, 0, 0, [_ShapeOnly(s.shape) for s in srcs], [_ShapeOnly(s.shape) for s in out_shapes])
    n_local, n_remote = len(probe[0]), len(probe[1])

    def body(*refs):
        src_refs, out_refs = refs[:n], refs[n:2 * n]
        send_sems, recv_sems, local_sems = refs[2 * n:]
        x, y, c = lax.axis_index("x"), lax.axis_index("y"), lax.axis_index("c")
        local, remote = plan(x, y, c, src_refs, out_refs)
        started = []
        for j, (s, d) in enumerate(local):
            cp = pltpu.make_async_copy(s, d, local_sems.at[j])
            cp.start()
            started.append(cp)
        sent = []
        for j, (s, d, peer) in enumerate(remote):
            cp = pltpu.make_async_remote_copy(src_ref=s, dst_ref=d, send_sem=send_sems.at[j], recv_sem=recv_sems.at[j],
                                              device_id=peer, device_id_type=_MESH)
            cp.start()
            sent.append(cp)
        for cp in sent:
            cp.wait()
        for cp in started:
            cp.wait()

    return pl.pallas_call(
        body, in_specs=[_ANY] * n, out_specs=[_ANY] * n,
        out_shape=[jax.ShapeDtypeStruct(s.shape, s.dtype) for s in out_shapes],
        scratch_shapes=[pltpu.SemaphoreType.DMA((n_remote,)), pltpu.SemaphoreType.DMA((n_remote,)),
                        pltpu.SemaphoreType.DMA((max(n_local, 1),))], name=name)(*srcs)


class _ShapeOnly:
    def __init__(self, shape):
        self.shape = tuple(shape)

    @property
    def at(self):
        return self

    def __getitem__(self, idx):
        return self


def _other_places(x, y):
    return [(1 - x, y), (x, 1 - y), (1 - x, 1 - y)]


def _gather_places(shards, row_major, name):
    n = len(shards)
    outs = []
    for s, rm in zip(shards, row_major):
        L_, r, c_ = s.shape
        assert L_ == 2
        outs.append(jax.ShapeDtypeStruct((L_, N_PLACES, r, c_) if rm else (N_PLACES, L_, r, c_), s.dtype))
    n_copies = 3 * sum(len(_row_chunks(s.shape[1:], s.dtype)) for s in shards)

    def body(*refs):
        src_refs, out_refs = refs[:n], refs[n:2 * n]
        ici_send, ici_recv, d2d_send, d2d_recv = refs[2 * n:]
        x, y, c = lax.axis_index("x"), lax.axis_index("y"), lax.axis_index("c")
        me = 2 * x + y

        def slot(o_ref, rm, place, layer, r0, rn):
            return o_ref.at[layer, place, pl.ds(r0, rn), :] if rm else o_ref.at[place, layer, pl.ds(r0, rn), :]

        over_ici, landed = [], []
        for s_ref, o_ref, rm, s in zip(src_refs, out_refs, row_major, shards):
            for r0, rn in _row_chunks(s.shape[1:], s.dtype):
                for px, py in _other_places(x, y):
                    j = len(over_ici)
                    cp = pltpu.make_async_remote_copy(
                        src_ref=s_ref.at[c, pl.ds(r0, rn), :], dst_ref=slot(o_ref, rm, me, c, r0, rn),
                        send_sem=ici_send.at[j], recv_sem=ici_recv.at[j], device_id=(px, py, c), device_id_type=_MESH)
                    cp.start()
                    over_ici.append(cp)
                    landed.append((o_ref, rm, 2 * px + py, r0, rn))
        passed = []
        for j, (o_ref, rm, place, r0, rn) in enumerate(landed):
            pltpu.make_async_remote_copy(
                src_ref=slot(o_ref, rm, place, c, r0, rn), dst_ref=slot(o_ref, rm, place, c, r0, rn),
                send_sem=ici_send.at[j], recv_sem=ici_recv.at[j], device_id=(x, y, c), device_id_type=_MESH).wait_recv()
            cp = pltpu.make_async_remote_copy(
                src_ref=slot(o_ref, rm, place, c, r0, rn), dst_ref=slot(o_ref, rm, place, c, r0, rn),
                send_sem=d2d_send.at[j], recv_sem=d2d_recv.at[j], device_id=(x, y, 1 - c), device_id_type=_MESH)
            cp.start()
            passed.append(cp)
        for j, (o_ref, rm, place, r0, rn) in enumerate(landed):
            pltpu.make_async_remote_copy(
                src_ref=slot(o_ref, rm, place, 1 - c, r0, rn), dst_ref=slot(o_ref, rm, place, 1 - c, r0, rn),
                send_sem=d2d_send.at[j], recv_sem=d2d_recv.at[j], device_id=(x, y, 1 - c), device_id_type=_MESH).wait_recv()
        for cp in over_ici + passed:
            cp.wait_send()

    return pl.pallas_call(
        body, in_specs=[_ANY] * n, out_specs=[_ANY] * n, out_shape=outs,
        scratch_shapes=[pltpu.SemaphoreType.DMA((n_copies,))] * 4, name=name)(*shards)


def _reduce_sibling(gs, name):
    outs = [jax.ShapeDtypeStruct((N_PLACES,) + g.shape[2:], g.dtype) for g in gs]

    def plan(x, y, c, src_refs, out_refs):
        remote = []
        for g_ref, o_ref, g in zip(src_refs, out_refs, gs):
            for r0, rn in _row_chunks(g.shape[2:], g.dtype):
                for p in range(N_PLACES):
                    remote.append((g_ref.at[p, 1 - c, pl.ds(r0, rn), :], o_ref.at[p, pl.ds(r0, rn), :], (x, y, 1 - c)))
        return [], remote

    return _comm_call(plan, gs, outs, name)


def _reduce_places(hs, name):
    outs = [jax.ShapeDtypeStruct((3,) + h.shape[1:], h.dtype) for h in hs]

    def plan(x, y, c, src_refs, out_refs):
        remote = []
        for h_ref, o_ref, h in zip(src_refs, out_refs, hs):
            for r0, rn in _row_chunks(h.shape[1:], h.dtype):
                for j, (px, py) in enumerate(_other_places(x, y)):
                    remote.append((h_ref.at[2 * px + py, pl.ds(r0, rn), :], o_ref.at[j, pl.ds(r0, rn), :], (px, py, c)))
        return [], remote

    return _comm_call(plan, hs, outs, name)


def _swap_sibling(qs, name):
    def plan(x, y, c, src_refs, out_refs):
        remote = []
        for q_ref, o_ref, q in zip(src_refs, out_refs, qs):
            for r0, rn in _row_chunks(q.shape, q.dtype):
                remote.append((q_ref.at[pl.ds(r0, rn), :], o_ref.at[pl.ds(r0, rn), :], (x, y, 1 - c)))
        return [], remote

    return _comm_call(plan, qs, qs, name)


def _scalar_call(body, scalars, grid, in_specs, out_specs, out_shape, args, name):
    return pl.pallas_call(
        body, grid_spec=pltpu.PrefetchScalarGridSpec(num_scalar_prefetch=1, grid=grid, in_specs=in_specs,
                                                     out_specs=out_specs),
        out_shape=out_shape, compiler_params=_params(("arbitrary",) * len(grid)), name=name)(scalars, *args)


def _add_own_slot(g, r_, pos, out_dtype, name):
    P_, _, R_, W = g.shape
    T = _row_tile(R_, W, 3)

    def body(pos_ref, g_ref, r_ref, o_ref):
        o_ref[...] = (g_ref[0] + r_ref[...]).astype(out_dtype)

    return _scalar_call(
        body, pos, (P_, R_ // T),
        [pl.BlockSpec((1, 1, T, W), lambda p, i, pos: (p, pos[1], i, 0)), pl.BlockSpec((1, T, W), lambda p, i, pos: (p, i, 0))],
        pl.BlockSpec((1, T, W), lambda p, i, pos: (p, i, 0)), jax.ShapeDtypeStruct((P_, R_, W), out_dtype), (g, r_), name)


def _sum_places(g, r_, recv, pos, name):
    _, _, R_, W = g.shape
    T = _row_tile(R_, W, 5)

    def body(pos_ref, g_ref, r_ref, recv_ref, o_ref):
        for m in range(N_PLACES):
            @pl.when(pos_ref[0] == m)
            def _(m=m):
                acc = None
                for p in range(N_PLACES):
                    if p == m:
                        term = g_ref[0, 0] + r_ref[0]
                    else:
                        dx, dy = (p >> 1) != (m >> 1), (p & 1) != (m & 1)
                        term = recv_ref[0 if (dx and not dy) else 1 if (dy and not dx) else 2].astype(F32)
                    acc = term if acc is None else acc + term
                o_ref[...] = acc

    return _scalar_call(
        body, pos, (R_ // T,),
        [pl.BlockSpec((1, 1, T, W), lambda i, pos: (pos[0], pos[1], i, 0)),
         pl.BlockSpec((1, T, W), lambda i, pos: (pos[0], i, 0)), pl.BlockSpec((3, T, W), lambda i, pos: (0, i, 0))],
        pl.BlockSpec((T, W), lambda i, pos: (i, 0)), jax.ShapeDtypeStruct((R_, W), F32), (g, r_, recv), name)


def _lane_tile(vec16):
    return jnp.concatenate([jnp.zeros((DT_LANE0,), F32), vec16,
                            jnp.zeros((LANE - DT_LANE0 - SSD_HEADS,), F32)])[None]


def _layer_consts(W, l):
    return dict(
        norm_mix=W['norm_mix'][l][None], mix=W['pool_mix'][l].astype(MXU_DTYPE), scale=W['pool_scale'][l][None],
        f_bias=W['f_bias'][l][:, None], cw=W['ssd_conv_w'][l], cb=W['ssd_conv_b'][l][None],
        dtb=_lane_tile(W['ssd_dt_bias'][l]), a_row=_lane_tile(-jnp.exp(W['ssd_a_log'][l])),
        d_full=jnp.repeat(W['ssd_d'][l], SSD_P)[None], ssd_norm=W['ssd_norm'][l][None],
        norm_ffn=W['norm_ffn'][l][None], fcw=W['ffn_conv_w'][l], fcb=W['ffn_conv_b'][l][None])


def _layer_fwd(x, W, l):
    n = f"l{l}_"
    cs = _layer_consts(W, l)
    win = {k: v[l] for k, v in W['w_in'].items()}
    u = _norm_fwd(x, cs['norm_mix'], n + "norm_mix")
    pqkv = _mm(u, win['p'], name=n + "in_p")
    z = _mm(u, win['z'], name=n + "in_z")
    xbc = _mm(u, win['x'], name=n + "in_x")
    gl = _mm(u, win['g'], name=n + "in_g")
    fdt = _mm(u, win['f'], name=n + "in_f")
    d, ypm = _pool_fwd(pqkv, cs['mix'], cs['scale'], n + "pool")
    yp = _mm(ypm, W['p_pool'][l], name=n + "p_pool")
    fT = fdt[:, :HEADS].T
    c = _logf_cumsum(fT, cs['f_bias'], n + "logf")
    c_row = c[:, None, :]
    o, lse = _attn_fwd(pqkv, c_row, n + "attn")
    ya = _mm(o, W['p_attn'][l], name=n + "p_attn")
    xa, dtw = _ssd_pre_fwd(xbc, fdt, cs['cw'], cs['cb'], cs['dtb'], n + "ssd_pre")
    y, hprev = _ssd_chunk_fwd(xa, dtw, cs['a_row'], cs['d_full'], n + "ssd_scan")
    yn = _ssd_post_fwd(y, z, cs['ssd_norm'], n + "ssd_post")
    ys = _mm(yn, W['p_ssd'][l], name=n + "p_ssd")
    merged = _merge_fwd(gl, yp, ya, ys, n + "merge")
    x1 = _mm(merged, W['w_out'][l], acc=x, name=n + "w_out")
    u2 = _norm_fwd(x1, cs['norm_ffn'], n + "norm_ffn")
    hpre = _mm(u2, W['ffn_up'][l], name=n + "ffn_up")
    act = _ffn_act_fwd(hpre, cs['fcw'], cs['fcb'], n + "ffn_act")
    x2 = _mm(act, W['ffn_down'][l], acc=x1, name=n + "ffn_down")
    saved = dict(x=x, u=u, pqkv=pqkv, z=z, xbc=xbc, gl=gl, fdt=fdt, d=d, ypm=ypm, yp=yp, fT=fT,
                 c_row=c_row, o=o, lse=lse, ya=ya, xa=xa, dtw=dtw, y=y, hprev=hprev,
                 yn=yn, ys=ys, merged=merged, x1=x1, u2=u2, hpre=hpre, act=act, win=win, cs=cs)
    return x2, saved


def _layer_bwd(dx2, sv, W, l):
    n = f"l{l}_b_"
    cs, win = sv['cs'], sv['win']
    g = {}
    dact = _mm(dx2, W['ffn_down'][l], tb=True, name=n + "ffn_down_dx")
    g['ffn_down'] = _mm(sv['act'], dx2, ta=True, name=n + "ffn_down_dw")
    dhc, g['ffn_conv_w'], dfcb = _ffn_act_bwd_a(sv['hpre'], dact, cs['fcw'], cs['fcb'], n + "ffn_act_a")
    g['ffn_conv_b'] = dfcb[0]
    dhpre = _conv_bwd_b(dhc, cs['fcw'], n + "ffn_act_b")
    du2 = _mm(dhpre, W['ffn_up'][l], tb=True, name=n + "ffn_up_dx")
    g['ffn_up'] = _mm(sv['u2'], dhpre, ta=True, name=n + "ffn_up_dw")
    dx1, dnf = _norm_bwd(sv['x1'], cs['norm_ffn'], du2, dx2, n + "norm_ffn")
    g['norm_ffn'] = dnf[0]
    dm = _mm(dx1, W['w_out'][l], tb=True, name=n + "w_out_dx")
    g['w_out'] = _mm(sv['merged'], dx1, ta=True, name=n + "w_out_dw")
    dyp, dya, dys, dgl = _merge_bwd(sv['gl'], sv['yp'], sv['ya'], sv['ys'], dm, n + "merge")
    dypm = _mm(dyp, W['p_pool'][l], tb=True, name=n + "p_pool_dx")
    g['p_pool'] = _mm(sv['ypm'], dyp, ta=True, name=n + "p_pool_dw")
    dd, dscale, dmix = _pool_bwd_a(dypm, sv['d'], cs['mix'], cs['scale'], n + "pool_a")
    g['pool_scale'] = dscale[0]
    g['pool_mix'] = dmix.reshape(len(POOL_WINDOWS), LANE, LANE)
    dpool_v = _pool_bwd_b(dd, n + "pool_b")
    do = _mm(dya, W['p_attn'][l], tb=True, name=n + "p_attn_dx")
    g['p_attn'] = _mm(sv['o'], dya, ta=True, name=n + "p_attn_dw")
    dq, dk, dv, dc, dcq = _attn_bwd(sv['pqkv'], do, sv['o'], sv['c_row'], sv['lse'], n + "attn")
    dcq = dcq.reshape(-1, HEADS // 2, LANE)[:, :, :2].reshape(-1, HEADS).T
    dfT, dfb = _logf_cumsum_bwd(sv['fT'], cs['f_bias'], dc[:, 0, :], dcq, n + "logf")
    g['f_bias'] = dfb[:, 0]
    dpqkv = jnp.concatenate([dpool_v, (dq * ATTN_SCALE).astype(MXU_DTYPE), dk.astype(MXU_DTYPE),
                             dv.astype(MXU_DTYPE)], axis=1)
    dyn = _mm(dys, W['p_ssd'][l], tb=True, name=n + "p_ssd_dx")
    g['p_ssd'] = _mm(sv['yn'], dys, ta=True, name=n + "p_ssd_dw")
    dy, dz, dsn = _ssd_post_bwd(sv['y'], sv['z'], dyn, cs['ssd_norm'], n + "ssd_post")
    g['ssd_norm'] = dsn[0]
    dxa, ddtw, dA, dD = _ssd_chunk_bwd(sv['xa'], sv['dtw'], dy, sv['hprev'], cs['a_row'], cs['d_full'], n + "ssd_scan")
    heads = slice(DT_LANE0, DT_LANE0 + SSD_HEADS)
    g['ssd_a_log'] = dA[0, heads] * cs['a_row'][0, heads]
    g['ssd_d'] = dD[0, heads]
    dpre, ddt_raw, g['ssd_conv_w'], dcb, ddtb = _ssd_pre_bwd_a(sv['xbc'], sv['fdt'], dxa, ddtw, cs['cw'], cs['cb'],
                                                              cs['dtb'], n + "ssd_pre_a")
    g['ssd_conv_b'] = dcb[0]
    g['ssd_dt_bias'] = ddtb[0, heads]
    dxbc = _conv_bwd_b(dpre, cs['cw'], n + "ssd_pre_b")
    dfdt = jnp.concatenate([dfT.T, ddt_raw[:, HEADS:]], axis=1).astype(MXU_DTYPE)
    dsegs = dict(p=dpqkv, z=dz, x=dxbc, g=dgl, f=dfdt)
    du, dwin = None, {}
    for key in ('p', 'z', 'x', 'g', 'f'):
        du = _mm(dsegs[key], win[key], tb=True, acc=du, name=n + "in_dx_" + key)
        dwin[key] = _mm(sv['u'], dsegs[key], ta=True, name=n + "in_dw_" + key)
    g['w_in'] = dwin
    dx, dnm = _norm_bwd(sv['x'], cs['norm_mix'], du, dx1, n + "norm_mix")
    g['norm_mix'] = dnm[0]
    return dx, g


def _local_step(x, target, W):
    depth = W['norm_mix'].shape[0]
    saved = []
    h = x
    for l in range(depth):
        h, sv = _layer_fwd(h, W, l)
        saved.append(sv)
    dx, dwf, loss = _loss_head(h, W['norm_final'][None], target, "loss_head")
    grads = [None] * depth
    for l in reversed(range(depth)):
        dx, grads[l] = _layer_bwd(dx, saved[l], W, l)
    return loss[0, 0], dx, grads, dwf[0]


def _pack_rows(parts, row_align=1):
    flat = jnp.concatenate([p.reshape(-1) for p in parts])
    n = flat.shape[0]
    total = -(-n // (PACK_W * row_align)) * PACK_W * row_align
    if total > n:
        flat = jnp.concatenate([flat, jnp.zeros((total - n,), flat.dtype)])
    return flat.reshape(-1, PACK_W)


def _unpack_rows(buf, shapes):
    flat = buf.reshape(-1)
    out, pos = [], 0
    for shp in shapes:
        size = 1
        for s in shp:
            size *= s
        out.append(flat[pos:pos + size].reshape(shp))
        pos += size
    return out


def _to_place_major(gfull, name):
    R_, C = gfull.shape
    if name in COL_SHARDED:
        return gfull.reshape(R_, N_PLACES, C // N_PLACES).transpose(1, 0, 2)
    return gfull.reshape(N_PLACES, R_ // N_PLACES, C)


_W_IN_LAYOUT = (('p', 0, 0, 2048), ('f', 0, 2048, HEADS), ('z', 0, 2056, 1024), ('x', 0, 3080, 1536),
                ('f', DT_LANE0, 4616, SSD_HEADS), ('g', 0, 4632, 3072))


def _w_in_segments(slabs):
    starts = [0]
    for s in slabs:
        starts.append(starts[-1] + s.shape[-1])

    def cols(a, b):
        parts = []
        for s, s0 in zip(slabs, starts):
            lo, hi = max(a, s0), min(b, s0 + s.shape[-1])
            if lo < hi:
                parts.append(s[..., lo - s0:hi - s0])
        return parts[0] if len(parts) == 1 else jnp.concatenate(parts, axis=-1)

    pad = jnp.zeros(slabs[0].shape[:-1] + (LANE - DT_LANE0 - SSD_HEADS,), slabs[0].dtype)
    return dict(p=cols(0, 2048), z=cols(2056, 3080), x=cols(3080, 4616), g=cols(4632, 7704),
                f=jnp.concatenate([cols(2048, 2056), cols(4616, 4632), pad], axis=-1))


def _w_in_columns(segs, a, b):
    parts = []
    for key, s0, g0, w in _W_IN_LAYOUT:
        lo, hi = max(a, g0), min(b, g0 + w)
        if lo < hi:
            parts.append(segs[key][..., s0 + lo - g0:s0 + hi - g0])
    return parts[0] if len(parts) == 1 else jnp.concatenate(parts, axis=-1)


def kernel(x, norm_mix, w_in, pool_mix, pool_scale, f_bias, ssd_conv_w, ssd_conv_b, ssd_dt_bias, ssd_a_log, ssd_d, ssd_norm, p_pool, p_attn, p_ssd, w_out, norm_ffn, ffn_up, ffn_conv_w, ffn_conv_b, ffn_down, norm_final, loss_target, m_norm_mix, m_w_in, m_pool_mix, m_pool_scale, m_f_bias, m_ssd_conv_w, m_ssd_conv_b, m_ssd_dt_bias, m_ssd_a_log, m_ssd_d, m_ssd_norm, m_p_pool, m_p_attn, m_p_ssd, m_w_out, m_norm_ffn, m_ffn_up, m_ffn_conv_w, m_ffn_conv_b, m_ffn_down, m_norm_final, v_norm_mix, v_w_in, v_pool_mix, v_pool_scale, v_f_bias, v_ssd_conv_w, v_ssd_conv_b, v_ssd_dt_bias, v_ssd_a_log, v_ssd_d, v_ssd_norm, v_p_pool, v_p_attn, v_p_ssd, v_w_out, v_norm_ffn, v_ffn_up, v_ffn_conv_w, v_ffn_conv_b, v_ffn_down, v_norm_final):
    args = dict(locals())
    w_sh = {k: args[k] for k in WEIGHTS}
    m_sh = {k: args['m_' + k] for k in WEIGHTS}
    v_sh = {k: args['v_' + k] for k in WEIGHTS}
    depth = norm_mix.shape[0]
    place = 2 * lax.axis_index("x") + lax.axis_index("y")
    row_sharded = [k for k in BIG if k not in COL_SHARDED]

    sent = {k: w_sh[k].astype(MXU_DTYPE) for k in BIG}
    sent.update({k: w_sh[k] for k in SMALL_SHARDED})
    gathered = _gather_places([sent[k] for k in BIG + SMALL_SHARDED],
                              [k in row_sharded for k in BIG + SMALL_SHARDED], "gather_weights")
    gathered = dict(zip(BIG + SMALL_SHARDED, gathered))
    W = {k: w_sh[k] for k in SMALL if k not in SMALL_SHARDED}
    zero = jnp.zeros((), jnp.int32)
    for k in BIG + SMALL_SHARDED:
        if k in row_sharded:
            gk = lax.dynamic_update_slice(gathered[k], sent[k][:, None], (zero, place, zero, zero))
            W[k] = gk.reshape(gk.shape[0], -1, gk.shape[-1])
            continue
        gk = lax.dynamic_update_slice(gathered[k], sent[k][None], (place, zero, zero, zero))
        if k == 'w_in':
            W[k] = _w_in_segments([gk[p] for p in range(N_PLACES)])
        else:
            W[k] = jnp.concatenate([gk[p] for p in range(N_PLACES)], axis=-1)

    loss_local, grad_x, grads, g_final = _local_step(x[0], loss_target[0], W)
    loss = lax.psum(loss_local, ("x", "y", "c"))

    def place_major(k, l):
        if k == 'w_in':
            c = IN_TOTAL // N_PLACES
            return jnp.stack([_w_in_columns(grads[l][k], p * c, (p + 1) * c) for p in range(N_PLACES)])
        return _to_place_major(grads[l][k], k)

    g_big = [jnp.stack([place_major(k, l) for l in range(depth)], axis=1) for k in BIG]
    small_names = [k for k in SMALL if k != 'norm_final'] + ['norm_final']
    small_full = [jnp.stack([grads[l][k] for l in range(depth)]) for k in small_names[:-1]] + [g_final]
    small_full_shapes = [a.shape for a in small_full]
    small_packed = _pack_rows(small_full, 32)
    g_small = jnp.broadcast_to(small_packed.reshape(1, 2, -1, PACK_W),
                               (N_PLACES, 2, small_packed.shape[0] // 2, PACK_W))

    core = lax.axis_index("c")
    pos = jnp.stack([place, core]).astype(jnp.int32)
    g_all = g_big + [g_small]
    theirs = _reduce_sibling(g_all, "reduce_sibling")
    wire = [WIRE_DTYPE] * len(BIG) + [F32]
    halves = [_add_own_slot(g, t, pos, dt, f"reduce_sibling_add{j}")
              for j, (g, t, dt) in enumerate(zip(g_all, theirs, wire))]
    recv = _reduce_places(halves, "reduce_places")
    qs = [_sum_places(g, t, r, pos, f"reduce_places_add{j}") for j, (g, t, r) in enumerate(zip(g_all, theirs, recv))]
    others = _swap_sibling(qs, "reduce_swap")

    def mine(k, a):
        if k in SMALL_SHARDED:
            c = a.shape[-1] // N_PLACES
            return lax.dynamic_slice_in_dim(a, place * c, c, axis=a.ndim - 1)
        return a

    outs = {}
    for j, k in enumerate(BIG):
        res = _adamw_pair(qs[j], others[j], w_sh[k], m_sh[k], v_sh[k], pos, "adamw_" + k)
        for prefix, a in zip(('grad_', 'delta_', 'new_m_', 'new_v_'), res):
            outs[prefix + k] = a
    small_sum = jnp.where(core == 0, jnp.concatenate([qs[-1], others[-1]]), jnp.concatenate([others[-1], qs[-1]]))
    g_small_list = [mine(k, a) for k, a in zip(small_names, _unpack_rows(small_sum, small_full_shapes))]
    shapes = [a.shape for a in g_small_list]
    gp = _pack_rows(g_small_list, 128)
    wp, mp, vp = (_pack_rows([d[k] for k in small_names], 128) for d in (w_sh, m_sh, v_sh))
    delta_p, m_p, v_p = _adamw(gp, wp, mp, vp, "adamw_small")
    for prefix, buf in (('grad_', gp), ('delta_', delta_p), ('new_m_', m_p), ('new_v_', v_p)):
        for k, a in zip(small_names, _unpack_rows(buf, shapes)):
            outs[prefix + k] = a
    result = [loss, grad_x[None]]
    for prefix in ('grad_', 'delta_', 'new_m_', 'new_v_'):
        result += [outs[prefix + k] for k in WEIGHTS]
    return tuple(result)
```

```python
import functools

import jax
import jax.numpy as jnp
from jax import lax
from jax.experimental import pallas as pl
from jax.experimental.pallas import tpu as pltpu

F32 = jnp.float32
MXU_DTYPE = jnp.bfloat16
WIRE_DTYPE = jnp.bfloat16
NORM_EPS = 1e-6
HALO = 16
LANE = 128
NEG_BIG = -1e30
VMEM_LIMIT = 52 * 1024 * 1024

D_MODEL = 1024
POOL_WINDOWS = (2, 4, 8, 16)
POOL_W = 512
HEADS = 8
HEAD_DIM = 64
ATTN_W = 512
ATTN_SCALE = HEAD_DIM ** -0.5
SSD_W = 1024
SSD_HEADS = 16
SSD_P = 64
SSD_N = 128
SSD_CHUNK = 128
SSD_CONV_CH = 1536
FFN = 2816
DT_LANE0 = 8
IN_SPLITS = (512, 512, 512, 512, 8, 1024, 1536, 16, 3072)
IN_TOTAL = sum(IN_SPLITS)
N_PLACES = 4

ADAM_LR, ADAM_B1, ADAM_B2, ADAM_EPS, ADAM_WD, ADAM_STEP = 0.001, 0.9, 0.999, 1e-08, 0.01, 10

BIG = ('w_in', 'p_pool', 'p_attn', 'p_ssd', 'w_out', 'ffn_up', 'ffn_down')
COL_SHARDED = ('w_in', 'p_pool', 'p_attn', 'ffn_up')
SMALL = ('norm_mix', 'pool_mix', 'pool_scale', 'f_bias', 'ssd_conv_w', 'ssd_conv_b', 'ssd_dt_bias',
         'ssd_a_log', 'ssd_d', 'ssd_norm', 'norm_ffn', 'ffn_conv_w', 'ffn_conv_b', 'norm_final')
SMALL_SHARDED = ('ssd_conv_w', 'ffn_conv_w')
WEIGHTS = ('norm_mix', 'w_in', 'pool_mix', 'pool_scale', 'f_bias', 'ssd_conv_w', 'ssd_conv_b', 'ssd_dt_bias',
           'ssd_a_log', 'ssd_d', 'ssd_norm', 'p_pool', 'p_attn', 'p_ssd', 'w_out', 'norm_ffn', 'ffn_up',
           'ffn_conv_w', 'ffn_conv_b', 'ffn_down', 'norm_final')
PACK_W = 1024


def _params(sem):
    return pltpu.CompilerParams(dimension_semantics=sem, vmem_limit_bytes=VMEM_LIMIT)


def _tile(n, prefs=(512, 256, 128)):
    for t in prefs:
        if n % t == 0:
            return t
    return n


def _sigmoid(x):
    return 0.5 * jnp.tanh(0.5 * x) + 0.5


def _softplus(x):
    return jnp.maximum(x, 0.0) + jnp.log1p(jnp.exp(-jnp.abs(x)))


def _dot(a, b, dims=((1,), (0,))):
    return lax.dot_general(a.astype(MXU_DTYPE), b.astype(MXU_DTYPE), (dims, ((), ())),
                           preferred_element_type=F32)


NT = ((1,), (1,))


def _mm(a, b, *, ta=False, tb=False, acc=None, out_dtype=F32, name):
    M, K = (a.shape[1], a.shape[0]) if ta else a.shape
    N = b.shape[0] if tb else b.shape[1]
    big = (1024, 1408, 512, 256, 128)
    tm, tn = _tile(M, big), _tile(N, big)
    tk = K if K <= 1024 else _tile(K, (512, 256, 128) if ta else big)
    nk = K // tk
    a_spec = pl.BlockSpec((tk, tm), lambda i, j, k: (k, i)) if ta else pl.BlockSpec((tm, tk), lambda i, j, k: (i, k))
    b_spec = pl.BlockSpec((tn, tk), lambda i, j, k: (j, k)) if tb else pl.BlockSpec((tk, tn), lambda i, j, k: (k, j))
    in_specs = [a_spec, b_spec]
    args = [a, b]
    if acc is not None:
        in_specs.append(pl.BlockSpec((tm, tn), lambda i, j, k: (i, j)))
        args.append(acc)

    def body(*refs):
        if acc is not None:
            a_ref, b_ref, c_ref, o_ref, acc_ref = refs
        else:
            a_ref, b_ref, o_ref, acc_ref = refs
        k = pl.program_id(2)

        @pl.when(k == 0)
        def _():
            if acc is not None:
                acc_ref[...] = c_ref[...].astype(F32)
            else:
                acc_ref[...] = jnp.zeros_like(acc_ref)

        av = a_ref[...]
        if ta:
            av = av.astype(F32).T
        acc_ref[...] += _dot(av, b_ref[...], NT if tb else ((1,), (0,)))

        @pl.when(k == nk - 1)
        def _():
            o_ref[...] = acc_ref[...].astype(out_dtype)

    return pl.pallas_call(
        body, grid=(M // tm, N // tn, nk), in_specs=in_specs,
        out_specs=pl.BlockSpec((tm, tn), lambda i, j, k: (i, j)),
        out_shape=jax.ShapeDtypeStruct((M, N), out_dtype),
        scratch_shapes=[pltpu.VMEM((tm, tn), F32)],
        compiler_params=_params(("parallel", "parallel", "arbitrary")), name=name)(*args)


def _rows(body, S, T, *, rows=(), consts=(), prevs=(), nexts=(), out_rows=(), out_accs=(), scratch=(), name):
    n = S // T
    hb = T // HALO
    last_h = S // HALO - 1

    def norm(r):
        return r if isinstance(r, tuple) else (r, r.shape[1], 0)

    rows, prevs, nexts = [norm(r) for r in rows], [norm(r) for r in prevs], [norm(r) for r in nexts]
    in_specs, args = [], []
    for arr, W, cb in rows:
        in_specs.append(pl.BlockSpec((T, W), lambda i, cb=cb: (i, cb)))
        args.append(arr)
    for cst in consts:
        in_specs.append(pl.BlockSpec(cst.shape, lambda i, nd=cst.ndim: (0,) * nd))
        args.append(cst)
    for arr, W, cb in prevs:
        in_specs.append(pl.BlockSpec((HALO, W), lambda i, cb=cb: (jnp.maximum(i * hb - 1, 0), cb)))
        args.append(arr)
    for arr, W, cb in nexts:
        in_specs.append(pl.BlockSpec((HALO, W), lambda i, cb=cb: (jnp.minimum((i + 1) * hb, last_h), cb)))
        args.append(arr)
    out_specs = [pl.BlockSpec((T, W), lambda i: (i, 0)) for W, _ in out_rows]
    out_specs += [pl.BlockSpec(shp, lambda i, nd=len(shp): (0,) * nd) for shp, _ in out_accs]
    out_shape = [jax.ShapeDtypeStruct((S, W), dt) for W, dt in out_rows]
    out_shape += [jax.ShapeDtypeStruct(shp, dt) for shp, dt in out_accs]
    cuts = [len(rows), len(consts), len(prevs), len(nexts), len(out_rows), len(out_accs), len(scratch)]

    def kern(*refs):
        groups, pos = [], 0
        for c in cuts:
            groups.append(list(refs[pos:pos + c]))
            pos += c
        i = pl.program_id(0)

        @pl.when(i == 0)
        def _():
            for a_ref in groups[5]:
                a_ref[...] = jnp.zeros_like(a_ref)

        body(i, n, *groups)

    outs = pl.pallas_call(kern, grid=(n,), in_specs=in_specs, out_specs=out_specs, out_shape=out_shape,
                          scratch_shapes=list(scratch), compiler_params=_params(("arbitrary",)), name=name)(*args)
    return outs


def _fill_prev(ext, prev_ref, cur, i):
    ext[0:HALO, :] = jnp.where(i > 0, prev_ref[...].astype(F32), 0.0)
    ext[HALO:, :] = cur


def _row_ids(i, T, W=1):
    return i * T + lax.broadcasted_iota(jnp.int32, (T, W), 0)


SUB_ROWS = 32
WIN_PAD = 8


def _tile_loop(T, fn):
    n = T // SUB_ROWS
    fn(0, True, n == 1)
    if n > 2:
        def body(rb, carry):
            fn(pl.multiple_of(rb * SUB_ROWS, SUB_ROWS), False, False)
            return carry
        lax.fori_loop(1, n - 1, body, 0)
    if n > 1:
        fn((n - 1) * SUB_ROWS, False, True)


def _win_prev(x_ref, prev_ref, i, r0, first, cols):
    if first:
        top = jnp.where(i > 0, prev_ref[HALO - WIN_PAD:HALO, cols].astype(F32), 0.0)
        return jnp.concatenate([top, x_ref[0:SUB_ROWS, cols].astype(F32)], axis=0)
    start = r0 - WIN_PAD if isinstance(r0, int) else pl.multiple_of(r0 - WIN_PAD, WIN_PAD)
    return x_ref[pl.ds(start, SUB_ROWS + WIN_PAD), cols].astype(F32)


def _behind(win, j):
    return win[WIN_PAD:, :] if j == 0 else pltpu.roll(win, j, axis=0)[WIN_PAD:, :]


def _win_next(x_ref, next_ref, i, n, r0, last, cols):
    if last:
        bot = jnp.where(i < n - 1, next_ref[0:WIN_PAD, cols].astype(F32), 0.0)
        return jnp.concatenate([x_ref[r0:r0 + SUB_ROWS, cols].astype(F32), bot], axis=0)
    return x_ref[pl.ds(r0, SUB_ROWS + WIN_PAD), cols].astype(F32)


def _ahead(win, j):
    return win[:SUB_ROWS, :] if j == 0 else pltpu.roll(win, SUB_ROWS + WIN_PAD - j, axis=0)[:SUB_ROWS, :]


def _taps(win, K):
    return [_behind(win, K - 1 - k) for k in range(K)]


def _conv_win(taps, w_ref, b_ref, cols):
    out = b_ref[:, cols]
    for k, tap in enumerate(taps):
        out = out + tap * w_ref[k:k + 1, cols]
    return out


def _conv_wgrad_win(acc_ref, taps, d, cols):
    for k, tap in enumerate(taps):
        acc_ref[k:k + 1, cols] += jnp.sum(d * tap, axis=0, keepdims=True)


def _norm_fwd(x, w, name):
    S, D = x.shape

    def body(i, n, R, C, P, N, O, A, Sc):
        xv = R[0][...]
        r = lax.rsqrt(jnp.mean(xv * xv, axis=-1, keepdims=True) + NORM_EPS)
        O[0][...] = (xv * r * C[0][...]).astype(MXU_DTYPE)

    return _rows(body, S, _tile(S), rows=[x], consts=[w], out_rows=[(D, MXU_DTYPE)], name=name)[0]


def _norm_bwd_math(xv, w, du):
    r = lax.rsqrt(jnp.mean(xv * xv, axis=-1, keepdims=True) + NORM_EPS)
    xh = xv * r
    g = du * w
    dx = r * (g - xh * jnp.mean(g * xh, axis=-1, keepdims=True))
    dw = jnp.sum(du * xh, axis=0, keepdims=True)
    return dx, dw


def _norm_bwd(x, w, du, dres, name):
    S, D = x.shape

    def body(i, n, R, C, P, N, O, A, Sc):
        dx, dw = _norm_bwd_math(R[0][...], C[0][...], R[1][...])
        O[0][...] = R[2][...] + dx
        A[0][...] += dw

    return _rows(body, S, _tile(S), rows=[x, du, dres], consts=[w], out_rows=[(D, F32)],
                 out_accs=[((1, D), F32)], name=name)


def _loss_head(x, w, target, name):
    S, D = x.shape

    def body(i, n, R, C, P, N, O, A, Sc):
        xv, w_, tg = R[0][...], C[0][...], R[1][...]
        r = lax.rsqrt(jnp.mean(xv * xv, axis=-1, keepdims=True) + NORM_EPS)
        e = xv * r * w_ - tg
        A[1][...] += jnp.broadcast_to(0.5 * jnp.sum(jnp.mean(e * e, axis=-1, keepdims=True)), (1, LANE))
        dx, dw = _norm_bwd_math(xv, w_, e / D)
        O[0][...] = dx
        A[0][...] += dw

    return _rows(body, S, _tile(S), rows=[x, target], consts=[w], out_rows=[(D, F32)],
                 out_accs=[((1, D), F32), ((1, LANE), F32)], name=name)


def _pool_fwd(pqkv, mix, scale, name):
    S = pqkv.shape[0]
    T = _tile(S, (256, 128))

    def body(i, n, R, C, P, N, O, A, Sc):
        ext = Sc[0]
        v = R[0][...]
        _fill_prev(ext, P[0], v, i)
        t1 = (_row_ids(i, T) + 1).astype(F32)
        for g, w in enumerate(POOL_WINDOWS):
            cols = slice(g * LANE, (g + 1) * LANE)
            acc = v[:, cols]
            for j in range(1, w):
                acc = acc + ext[pl.ds(HALO - j, T), cols]
            d = (acc / jnp.minimum(t1, float(w)) - v[:, cols]).astype(MXU_DTYPE)
            O[0][:, cols] = d
            O[1][:, cols] = (_dot(d, C[0][g]) * C[1][:, cols]).astype(MXU_DTYPE)

    return _rows(body, S, T, rows=[(pqkv, POOL_W, 0)], prevs=[(pqkv, POOL_W, 0)], consts=[mix, scale],
                 out_rows=[(POOL_W, MXU_DTYPE), (POOL_W, MXU_DTYPE)],
                 scratch=[pltpu.VMEM((HALO + T, POOL_W), F32)], name=name)


def _pool_bwd_a(dypm, d, mix, scale, name):
    S = d.shape[0]
    T = _tile(S, (256, 128))

    def body(i, n, R, C, P, N, O, A, Sc):
        for g in range(len(POOL_WINDOWS)):
            cols = slice(g * LANE, (g + 1) * LANE)
            dg = R[1][:, cols]
            dy = R[0][:, cols]
            yg = _dot(dg, C[0][g])
            A[0][:, cols] += jnp.sum(dy * yg, axis=0, keepdims=True)
            dys = dy * C[1][:, cols]
            A[1][cols, :] += _dot(dg.astype(F32).T, dys)
            O[0][:, cols] = _dot(dys, C[0][g], NT)

    return _rows(body, S, T, rows=[dypm, d], consts=[mix, scale], out_rows=[(POOL_W, F32)],
                 out_accs=[((1, POOL_W), F32), ((POOL_W, LANE), F32)], name=name)


def _pool_bwd_b(dd, name):
    S = dd.shape[0]
    T = _tile(S, (256, 128))

    def body(i, n, R, C, P, N, O, A, Sc):
        ext = Sc[0]
        ddv = R[0][...]
        t1 = (_row_ids(i, T) + 1).astype(F32)
        nxt = jnp.where(i < n - 1, N[0][...], 0.0)
        for g, w in enumerate(POOL_WINDOWS):
            cols = slice(g * LANE, (g + 1) * LANE)
            ext[0:T, cols] = ddv[:, cols] / jnp.minimum(t1, float(w))
            ext[T:, cols] = nxt[:, cols] / float(w)
        for g, w in enumerate(POOL_WINDOWS):
            cols = slice(g * LANE, (g + 1) * LANE)
            acc = ext[0:T, cols]
            for j in range(1, w):
                acc = acc + ext[pl.ds(j, T), cols]
            O[0][:, cols] = (acc - ddv[:, cols]).astype(MXU_DTYPE)

    return _rows(body, S, T, rows=[dd], nexts=[dd], out_rows=[(POOL_W, MXU_DTYPE)],
                 scratch=[pltpu.VMEM((T + HALO, POOL_W), F32)], name=name)[0]


def _lane_cumsum(seg, reverse=False):
    lane = lax.broadcasted_iota(jnp.int32, seg.shape, 1)
    sh = 1
    while sh < LANE:
        if reverse:
            seg = seg + jnp.where(lane < LANE - sh, pltpu.roll(seg, LANE - sh, axis=1), 0.0)
        else:
            seg = seg + jnp.where(lane >= sh, pltpu.roll(seg, sh, axis=1), 0.0)
        sh *= 2
    return seg


def _logf_cumsum(fT, bias, name):
    H, S = fT.shape
    TB = _tile(S)
    nb = S // TB

    def body(f_ref, b_ref, o_ref, carry):
        @pl.when(pl.program_id(0) == 0)
        def _():
            carry[...] = jnp.zeros_like(carry)

        x = f_ref[...] + b_ref[...]
        lf = jnp.minimum(x, 0.0) - jnp.log1p(jnp.exp(-jnp.abs(x)))
        c = carry[...]
        for j in range(TB // LANE):
            seg = _lane_cumsum(lf[:, j * LANE:(j + 1) * LANE]) + c
            o_ref[:, j * LANE:(j + 1) * LANE] = seg
            c = seg[:, LANE - 1:LANE]
        carry[...] = c

    return pl.pallas_call(
        body, grid=(nb,), in_specs=[pl.BlockSpec((H, TB), lambda i: (0, i)), pl.BlockSpec((H, 1), lambda i: (0, 0))],
        out_specs=pl.BlockSpec((H, TB), lambda i: (0, i)), out_shape=jax.ShapeDtypeStruct((H, S), F32),
        scratch_shapes=[pltpu.VMEM((H, 1), F32)], compiler_params=_params(("arbitrary",)), name=name)(fT, bias)


def _logf_cumsum_bwd(fT, bias, dc, dcq, name):
    H, S = fT.shape
    TB = _tile(S)
    nb = S // TB

    def body(f_ref, b_ref, dc_ref, dcq_ref, o_ref, db_ref, carry):
        @pl.when(pl.program_id(0) == 0)
        def _():
            carry[...] = jnp.zeros_like(carry)
            db_ref[...] = jnp.zeros_like(db_ref)

        x = f_ref[...] + b_ref[...]
        sg = _sigmoid(-x)
        dcv = dc_ref[...] + dcq_ref[...]
        c = carry[...]
        db = jnp.zeros((H, 1), F32)
        for j in reversed(range(TB // LANE)):
            seg = _lane_cumsum(dcv[:, j * LANE:(j + 1) * LANE], reverse=True) + c
            df = seg * sg[:, j * LANE:(j + 1) * LANE]
            o_ref[:, j * LANE:(j + 1) * LANE] = df
            db = db + jnp.sum(df, axis=1, keepdims=True)
            c = seg[:, 0:1]
        carry[...] = c
        db_ref[...] += db

    rev = lambda i: (0, nb - 1 - i)
    return pl.pallas_call(
        body, grid=(nb,),
        in_specs=[pl.BlockSpec((H, TB), rev), pl.BlockSpec((H, 1), lambda i: (0, 0)), pl.BlockSpec((H, TB), rev),
                  pl.BlockSpec((H, TB), rev)],
        out_specs=[pl.BlockSpec((H, TB), rev), pl.BlockSpec((H, 1), lambda i: (0, 0))],
        out_shape=[jax.ShapeDtypeStruct((H, S), F32), jax.ShapeDtypeStruct((H, 1), F32)],
        scratch_shapes=[pltpu.VMEM((H, 1), F32)], compiler_params=_params(("arbitrary",)), name=name)(
            fT, bias, dc, dcq)


def _attn_scores(q, k, ck, diagonal, T):
    s = _dot(q, k, NT) - ck
    if diagonal:
        tril = lax.broadcasted_iota(jnp.int32, (T, T), 1) <= lax.broadcasted_iota(jnp.int32, (T, T), 0)
        s = jnp.where(tril, s, NEG_BIG)
    return s


def _attn_fwd(pqkv, c_row, name):
    S = pqkv.shape[0]
    T = _tile(S, (1024, 512, 256, 128))
    nq = S // T
    qb, kb, vb = (ATTN_W * j // LANE for j in (1, 2, 3))

    steps = jnp.asarray([[qi for qi in range(nq) for ki in range(qi + 1)],
                         [ki for qi in range(nq) for ki in range(qi + 1)]], jnp.int32)

    def body(st_ref, q_ref, k_ref, v_ref, ck_ref, o_ref, lse_ref, m_s, l_s, acc_s):
        qi, ki = st_ref[0, pl.program_id(1)], st_ref[1, pl.program_id(1)]

        @pl.when(ki == 0)
        def _():
            m_s[...] = jnp.full_like(m_s, NEG_BIG)
            l_s[...] = jnp.zeros_like(l_s)
            acc_s[...] = jnp.zeros_like(acc_s)

        def step(diagonal):
            qf, kv, vv = q_ref[...] * ATTN_SCALE, k_ref[...].astype(MXU_DTYPE), v_ref[...].astype(MXU_DTYPE)
            acc = acc_s[...]
            for a in range(2):
                mk = _half_mask(a)
                s = _attn_scores(qf * mk, kv, ck_ref[a], diagonal, T)
                m_new = jnp.maximum(m_s[a], jnp.max(s, axis=1, keepdims=True))
                alpha = jnp.exp(m_s[a] - m_new)
                p = jnp.exp(s - m_new)
                l_s[a] = alpha * l_s[a] + jnp.sum(p, axis=1, keepdims=True)
                acc = acc * (1.0 + mk * (alpha - 1.0)) + _dot(p, vv) * mk
                m_s[a] = m_new
            acc_s[...] = acc

        @pl.when(ki < qi)
        def _():
            step(False)

        @pl.when(ki == qi)
        def _():
            step(True)
            o_ref[...] = acc_s[...] / (l_s[0] * _half_mask(0) + l_s[1] * _half_mask(1))
            for a in range(2):
                lse_ref[a] = m_s[a] + jnp.log(l_s[a])

    return pl.pallas_call(
        body, grid_spec=pltpu.PrefetchScalarGridSpec(
            num_scalar_prefetch=1, grid=(HEADS // 2, steps.shape[1]),
            in_specs=[pl.BlockSpec((T, LANE), lambda hp, s, st: (st[0, s], qb + hp)),
                      pl.BlockSpec((T, LANE), lambda hp, s, st: (st[1, s], kb + hp)),
                      pl.BlockSpec((T, LANE), lambda hp, s, st: (st[1, s], vb + hp)),
                      pl.BlockSpec((2, 1, T), lambda hp, s, st: (hp, 0, st[1, s]))],
            out_specs=[pl.BlockSpec((T, LANE), lambda hp, s, st: (st[0, s], hp)),
                       pl.BlockSpec((2, T, 1), lambda hp, s, st: (hp, st[0, s], 0))],
            scratch_shapes=[pltpu.VMEM((2, T, 1), F32), pltpu.VMEM((2, T, 1), F32), pltpu.VMEM((T, LANE), F32)]),
        out_shape=[jax.ShapeDtypeStruct((S, ATTN_W), F32), jax.ShapeDtypeStruct((HEADS, S, 1), F32)],
        compiler_params=_params(("arbitrary", "arbitrary")), name=name)(steps, pqkv, pqkv, pqkv, c_row)


def _attn_bwd(pqkv, do, o, c_row, lse, name):
    S = pqkv.shape[0]
    T = _tile(S, (1024, 512, 256, 128))
    nq = S // T
    qb, kb, vb = (ATTN_W * j // LANE for j in (1, 2, 3))

    steps = jnp.asarray([[ki for ki in range(nq) for qi in range(ki, nq)],
                         [qi for ki in range(nq) for qi in range(ki, nq)]], jnp.int32)

    def body(st_ref, q_ref, k_ref, v_ref, do_ref, o_ref, ck_ref, lse_ref, dq_ref, dk_ref, dv_ref, dc_ref, dcq_ref,
             dk_s, dv_s, dc_s):
        ki, qi = st_ref[0, pl.program_id(1)], st_ref[1, pl.program_id(1)]

        @pl.when(pl.program_id(1) == 0)
        def _():
            dq_ref[...] = jnp.zeros_like(dq_ref)
            dcq_ref[...] = jnp.zeros_like(dcq_ref)

        @pl.when(qi == ki)
        def _():
            dk_s[...] = jnp.zeros_like(dk_s)
            dv_s[...] = jnp.zeros_like(dv_s)
            dc_s[...] = jnp.zeros_like(dc_s)

        def step(diagonal):
            qf, kf, vv = q_ref[...] * ATTN_SCALE, k_ref[...], v_ref[...].astype(MXU_DTYPE)
            dof, ov = do_ref[...], o_ref[...]
            rows = pl.ds(pl.multiple_of(qi * T, T), T)
            lane = lax.broadcasted_iota(jnp.int32, (1, LANE), 1)
            for a in range(2):
                mk = _half_mask(a)
                qa, ka, doa = (qf * mk).astype(MXU_DTYPE), (kf * mk).astype(MXU_DTYPE), dof * mk
                s = _attn_scores(qa, ka, ck_ref[a], diagonal, T)
                p = jnp.exp(s - lse_ref[a])
                delta = jnp.sum(doa * ov, axis=1, keepdims=True)
                dv_s[...] += _dot(p.T, doa)
                dp = _dot(doa, vv, NT)
                ds = p * (dp - delta)
                dc_s[a] -= jnp.sum(ds, axis=0, keepdims=True)
                dq_ref[rows, :] += _dot(ds, ka)
                dcq_ref[rows, :] += jnp.sum(ds, axis=1, keepdims=True) * (lane == a).astype(F32)
                dk_s[...] += _dot(ds.T, qa)

        @pl.when(qi > ki)
        def _():
            step(False)

        @pl.when(qi == ki)
        def _():
            step(True)

        @pl.when(qi == nq - 1)
        def _():
            dk_ref[...] = dk_s[...]
            dv_ref[...] = dv_s[...]
            dc_ref[...] = dc_s[...]

    qrow = lambda hp, s, st: (st[1, s], hp)
    krow = lambda hp, s, st: (st[0, s], hp)
    whole = lambda hp, s, st: (0, hp)
    return pl.pallas_call(
        body, grid_spec=pltpu.PrefetchScalarGridSpec(
            num_scalar_prefetch=1, grid=(HEADS // 2, steps.shape[1]),
            in_specs=[pl.BlockSpec((T, LANE), lambda hp, s, st: (st[1, s], qb + hp)),
                      pl.BlockSpec((T, LANE), lambda hp, s, st: (st[0, s], kb + hp)),
                      pl.BlockSpec((T, LANE), lambda hp, s, st: (st[0, s], vb + hp)),
                      pl.BlockSpec((T, LANE), qrow), pl.BlockSpec((T, LANE), qrow),
                      pl.BlockSpec((2, 1, T), lambda hp, s, st: (hp, 0, st[0, s])),
                      pl.BlockSpec((2, T, 1), lambda hp, s, st: (hp, st[1, s], 0))],
            out_specs=[pl.BlockSpec((S, LANE), whole), pl.BlockSpec((T, LANE), krow), pl.BlockSpec((T, LANE), krow),
                       pl.BlockSpec((2, 1, T), lambda hp, s, st: (hp, 0, st[0, s])), pl.BlockSpec((S, LANE), whole)],
            scratch_shapes=[pltpu.VMEM((T, LANE), F32), pltpu.VMEM((T, LANE), F32), pltpu.VMEM((2, 1, T), F32)]),
        out_shape=[jax.ShapeDtypeStruct((S, ATTN_W), F32), jax.ShapeDtypeStruct((S, ATTN_W), F32),
                   jax.ShapeDtypeStruct((S, ATTN_W), F32), jax.ShapeDtypeStruct((HEADS, 1, S), F32),
                   jax.ShapeDtypeStruct((S, ATTN_W), F32)],
        compiler_params=_params(("arbitrary", "arbitrary")), name=name)(
            steps, pqkv, pqkv, pqkv, do, o, c_row, lse)


CONV_COLS = 512


def _conv_bwd_b(dpre, w, name):
    S, C = dpre.shape
    K = w.shape[0]
    T = _tile(S, (256, 128))

    def body(i, n, R, Cs, P, N, O, A, Sc):
        def tile(r0, first, last):
            for c0 in range(0, C, CONV_COLS):
                cols = slice(c0, c0 + CONV_COLS)
                win = _win_next(R[0], N[0], i, n, r0, last, cols)
                out = None
                for k in range(K):
                    term = _ahead(win, K - 1 - k) * Cs[0][k:k + 1, cols]
                    out = term if out is None else out + term
                O[0][pl.ds(r0, SUB_ROWS), cols] = out.astype(MXU_DTYPE)

        _tile_loop(T, tile)

    return _rows(body, S, T, rows=[dpre], nexts=[dpre], consts=[w], out_rows=[(C, MXU_DTYPE)], name=name)[0]


def _dt_mask():
    lane = lax.broadcasted_iota(jnp.int32, (1, LANE), 1)
    return ((lane >= DT_LANE0) & (lane < DT_LANE0 + SSD_HEADS)).astype(F32)


def _ssd_pre_fwd(xbc, fdt, cw, cb, dtb, name):
    S, C = xbc.shape
    T = _tile(S, (256, 128))
    K = cw.shape[0]

    def body(i, n, R, Cs, P, N, O, A, Sc):
        def tile(r0, first, last):
            rows = pl.ds(r0, SUB_ROWS)
            for c0 in range(0, C, CONV_COLS):
                cols = slice(c0, c0 + CONV_COLS)
                pre = _conv_win(_taps(_win_prev(R[0], P[0], i, r0, first, cols), K), Cs[0], Cs[1], cols)
                O[0][rows, cols] = pre * _sigmoid(pre)
            O[1][rows, :] = _softplus(R[1][rows, :] + Cs[2][...]) * _dt_mask()

        _tile_loop(T, tile)

    return _rows(body, S, T, rows=[xbc, fdt], prevs=[xbc], consts=[cw, cb, dtb],
                 out_rows=[(C, F32), (LANE, F32)], name=name)


def _silu_grad(pre):
    sg = _sigmoid(pre)
    return sg * (1.0 + pre * (1.0 - sg))


def _ssd_pre_bwd_a(xbc, fdt, dxa, ddtw, cw, cb, dtb, name):
    S, C = xbc.shape
    T = _tile(S, (256, 128))
    K = cw.shape[0]

    def body(i, n, R, Cs, P, N, O, A, Sc):
        def tile(r0, first, last):
            rows = pl.ds(r0, SUB_ROWS)
            for c0 in range(0, C, CONV_COLS):
                cols = slice(c0, c0 + CONV_COLS)
                win = _taps(_win_prev(R[0], P[0], i, r0, first, cols), K)
                dpre = R[2][rows, cols] * _silu_grad(_conv_win(win, Cs[0], Cs[1], cols))
                O[0][rows, cols] = dpre
                _conv_wgrad_win(A[0], win, dpre, cols)
                A[1][:, cols] += jnp.sum(dpre, axis=0, keepdims=True)
            ddt = R[3][rows, :] * _sigmoid(R[1][rows, :] + Cs[2][...]) * _dt_mask()
            O[1][rows, :] = ddt
            A[2][...] += jnp.sum(ddt, axis=0, keepdims=True)

        _tile_loop(T, tile)

    return _rows(body, S, T, rows=[xbc, fdt, dxa, ddtw], prevs=[xbc], consts=[cw, cb, dtb],
                 out_rows=[(C, F32), (LANE, F32)],
                 out_accs=[((K, C), F32), ((1, C), F32), ((1, LANE), F32)], name=name)


def _split3(x):
    hi = x.astype(jnp.bfloat16)
    r1 = x - hi.astype(F32)
    mid = r1.astype(jnp.bfloat16)
    lo = (r1 - mid.astype(F32)).astype(jnp.bfloat16)
    return hi, mid, lo


def _expand_mat():
    r = lax.broadcasted_iota(jnp.int32, (LANE, SSD_W), 0)
    c = lax.broadcasted_iota(jnp.int32, (LANE, SSD_W), 1)
    return (r - DT_LANE0 == c // SSD_P).astype(jnp.bfloat16)


def _headsum_mat():
    r = lax.broadcasted_iota(jnp.int32, (SSD_W, LANE), 0)
    c = lax.broadcasted_iota(jnp.int32, (SSD_W, LANE), 1)
    return (c - DT_LANE0 == r // SSD_P).astype(jnp.bfloat16)


def _expand(tile, ex):
    return sum(lax.dot_general(part, ex, (((1,), (0,)), ((), ())), preferred_element_type=F32)
               for part in _split3(tile))


def _headsum(full, hs):
    return sum(lax.dot_general(part, hs, (((1,), (0,)), ((), ())), preferred_element_type=F32)
               for part in _split3(full))


def _sub_cumsum(a, reverse=False):
    n = a.shape[0]
    row = lax.broadcasted_iota(jnp.int32, a.shape, 0)
    sh = 1
    while sh < n:
        if reverse:
            a = a + jnp.where(row < n - sh, pltpu.roll(a, n - sh, axis=0), 0.0)
        else:
            a = a + jnp.where(row >= sh, pltpu.roll(a, sh, axis=0), 0.0)
        sh *= 2
    return a


def _chunk_common(xa_ref, dtw_ref, a_row, ex):
    L = SSD_CHUNK
    xs = xa_ref[:, 0:SSD_W]
    dtv = dtw_ref[...]
    acs = _sub_cumsum(dtv * a_row)
    last = acs[L - 1:L, :]
    full = _expand(jnp.concatenate([dtv, jnp.exp(last - acs), jnp.exp(acs),
                                    jnp.broadcast_to(jnp.exp(last), (8, LANE))], axis=0), ex)
    dt_full, dec_full, e_full, elast_full = full[0:L], full[L:2 * L], full[2 * L:3 * L], full[3 * L:3 * L + 1]
    xd = xs * dt_full
    return xs, dtv, acs, last, dt_full, xd, dec_full, e_full, elast_full


def _decay_mask(acs, acsT, col):
    L = SSD_CHUNK
    diff = acs[:, col:col + 1] - acsT[col:col + 1, :]
    tril = lax.broadcasted_iota(jnp.int32, (L, L), 0) >= lax.broadcasted_iota(jnp.int32, (L, L), 1)
    return jnp.where(tril, jnp.exp(jnp.minimum(diff, 0.0)), 0.0)


def _half_mask(h):
    lane = lax.broadcasted_iota(jnp.int32, (1, LANE), 1)
    return ((lane // SSD_P) == (h % 2)).astype(F32)


def _ssd_chunk_fwd(xa, dtw, a_row, d_full, name):
    S = xa.shape[0]
    L, G = SSD_CHUNK, 2
    nc = S // L
    GW = SSD_W // G

    def body(xa_ref, dtw_ref, a_ref, d_ref, y_ref, hp_ref, state):
        @pl.when(pl.program_id(0) == 0)
        def _():
            state[...] = jnp.zeros_like(state)

        ex = _expand_mat()
        xs, dtv, acs, last, dt_full, xd, dec_full, e_full, elast_full = _chunk_common(xa_ref, dtw_ref, a_ref[...], ex)
        acsT = acs.T
        hp_ref[0] = state[...]
        for g in range(G):
            gc = slice(g * GW, (g + 1) * GW)
            Bg = xa_ref[:, SSD_W + g * SSD_N: SSD_W + (g + 1) * SSD_N]
            Cg = xa_ref[:, SSD_W + G * SSD_N + g * SSD_N: SSD_W + G * SSD_N + (g + 1) * SSD_N]
            cb = _dot(Cg, Bg, NT)
            y_off = e_full[:, gc] * _dot(Cg, state[:, gc])
            for hp in range(GW // LANE):
                pc = slice(g * GW + hp * LANE, g * GW + (hp + 1) * LANE)
                xd_pair = xd[:, pc]
                yp = y_off[:, hp * LANE:(hp + 1) * LANE] + d_ref[:, pc] * xs[:, pc]
                for h2 in range(2):
                    h = (g * GW + hp * LANE) // SSD_P + h2
                    m = cb * _decay_mask(acs, acsT, DT_LANE0 + h)
                    yp = yp + _dot(m, xd_pair * _half_mask(h))
                y_ref[:, pc] = yp
            st_new = _dot(Bg.T, xd[:, gc] * dec_full[:, gc])
            state[:, gc] = elast_full[:, gc] * state[:, gc] + st_new

    return pl.pallas_call(
        body, grid=(nc,),
        in_specs=[pl.BlockSpec((L, SSD_CONV_CH), lambda c: (c, 0)), pl.BlockSpec((L, LANE), lambda c: (c, 0)),
                  pl.BlockSpec((1, LANE), lambda c: (0, 0)), pl.BlockSpec((1, SSD_W), lambda c: (0, 0))],
        out_specs=[pl.BlockSpec((L, SSD_W), lambda c: (c, 0)), pl.BlockSpec((1, SSD_N, SSD_W), lambda c: (c, 0, 0))],
        out_shape=[jax.ShapeDtypeStruct((S, SSD_W), F32), jax.ShapeDtypeStruct((nc, SSD_N, SSD_W), F32)],
        scratch_shapes=[pltpu.VMEM((SSD_N, SSD_W), F32)],
        compiler_params=_params(("arbitrary",)), name=name)(xa, dtw, a_row, d_full)


def _ssd_chunk_bwd(xa, dtw, dy, hprev, a_row, d_full, name):
    S = xa.shape[0]
    L, G = SSD_CHUNK, 2
    nc = S // L
    GW = SSD_W // G

    def body(xa_ref, dtw_ref, dy_ref, hp_ref, a_ref, d_ref, dxa_ref, ddt_ref, da_ref, dd_ref, dstate):
        @pl.when(pl.program_id(0) == 0)
        def _():
            dstate[...] = jnp.zeros_like(dstate)
            da_ref[...] = jnp.zeros_like(da_ref)
            dd_ref[...] = jnp.zeros_like(dd_ref)

        ex, hs = _expand_mat(), _headsum_mat()
        a_row = a_ref[...]
        xs, dtv, acs, last, dt_full, xd, dec_full, e_full, elast_full = _chunk_common(xa_ref, dtw_ref, a_row, ex)
        acsT = acs.T
        dyv = dy_ref[...]
        lane = lax.broadcasted_iota(jnp.int32, (L, LANE), 1)
        sub = lax.broadcasted_iota(jnp.int32, (LANE, L), 0)
        dacs_c = jnp.zeros((L, LANE), F32)
        dacs_r = jnp.zeros((LANE, L), F32)
        dxd_parts, yoff_parts, dxdd_parts, hh_parts = [], [], [], []
        for g in range(G):
            gc = slice(g * GW, (g + 1) * GW)
            b0 = SSD_W + g * SSD_N
            c0 = SSD_W + G * SSD_N + g * SSD_N
            Bg = xa_ref[:, b0:b0 + SSD_N]
            Cg = xa_ref[:, c0:c0 + SSD_N]
            Hp = hp_ref[0, :, gc]
            dH = dstate[:, gc]
            cb = _dot(Cg, Bg, NT)
            Gm = _dot(Cg, Hp)
            yoff_parts.append(e_full[:, gc] * Gm)
            dG = e_full[:, gc] * dyv[:, gc]
            dC = _dot(dG, Hp, NT)
            dHp = _dot(Cg.T, dG)
            xdd = xd[:, gc] * dec_full[:, gc]
            dB = _dot(xdd, dH, NT)
            dxdd = _dot(Bg, dH)
            dxdd_parts.append(dxdd)
            hh_parts.append(dH * Hp)
            dstate[:, gc] = dHp + elast_full[:, gc] * dH
            dcb = jnp.zeros((L, L), F32)
            dxd_g = []
            for hp in range(GW // LANE):
                pc = slice(g * GW + hp * LANE, g * GW + (hp + 1) * LANE)
                xd_pair = xd[:, pc]
                dxd_pair = dxdd[:, hp * LANE:(hp + 1) * LANE] * dec_full[:, pc]
                for h2 in range(2):
                    h = (g * GW + hp * LANE) // SSD_P + h2
                    col = DT_LANE0 + h
                    lm = _decay_mask(acs, acsT, col)
                    m = cb * lm
                    dy_h = dyv[:, pc] * _half_mask(h)
                    dm = _dot(dy_h, xd_pair, NT)
                    dxd_pair = dxd_pair + _dot(m.T, dy_h)
                    wm = dm * m
                    dacs_c = dacs_c + jnp.where(lane == col, jnp.sum(wm, axis=1, keepdims=True), 0.0)
                    dacs_r = dacs_r - jnp.where(sub == col, jnp.sum(wm, axis=0, keepdims=True), 0.0)
                    dcb = dcb + dm * lm
                dxd_g.append(dxd_pair)
            dxd_parts.append(jnp.concatenate(dxd_g, axis=1))
            dxa_ref[:, c0:c0 + SSD_N] = dC + _dot(dcb, Bg)
            dxa_ref[:, b0:b0 + SSD_N] = dB + _dot(dcb.T, Cg)
        dxd = jnp.concatenate(dxd_parts, axis=1)
        y_off = jnp.concatenate(yoff_parts, axis=1)
        dxdd_full = jnp.concatenate(dxdd_parts, axis=1)
        hh = jnp.concatenate(hh_parts, axis=1)
        dxa_ref[:, 0:SSD_W] = d_ref[...] * dyv + dxd * dt_full
        sums = _headsum(jnp.concatenate([dyv * xs, dxd * xs, dxdd_full * xd, hh, dyv * y_off], axis=0), hs)
        hs_skip, ddt, hs_dec, hs_state, hs_off = (sums[j * L:(j + 1) * L] for j in range(5))
        dd_ref[...] += jnp.sum(hs_skip, axis=0, keepdims=True)
        w_dec = hs_dec * jnp.exp(last - acs)
        dlast = jnp.sum(w_dec, axis=0, keepdims=True) + jnp.exp(last) * jnp.sum(hs_state, axis=0, keepdims=True)
        dacs = dacs_c + dacs_r.T + hs_off - w_dec
        rowid = lax.broadcasted_iota(jnp.int32, (L, LANE), 0)
        dacs = dacs + jnp.where(rowid == L - 1, dlast, 0.0)
        da = _sub_cumsum(dacs, reverse=True)
        ddt_ref[...] = ddt + da * a_row
        da_ref[...] += jnp.sum(da * dtv, axis=0, keepdims=True)

    rev = lambda c: (nc - 1 - c, 0)
    return pl.pallas_call(
        body, grid=(nc,),
        in_specs=[pl.BlockSpec((L, SSD_CONV_CH), rev), pl.BlockSpec((L, LANE), rev), pl.BlockSpec((L, SSD_W), rev),
                  pl.BlockSpec((1, SSD_N, SSD_W), lambda c: (nc - 1 - c, 0, 0)),
                  pl.BlockSpec((1, LANE), lambda c: (0, 0)), pl.BlockSpec((1, SSD_W), lambda c: (0, 0))],
        out_specs=[pl.BlockSpec((L, SSD_CONV_CH), rev), pl.BlockSpec((L, LANE), rev),
                   pl.BlockSpec((1, LANE), lambda c: (0, 0)), pl.BlockSpec((1, LANE), lambda c: (0, 0))],
        out_shape=[jax.ShapeDtypeStruct((S, SSD_CONV_CH), F32), jax.ShapeDtypeStruct((S, LANE), F32),
                   jax.ShapeDtypeStruct((1, LANE), F32), jax.ShapeDtypeStruct((1, LANE), F32)],
        scratch_shapes=[pltpu.VMEM((SSD_N, SSD_W), F32)],
        compiler_params=_params(("arbitrary",)), name=name)(xa, dtw, dy, hprev, a_row, d_full)


def _ssd_post_fwd(y, z, w, name):
    S = y.shape[0]
    GW = SSD_W // 2

    def body(i, n, R, C, P, N, O, A, Sc):
        zv = R[1][...]
        v = R[0][...] * (zv * _sigmoid(zv))
        for g in range(2):
            gc = slice(g * GW, (g + 1) * GW)
            vg = v[:, gc]
            r = lax.rsqrt(jnp.mean(vg * vg, axis=-1, keepdims=True) + NORM_EPS)
            O[0][:, gc] = (vg * r * C[0][:, gc]).astype(MXU_DTYPE)

    return _rows(body, S, _tile(S, (256, 128)), rows=[y, z], consts=[w], out_rows=[(SSD_W, MXU_DTYPE)], name=name)[0]


def _ssd_post_bwd(y, z, dyn, w, name):
    S = y.shape[0]
    GW = SSD_W // 2

    def body(i, n, R, C, P, N, O, A, Sc):
        yv, zv, dn = R[0][...], R[1][...], R[2][...]
        sz = zv * _sigmoid(zv)
        v = yv * sz
        for g in range(2):
            gc = slice(g * GW, (g + 1) * GW)
            dv, dw = _norm_bwd_math(v[:, gc], C[0][:, gc], dn[:, gc])
            A[0][:, gc] += dw
            O[0][:, gc] = dv * sz[:, gc]
            O[1][:, gc] = (dv * yv[:, gc] * _silu_grad(zv[:, gc])).astype(MXU_DTYPE)

    return _rows(body, S, _tile(S, (256, 128)), rows=[y, z, dyn], consts=[w],
                 out_rows=[(SSD_W, F32), (SSD_W, MXU_DTYPE)], out_accs=[((1, SSD_W), F32)], name=name)


def _merge_fwd(gl, yp, ya, ys, name):
    S, D = yp.shape

    def body(i, n, R, C, P, N, O, A, Sc):
        def tile(r0, first, last):
            rows = pl.ds(r0, SUB_ROWS)
            for c0 in range(0, D, CONV_COLS):
                cols = slice(c0, c0 + CONV_COLS)
                acc = None
                for b in range(3):
                    term = _sigmoid(R[0][rows, b * D + c0:b * D + c0 + CONV_COLS]) * R[1 + b][rows, cols]
                    acc = term if acc is None else acc + term
                O[0][rows, cols] = acc.astype(MXU_DTYPE)

        _tile_loop(T, tile)

    T = _tile(S, (256, 128))
    return _rows(body, S, T, rows=[gl, yp, ya, ys], out_rows=[(D, MXU_DTYPE)], name=name)[0]


def _merge_bwd(gl, yp, ya, ys, dm, name):
    S, D = yp.shape
    T = _tile(S, (256, 128))

    def body(i, n, R, C, P, N, O, A, Sc):
        def tile(r0, first, last):
            rows = pl.ds(r0, SUB_ROWS)
            for c0 in range(0, D, CONV_COLS):
                cols = slice(c0, c0 + CONV_COLS)
                dmv = R[4][rows, cols]
                for b in range(3):
                    gcols = slice(b * D + c0, b * D + c0 + CONV_COLS)
                    gt = _sigmoid(R[0][rows, gcols])
                    O[b][rows, cols] = (gt * dmv).astype(MXU_DTYPE)
                    O[3][rows, gcols] = (dmv * R[1 + b][rows, cols] * gt * (1.0 - gt)).astype(MXU_DTYPE)

        _tile_loop(T, tile)

    return _rows(body, S, T, rows=[gl, yp, ya, ys, dm],
                 out_rows=[(D, MXU_DTYPE)] * 3 + [(3 * D, MXU_DTYPE)], name=name)


FFN_COLS = 256


def _ffn_act_fwd(hpre, cw, cb, name):
    S, C = hpre.shape
    K = cw.shape[0]
    T = _tile(S, (256, 128))
    Fd = C // 2

    def body(i, n, R, Cs, P, N, O, A, Sc):
        def tile(r0, first, last):
            for c0 in range(0, Fd, FFN_COLS):
                gcols, vcols = slice(c0, c0 + FFN_COLS), slice(Fd + c0, Fd + c0 + FFN_COLS)
                gt = _conv_win(_taps(_win_prev(R[0], P[0], i, r0, first, gcols), K), Cs[0], Cs[1], gcols)
                val = _conv_win(_taps(_win_prev(R[0], P[0], i, r0, first, vcols), K), Cs[0], Cs[1], vcols)
                O[0][pl.ds(r0, SUB_ROWS), gcols] = (gt * _sigmoid(gt) * val).astype(MXU_DTYPE)

        _tile_loop(T, tile)

    return _rows(body, S, T, rows=[hpre], prevs=[hpre], consts=[cw, cb], out_rows=[(Fd, MXU_DTYPE)], name=name)[0]


def _ffn_act_bwd_a(hpre, dact, cw, cb, name):
    S, C = hpre.shape
    K = cw.shape[0]
    T = _tile(S, (256, 128))
    Fd = C // 2

    def body(i, n, R, Cs, P, N, O, A, Sc):
        def tile(r0, first, last):
            rows = pl.ds(r0, SUB_ROWS)
            for c0 in range(0, Fd, FFN_COLS):
                gcols, vcols = slice(c0, c0 + FFN_COLS), slice(Fd + c0, Fd + c0 + FFN_COLS)
                gwin = _taps(_win_prev(R[0], P[0], i, r0, first, gcols), K)
                vwin = _taps(_win_prev(R[0], P[0], i, r0, first, vcols), K)
                gt = _conv_win(gwin, Cs[0], Cs[1], gcols)
                val = _conv_win(vwin, Cs[0], Cs[1], vcols)
                da = R[1][rows, gcols]
                for cols, win, d in ((gcols, gwin, da * val * _silu_grad(gt)), (vcols, vwin, da * gt * _sigmoid(gt))):
                    O[0][rows, cols] = d
                    _conv_wgrad_win(A[0], win, d, cols)
                    A[1][:, cols] += jnp.sum(d, axis=0, keepdims=True)

        _tile_loop(T, tile)

    return _rows(body, S, T, rows=[hpre, dact], prevs=[hpre], consts=[cw, cb], out_rows=[(C, F32)],
                 out_accs=[((K, C), F32), ((1, C), F32)], name=name)


def _adamw_math(g, w, m, v):
    c1 = 1.0 - ADAM_B1 ** ADAM_STEP
    c2 = 1.0 - ADAM_B2 ** ADAM_STEP
    mn = ADAM_B1 * m + (1.0 - ADAM_B1) * g
    vn = ADAM_B2 * v + (1.0 - ADAM_B2) * (g * g)
    return -ADAM_LR * ((mn / c1) / (jnp.sqrt(vn / c2) + ADAM_EPS) + ADAM_WD * w), mn, vn


def _adamw(g, w, m, v, name):
    R_, W = g.shape

    def body(i, n, R, C, P, N, O, A, Sc):
        O[0][...], O[1][...], O[2][...] = _adamw_math(R[0][...], R[1][...], R[2][...], R[3][...])

    return _rows(body, R_, _row_tile(R_, W, 7), rows=[g, w, m, v], out_rows=[(W, F32)] * 3, name=name)


def _adamw_pair(q, other, w, m, v, pos, name):
    _, R_, W = w.shape
    T = _row_tile(R_, W, 9)

    def body(pos_ref, q_ref, o_ref, w_ref, m_ref, v_ref, g_out, d_out, m_out, v_out):
        g = jnp.where(pl.program_id(0) == pos_ref[1], q_ref[...], o_ref[...])
        g_out[0] = g
        d_out[0], m_out[0], v_out[0] = _adamw_math(g, w_ref[0], m_ref[0], v_ref[0])

    flat = pl.BlockSpec((T, W), lambda l, i, pos: (i, 0))
    full = pl.BlockSpec((1, T, W), lambda l, i, pos: (l, i, 0))
    return _scalar_call(body, pos, (2, R_ // T), [flat, flat, full, full, full], [full] * 4,
                        [jax.ShapeDtypeStruct(w.shape, F32)] * 4, (q, other, w, m, v), name)


def _row_tile(rows, width, n_blocks, budget=14 * 1024 * 1024):
    wpad = -(-width // LANE) * LANE
    for t in (512, 256, 128, 64, 32, 16, 8):
        if rows % t == 0 and n_blocks * t * wpad * 4 <= budget:
            return t
    return rows


_ANY = pl.BlockSpec(memory_space=pl.ANY)
_MESH = pl.DeviceIdType.MESH


DMA_CHUNK_BYTES = 2 * 1024 * 1024


def _row_chunks(shape, dtype):
    r = shape[-2]
    total = 1
    for s in shape:
        total *= s
    want = max(1, (total * jnp.dtype(dtype).itemsize) // DMA_CHUNK_BYTES)
    n = 1
    while n * 2 <= want and r % (n * 2 * 16) == 0 and n < 8:
        n *= 2
    return [(j * (r // n), r // n) for j in range(n)]


def _comm_call(plan, srcs, out_shapes, name, by_core=False):
    n = len(srcs)
    n_remote = len(plan(0, 0, 0, [_ShapeOnly(s.shape) for s in srcs], [_ShapeOnly(s.shape) for s in out_shapes]))

    def body(*refs):
        src_refs, out_refs = refs[:n], refs[n:n + len(out_shapes)]
        send_sems, recv_sems = refs[n + len(out_shapes):]
        x, y, c = lax.axis_index("x"), lax.axis_index("y"), lax.axis_index("c")

        def run(core):
            sent = []
            for j, (s, d, peer) in enumerate(plan(x, y, core, src_refs, out_refs)):
                cp = pltpu.make_async_remote_copy(src_ref=s, dst_ref=d, send_sem=send_sems.at[j],
                                                  recv_sem=recv_sems.at[j], device_id=peer, device_id_type=_MESH)
                cp.start()
                sent.append(cp)
            for cp in sent:
                cp.wait()

        if by_core:
            for core in (0, 1):
                pl.when(c == core)(functools.partial(run, core))
        else:
            run(c)

    return pl.pallas_call(
        body, in_specs=[_ANY] * n, out_specs=[_ANY] * len(out_shapes),
        out_shape=[jax.ShapeDtypeStruct(s.shape, s.dtype) for s in out_shapes],
        scratch_shapes=[pltpu.SemaphoreType.DMA((n_remote,)), pltpu.SemaphoreType.DMA((n_remote,))], name=name)(*srcs)


class _ShapeOnly:
    def __init__(self, shape):
        self.shape = tuple(shape)

    @property
    def at(self):
        return self

    def __getitem__(self, idx):
        return self


def _other_places(x, y):
    return [(1 - x, y), (x, 1 - y), (1 - x, 1 - y)]


def _gather_places(shards, row_major, name):
    n = len(shards)
    outs = []
    for s, rm in zip(shards, row_major):
        L_, r, c_ = s.shape
        assert L_ == 2
        outs.append(jax.ShapeDtypeStruct((L_, N_PLACES, r, c_) if rm else (N_PLACES, L_, r, c_), s.dtype))
    n_copies = 3 * sum(len(_row_chunks(s.shape[1:], s.dtype)) for s in shards)

    def body(*refs):
        src_refs, out_refs = refs[:n], refs[n:2 * n]
        ici_send, ici_recv, d2d_send, d2d_recv = refs[2 * n:]
        x, y, c = lax.axis_index("x"), lax.axis_index("y"), lax.axis_index("c")
        me = 2 * x + y

        def slot(o_ref, rm, place, layer, r0, rn):
            return o_ref.at[layer, place, pl.ds(r0, rn), :] if rm else o_ref.at[place, layer, pl.ds(r0, rn), :]

        over_ici, landed = [], []
        for s_ref, o_ref, rm, s in zip(src_refs, out_refs, row_major, shards):
            for r0, rn in _row_chunks(s.shape[1:], s.dtype):
                for px, py in _other_places(x, y):
                    j = len(over_ici)
                    cp = pltpu.make_async_remote_copy(
                        src_ref=s_ref.at[c, pl.ds(r0, rn), :], dst_ref=slot(o_ref, rm, me, c, r0, rn),
                        send_sem=ici_send.at[j], recv_sem=ici_recv.at[j], device_id=(px, py, c), device_id_type=_MESH)
                    cp.start()
                    over_ici.append(cp)
                    landed.append((o_ref, rm, 2 * px + py, r0, rn))
        passed = []
        for j, (o_ref, rm, place, r0, rn) in enumerate(landed):
            pltpu.make_async_remote_copy(
                src_ref=slot(o_ref, rm, place, c, r0, rn), dst_ref=slot(o_ref, rm, place, c, r0, rn),
                send_sem=ici_send.at[j], recv_sem=ici_recv.at[j], device_id=(x, y, c), device_id_type=_MESH).wait_recv()
            cp = pltpu.make_async_remote_copy(
                src_ref=slot(o_ref, rm, place, c, r0, rn), dst_ref=slot(o_ref, rm, place, c, r0, rn),
                send_sem=d2d_send.at[j], recv_sem=d2d_recv.at[j], device_id=(x, y, 1 - c), device_id_type=_MESH)
            cp.start()
            passed.append(cp)
        for j, (o_ref, rm, place, r0, rn) in enumerate(landed):
            pltpu.make_async_remote_copy(
                src_ref=slot(o_ref, rm, place, 1 - c, r0, rn), dst_ref=slot(o_ref, rm, place, 1 - c, r0, rn),
                send_sem=d2d_send.at[j], recv_sem=d2d_recv.at[j], device_id=(x, y, 1 - c), device_id_type=_MESH).wait_recv()
        for cp in over_ici + passed:
            cp.wait_send()

    return pl.pallas_call(
        body, in_specs=[_ANY] * n, out_specs=[_ANY] * n, out_shape=outs,
        scratch_shapes=[pltpu.SemaphoreType.DMA((n_copies,))] * 4, name=name)(*shards)


def _reduce_sibling(pairs, name):
    outs = [jax.ShapeDtypeStruct(g0.shape, g0.dtype) for g0, _ in pairs]

    def plan(x, y, c, src_refs, out_refs):
        remote = []
        for t, (o_ref, (g0, _)) in enumerate(zip(out_refs, pairs)):
            g_ref = src_refs[2 * t + (1 - c)]
            for r0, rn in _row_chunks(g0.shape[1:], g0.dtype):
                for p in range(N_PLACES):
                    remote.append((g_ref.at[p, pl.ds(r0, rn), :], o_ref.at[p, pl.ds(r0, rn), :], (x, y, 1 - c)))
        return remote

    return _comm_call(plan, [g for pair in pairs for g in pair], outs, name, by_core=True)


def _reduce_places(hs, name):
    outs = [jax.ShapeDtypeStruct((3,) + h.shape[1:], h.dtype) for h in hs]

    def plan(x, y, c, src_refs, out_refs):
        remote = []
        for h_ref, o_ref, h in zip(src_refs, out_refs, hs):
            for r0, rn in _row_chunks(h.shape[1:], h.dtype):
                for j, (px, py) in enumerate(_other_places(x, y)):
                    remote.append((h_ref.at[2 * px + py, pl.ds(r0, rn), :], o_ref.at[j, pl.ds(r0, rn), :], (px, py, c)))
        return remote

    return _comm_call(plan, hs, outs, name)


def _swap_sibling(qs, name):
    def plan(x, y, c, src_refs, out_refs):
        remote = []
        for q_ref, o_ref, q in zip(src_refs, out_refs, qs):
            for r0, rn in _row_chunks(q.shape, q.dtype):
                remote.append((q_ref.at[pl.ds(r0, rn), :], o_ref.at[pl.ds(r0, rn), :], (x, y, 1 - c)))
        return remote

    return _comm_call(plan, qs, qs, name)


def _scalar_call(body, scalars, grid, in_specs, out_specs, out_shape, args, name):
    return pl.pallas_call(
        body, grid_spec=pltpu.PrefetchScalarGridSpec(num_scalar_prefetch=1, grid=grid, in_specs=in_specs,
                                                     out_specs=out_specs),
        out_shape=out_shape, compiler_params=_params(("arbitrary",) * len(grid)), name=name)(scalars, *args)


def _own_layer(pos_ref, g0_ref, g1_ref):
    return jnp.where(pos_ref[1] == 0, g0_ref[...], g1_ref[...])


def _add_own_slot(pair, r_, pos, out_dtype, name):
    P_, R_, W = r_.shape
    T = _row_tile(R_, W, 4)

    def body(pos_ref, g0_ref, g1_ref, r_ref, o_ref):
        o_ref[...] = (_own_layer(pos_ref, g0_ref, g1_ref) + r_ref[...]).astype(out_dtype)

    blk = pl.BlockSpec((1, T, W), lambda p, i, pos: (p, i, 0))
    return _scalar_call(body, pos, (P_, R_ // T), [blk, blk, blk], blk,
                        jax.ShapeDtypeStruct((P_, R_, W), out_dtype), (pair[0], pair[1], r_), name)


def _sum_places(pair, r_, recv, pos, name):
    _, R_, W = r_.shape
    T = _row_tile(R_, W, 6)

    def body(pos_ref, g0_ref, g1_ref, r_ref, recv_ref, o_ref):
        for m in range(N_PLACES):
            @pl.when(pos_ref[0] == m)
            def _(m=m):
                acc = None
                for p in range(N_PLACES):
                    if p == m:
                        term = _own_layer(pos_ref, g0_ref, g1_ref)[0] + r_ref[0]
                    else:
                        dx, dy = (p >> 1) != (m >> 1), (p & 1) != (m & 1)
                        term = recv_ref[0 if (dx and not dy) else 1 if (dy and not dx) else 2].astype(F32)
                    acc = term if acc is None else acc + term
                o_ref[...] = acc

    mine = pl.BlockSpec((1, T, W), lambda i, pos: (pos[0], i, 0))
    return _scalar_call(
        body, pos, (R_ // T,), [mine, mine, mine, pl.BlockSpec((3, T, W), lambda i, pos: (0, i, 0))],
        pl.BlockSpec((T, W), lambda i, pos: (i, 0)), jax.ShapeDtypeStruct((R_, W), F32),
        (pair[0], pair[1], r_, recv), name)


def _lane_tile(vec16):
    return jnp.concatenate([jnp.zeros((DT_LANE0,), F32), vec16,
                            jnp.zeros((LANE - DT_LANE0 - SSD_HEADS,), F32)])[None]


def _layer_consts(W, l):
    return dict(
        norm_mix=W['norm_mix'][l][None], mix=W['pool_mix'][l].astype(MXU_DTYPE), scale=W['pool_scale'][l][None],
        f_bias=W['f_bias'][l][:, None], cw=W['ssd_conv_w'][l], cb=W['ssd_conv_b'][l][None],
        dtb=_lane_tile(W['ssd_dt_bias'][l]), a_row=_lane_tile(-jnp.exp(W['ssd_a_log'][l])),
        d_full=jnp.repeat(W['ssd_d'][l], SSD_P)[None], ssd_norm=W['ssd_norm'][l][None],
        norm_ffn=W['norm_ffn'][l][None], fcw=W['ffn_conv_w'][l], fcb=W['ffn_conv_b'][l][None])


def _layer_fwd(x, W, l):
    n = f"l{l}_"
    cs = _layer_consts(W, l)
    win = {k: v[l] for k, v in W['w_in'].items()}
    u = _norm_fwd(x, cs['norm_mix'], n + "norm_mix")
    pqkv = _mm(u, win['p'], name=n + "in_p")
    z = _mm(u, win['z'], name=n + "in_z")
    xbc = _mm(u, win['x'], name=n + "in_x")
    gl = _mm(u, win['g'], name=n + "in_g")
    fdt = _mm(u, win['f'], name=n + "in_f")
    d, ypm = _pool_fwd(pqkv, cs['mix'], cs['scale'], n + "pool")
    yp = _mm(ypm, W['p_pool'][l], name=n + "p_pool")
    fT = fdt[:, :HEADS].T
    c = _logf_cumsum(fT, cs['f_bias'], n + "logf")
    c_row = c[:, None, :]
    o, lse = _attn_fwd(pqkv, c_row, n + "attn")
    ya = _mm(o, W['p_attn'][l], name=n + "p_attn")
    xa, dtw = _ssd_pre_fwd(xbc, fdt, cs['cw'], cs['cb'], cs['dtb'], n + "ssd_pre")
    y, hprev = _ssd_chunk_fwd(xa, dtw, cs['a_row'], cs['d_full'], n + "ssd_scan")
    yn = _ssd_post_fwd(y, z, cs['ssd_norm'], n + "ssd_post")
    ys = _mm(yn, W['p_ssd'][l], name=n + "p_ssd")
    merged = _merge_fwd(gl, yp, ya, ys, n + "merge")
    x1 = _mm(merged, W['w_out'][l], acc=x, name=n + "w_out")
    u2 = _norm_fwd(x1, cs['norm_ffn'], n + "norm_ffn")
    hpre = _mm(u2, W['ffn_up'][l], name=n + "ffn_up")
    act = _ffn_act_fwd(hpre, cs['fcw'], cs['fcb'], n + "ffn_act")
    x2 = _mm(act, W['ffn_down'][l], acc=x1, name=n + "ffn_down")
    saved = dict(x=x, u=u, pqkv=pqkv, z=z, xbc=xbc, gl=gl, fdt=fdt, d=d, ypm=ypm, yp=yp, fT=fT,
                 c_row=c_row, o=o, lse=lse, ya=ya, xa=xa, dtw=dtw, y=y, hprev=hprev,
                 yn=yn, ys=ys, merged=merged, x1=x1, u2=u2, hpre=hpre, act=act, win=win, cs=cs)
    return x2, saved


def _layer_bwd(dx2, sv, W, l):
    n = f"l{l}_b_"
    cs, win = sv['cs'], sv['win']
    g = {}
    dact = _mm(dx2, W['ffn_down'][l], tb=True, name=n + "ffn_down_dx")
    g['ffn_down'] = _mm(sv['act'], dx2, ta=True, name=n + "ffn_down_dw")
    dhc, g['ffn_conv_w'], dfcb = _ffn_act_bwd_a(sv['hpre'], dact, cs['fcw'], cs['fcb'], n + "ffn_act_a")
    g['ffn_conv_b'] = dfcb[0]
    dhpre = _conv_bwd_b(dhc, cs['fcw'], n + "ffn_act_b")
    du2 = _mm(dhpre, W['ffn_up'][l], tb=True, name=n + "ffn_up_dx")
    g['ffn_up'] = _mm(sv['u2'], dhpre, ta=True, name=n + "ffn_up_dw")
    dx1, dnf = _norm_bwd(sv['x1'], cs['norm_ffn'], du2, dx2, n + "norm_ffn")
    g['norm_ffn'] = dnf[0]
    dm = _mm(dx1, W['w_out'][l], tb=True, name=n + "w_out_dx")
    g['w_out'] = _mm(sv['merged'], dx1, ta=True, name=n + "w_out_dw")
    dyp, dya, dys, dgl = _merge_bwd(sv['gl'], sv['yp'], sv['ya'], sv['ys'], dm, n + "merge")
    dypm = _mm(dyp, W['p_pool'][l], tb=True, name=n + "p_pool_dx")
    g['p_pool'] = _mm(sv['ypm'], dyp, ta=True, name=n + "p_pool_dw")
    dd, dscale, dmix = _pool_bwd_a(dypm, sv['d'], cs['mix'], cs['scale'], n + "pool_a")
    g['pool_scale'] = dscale[0]
    g['pool_mix'] = dmix.reshape(len(POOL_WINDOWS), LANE, LANE)
    dpool_v = _pool_bwd_b(dd, n + "pool_b")
    do = _mm(dya, W['p_attn'][l], tb=True, name=n + "p_attn_dx")
    g['p_attn'] = _mm(sv['o'], dya, ta=True, name=n + "p_attn_dw")
    dq, dk, dv, dc, dcq = _attn_bwd(sv['pqkv'], do, sv['o'], sv['c_row'], sv['lse'], n + "attn")
    dcq = dcq.reshape(-1, HEADS // 2, LANE)[:, :, :2].reshape(-1, HEADS).T
    dfT, dfb = _logf_cumsum_bwd(sv['fT'], cs['f_bias'], dc[:, 0, :], dcq, n + "logf")
    g['f_bias'] = dfb[:, 0]
    dpqkv = jnp.concatenate([dpool_v, (dq * ATTN_SCALE).astype(MXU_DTYPE), dk.astype(MXU_DTYPE),
                             dv.astype(MXU_DTYPE)], axis=1)
    dyn = _mm(dys, W['p_ssd'][l], tb=True, name=n + "p_ssd_dx")
    g['p_ssd'] = _mm(sv['yn'], dys, ta=True, name=n + "p_ssd_dw")
    dy, dz, dsn = _ssd_post_bwd(sv['y'], sv['z'], dyn, cs['ssd_norm'], n + "ssd_post")
    g['ssd_norm'] = dsn[0]
    dxa, ddtw, dA, dD = _ssd_chunk_bwd(sv['xa'], sv['dtw'], dy, sv['hprev'], cs['a_row'], cs['d_full'], n + "ssd_scan")
    heads = slice(DT_LANE0, DT_LANE0 + SSD_HEADS)
    g['ssd_a_log'] = dA[0, heads] * cs['a_row'][0, heads]
    g['ssd_d'] = dD[0, heads]
    dpre, ddt_raw, g['ssd_conv_w'], dcb, ddtb = _ssd_pre_bwd_a(sv['xbc'], sv['fdt'], dxa, ddtw, cs['cw'], cs['cb'],
                                                              cs['dtb'], n + "ssd_pre_a")
    g['ssd_conv_b'] = dcb[0]
    g['ssd_dt_bias'] = ddtb[0, heads]
    dxbc = _conv_bwd_b(dpre, cs['cw'], n + "ssd_pre_b")
    dfdt = jnp.concatenate([dfT.T, ddt_raw[:, HEADS:]], axis=1).astype(MXU_DTYPE)
    dsegs = dict(p=dpqkv, z=dz, x=dxbc, g=dgl, f=dfdt)
    du, dwin = None, {}
    for key in ('p', 'z', 'x', 'g', 'f'):
        du = _mm(dsegs[key], win[key], tb=True, acc=du, name=n + "in_dx_" + key)
        dwin[key] = _mm(sv['u'], dsegs[key], ta=True, name=n + "in_dw_" + key)
    g['w_in'] = dwin
    dx, dnm = _norm_bwd(sv['x'], cs['norm_mix'], du, dx1, n + "norm_mix")
    g['norm_mix'] = dnm[0]
    return dx, g


def _local_step(x, target, W):
    depth = W['norm_mix'].shape[0]
    saved = []
    h = x
    for l in range(depth):
        h, sv = _layer_fwd(h, W, l)
        saved.append(sv)
    dx, dwf, loss = _loss_head(h, W['norm_final'][None], target, "loss_head")
    grads = [None] * depth
    for l in reversed(range(depth)):
        dx, grads[l] = _layer_bwd(dx, saved[l], W, l)
    return loss[0, 0], dx, grads, dwf[0]


def _pack_rows(parts, row_align=1):
    flat = jnp.concatenate([p.reshape(-1) for p in parts])
    n = flat.shape[0]
    total = -(-n // (PACK_W * row_align)) * PACK_W * row_align
    if total > n:
        flat = jnp.concatenate([flat, jnp.zeros((total - n,), flat.dtype)])
    return flat.reshape(-1, PACK_W)


def _unpack_rows(buf, shapes):
    flat = buf.reshape(-1)
    out, pos = [], 0
    for shp in shapes:
        size = 1
        for s in shp:
            size *= s
        out.append(flat[pos:pos + size].reshape(shp))
        pos += size
    return out


def _to_place_major(gfull, name):
    R_, C = gfull.shape
    if name in COL_SHARDED:
        return gfull.reshape(R_, N_PLACES, C // N_PLACES).transpose(1, 0, 2)
    return gfull.reshape(N_PLACES, R_ // N_PLACES, C)


_W_IN_LAYOUT = (('p', 0, 0, 2048), ('f', 0, 2048, HEADS), ('z', 0, 2056, 1024), ('x', 0, 3080, 1536),
                ('f', DT_LANE0, 4616, SSD_HEADS), ('g', 0, 4632, 3072))


def _w_in_segments(slabs):
    starts = [0]
    for s in slabs:
        starts.append(starts[-1] + s.shape[-1])

    def cols(a, b):
        parts = []
        for s, s0 in zip(slabs, starts):
            lo, hi = max(a, s0), min(b, s0 + s.shape[-1])
            if lo < hi:
                parts.append(s[..., lo - s0:hi - s0])
        return parts[0] if len(parts) == 1 else jnp.concatenate(parts, axis=-1)

    pad = jnp.zeros(slabs[0].shape[:-1] + (LANE - DT_LANE0 - SSD_HEADS,), slabs[0].dtype)
    return dict(p=cols(0, 2048), z=cols(2056, 3080), x=cols(3080, 4616), g=cols(4632, 7704),
                f=jnp.concatenate([cols(2048, 2056), cols(4616, 4632), pad], axis=-1))


def _w_in_columns(segs, a, b):
    parts = []
    for key, s0, g0, w in _W_IN_LAYOUT:
        lo, hi = max(a, g0), min(b, g0 + w)
        if lo < hi:
            parts.append(segs[key][..., s0 + lo - g0:s0 + hi - g0])
    return parts[0] if len(parts) == 1 else jnp.concatenate(parts, axis=-1)


def kernel(x, norm_mix, w_in, pool_mix, pool_scale, f_bias, ssd_conv_w, ssd_conv_b, ssd_dt_bias, ssd_a_log, ssd_d, ssd_norm, p_pool, p_attn, p_ssd, w_out, norm_ffn, ffn_up, ffn_conv_w, ffn_conv_b, ffn_down, norm_final, loss_target, m_norm_mix, m_w_in, m_pool_mix, m_pool_scale, m_f_bias, m_ssd_conv_w, m_ssd_conv_b, m_ssd_dt_bias, m_ssd_a_log, m_ssd_d, m_ssd_norm, m_p_pool, m_p_attn, m_p_ssd, m_w_out, m_norm_ffn, m_ffn_up, m_ffn_conv_w, m_ffn_conv_b, m_ffn_down, m_norm_final, v_norm_mix, v_w_in, v_pool_mix, v_pool_scale, v_f_bias, v_ssd_conv_w, v_ssd_conv_b, v_ssd_dt_bias, v_ssd_a_log, v_ssd_d, v_ssd_norm, v_p_pool, v_p_attn, v_p_ssd, v_w_out, v_norm_ffn, v_ffn_up, v_ffn_conv_w, v_ffn_conv_b, v_ffn_down, v_norm_final):
    args = dict(locals())
    w_sh = {k: args[k] for k in WEIGHTS}
    m_sh = {k: args['m_' + k] for k in WEIGHTS}
    v_sh = {k: args['v_' + k] for k in WEIGHTS}
    depth = norm_mix.shape[0]
    place = 2 * lax.axis_index("x") + lax.axis_index("y")
    row_sharded = [k for k in BIG if k not in COL_SHARDED]

    sent = {k: w_sh[k].astype(MXU_DTYPE) for k in BIG}
    sent.update({k: w_sh[k] for k in SMALL_SHARDED})
    gathered = _gather_places([sent[k] for k in BIG + SMALL_SHARDED],
                              [k in row_sharded for k in BIG + SMALL_SHARDED], "gather_weights")
    gathered = dict(zip(BIG + SMALL_SHARDED, gathered))
    W = {k: w_sh[k] for k in SMALL if k not in SMALL_SHARDED}
    zero = jnp.zeros((), jnp.int32)
    for k in BIG + SMALL_SHARDED:
        if k in row_sharded:
            gk = lax.dynamic_update_slice(gathered[k], sent[k][:, None], (zero, place, zero, zero))
            W[k] = gk.reshape(gk.shape[0], -1, gk.shape[-1])
            continue
        gk = lax.dynamic_update_slice(gathered[k], sent[k][None], (place, zero, zero, zero))
        if k == 'w_in':
            W[k] = _w_in_segments([gk[p] for p in range(N_PLACES)])
        else:
            W[k] = jnp.concatenate([gk[p] for p in range(N_PLACES)], axis=-1)

    loss_local, grad_x, grads, g_final = _local_step(x[0], loss_target[0], W)
    loss = lax.psum(loss_local, ("x", "y", "c"))

    assert depth == 2
    def place_major(k, l):
        if k == 'w_in':
            c = IN_TOTAL // N_PLACES
            return jnp.stack([_w_in_columns(grads[l][k], p * c, (p + 1) * c) for p in range(N_PLACES)])
        return _to_place_major(grads[l][k], k)

    g_big = [[place_major(k, l) for l in range(depth)] for k in BIG]
    small_names = [k for k in SMALL if k != 'norm_final'] + ['norm_final']
    small_full = [jnp.stack([grads[l][k] for l in range(depth)]) for k in small_names[:-1]] + [g_final]
    small_full_shapes = [a.shape for a in small_full]
    small_packed = _pack_rows(small_full, 32).reshape(2, -1, PACK_W)
    g_small = [jnp.broadcast_to(small_packed[h][None], (N_PLACES,) + small_packed.shape[1:]) for h in range(2)]

    core = lax.axis_index("c")
    pos = jnp.stack([place, core]).astype(jnp.int32)
    g_all = g_big + [g_small]
    theirs = _reduce_sibling(g_all, "reduce_sibling")
    wire = [WIRE_DTYPE] * len(BIG) + [F32]
    halves = [_add_own_slot(g, t, pos, dt, f"reduce_sibling_add{j}")
              for j, (g, t, dt) in enumerate(zip(g_all, theirs, wire))]
    recv = _reduce_places(halves, "reduce_places")
    qs = [_sum_places(g, t, r, pos, f"reduce_places_add{j}") for j, (g, t, r) in enumerate(zip(g_all, theirs, recv))]
    others = _swap_sibling(qs, "reduce_swap")

    def mine(k, a):
        if k in SMALL_SHARDED:
            c = a.shape[-1] // N_PLACES
            return lax.dynamic_slice_in_dim(a, place * c, c, axis=a.ndim - 1)
        return a

    outs = {}
    for j, k in enumerate(BIG):
        res = _adamw_pair(qs[j], others[j], w_sh[k], m_sh[k], v_sh[k], pos, "adamw_" + k)
        for prefix, a in zip(('grad_', 'delta_', 'new_m_', 'new_v_'), res):
            outs[prefix + k] = a
    small_sum = jnp.where(core == 0, jnp.concatenate([qs[-1], others[-1]]), jnp.concatenate([others[-1], qs[-1]]))
    g_small_list = [mine(k, a) for k, a in zip(small_names, _unpack_rows(small_sum, small_full_shapes))]
    shapes = [a.shape for a in g_small_list]
    gp = _pack_rows(g_small_list, 128)
    wp, mp, vp = (_pack_rows([d[k] for k in small_names], 128) for d in (w_sh, m_sh, v_sh))
    delta_p, m_p, v_p = _adamw(gp, wp, mp, vp, "adamw_small")
    for prefix, buf in (('grad_', gp), ('delta_', delta_p), ('new_m_', m_p), ('new_v_', v_p)):
        for k, a in zip(small_names, _unpack_rows(buf, shapes)):
            outs[prefix + k] = a
    result = [loss, grad_x[None]]
    for prefix in ('grad_', 'delta_', 'new_m_', 'new_v_'):
        result += [outs[prefix + k] for k in WEIGHTS]
    return tuple(result)
```

```python
import functools

import jax
import jax.numpy as jnp
from jax import lax
from jax.experimental import pallas as pl
from jax.experimental.pallas import tpu as pltpu

F32 = jnp.float32
MXU_DTYPE = jnp.bfloat16
WIRE_DTYPE = jnp.bfloat16
NORM_EPS = 1e-6
HALO = 16
LANE = 128
NEG_BIG = -1e30
VMEM_LIMIT = 52 * 1024 * 1024

D_MODEL = 1024
POOL_WINDOWS = (2, 4, 8, 16)
POOL_W = 512
HEADS = 8
HEAD_DIM = 64
ATTN_W = 512
ATTN_SCALE = HEAD_DIM ** -0.5
SSD_W = 1024
SSD_HEADS = 16
SSD_P = 64
SSD_N = 128
SSD_CHUNK = 128
SSD_CONV_CH = 1536
FFN = 2816
DT_LANE0 = 8
IN_SPLITS = (512, 512, 512, 512, 8, 1024, 1536, 16, 3072)
IN_TOTAL = sum(IN_SPLITS)
N_PLACES = 4

ADAM_LR, ADAM_B1, ADAM_B2, ADAM_EPS, ADAM_WD, ADAM_STEP = 0.001, 0.9, 0.999, 1e-08, 0.01, 10

BIG = ('w_in', 'p_pool', 'p_attn', 'p_ssd', 'w_out', 'ffn_up', 'ffn_down')
COL_SHARDED = ('w_in', 'p_pool', 'p_attn', 'ffn_up')
SMALL = ('norm_mix', 'pool_mix', 'pool_scale', 'f_bias', 'ssd_conv_w', 'ssd_conv_b', 'ssd_dt_bias',
         'ssd_a_log', 'ssd_d', 'ssd_norm', 'norm_ffn', 'ffn_conv_w', 'ffn_conv_b', 'norm_final')
SMALL_SHARDED = ('ssd_conv_w', 'ffn_conv_w')
WEIGHTS = ('norm_mix', 'w_in', 'pool_mix', 'pool_scale', 'f_bias', 'ssd_conv_w', 'ssd_conv_b', 'ssd_dt_bias',
           'ssd_a_log', 'ssd_d', 'ssd_norm', 'p_pool', 'p_attn', 'p_ssd', 'w_out', 'norm_ffn', 'ffn_up',
           'ffn_conv_w', 'ffn_conv_b', 'ffn_down', 'norm_final')
PACK_W = 1024


def _params(sem):
    return pltpu.CompilerParams(dimension_semantics=sem, vmem_limit_bytes=VMEM_LIMIT)


def _tile(n, prefs=(512, 256, 128)):
    for t in prefs:
        if n % t == 0:
            return t
    return n


def _sigmoid(x):
    return 0.5 * jnp.tanh(0.5 * x) + 0.5


def _softplus(x):
    return jnp.maximum(x, 0.0) + jnp.log1p(jnp.exp(-jnp.abs(x)))


def _dot(a, b, dims=((1,), (0,))):
    return lax.dot_general(a.astype(MXU_DTYPE), b.astype(MXU_DTYPE), (dims, ((), ())),
                           preferred_element_type=F32)


NT = ((1,), (1,))


def _mm(a, b, *, ta=False, tb=False, acc=None, out_dtype=F32, name):
    M, K = (a.shape[1], a.shape[0]) if ta else a.shape
    N = b.shape[0] if tb else b.shape[1]
    big = (1024, 1408, 512, 256, 128)
    tm, tn = _tile(M, big), _tile(N, big)
    tk = K if K <= 1024 else _tile(K, (512, 256, 128) if ta else big)
    nk = K // tk
    a_spec = pl.BlockSpec((tk, tm), lambda i, j, k: (k, i)) if ta else pl.BlockSpec((tm, tk), lambda i, j, k: (i, k))
    b_spec = pl.BlockSpec((tn, tk), lambda i, j, k: (j, k)) if tb else pl.BlockSpec((tk, tn), lambda i, j, k: (k, j))
    in_specs = [a_spec, b_spec]
    args = [a, b]
    if acc is not None:
        in_specs.append(pl.BlockSpec((tm, tn), lambda i, j, k: (i, j)))
        args.append(acc)

    def body(*refs):
        a_ref, b_ref = refs[:2]
        c_ref = refs[2] if acc is not None else None
        o_ref, acc_ref = (refs[-1], None) if nk == 1 else refs[-2:]
        k = pl.program_id(2)
        if nk == 1:
            av = a_ref[...].astype(F32).T if ta else a_ref[...]
            part = _dot(av, b_ref[...], NT if tb else ((1,), (0,)))
            o_ref[...] = (part + c_ref[...] if acc is not None else part).astype(out_dtype)
            return

        @pl.when(k == 0)
        def _():
            if acc is not None:
                acc_ref[...] = c_ref[...].astype(F32)
            else:
                acc_ref[...] = jnp.zeros_like(acc_ref)

        av = a_ref[...]
        if ta:
            av = av.astype(F32).T
        acc_ref[...] += _dot(av, b_ref[...], NT if tb else ((1,), (0,)))

        @pl.when(k == nk - 1)
        def _():
            o_ref[...] = acc_ref[...].astype(out_dtype)

    return pl.pallas_call(
        body, grid=(M // tm, N // tn, nk), in_specs=in_specs,
        out_specs=pl.BlockSpec((tm, tn), lambda i, j, k: (i, j)),
        out_shape=jax.ShapeDtypeStruct((M, N), out_dtype),
        scratch_shapes=[] if nk == 1 else [pltpu.VMEM((tm, tn), F32)],
        compiler_params=_params(("parallel", "parallel", "arbitrary")), name=name)(*args)


def _rows(body, S, T, *, rows=(), consts=(), prevs=(), nexts=(), out_rows=(), out_accs=(), scratch=(), name):
    n = S // T
    hb = T // HALO
    last_h = S // HALO - 1

    def norm(r):
        return r if isinstance(r, tuple) else (r, r.shape[1], 0)

    rows, prevs, nexts = [norm(r) for r in rows], [norm(r) for r in prevs], [norm(r) for r in nexts]
    in_specs, args = [], []
    for arr, W, cb in rows:
        in_specs.append(pl.BlockSpec((T, W), lambda i, cb=cb: (i, cb)))
        args.append(arr)
    for cst in consts:
        in_specs.append(pl.BlockSpec(cst.shape, lambda i, nd=cst.ndim: (0,) * nd))
        args.append(cst)
    for arr, W, cb in prevs:
        in_specs.append(pl.BlockSpec((HALO, W), lambda i, cb=cb: (jnp.maximum(i * hb - 1, 0), cb)))
        args.append(arr)
    for arr, W, cb in nexts:
        in_specs.append(pl.BlockSpec((HALO, W), lambda i, cb=cb: (jnp.minimum((i + 1) * hb, last_h), cb)))
        args.append(arr)
    out_specs = [pl.BlockSpec((T, W), lambda i: (i, 0)) for W, _ in out_rows]
    out_specs += [pl.BlockSpec(shp, lambda i, nd=len(shp): (0,) * nd) for shp, _ in out_accs]
    out_shape = [jax.ShapeDtypeStruct((S, W), dt) for W, dt in out_rows]
    out_shape += [jax.ShapeDtypeStruct(shp, dt) for shp, dt in out_accs]
    cuts = [len(rows), len(consts), len(prevs), len(nexts), len(out_rows), len(out_accs), len(scratch)]

    def kern(*refs):
        groups, pos = [], 0
        for c in cuts:
            groups.append(list(refs[pos:pos + c]))
            pos += c
        i = pl.program_id(0)

        @pl.when(i == 0)
        def _():
            for a_ref in groups[5]:
                a_ref[...] = jnp.zeros_like(a_ref)

        body(i, n, *groups)

    outs = pl.pallas_call(kern, grid=(n,), in_specs=in_specs, out_specs=out_specs, out_shape=out_shape,
                          scratch_shapes=list(scratch), compiler_params=_params(("arbitrary",)), name=name)(*args)
    return outs


def _fill_prev(ext, prev_ref, cur, i):
    ext[0:HALO, :] = jnp.where(i > 0, prev_ref[...].astype(F32), 0.0)
    ext[HALO:, :] = cur


def _row_ids(i, T, W=1):
    return i * T + lax.broadcasted_iota(jnp.int32, (T, W), 0)


SUB_ROWS = 32
WIN_PAD = 8


def _tile_loop(T, fn):
    n = T // SUB_ROWS
    fn(0, True, n == 1)
    if n > 2:
        def body(rb, carry):
            fn(pl.multiple_of(rb * SUB_ROWS, SUB_ROWS), False, False)
            return carry
        lax.fori_loop(1, n - 1, body, 0)
    if n > 1:
        fn((n - 1) * SUB_ROWS, False, True)


def _win_prev(x_ref, prev_ref, i, r0, first, cols):
    if first:
        top = jnp.where(i > 0, prev_ref[HALO - WIN_PAD:HALO, cols].astype(F32), 0.0)
        return jnp.concatenate([top, x_ref[0:SUB_ROWS, cols].astype(F32)], axis=0)
    start = r0 - WIN_PAD if isinstance(r0, int) else pl.multiple_of(r0 - WIN_PAD, WIN_PAD)
    return x_ref[pl.ds(start, SUB_ROWS + WIN_PAD), cols].astype(F32)


def _behind(win, j):
    return win[WIN_PAD:, :] if j == 0 else pltpu.roll(win, j, axis=0)[WIN_PAD:, :]


def _win_next(x_ref, next_ref, i, n, r0, last, cols):
    if last:
        bot = jnp.where(i < n - 1, next_ref[0:WIN_PAD, cols].astype(F32), 0.0)
        return jnp.concatenate([x_ref[r0:r0 + SUB_ROWS, cols].astype(F32), bot], axis=0)
    return x_ref[pl.ds(r0, SUB_ROWS + WIN_PAD), cols].astype(F32)


def _ahead(win, j):
    return win[:SUB_ROWS, :] if j == 0 else pltpu.roll(win, SUB_ROWS + WIN_PAD - j, axis=0)[:SUB_ROWS, :]


def _taps(win, K):
    return [_behind(win, K - 1 - k) for k in range(K)]


def _conv_win(taps, w_ref, b_ref, cols):
    out = b_ref[:, cols]
    for k, tap in enumerate(taps):
        out = out + tap * w_ref[k:k + 1, cols]
    return out


def _conv_wgrad_win(acc_ref, taps, d, cols):
    for k, tap in enumerate(taps):
        acc_ref[k:k + 1, cols] += jnp.sum(d * tap, axis=0, keepdims=True)


def _norm_fwd(x, w, name):
    S, D = x.shape

    def body(i, n, R, C, P, N, O, A, Sc):
        xv = R[0][...]
        r = lax.rsqrt(jnp.mean(xv * xv, axis=-1, keepdims=True) + NORM_EPS)
        O[0][...] = (xv * r * C[0][...]).astype(MXU_DTYPE)

    return _rows(body, S, _tile(S), rows=[x], consts=[w], out_rows=[(D, MXU_DTYPE)], name=name)[0]


def _norm_bwd_math(xv, w, du):
    r = lax.rsqrt(jnp.mean(xv * xv, axis=-1, keepdims=True) + NORM_EPS)
    xh = xv * r
    g = du * w
    dx = r * (g - xh * jnp.mean(g * xh, axis=-1, keepdims=True))
    dw = jnp.sum(du * xh, axis=0, keepdims=True)
    return dx, dw


def _norm_bwd(x, w, du, dres, name):
    S, D = x.shape

    def body(i, n, R, C, P, N, O, A, Sc):
        dx, dw = _norm_bwd_math(R[0][...], C[0][...], R[1][...])
        O[0][...] = R[2][...] + dx
        A[0][...] += dw

    return _rows(body, S, _tile(S), rows=[x, du, dres], consts=[w], out_rows=[(D, F32)],
                 out_accs=[((1, D), F32)], name=name)


def _loss_head(x, w, target, name):
    S, D = x.shape

    def body(i, n, R, C, P, N, O, A, Sc):
        xv, w_, tg = R[0][...], C[0][...], R[1][...]
        r = lax.rsqrt(jnp.mean(xv * xv, axis=-1, keepdims=True) + NORM_EPS)
        e = xv * r * w_ - tg
        A[1][...] += jnp.broadcast_to(0.5 * jnp.sum(jnp.mean(e * e, axis=-1, keepdims=True)), (1, LANE))
        dx, dw = _norm_bwd_math(xv, w_, e / D)
        O[0][...] = dx
        A[0][...] += dw

    return _rows(body, S, _tile(S), rows=[x, target], consts=[w], out_rows=[(D, F32)],
                 out_accs=[((1, D), F32), ((1, LANE), F32)], name=name)


def _pool_fwd(pqkv, mix, scale, name):
    S = pqkv.shape[0]
    T = _tile(S, (256, 128))

    def body(i, n, R, C, P, N, O, A, Sc):
        ext = Sc[0]
        v = R[0][...]
        _fill_prev(ext, P[0], v, i)
        t1 = (_row_ids(i, T) + 1).astype(F32)
        for g, w in enumerate(POOL_WINDOWS):
            cols = slice(g * LANE, (g + 1) * LANE)
            acc = v[:, cols]
            for j in range(1, w):
                acc = acc + ext[pl.ds(HALO - j, T), cols]
            d = (acc / jnp.minimum(t1, float(w)) - v[:, cols]).astype(MXU_DTYPE)
            O[0][:, cols] = d
            O[1][:, cols] = (_dot(d, C[0][g]) * C[1][:, cols]).astype(MXU_DTYPE)

    return _rows(body, S, T, rows=[(pqkv, POOL_W, 0)], prevs=[(pqkv, POOL_W, 0)], consts=[mix, scale],
                 out_rows=[(POOL_W, MXU_DTYPE), (POOL_W, MXU_DTYPE)],
                 scratch=[pltpu.VMEM((HALO + T, POOL_W), F32)], name=name)


def _pool_bwd_a(dypm, d, mix, scale, name):
    S = d.shape[0]
    T = _tile(S, (256, 128))

    def body(i, n, R, C, P, N, O, A, Sc):
        for g in range(len(POOL_WINDOWS)):
            cols = slice(g * LANE, (g + 1) * LANE)
            dg = R[1][:, cols]
            dy = R[0][:, cols]
            yg = _dot(dg, C[0][g])
            A[0][:, cols] += jnp.sum(dy * yg, axis=0, keepdims=True)
            dys = dy * C[1][:, cols]
            A[1][cols, :] += _dot(dg.astype(F32).T, dys)
            O[0][:, cols] = _dot(dys, C[0][g], NT)

    return _rows(body, S, T, rows=[dypm, d], consts=[mix, scale], out_rows=[(POOL_W, F32)],
                 out_accs=[((1, POOL_W), F32), ((POOL_W, LANE), F32)], name=name)


def _pool_bwd_b(dd, name):
    S = dd.shape[0]
    T = _tile(S, (256, 128))

    def body(i, n, R, C, P, N, O, A, Sc):
        ext = Sc[0]
        ddv = R[0][...]
        t1 = (_row_ids(i, T) + 1).astype(F32)
        nxt = jnp.where(i < n - 1, N[0][...], 0.0)
        for g, w in enumerate(POOL_WINDOWS):
            cols = slice(g * LANE, (g + 1) * LANE)
            ext[0:T, cols] = ddv[:, cols] / jnp.minimum(t1, float(w))
            ext[T:, cols] = nxt[:, cols] / float(w)
        for g, w in enumerate(POOL_WINDOWS):
            cols = slice(g * LANE, (g + 1) * LANE)
            acc = ext[0:T, cols]
            for j in range(1, w):
                acc = acc + ext[pl.ds(j, T), cols]
            O[0][:, cols] = (acc - ddv[:, cols]).astype(MXU_DTYPE)

    return _rows(body, S, T, rows=[dd], nexts=[dd], out_rows=[(POOL_W, MXU_DTYPE)],
                 scratch=[pltpu.VMEM((T + HALO, POOL_W), F32)], name=name)[0]


def _lane_cumsum(seg, reverse=False):
    lane = lax.broadcasted_iota(jnp.int32, seg.shape, 1)
    sh = 1
    while sh < LANE:
        if reverse:
            seg = seg + jnp.where(lane < LANE - sh, pltpu.roll(seg, LANE - sh, axis=1), 0.0)
        else:
            seg = seg + jnp.where(lane >= sh, pltpu.roll(seg, sh, axis=1), 0.0)
        sh *= 2
    return seg


def _logf_cumsum(fT, bias, name):
    H, S = fT.shape
    TB = _tile(S)
    nb = S // TB

    def body(f_ref, b_ref, o_ref, carry):
        @pl.when(pl.program_id(0) == 0)
        def _():
            carry[...] = jnp.zeros_like(carry)

        x = f_ref[...] + b_ref[...]
        lf = jnp.minimum(x, 0.0) - jnp.log1p(jnp.exp(-jnp.abs(x)))
        c = carry[...]
        for j in range(TB // LANE):
            seg = _lane_cumsum(lf[:, j * LANE:(j + 1) * LANE]) + c
            o_ref[:, j * LANE:(j + 1) * LANE] = seg
            c = seg[:, LANE - 1:LANE]
        carry[...] = c

    return pl.pallas_call(
        body, grid=(nb,), in_specs=[pl.BlockSpec((H, TB), lambda i: (0, i)), pl.BlockSpec((H, 1), lambda i: (0, 0))],
        out_specs=pl.BlockSpec((H, TB), lambda i: (0, i)), out_shape=jax.ShapeDtypeStruct((H, S), F32),
        scratch_shapes=[pltpu.VMEM((H, 1), F32)], compiler_params=_params(("arbitrary",)), name=name)(fT, bias)


def _logf_cumsum_bwd(fT, bias, dc, dcq, name):
    H, S = fT.shape
    TB = _tile(S)
    nb = S // TB

    def body(f_ref, b_ref, dc_ref, dcq_ref, o_ref, db_ref, carry):
        @pl.when(pl.program_id(0) == 0)
        def _():
            carry[...] = jnp.zeros_like(carry)
            db_ref[...] = jnp.zeros_like(db_ref)

        x = f_ref[...] + b_ref[...]
        sg = _sigmoid(-x)
        dcv = dc_ref[...] + dcq_ref[...]
        c = carry[...]
        db = jnp.zeros((H, 1), F32)
        for j in reversed(range(TB // LANE)):
            seg = _lane_cumsum(dcv[:, j * LANE:(j + 1) * LANE], reverse=True) + c
            df = seg * sg[:, j * LANE:(j + 1) * LANE]
            o_ref[:, j * LANE:(j + 1) * LANE] = df
            db = db + jnp.sum(df, axis=1, keepdims=True)
            c = seg[:, 0:1]
        carry[...] = c
        db_ref[...] += db

    rev = lambda i: (0, nb - 1 - i)
    return pl.pallas_call(
        body, grid=(nb,),
        in_specs=[pl.BlockSpec((H, TB), rev), pl.BlockSpec((H, 1), lambda i: (0, 0)), pl.BlockSpec((H, TB), rev),
                  pl.BlockSpec((H, TB), rev)],
        out_specs=[pl.BlockSpec((H, TB), rev), pl.BlockSpec((H, 1), lambda i: (0, 0))],
        out_shape=[jax.ShapeDtypeStruct((H, S), F32), jax.ShapeDtypeStruct((H, 1), F32)],
        scratch_shapes=[pltpu.VMEM((H, 1), F32)], compiler_params=_params(("arbitrary",)), name=name)(
            fT, bias, dc, dcq)


def _attn_scores(q, k, ck, diagonal, T):
    s = _dot(q, k, NT) - ck
    if diagonal:
        tril = lax.broadcasted_iota(jnp.int32, (T, T), 1) <= lax.broadcasted_iota(jnp.int32, (T, T), 0)
        s = jnp.where(tril, s, NEG_BIG)
    return s


def _attn_fwd(pqkv, c_row, name):
    S = pqkv.shape[0]
    T = _tile(S, (1024, 512, 256, 128))
    nq = S // T
    qb, kb, vb = (ATTN_W * j // LANE for j in (1, 2, 3))

    steps = jnp.asarray([[qi for qi in range(nq) for ki in range(qi + 1)],
                         [ki for qi in range(nq) for ki in range(qi + 1)]], jnp.int32)

    def body(st_ref, q_ref, k_ref, v_ref, ck_ref, o_ref, lse_ref, m_s, l_s, acc_s):
        qi, ki = st_ref[0, pl.program_id(1)], st_ref[1, pl.program_id(1)]

        @pl.when(ki == 0)
        def _():
            m_s[...] = jnp.full_like(m_s, NEG_BIG)
            l_s[...] = jnp.zeros_like(l_s)
            acc_s[...] = jnp.zeros_like(acc_s)

        def step(diagonal):
            qf, kv, vv = q_ref[...] * ATTN_SCALE, k_ref[...].astype(MXU_DTYPE), v_ref[...].astype(MXU_DTYPE)
            acc = acc_s[...]
            for a in range(2):
                mk = _half_mask(a)
                s = _attn_scores(qf * mk, kv, ck_ref[a], diagonal, T)
                m_new = jnp.maximum(m_s[a], jnp.max(s, axis=1, keepdims=True))
                alpha = jnp.exp(m_s[a] - m_new)
                p = jnp.exp(s - m_new)
                l_s[a] = alpha * l_s[a] + jnp.sum(p, axis=1, keepdims=True)
                acc = acc * (1.0 + mk * (alpha - 1.0)) + _dot(p, vv) * mk
                m_s[a] = m_new
            acc_s[...] = acc

        @pl.when(ki < qi)
        def _():
            step(False)

        @pl.when(ki == qi)
        def _():
            step(True)
            o_ref[...] = acc_s[...] / (l_s[0] * _half_mask(0) + l_s[1] * _half_mask(1))
            for a in range(2):
                lse_ref[a] = m_s[a] + jnp.log(l_s[a])

    return pl.pallas_call(
        body, grid_spec=pltpu.PrefetchScalarGridSpec(
            num_scalar_prefetch=1, grid=(HEADS // 2, steps.shape[1]),
            in_specs=[pl.BlockSpec((T, LANE), lambda hp, s, st: (st[0, s], qb + hp)),
                      pl.BlockSpec((T, LANE), lambda hp, s, st: (st[1, s], kb + hp)),
                      pl.BlockSpec((T, LANE), lambda hp, s, st: (st[1, s], vb + hp)),
                      pl.BlockSpec((2, 1, T), lambda hp, s, st: (hp, 0, st[1, s]))],
            out_specs=[pl.BlockSpec((T, LANE), lambda hp, s, st: (st[0, s], hp)),
                       pl.BlockSpec((2, T, 1), lambda hp, s, st: (hp, st[0, s], 0))],
            scratch_shapes=[pltpu.VMEM((2, T, 1), F32), pltpu.VMEM((2, T, 1), F32), pltpu.VMEM((T, LANE), F32)]),
        out_shape=[jax.ShapeDtypeStruct((S, ATTN_W), F32), jax.ShapeDtypeStruct((HEADS, S, 1), F32)],
        compiler_params=_params(("arbitrary", "arbitrary")), name=name)(steps, pqkv, pqkv, pqkv, c_row)


def _attn_bwd(pqkv, do, o, c_row, lse, name):
    S = pqkv.shape[0]
    T = _tile(S, (1024, 512, 256, 128))
    nq = S // T
    qb, kb, vb = (ATTN_W * j // LANE for j in (1, 2, 3))

    steps = jnp.asarray([[ki for ki in range(nq) for qi in range(ki, nq)],
                         [qi for ki in range(nq) for qi in range(ki, nq)]], jnp.int32)

    def body(st_ref, q_ref, k_ref, v_ref, do_ref, o_ref, ck_ref, lse_ref, dq_ref, dk_ref, dv_ref, dc_ref, dcq_ref,
             dk_s, dv_s, dc_s):
        ki, qi = st_ref[0, pl.program_id(1)], st_ref[1, pl.program_id(1)]

        @pl.when(pl.program_id(1) == 0)
        def _():
            dq_ref[...] = jnp.zeros_like(dq_ref)
            dcq_ref[...] = jnp.zeros_like(dcq_ref)

        @pl.when(qi == ki)
        def _():
            dk_s[...] = jnp.zeros_like(dk_s)
            dv_s[...] = jnp.zeros_like(dv_s)
            dc_s[...] = jnp.zeros_like(dc_s)

        def step(diagonal):
            qf, kf, vv = q_ref[...] * ATTN_SCALE, k_ref[...], v_ref[...].astype(MXU_DTYPE)
            dof, ov = do_ref[...], o_ref[...]
            rows = pl.ds(pl.multiple_of(qi * T, T), T)
            lane = lax.broadcasted_iota(jnp.int32, (1, LANE), 1)
            for a in range(2):
                mk = _half_mask(a)
                qa, ka, doa = (qf * mk).astype(MXU_DTYPE), (kf * mk).astype(MXU_DTYPE), dof * mk
                s = _attn_scores(qa, ka, ck_ref[a], diagonal, T)
                p = jnp.exp(s - lse_ref[a])
                delta = jnp.sum(doa * ov, axis=1, keepdims=True)
                dv_s[...] += _dot(p.T, doa)
                dp = _dot(doa, vv, NT)
                ds = p * (dp - delta)
                dc_s[a] -= jnp.sum(ds, axis=0, keepdims=True)
                dq_ref[rows, :] += _dot(ds, ka)
                dcq_ref[rows, :] += jnp.sum(ds, axis=1, keepdims=True) * (lane == a).astype(F32)
                dk_s[...] += _dot(ds.T, qa)

        @pl.when(qi > ki)
        def _():
            step(False)

        @pl.when(qi == ki)
        def _():
            step(True)

        @pl.when(qi == nq - 1)
        def _():
            dk_ref[...] = dk_s[...]
            dv_ref[...] = dv_s[...]
            dc_ref[...] = dc_s[...]

    qrow = lambda hp, s, st: (st[1, s], hp)
    krow = lambda hp, s, st: (st[0, s], hp)
    whole = lambda hp, s, st: (0, hp)
    return pl.pallas_call(
        body, grid_spec=pltpu.PrefetchScalarGridSpec(
            num_scalar_prefetch=1, grid=(HEADS // 2, steps.shape[1]),
            in_specs=[pl.BlockSpec((T, LANE), lambda hp, s, st: (st[1, s], qb + hp)),
                      pl.BlockSpec((T, LANE), lambda hp, s, st: (st[0, s], kb + hp)),
                      pl.BlockSpec((T, LANE), lambda hp, s, st: (st[0, s], vb + hp)),
                      pl.BlockSpec((T, LANE), qrow), pl.BlockSpec((T, LANE), qrow),
                      pl.BlockSpec((2, 1, T), lambda hp, s, st: (hp, 0, st[0, s])),
                      pl.BlockSpec((2, T, 1), lambda hp, s, st: (hp, st[1, s], 0))],
            out_specs=[pl.BlockSpec((S, LANE), whole), pl.BlockSpec((T, LANE), krow), pl.BlockSpec((T, LANE), krow),
                       pl.BlockSpec((2, 1, T), lambda hp, s, st: (hp, 0, st[0, s])), pl.BlockSpec((S, LANE), whole)],
            scratch_shapes=[pltpu.VMEM((T, LANE), F32), pltpu.VMEM((T, LANE), F32), pltpu.VMEM((2, 1, T), F32)]),
        out_shape=[jax.ShapeDtypeStruct((S, ATTN_W), F32), jax.ShapeDtypeStruct((S, ATTN_W), F32),
                   jax.ShapeDtypeStruct((S, ATTN_W), F32), jax.ShapeDtypeStruct((HEADS, 1, S), F32),
                   jax.ShapeDtypeStruct((S, ATTN_W), F32)],
        compiler_params=_params(("arbitrary", "arbitrary")), name=name)(
            steps, pqkv, pqkv, pqkv, do, o, c_row, lse)


CONV_COLS = 512


def _conv_bwd_b(dpre, w, name):
    S, C = dpre.shape
    K = w.shape[0]
    T = _tile(S, (256, 128))

    def body(i, n, R, Cs, P, N, O, A, Sc):
        def tile(r0, first, last):
            for c0 in range(0, C, CONV_COLS):
                cols = slice(c0, c0 + CONV_COLS)
                win = _win_next(R[0], N[0], i, n, r0, last, cols)
                out = None
                for k in range(K):
                    term = _ahead(win, K - 1 - k) * Cs[0][k:k + 1, cols]
                    out = term if out is None else out + term
                O[0][pl.ds(r0, SUB_ROWS), cols] = out.astype(MXU_DTYPE)

        _tile_loop(T, tile)

    return _rows(body, S, T, rows=[dpre], nexts=[dpre], consts=[w], out_rows=[(C, MXU_DTYPE)], name=name)[0]


def _dt_mask():
    lane = lax.broadcasted_iota(jnp.int32, (1, LANE), 1)
    return ((lane >= DT_LANE0) & (lane < DT_LANE0 + SSD_HEADS)).astype(F32)


def _ssd_pre_fwd(xbc, fdt, cw, cb, dtb, name):
    S, C = xbc.shape
    T = _tile(S, (256, 128))
    K = cw.shape[0]

    def body(i, n, R, Cs, P, N, O, A, Sc):
        def tile(r0, first, last):
            rows = pl.ds(r0, SUB_ROWS)
            for c0 in range(0, C, CONV_COLS):
                cols = slice(c0, c0 + CONV_COLS)
                pre = _conv_win(_taps(_win_prev(R[0], P[0], i, r0, first, cols), K), Cs[0], Cs[1], cols)
                O[0][rows, cols] = pre * _sigmoid(pre)
            O[1][rows, :] = _softplus(R[1][rows, :] + Cs[2][...]) * _dt_mask()

        _tile_loop(T, tile)

    return _rows(body, S, T, rows=[xbc, fdt], prevs=[xbc], consts=[cw, cb, dtb],
                 out_rows=[(C, F32), (LANE, F32)], name=name)


def _silu_grad(pre):
    sg = _sigmoid(pre)
    return sg * (1.0 + pre * (1.0 - sg))


def _ssd_pre_bwd_a(xbc, fdt, dxa, ddtw, cw, cb, dtb, name):
    S, C = xbc.shape
    T = _tile(S, (256, 128))
    K = cw.shape[0]

    def body(i, n, R, Cs, P, N, O, A, Sc):
        def tile(r0, first, last):
            rows = pl.ds(r0, SUB_ROWS)
            for c0 in range(0, C, CONV_COLS):
                cols = slice(c0, c0 + CONV_COLS)
                win = _taps(_win_prev(R[0], P[0], i, r0, first, cols), K)
                dpre = R[2][rows, cols] * _silu_grad(_conv_win(win, Cs[0], Cs[1], cols))
                O[0][rows, cols] = dpre
                _conv_wgrad_win(A[0], win, dpre, cols)
                A[1][:, cols] += jnp.sum(dpre, axis=0, keepdims=True)
            ddt = R[3][rows, :] * _sigmoid(R[1][rows, :] + Cs[2][...]) * _dt_mask()
            O[1][rows, :] = ddt
            A[2][...] += jnp.sum(ddt, axis=0, keepdims=True)

        _tile_loop(T, tile)

    return _rows(body, S, T, rows=[xbc, fdt, dxa, ddtw], prevs=[xbc], consts=[cw, cb, dtb],
                 out_rows=[(C, F32), (LANE, F32)],
                 out_accs=[((K, C), F32), ((1, C), F32), ((1, LANE), F32)], name=name)


def _split3(x):
    hi = x.astype(jnp.bfloat16)
    r1 = x - hi.astype(F32)
    mid = r1.astype(jnp.bfloat16)
    lo = (r1 - mid.astype(F32)).astype(jnp.bfloat16)
    return hi, mid, lo


def _expand_mat():
    r = lax.broadcasted_iota(jnp.int32, (LANE, SSD_W), 0)
    c = lax.broadcasted_iota(jnp.int32, (LANE, SSD_W), 1)
    return (r - DT_LANE0 == c // SSD_P).astype(jnp.bfloat16)


def _headsum_mat():
    r = lax.broadcasted_iota(jnp.int32, (SSD_W, LANE), 0)
    c = lax.broadcasted_iota(jnp.int32, (SSD_W, LANE), 1)
    return (c - DT_LANE0 == r // SSD_P).astype(jnp.bfloat16)


def _expand(tile, ex):
    return sum(lax.dot_general(part, ex, (((1,), (0,)), ((), ())), preferred_element_type=F32)
               for part in _split3(tile))


def _headsum(full, hs):
    return sum(lax.dot_general(part, hs, (((1,), (0,)), ((), ())), preferred_element_type=F32)
               for part in _split3(full))


def _sub_cumsum(a, reverse=False):
    n = a.shape[0]
    row = lax.broadcasted_iota(jnp.int32, a.shape, 0)
    sh = 1
    while sh < n:
        if reverse:
            a = a + jnp.where(row < n - sh, pltpu.roll(a, n - sh, axis=0), 0.0)
        else:
            a = a + jnp.where(row >= sh, pltpu.roll(a, sh, axis=0), 0.0)
        sh *= 2
    return a


def _chunk_common(xa_ref, dtw_ref, a_row, ex):
    L = SSD_CHUNK
    xs = xa_ref[:, 0:SSD_W]
    dtv = dtw_ref[...]
    acs = _sub_cumsum(dtv * a_row)
    last = acs[L - 1:L, :]
    full = _expand(jnp.concatenate([dtv, jnp.exp(last - acs), jnp.exp(acs),
                                    jnp.broadcast_to(jnp.exp(last), (8, LANE))], axis=0), ex)
    dt_full, dec_full, e_full, elast_full = full[0:L], full[L:2 * L], full[2 * L:3 * L], full[3 * L:3 * L + 1]
    xd = xs * dt_full
    return xs, dtv, acs, last, dt_full, xd, dec_full, e_full, elast_full


def _decay_mask(acs, acsT, col):
    L = SSD_CHUNK
    diff = acs[:, col:col + 1] - acsT[col:col + 1, :]
    tril = lax.broadcasted_iota(jnp.int32, (L, L), 0) >= lax.broadcasted_iota(jnp.int32, (L, L), 1)
    return jnp.where(tril, jnp.exp(jnp.minimum(diff, 0.0)), 0.0)


def _half_mask(h):
    lane = lax.broadcasted_iota(jnp.int32, (1, LANE), 1)
    return ((lane // SSD_P) == (h % 2)).astype(F32)


def _ssd_chunk_fwd(xa, dtw, a_row, d_full, name):
    S = xa.shape[0]
    L, G = SSD_CHUNK, 2
    nc = S // L
    GW = SSD_W // G

    def body(xa_ref, dtw_ref, a_ref, d_ref, y_ref, hp_ref, state):
        @pl.when(pl.program_id(0) == 0)
        def _():
            state[...] = jnp.zeros_like(state)

        ex = _expand_mat()
        xs, dtv, acs, last, dt_full, xd, dec_full, e_full, elast_full = _chunk_common(xa_ref, dtw_ref, a_ref[...], ex)
        acsT = acs.T
        hp_ref[0] = state[...]
        for g in range(G):
            gc = slice(g * GW, (g + 1) * GW)
            Bg = xa_ref[:, SSD_W + g * SSD_N: SSD_W + (g + 1) * SSD_N]
            Cg = xa_ref[:, SSD_W + G * SSD_N + g * SSD_N: SSD_W + G * SSD_N + (g + 1) * SSD_N]
            cb = _dot(Cg, Bg, NT)
            y_off = e_full[:, gc] * _dot(Cg, state[:, gc])
            for hp in range(GW // LANE):
                pc = slice(g * GW + hp * LANE, g * GW + (hp + 1) * LANE)
                xd_pair = xd[:, pc]
                yp = y_off[:, hp * LANE:(hp + 1) * LANE] + d_ref[:, pc] * xs[:, pc]
                for h2 in range(2):
                    h = (g * GW + hp * LANE) // SSD_P + h2
                    m = cb * _decay_mask(acs, acsT, DT_LANE0 + h)
                    yp = yp + _dot(m, xd_pair * _half_mask(h))
                y_ref[:, pc] = yp
            st_new = _dot(Bg.T, xd[:, gc] * dec_full[:, gc])
            state[:, gc] = elast_full[:, gc] * state[:, gc] + st_new

    return pl.pallas_call(
        body, grid=(nc,),
        in_specs=[pl.BlockSpec((L, SSD_CONV_CH), lambda c: (c, 0)), pl.BlockSpec((L, LANE), lambda c: (c, 0)),
                  pl.BlockSpec((1, LANE), lambda c: (0, 0)), pl.BlockSpec((1, SSD_W), lambda c: (0, 0))],
        out_specs=[pl.BlockSpec((L, SSD_W), lambda c: (c, 0)), pl.BlockSpec((1, SSD_N, SSD_W), lambda c: (c, 0, 0))],
        out_shape=[jax.ShapeDtypeStruct((S, SSD_W), F32), jax.ShapeDtypeStruct((nc, SSD_N, SSD_W), F32)],
        scratch_shapes=[pltpu.VMEM((SSD_N, SSD_W), F32)],
        compiler_params=_params(("arbitrary",)), name=name)(xa, dtw, a_row, d_full)


def _ssd_chunk_bwd(xa, dtw, dy, hprev, a_row, d_full, name):
    S = xa.shape[0]
    L, G = SSD_CHUNK, 2
    nc = S // L
    GW = SSD_W // G

    def body(xa_ref, dtw_ref, dy_ref, hp_ref, a_ref, d_ref, dxa_ref, ddt_ref, da_ref, dd_ref, dstate):
        @pl.when(pl.program_id(0) == 0)
        def _():
            dstate[...] = jnp.zeros_like(dstate)
            da_ref[...] = jnp.zeros_like(da_ref)
            dd_ref[...] = jnp.zeros_like(dd_ref)

        ex, hs = _expand_mat(), _headsum_mat()
        a_row = a_ref[...]
        xs, dtv, acs, last, dt_full, xd, dec_full, e_full, elast_full = _chunk_common(xa_ref, dtw_ref, a_row, ex)
        acsT = acs.T
        dyv = dy_ref[...]
        lane = lax.broadcasted_iota(jnp.int32, (L, LANE), 1)
        sub = lax.broadcasted_iota(jnp.int32, (LANE, L), 0)
        dacs_c = jnp.zeros((L, LANE), F32)
        dacs_r = jnp.zeros((LANE, L), F32)
        dxd_parts, yoff_parts, dxdd_parts, hh_parts = [], [], [], []
        for g in range(G):
            gc = slice(g * GW, (g + 1) * GW)
            b0 = SSD_W + g * SSD_N
            c0 = SSD_W + G * SSD_N + g * SSD_N
            Bg = xa_ref[:, b0:b0 + SSD_N]
            Cg = xa_ref[:, c0:c0 + SSD_N]
            Hp = hp_ref[0, :, gc]
            dH = dstate[:, gc]
            cb = _dot(Cg, Bg, NT)
            Gm = _dot(Cg, Hp)
            yoff_parts.append(e_full[:, gc] * Gm)
            dG = e_full[:, gc] * dyv[:, gc]
            dC = _dot(dG, Hp, NT)
            dHp = _dot(Cg.T, dG)
            xdd = xd[:, gc] * dec_full[:, gc]
            dB = _dot(xdd, dH, NT)
            dxdd = _dot(Bg, dH)
            dxdd_parts.append(dxdd)
            hh_parts.append(dH * Hp)
            dstate[:, gc] = dHp + elast_full[:, gc] * dH
            dcb = jnp.zeros((L, L), F32)
            dxd_g = []
            for hp in range(GW // LANE):
                pc = slice(g * GW + hp * LANE, g * GW + (hp + 1) * LANE)
                xd_pair = xd[:, pc]
                dxd_pair = dxdd[:, hp * LANE:(hp + 1) * LANE] * dec_full[:, pc]
                for h2 in range(2):
                    h = (g * GW + hp * LANE) // SSD_P + h2
                    col = DT_LANE0 + h
                    lm = _decay_mask(acs, acsT, col)
                    m = cb * lm
                    dy_h = dyv[:, pc] * _half_mask(h)
                    dm = _dot(dy_h, xd_pair, NT)
                    dxd_pair = dxd_pair + _dot(m.T, dy_h)
                    wm = dm * m
                    dacs_c = dacs_c + jnp.where(lane == col, jnp.sum(wm, axis=1, keepdims=True), 0.0)
                    dacs_r = dacs_r - jnp.where(sub == col, jnp.sum(wm, axis=0, keepdims=True), 0.0)
                    dcb = dcb + dm * lm
                dxd_g.append(dxd_pair)
            dxd_parts.append(jnp.concatenate(dxd_g, axis=1))
            dxa_ref[:, c0:c0 + SSD_N] = dC + _dot(dcb, Bg)
            dxa_ref[:, b0:b0 + SSD_N] = dB + _dot(dcb.T, Cg)
        dxd = jnp.concatenate(dxd_parts, axis=1)
        y_off = jnp.concatenate(yoff_parts, axis=1)
        dxdd_full = jnp.concatenate(dxdd_parts, axis=1)
        hh = jnp.concatenate(hh_parts, axis=1)
        dxa_ref[:, 0:SSD_W] = d_ref[...] * dyv + dxd * dt_full
        sums = _headsum(jnp.concatenate([dyv * xs, dxd * xs, dxdd_full * xd, hh, dyv * y_off], axis=0), hs)
        hs_skip, ddt, hs_dec, hs_state, hs_off = (sums[j * L:(j + 1) * L] for j in range(5))
        dd_ref[...] += jnp.sum(hs_skip, axis=0, keepdims=True)
        w_dec = hs_dec * jnp.exp(last - acs)
        dlast = jnp.sum(w_dec, axis=0, keepdims=True) + jnp.exp(last) * jnp.sum(hs_state, axis=0, keepdims=True)
        dacs = dacs_c + dacs_r.T + hs_off - w_dec
        rowid = lax.broadcasted_iota(jnp.int32, (L, LANE), 0)
        dacs = dacs + jnp.where(rowid == L - 1, dlast, 0.0)
        da = _sub_cumsum(dacs, reverse=True)
        ddt_ref[...] = ddt + da * a_row
        da_ref[...] += jnp.sum(da * dtv, axis=0, keepdims=True)

    rev = lambda c: (nc - 1 - c, 0)
    return pl.pallas_call(
        body, grid=(nc,),
        in_specs=[pl.BlockSpec((L, SSD_CONV_CH), rev), pl.BlockSpec((L, LANE), rev), pl.BlockSpec((L, SSD_W), rev),
                  pl.BlockSpec((1, SSD_N, SSD_W), lambda c: (nc - 1 - c, 0, 0)),
                  pl.BlockSpec((1, LANE), lambda c: (0, 0)), pl.BlockSpec((1, SSD_W), lambda c: (0, 0))],
        out_specs=[pl.BlockSpec((L, SSD_CONV_CH), rev), pl.BlockSpec((L, LANE), rev),
                   pl.BlockSpec((1, LANE), lambda c: (0, 0)), pl.BlockSpec((1, LANE), lambda c: (0, 0))],
        out_shape=[jax.ShapeDtypeStruct((S, SSD_CONV_CH), F32), jax.ShapeDtypeStruct((S, LANE), F32),
                   jax.ShapeDtypeStruct((1, LANE), F32), jax.ShapeDtypeStruct((1, LANE), F32)],
        scratch_shapes=[pltpu.VMEM((SSD_N, SSD_W), F32)],
        compiler_params=_params(("arbitrary",)), name=name)(xa, dtw, dy, hprev, a_row, d_full)


def _ssd_post_fwd(y, z, w, name):
    S = y.shape[0]
    GW = SSD_W // 2

    def body(i, n, R, C, P, N, O, A, Sc):
        zv = R[1][...]
        v = R[0][...] * (zv * _sigmoid(zv))
        for g in range(2):
            gc = slice(g * GW, (g + 1) * GW)
            vg = v[:, gc]
            r = lax.rsqrt(jnp.mean(vg * vg, axis=-1, keepdims=True) + NORM_EPS)
            O[0][:, gc] = (vg * r * C[0][:, gc]).astype(MXU_DTYPE)

    return _rows(body, S, _tile(S, (256, 128)), rows=[y, z], consts=[w], out_rows=[(SSD_W, MXU_DTYPE)], name=name)[0]


def _ssd_post_bwd(y, z, dyn, w, name):
    S = y.shape[0]
    GW = SSD_W // 2

    def body(i, n, R, C, P, N, O, A, Sc):
        yv, zv, dn = R[0][...], R[1][...], R[2][...]
        sz = zv * _sigmoid(zv)
        v = yv * sz
        for g in range(2):
            gc = slice(g * GW, (g + 1) * GW)
            dv, dw = _norm_bwd_math(v[:, gc], C[0][:, gc], dn[:, gc])
            A[0][:, gc] += dw
            O[0][:, gc] = dv * sz[:, gc]
            O[1][:, gc] = (dv * yv[:, gc] * _silu_grad(zv[:, gc])).astype(MXU_DTYPE)

    return _rows(body, S, _tile(S, (256, 128)), rows=[y, z, dyn], consts=[w],
                 out_rows=[(SSD_W, F32), (SSD_W, MXU_DTYPE)], out_accs=[((1, SSD_W), F32)], name=name)


def _merge_fwd(gl, yp, ya, ys, name):
    S, D = yp.shape

    def body(i, n, R, C, P, N, O, A, Sc):
        def tile(r0, first, last):
            rows = pl.ds(r0, SUB_ROWS)
            for c0 in range(0, D, CONV_COLS):
                cols = slice(c0, c0 + CONV_COLS)
                acc = None
                for b in range(3):
                    term = _sigmoid(R[0][rows, b * D + c0:b * D + c0 + CONV_COLS]) * R[1 + b][rows, cols]
                    acc = term if acc is None else acc + term
                O[0][rows, cols] = acc.astype(MXU_DTYPE)

        _tile_loop(T, tile)

    T = _tile(S, (256, 128))
    return _rows(body, S, T, rows=[gl, yp, ya, ys], out_rows=[(D, MXU_DTYPE)], name=name)[0]


def _merge_bwd(gl, yp, ya, ys, dm, name):
    S, D = yp.shape
    T = _tile(S, (256, 128))

    def body(i, n, R, C, P, N, O, A, Sc):
        def tile(r0, first, last):
            rows = pl.ds(r0, SUB_ROWS)
            for c0 in range(0, D, CONV_COLS):
                cols = slice(c0, c0 + CONV_COLS)
                dmv = R[4][rows, cols]
                for b in range(3):
                    gcols = slice(b * D + c0, b * D + c0 + CONV_COLS)
                    gt = _sigmoid(R[0][rows, gcols])
                    O[b][rows, cols] = (gt * dmv).astype(MXU_DTYPE)
                    O[3][rows, gcols] = (dmv * R[1 + b][rows, cols] * gt * (1.0 - gt)).astype(MXU_DTYPE)

        _tile_loop(T, tile)

    return _rows(body, S, T, rows=[gl, yp, ya, ys, dm],
                 out_rows=[(D, MXU_DTYPE)] * 3 + [(3 * D, MXU_DTYPE)], name=name)


FFN_COLS = 256


def _ffn_act_fwd(hpre, cw, cb, name):
    S, C = hpre.shape
    K = cw.shape[0]
    T = _tile(S, (256, 128))
    Fd = C // 2

    def body(i, n, R, Cs, P, N, O, A, Sc):
        def tile(r0, first, last):
            for c0 in range(0, Fd, FFN_COLS):
                gcols, vcols = slice(c0, c0 + FFN_COLS), slice(Fd + c0, Fd + c0 + FFN_COLS)
                gt = _conv_win(_taps(_win_prev(R[0], P[0], i, r0, first, gcols), K), Cs[0], Cs[1], gcols)
                val = _conv_win(_taps(_win_prev(R[0], P[0], i, r0, first, vcols), K), Cs[0], Cs[1], vcols)
                O[0][pl.ds(r0, SUB_ROWS), gcols] = (gt * _sigmoid(gt) * val).astype(MXU_DTYPE)

        _tile_loop(T, tile)

    return _rows(body, S, T, rows=[hpre], prevs=[hpre], consts=[cw, cb], out_rows=[(Fd, MXU_DTYPE)], name=name)[0]


def _ffn_act_bwd_a(hpre, dact, cw, cb, name):
    S, C = hpre.shape
    K = cw.shape[0]
    T = _tile(S, (256, 128))
    Fd = C // 2

    def body(i, n, R, Cs, P, N, O, A, Sc):
        def tile(r0, first, last):
            rows = pl.ds(r0, SUB_ROWS)
            for c0 in range(0, Fd, FFN_COLS):
                gcols, vcols = slice(c0, c0 + FFN_COLS), slice(Fd + c0, Fd + c0 + FFN_COLS)
                gwin = _taps(_win_prev(R[0], P[0], i, r0, first, gcols), K)
                vwin = _taps(_win_prev(R[0], P[0], i, r0, first, vcols), K)
                gt = _conv_win(gwin, Cs[0], Cs[1], gcols)
                val = _conv_win(vwin, Cs[0], Cs[1], vcols)
                da = R[1][rows, gcols]
                for cols, win, d in ((gcols, gwin, da * val * _silu_grad(gt)), (vcols, vwin, da * gt * _sigmoid(gt))):
                    O[0][rows, cols] = d
                    _conv_wgrad_win(A[0], win, d, cols)
                    A[1][:, cols] += jnp.sum(d, axis=0, keepdims=True)

        _tile_loop(T, tile)

    return _rows(body, S, T, rows=[hpre, dact], prevs=[hpre], consts=[cw, cb], out_rows=[(C, F32)],
                 out_accs=[((K, C), F32), ((1, C), F32)], name=name)


def _adamw_math(g, w, m, v):
    c1 = 1.0 - ADAM_B1 ** ADAM_STEP
    c2 = 1.0 - ADAM_B2 ** ADAM_STEP
    mn = ADAM_B1 * m + (1.0 - ADAM_B1) * g
    vn = ADAM_B2 * v + (1.0 - ADAM_B2) * (g * g)
    return -ADAM_LR * ((mn / c1) / (jnp.sqrt(vn / c2) + ADAM_EPS) + ADAM_WD * w), mn, vn


def _adamw(g, w, m, v, name):
    R_, W = g.shape

    def body(i, n, R, C, P, N, O, A, Sc):
        O[0][...], O[1][...], O[2][...] = _adamw_math(R[0][...], R[1][...], R[2][...], R[3][...])

    return _rows(body, R_, _row_tile(R_, W, 7), rows=[g, w, m, v], out_rows=[(W, F32)] * 3, name=name)


def _adamw_pair(q, other, w, m, v, pos, name):
    _, R_, W = w.shape
    T = _row_tile(R_, W, 9)

    def body(pos_ref, q_ref, o_ref, w_ref, m_ref, v_ref, g_out, d_out, m_out, v_out):
        g = jnp.where(pl.program_id(0) == pos_ref[1], q_ref[...], o_ref[...])
        g_out[0] = g
        d_out[0], m_out[0], v_out[0] = _adamw_math(g, w_ref[0], m_ref[0], v_ref[0])

    flat = pl.BlockSpec((T, W), lambda l, i, pos: (i, 0))
    full = pl.BlockSpec((1, T, W), lambda l, i, pos: (l, i, 0))
    return _scalar_call(body, pos, (2, R_ // T), [flat, flat, full, full, full], [full] * 4,
                        [jax.ShapeDtypeStruct(w.shape, F32)] * 4, (q, other, w, m, v), name)


def _row_tile(rows, width, n_blocks, budget=14 * 1024 * 1024):
    wpad = -(-width // LANE) * LANE
    for t in (512, 256, 128, 64, 32, 16, 8):
        if rows % t == 0 and n_blocks * t * wpad * 4 <= budget:
            return t
    return rows


_ANY = pl.BlockSpec(memory_space=pl.ANY)
_MESH = pl.DeviceIdType.MESH


DMA_CHUNK_BYTES = 2 * 1024 * 1024


def _row_chunks(shape, dtype):
    r = shape[-2]
    total = 1
    for s in shape:
        total *= s
    want = max(1, (total * jnp.dtype(dtype).itemsize) // DMA_CHUNK_BYTES)
    n = 1
    while n * 2 <= want and r % (n * 2 * 16) == 0 and n < 8:
        n *= 2
    return [(j * (r // n), r // n) for j in range(n)]


def _comm_call(plan, srcs, out_shapes, name, by_core=False):
    n = len(srcs)
    n_remote = len(plan(0, 0, 0, [_ShapeOnly(s.shape) for s in srcs], [_ShapeOnly(s.shape) for s in out_shapes]))

    def body(*refs):
        src_refs, out_refs = refs[:n], refs[n:n + len(out_shapes)]
        send_sems, recv_sems = refs[n + len(out_shapes):]
        x, y, c = lax.axis_index("x"), lax.axis_index("y"), lax.axis_index("c")

        def run(core):
            sent = []
            for j, (s, d, peer) in enumerate(plan(x, y, core, src_refs, out_refs)):
                cp = pltpu.make_async_remote_copy(src_ref=s, dst_ref=d, send_sem=send_sems.at[j],
                                                  recv_sem=recv_sems.at[j], device_id=peer, device_id_type=_MESH)
                cp.start()
                sent.append(cp)
            for cp in sent:
                cp.wait()

        if by_core:
            for core in (0, 1):
                pl.when(c == core)(functools.partial(run, core))
        else:
            run(c)

    return pl.pallas_call(
        body, in_specs=[_ANY] * n, out_specs=[_ANY] * len(out_shapes),
        out_shape=[jax.ShapeDtypeStruct(s.shape, s.dtype) for s in out_shapes],
        scratch_shapes=[pltpu.SemaphoreType.DMA((n_remote,)), pltpu.SemaphoreType.DMA((n_remote,))], name=name)(*srcs)


class _ShapeOnly:
    def __init__(self, shape):
        self.shape = tuple(shape)

    @property
    def at(self):
        return self

    def __getitem__(self, idx):
        return self


def _other_places(x, y):
    return [(1 - x, y), (x, 1 - y), (1 - x, 1 - y)]


def _gather_places(shards, row_major, name):
    n = len(shards)
    outs = []
    for s, rm in zip(shards, row_major):
        L_, r, c_ = s.shape
        assert L_ == 2
        outs.append(jax.ShapeDtypeStruct((L_, N_PLACES, r, c_) if rm else (N_PLACES, L_, r, c_), s.dtype))
    n_copies = 3 * sum(len(_row_chunks(s.shape[1:], s.dtype)) for s in shards)

    def body(*refs):
        src_refs, out_refs = refs[:n], refs[n:2 * n]
        ici_send, ici_recv, d2d_send, d2d_recv = refs[2 * n:]
        x, y, c = lax.axis_index("x"), lax.axis_index("y"), lax.axis_index("c")
        me = 2 * x + y

        def slot(o_ref, rm, place, layer, r0, rn):
            return o_ref.at[layer, place, pl.ds(r0, rn), :] if rm else o_ref.at[place, layer, pl.ds(r0, rn), :]

        over_ici, landed = [], []
        for s_ref, o_ref, rm, s in zip(src_refs, out_refs, row_major, shards):
            for r0, rn in _row_chunks(s.shape[1:], s.dtype):
                for px, py in _other_places(x, y):
                    j = len(over_ici)
                    cp = pltpu.make_async_remote_copy(
                        src_ref=s_ref.at[c, pl.ds(r0, rn), :], dst_ref=slot(o_ref, rm, me, c, r0, rn),
                        send_sem=ici_send.at[j], recv_sem=ici_recv.at[j], device_id=(px, py, c), device_id_type=_MESH)
                    cp.start()
                    over_ici.append(cp)
                    landed.append((o_ref, rm, 2 * px + py, r0, rn))
        passed = []
        for j, (o_ref, rm, place, r0, rn) in enumerate(landed):
            pltpu.make_async_remote_copy(
                src_ref=slot(o_ref, rm, place, c, r0, rn), dst_ref=slot(o_ref, rm, place, c, r0, rn),
                send_sem=ici_send.at[j], recv_sem=ici_recv.at[j], device_id=(x, y, c), device_id_type=_MESH).wait_recv()
            cp = pltpu.make_async_remote_copy(
                src_ref=slot(o_ref, rm, place, c, r0, rn), dst_ref=slot(o_ref, rm, place, c, r0, rn),
                send_sem=d2d_send.at[j], recv_sem=d2d_recv.at[j], device_id=(x, y, 1 - c), device_id_type=_MESH)
            cp.start()
            passed.append(cp)
        for j, (o_ref, rm, place, r0, rn) in enumerate(landed):
            pltpu.make_async_remote_copy(
                src_ref=slot(o_ref, rm, place, 1 - c, r0, rn), dst_ref=slot(o_ref, rm, place, 1 - c, r0, rn),
                send_sem=d2d_send.at[j], recv_sem=d2d_recv.at[j], device_id=(x, y, 1 - c), device_id_type=_MESH).wait_recv()
        for cp in over_ici + passed:
            cp.wait_send()

    return pl.pallas_call(
        body, in_specs=[_ANY] * n, out_specs=[_ANY] * n, out_shape=outs,
        scratch_shapes=[pltpu.SemaphoreType.DMA((n_copies,))] * 4, name=name)(*shards)


def _reduce_sibling(pairs, name):
    outs = [jax.ShapeDtypeStruct(g0.shape, g0.dtype) for g0, _ in pairs]

    def plan(x, y, c, src_refs, out_refs):
        remote = []
        for t, (o_ref, (g0, _)) in enumerate(zip(out_refs, pairs)):
            g_ref = src_refs[2 * t + (1 - c)]
            for r0, rn in _row_chunks(g0.shape[1:], g0.dtype):
                for p in range(N_PLACES):
                    remote.append((g_ref.at[p, pl.ds(r0, rn), :], o_ref.at[p, pl.ds(r0, rn), :], (x, y, 1 - c)))
        return remote

    return _comm_call(plan, [g for pair in pairs for g in pair], outs, name, by_core=True)


def _reduce_places(hs, name):
    outs = [jax.ShapeDtypeStruct((3,) + h.shape[1:], h.dtype) for h in hs]

    def plan(x, y, c, src_refs, out_refs):
        remote = []
        for h_ref, o_ref, h in zip(src_refs, out_refs, hs):
            for r0, rn in _row_chunks(h.shape[1:], h.dtype):
                for j, (px, py) in enumerate(_other_places(x, y)):
                    remote.append((h_ref.at[2 * px + py, pl.ds(r0, rn), :], o_ref.at[j, pl.ds(r0, rn), :], (px, py, c)))
        return remote

    return _comm_call(plan, hs, outs, name)


def _swap_sibling(qs, name):
    def plan(x, y, c, src_refs, out_refs):
        remote = []
        for q_ref, o_ref, q in zip(src_refs, out_refs, qs):
            for r0, rn in _row_chunks(q.shape, q.dtype):
                remote.append((q_ref.at[pl.ds(r0, rn), :], o_ref.at[pl.ds(r0, rn), :], (x, y, 1 - c)))
        return remote

    return _comm_call(plan, qs, qs, name)


def _scalar_call(body, scalars, grid, in_specs, out_specs, out_shape, args, name):
    return pl.pallas_call(
        body, grid_spec=pltpu.PrefetchScalarGridSpec(num_scalar_prefetch=1, grid=grid, in_specs=in_specs,
                                                     out_specs=out_specs),
        out_shape=out_shape, compiler_params=_params(("arbitrary",) * len(grid)), name=name)(scalars, *args)


def _own_layer(pos_ref, g0_ref, g1_ref):
    return jnp.where(pos_ref[1] == 0, g0_ref[...], g1_ref[...])


def _add_own_slot(pair, r_, pos, out_dtype, name):
    P_, R_, W = r_.shape
    T = _row_tile(R_, W, 4)

    def body(pos_ref, g0_ref, g1_ref, r_ref, o_ref):
        o_ref[...] = (_own_layer(pos_ref, g0_ref, g1_ref) + r_ref[...]).astype(out_dtype)

    blk = pl.BlockSpec((1, T, W), lambda p, i, pos: (p, i, 0))
    return _scalar_call(body, pos, (P_, R_ // T), [blk, blk, blk], blk,
                        jax.ShapeDtypeStruct((P_, R_, W), out_dtype), (pair[0], pair[1], r_), name)


def _sum_places(pair, r_, recv, pos, name):
    _, R_, W = r_.shape
    T = _row_tile(R_, W, 6)

    def body(pos_ref, g0_ref, g1_ref, r_ref, recv_ref, o_ref):
        for m in range(N_PLACES):
            @pl.when(pos_ref[0] == m)
            def _(m=m):
                acc = None
                for p in range(N_PLACES):
                    if p == m:
                        term = _own_layer(pos_ref, g0_ref, g1_ref)[0] + r_ref[0]
                    else:
                        dx, dy = (p >> 1) != (m >> 1), (p & 1) != (m & 1)
                        term = recv_ref[0 if (dx and not dy) else 1 if (dy and not dx) else 2].astype(F32)
                    acc = term if acc is None else acc + term
                o_ref[...] = acc

    mine = pl.BlockSpec((1, T, W), lambda i, pos: (pos[0], i, 0))
    return _scalar_call(
        body, pos, (R_ // T,), [mine, mine, mine, pl.BlockSpec((3, T, W), lambda i, pos: (0, i, 0))],
        pl.BlockSpec((T, W), lambda i, pos: (i, 0)), jax.ShapeDtypeStruct((R_, W), F32),
        (pair[0], pair[1], r_, recv), name)


def _lane_tile(vec16):
    return jnp.concatenate([jnp.zeros((DT_LANE0,), F32), vec16,
                            jnp.zeros((LANE - DT_LANE0 - SSD_HEADS,), F32)])[None]


def _layer_consts(W, l):
    return dict(
        norm_mix=W['norm_mix'][l][None], mix=W['pool_mix'][l].astype(MXU_DTYPE), scale=W['pool_scale'][l][None],
        f_bias=W['f_bias'][l][:, None], cw=W['ssd_conv_w'][l], cb=W['ssd_conv_b'][l][None],
        dtb=_lane_tile(W['ssd_dt_bias'][l]), a_row=_lane_tile(-jnp.exp(W['ssd_a_log'][l])),
        d_full=jnp.repeat(W['ssd_d'][l], SSD_P)[None], ssd_norm=W['ssd_norm'][l][None],
        norm_ffn=W['norm_ffn'][l][None], fcw=W['ffn_conv_w'][l], fcb=W['ffn_conv_b'][l][None])


def _layer_fwd(x, W, l):
    n = f"l{l}_"
    cs = _layer_consts(W, l)
    win = {k: v[l] for k, v in W['w_in'].items()}
    u = _norm_fwd(x, cs['norm_mix'], n + "norm_mix")
    pqkv = _mm(u, win['p'], name=n + "in_p")
    z = _mm(u, win['z'], name=n + "in_z")
    xbc = _mm(u, win['x'], name=n + "in_x")
    gl = _mm(u, win['g'], name=n + "in_g")
    fdt = _mm(u, win['f'], name=n + "in_f")
    d, ypm = _pool_fwd(pqkv, cs['mix'], cs['scale'], n + "pool")
    yp = _mm(ypm, W['p_pool'][l], name=n + "p_pool")
    fT = fdt[:, :HEADS].T
    c = _logf_cumsum(fT, cs['f_bias'], n + "logf")
    c_row = c[:, None, :]
    o, lse = _attn_fwd(pqkv, c_row, n + "attn")
    ya = _mm(o, W['p_attn'][l], name=n + "p_attn")
    xa, dtw = _ssd_pre_fwd(xbc, fdt, cs['cw'], cs['cb'], cs['dtb'], n + "ssd_pre")
    y, hprev = _ssd_chunk_fwd(xa, dtw, cs['a_row'], cs['d_full'], n + "ssd_scan")
    yn = _ssd_post_fwd(y, z, cs['ssd_norm'], n + "ssd_post")
    ys = _mm(yn, W['p_ssd'][l], name=n + "p_ssd")
    merged = _merge_fwd(gl, yp, ya, ys, n + "merge")
    x1 = _mm(merged, W['w_out'][l], acc=x, name=n + "w_out")
    u2 = _norm_fwd(x1, cs['norm_ffn'], n + "norm_ffn")
    hpre = _mm(u2, W['ffn_up'][l], name=n + "ffn_up")
    act = _ffn_act_fwd(hpre, cs['fcw'], cs['fcb'], n + "ffn_act")
    x2 = _mm(act, W['ffn_down'][l], acc=x1, name=n + "ffn_down")
    saved = dict(x=x, u=u, pqkv=pqkv, z=z, xbc=xbc, gl=gl, fdt=fdt, d=d, ypm=ypm, yp=yp, fT=fT,
                 c_row=c_row, o=o, lse=lse, ya=ya, xa=xa, dtw=dtw, y=y, hprev=hprev,
                 yn=yn, ys=ys, merged=merged, x1=x1, u2=u2, hpre=hpre, act=act, win=win, cs=cs)
    return x2, saved


def _layer_bwd(dx2, sv, W, l):
    n = f"l{l}_b_"
    cs, win = sv['cs'], sv['win']
    g = {}
    dact = _mm(dx2, W['ffn_down'][l], tb=True, name=n + "ffn_down_dx")
    g['ffn_down'] = _mm(sv['act'], dx2, ta=True, name=n + "ffn_down_dw")
    dhc, g['ffn_conv_w'], dfcb = _ffn_act_bwd_a(sv['hpre'], dact, cs['fcw'], cs['fcb'], n + "ffn_act_a")
    g['ffn_conv_b'] = dfcb[0]
    dhpre = _conv_bwd_b(dhc, cs['fcw'], n + "ffn_act_b")
    du2 = _mm(dhpre, W['ffn_up'][l], tb=True, name=n + "ffn_up_dx")
    g['ffn_up'] = _mm(sv['u2'], dhpre, ta=True, name=n + "ffn_up_dw")
    dx1, dnf = _norm_bwd(sv['x1'], cs['norm_ffn'], du2, dx2, n + "norm_ffn")
    g['norm_ffn'] = dnf[0]
    dm = _mm(dx1, W['w_out'][l], tb=True, name=n + "w_out_dx")
    g['w_out'] = _mm(sv['merged'], dx1, ta=True, name=n + "w_out_dw")
    dyp, dya, dys, dgl = _merge_bwd(sv['gl'], sv['yp'], sv['ya'], sv['ys'], dm, n + "merge")
    dypm = _mm(dyp, W['p_pool'][l], tb=True, name=n + "p_pool_dx")
    g['p_pool'] = _mm(sv['ypm'], dyp, ta=True, name=n + "p_pool_dw")
    dd, dscale, dmix = _pool_bwd_a(dypm, sv['d'], cs['mix'], cs['scale'], n + "pool_a")
    g['pool_scale'] = dscale[0]
    g['pool_mix'] = dmix.reshape(len(POOL_WINDOWS), LANE, LANE)
    dpool_v = _pool_bwd_b(dd, n + "pool_b")
    do = _mm(dya, W['p_attn'][l], tb=True, name=n + "p_attn_dx")
    g['p_attn'] = _mm(sv['o'], dya, ta=True, name=n + "p_attn_dw")
    dq, dk, dv, dc, dcq = _attn_bwd(sv['pqkv'], do, sv['o'], sv['c_row'], sv['lse'], n + "attn")
    dcq = dcq.reshape(-1, HEADS // 2, LANE)[:, :, :2].reshape(-1, HEADS).T
    dfT, dfb = _logf_cumsum_bwd(sv['fT'], cs['f_bias'], dc[:, 0, :], dcq, n + "logf")
    g['f_bias'] = dfb[:, 0]
    dpqkv = jnp.concatenate([dpool_v, (dq * ATTN_SCALE).astype(MXU_DTYPE), dk.astype(MXU_DTYPE),
                             dv.astype(MXU_DTYPE)], axis=1)
    dyn = _mm(dys, W['p_ssd'][l], tb=True, name=n + "p_ssd_dx")
    g['p_ssd'] = _mm(sv['yn'], dys, ta=True, name=n + "p_ssd_dw")
    dy, dz, dsn = _ssd_post_bwd(sv['y'], sv['z'], dyn, cs['ssd_norm'], n + "ssd_post")
    g['ssd_norm'] = dsn[0]
    dxa, ddtw, dA, dD = _ssd_chunk_bwd(sv['xa'], sv['dtw'], dy, sv['hprev'], cs['a_row'], cs['d_full'], n + "ssd_scan")
    heads = slice(DT_LANE0, DT_LANE0 + SSD_HEADS)
    g['ssd_a_log'] = dA[0, heads] * cs['a_row'][0, heads]
    g['ssd_d'] = dD[0, heads]
    dpre, ddt_raw, g['ssd_conv_w'], dcb, ddtb = _ssd_pre_bwd_a(sv['xbc'], sv['fdt'], dxa, ddtw, cs['cw'], cs['cb'],
                                                              cs['dtb'], n + "ssd_pre_a")
    g['ssd_conv_b'] = dcb[0]
    g['ssd_dt_bias'] = ddtb[0, heads]
    dxbc = _conv_bwd_b(dpre, cs['cw'], n + "ssd_pre_b")
    dfdt = jnp.concatenate([dfT.T, ddt_raw[:, HEADS:]], axis=1).astype(MXU_DTYPE)
    dsegs = dict(p=dpqkv, z=dz, x=dxbc, g=dgl, f=dfdt)
    du, dwin = None, {}
    for key in ('p', 'z', 'x', 'g', 'f'):
        du = _mm(dsegs[key], win[key], tb=True, acc=du, name=n + "in_dx_" + key)
        dwin[key] = _mm(sv['u'], dsegs[key], ta=True, name=n + "in_dw_" + key)
    g['w_in'] = dwin
    dx, dnm = _norm_bwd(sv['x'], cs['norm_mix'], du, dx1, n + "norm_mix")
    g['norm_mix'] = dnm[0]
    return dx, g


def _local_step(x, target, W):
    depth = W['norm_mix'].shape[0]
    saved = []
    h = x
    for l in range(depth):
        h, sv = _layer_fwd(h, W, l)
        saved.append(sv)
    dx, dwf, loss = _loss_head(h, W['norm_final'][None], target, "loss_head")
    grads = [None] * depth
    for l in reversed(range(depth)):
        dx, grads[l] = _layer_bwd(dx, saved[l], W, l)
    return loss[0, 0], dx, grads, dwf[0]


def _pack_rows(parts, row_align=1):
    flat = jnp.concatenate([p.reshape(-1) for p in parts])
    n = flat.shape[0]
    total = -(-n // (PACK_W * row_align)) * PACK_W * row_align
    if total > n:
        flat = jnp.concatenate([flat, jnp.zeros((total - n,), flat.dtype)])
    return flat.reshape(-1, PACK_W)


def _unpack_rows(buf, shapes):
    flat = buf.reshape(-1)
    out, pos = [], 0
    for shp in shapes:
        size = 1
        for s in shp:
            size *= s
        out.append(flat[pos:pos + size].reshape(shp))
        pos += size
    return out


def _to_place_major(gfull, name):
    R_, C = gfull.shape
    if name in COL_SHARDED:
        return gfull.reshape(R_, N_PLACES, C // N_PLACES).transpose(1, 0, 2)
    return gfull.reshape(N_PLACES, R_ // N_PLACES, C)


_W_IN_LAYOUT = (('p', 0, 0, 2048), ('f', 0, 2048, HEADS), ('z', 0, 2056, 1024), ('x', 0, 3080, 1536),
                ('f', DT_LANE0, 4616, SSD_HEADS), ('g', 0, 4632, 3072))


def _w_in_segments(slabs):
    starts = [0]
    for s in slabs:
        starts.append(starts[-1] + s.shape[-1])

    def cols(a, b):
        parts = []
        for s, s0 in zip(slabs, starts):
            lo, hi = max(a, s0), min(b, s0 + s.shape[-1])
            if lo < hi:
                parts.append(s[..., lo - s0:hi - s0])
        return parts[0] if len(parts) == 1 else jnp.concatenate(parts, axis=-1)

    pad = jnp.zeros(slabs[0].shape[:-1] + (LANE - DT_LANE0 - SSD_HEADS,), slabs[0].dtype)
    return dict(p=cols(0, 2048), z=cols(2056, 3080), x=cols(3080, 4616), g=cols(4632, 7704),
                f=jnp.concatenate([cols(2048, 2056), cols(4616, 4632), pad], axis=-1))


def _w_in_columns(segs, a, b):
    parts = []
    for key, s0, g0, w in _W_IN_LAYOUT:
        lo, hi = max(a, g0), min(b, g0 + w)
        if lo < hi:
            parts.append(segs[key][..., s0 + lo - g0:s0 + hi - g0])
    return parts[0] if len(parts) == 1 else jnp.concatenate(parts, axis=-1)


def kernel(x, norm_mix, w_in, pool_mix, pool_scale, f_bias, ssd_conv_w, ssd_conv_b, ssd_dt_bias, ssd_a_log, ssd_d, ssd_norm, p_pool, p_attn, p_ssd, w_out, norm_ffn, ffn_up, ffn_conv_w, ffn_conv_b, ffn_down, norm_final, loss_target, m_norm_mix, m_w_in, m_pool_mix, m_pool_scale, m_f_bias, m_ssd_conv_w, m_ssd_conv_b, m_ssd_dt_bias, m_ssd_a_log, m_ssd_d, m_ssd_norm, m_p_pool, m_p_attn, m_p_ssd, m_w_out, m_norm_ffn, m_ffn_up, m_ffn_conv_w, m_ffn_conv_b, m_ffn_down, m_norm_final, v_norm_mix, v_w_in, v_pool_mix, v_pool_scale, v_f_bias, v_ssd_conv_w, v_ssd_conv_b, v_ssd_dt_bias, v_ssd_a_log, v_ssd_d, v_ssd_norm, v_p_pool, v_p_attn, v_p_ssd, v_w_out, v_norm_ffn, v_ffn_up, v_ffn_conv_w, v_ffn_conv_b, v_ffn_down, v_norm_final):
    args = dict(locals())
    w_sh = {k: args[k] for k in WEIGHTS}
    m_sh = {k: args['m_' + k] for k in WEIGHTS}
    v_sh = {k: args['v_' + k] for k in WEIGHTS}
    depth = norm_mix.shape[0]
    place = 2 * lax.axis_index("x") + lax.axis_index("y")
    row_sharded = [k for k in BIG if k not in COL_SHARDED]

    sent = {k: w_sh[k].astype(MXU_DTYPE) for k in BIG}
    sent.update({k: w_sh[k] for k in SMALL_SHARDED})
    gathered = _gather_places([sent[k] for k in BIG + SMALL_SHARDED],
                              [k in row_sharded for k in BIG + SMALL_SHARDED], "gather_weights")
    gathered = dict(zip(BIG + SMALL_SHARDED, gathered))
    W = {k: w_sh[k] for k in SMALL if k not in SMALL_SHARDED}
    zero = jnp.zeros((), jnp.int32)
    for k in BIG + SMALL_SHARDED:
        if k in row_sharded:
            gk = lax.dynamic_update_slice(gathered[k], sent[k][:, None], (zero, place, zero, zero))
            W[k] = gk.reshape(gk.shape[0], -1, gk.shape[-1])
            continue
        gk = lax.dynamic_update_slice(gathered[k], sent[k][None], (place, zero, zero, zero))
        if k == 'w_in':
            W[k] = _w_in_segments([gk[p] for p in range(N_PLACES)])
        else:
            W[k] = jnp.concatenate([gk[p] for p in range(N_PLACES)], axis=-1)

    loss_local, grad_x, grads, g_final = _local_step(x[0], loss_target[0], W)
    loss = lax.psum(loss_local, ("x", "y", "c"))

    assert depth == 2
    def place_major(k, l):
        if k == 'w_in':
            c = IN_TOTAL // N_PLACES
            return jnp.stack([_w_in_columns(grads[l][k], p * c, (p + 1) * c) for p in range(N_PLACES)])
        return _to_place_major(grads[l][k], k)

    g_big = [[place_major(k, l) for l in range(depth)] for k in BIG]
    small_names = [k for k in SMALL if k != 'norm_final'] + ['norm_final']
    small_full = [jnp.stack([grads[l][k] for l in range(depth)]) for k in small_names[:-1]] + [g_final]
    small_full_shapes = [a.shape for a in small_full]
    small_packed = _pack_rows(small_full, 32).reshape(2, -1, PACK_W)
    g_small = [jnp.broadcast_to(small_packed[h][None], (N_PLACES,) + small_packed.shape[1:]) for h in range(2)]

    core = lax.axis_index("c")
    pos = jnp.stack([place, core]).astype(jnp.int32)
    g_all = g_big + [g_small]
    theirs = _reduce_sibling(g_all, "reduce_sibling")
    wire = [WIRE_DTYPE] * len(BIG) + [F32]
    halves = [_add_own_slot(g, t, pos, dt, f"reduce_sibling_add{j}")
              for j, (g, t, dt) in enumerate(zip(g_all, theirs, wire))]
    recv = _reduce_places(halves, "reduce_places")
    qs = [_sum_places(g, t, r, pos, f"reduce_places_add{j}") for j, (g, t, r) in enumerate(zip(g_all, theirs, recv))]
    others = _swap_sibling(qs, "reduce_swap")

    def mine(k, a):
        if k in SMALL_SHARDED:
            c = a.shape[-1] // N_PLACES
            return lax.dynamic_slice_in_dim(a, place * c, c, axis=a.ndim - 1)
        return a

    outs = {}
    for j, k in enumerate(BIG):
        res = _adamw_pair(qs[j], others[j], w_sh[k], m_sh[k], v_sh[k], pos, "adamw_" + k)
        for prefix, a in zip(('grad_', 'delta_', 'new_m_', 'new_v_'), res):
            outs[prefix + k] = a
    small_sum = jnp.where(core == 0, jnp.concatenate([qs[-1], others[-1]]), jnp.concatenate([others[-1], qs[-1]]))
    g_small_list = [mine(k, a) for k, a in zip(small_names, _unpack_rows(small_sum, small_full_shapes))]
    shapes = [a.shape for a in g_small_list]
    gp = _pack_rows(g_small_list, 128)
    wp, mp, vp = (_pack_rows([d[k] for k in small_names], 128) for d in (w_sh, m_sh, v_sh))
    delta_p, m_p, v_p = _adamw(gp, wp, mp, vp, "adamw_small")
    for prefix, buf in (('grad_', gp), ('delta_', delta_p), ('new_m_', m_p), ('new_v_', v_p)):
        for k, a in zip(small_names, _unpack_rows(buf, shapes)):
            outs[prefix + k] = a
    result = [loss, grad_x[None]]
    for prefix in ('grad_', 'delta_', 'new_m_', 'new_v_'):
        result += [outs[prefix + k] for k in WEIGHTS]
    return tuple(result)
```

```python
import functools

import jax
import jax.numpy as jnp
from jax import lax
from jax.experimental import pallas as pl
from jax.experimental.pallas import tpu as pltpu

F32 = jnp.float32
MXU_DTYPE = jnp.bfloat16
WIRE_DTYPE = jnp.bfloat16
NORM_EPS = 1e-6
HALO = 16
LANE = 128
NEG_BIG = -1e30
VMEM_LIMIT = 52 * 1024 * 1024

D_MODEL = 1024
POOL_WINDOWS = (2, 4, 8, 16)
POOL_W = 512
HEADS = 8
HEAD_DIM = 64
ATTN_W = 512
ATTN_SCALE = HEAD_DIM ** -0.5
SSD_W = 1024
SSD_HEADS = 16
SSD_P = 64
SSD_N = 128
SSD_CHUNK = 128
SSD_CONV_CH = 1536
FFN = 2816
DT_LANE0 = 8
IN_SPLITS = (512, 512, 512, 512, 8, 1024, 1536, 16, 3072)
IN_TOTAL = sum(IN_SPLITS)
N_PLACES = 4

ADAM_LR, ADAM_B1, ADAM_B2, ADAM_EPS, ADAM_WD, ADAM_STEP = 0.001, 0.9, 0.999, 1e-08, 0.01, 10

BIG = ('w_in', 'p_pool', 'p_attn', 'p_ssd', 'w_out', 'ffn_up', 'ffn_down')
COL_SHARDED = ('w_in', 'p_pool', 'p_attn', 'ffn_up')
SMALL = ('norm_mix', 'pool_mix', 'pool_scale', 'f_bias', 'ssd_conv_w', 'ssd_conv_b', 'ssd_dt_bias',
         'ssd_a_log', 'ssd_d', 'ssd_norm', 'norm_ffn', 'ffn_conv_w', 'ffn_conv_b', 'norm_final')
SMALL_SHARDED = ('ssd_conv_w', 'ffn_conv_w')
WEIGHTS = ('norm_mix', 'w_in', 'pool_mix', 'pool_scale', 'f_bias', 'ssd_conv_w', 'ssd_conv_b', 'ssd_dt_bias',
           'ssd_a_log', 'ssd_d', 'ssd_norm', 'p_pool', 'p_attn', 'p_ssd', 'w_out', 'norm_ffn', 'ffn_up',
           'ffn_conv_w', 'ffn_conv_b', 'ffn_down', 'norm_final')
PACK_W = 1024


def _params(sem):
    return pltpu.CompilerParams(dimension_semantics=sem, vmem_limit_bytes=VMEM_LIMIT)


def _tile(n, prefs=(512, 256, 128)):
    for t in prefs:
        if n % t == 0:
            return t
    return n


def _sigmoid(x):
    return 0.5 * jnp.tanh(0.5 * x) + 0.5


def _softplus(x):
    return jnp.maximum(x, 0.0) + jnp.log1p(jnp.exp(-jnp.abs(x)))


def _dot(a, b, dims=((1,), (0,))):
    return lax.dot_general(a.astype(MXU_DTYPE), b.astype(MXU_DTYPE), (dims, ((), ())),
                           preferred_element_type=F32)


NT = ((1,), (1,))


def _mm(a, b, *, ta=False, tb=False, acc=None, out_dtype=F32, name):
    M, K = (a.shape[1], a.shape[0]) if ta else a.shape
    N = b.shape[0] if tb else b.shape[1]
    big = (1024, 1408, 512, 256, 128)
    tm, tn = _tile(M, big), _tile(N, big)
    tk = K if K <= 1024 else _tile(K, (512, 256, 128) if ta else big)
    nk = K // tk
    a_spec = pl.BlockSpec((tk, tm), lambda i, j, k: (k, i)) if ta else pl.BlockSpec((tm, tk), lambda i, j, k: (i, k))
    b_spec = pl.BlockSpec((tn, tk), lambda i, j, k: (j, k)) if tb else pl.BlockSpec((tk, tn), lambda i, j, k: (k, j))
    in_specs = [a_spec, b_spec]
    args = [a, b]
    if acc is not None:
        in_specs.append(pl.BlockSpec((tm, tn), lambda i, j, k: (i, j)))
        args.append(acc)

    def body(*refs):
        if acc is not None:
            a_ref, b_ref, c_ref, o_ref, acc_ref = refs
        else:
            a_ref, b_ref, o_ref, acc_ref = refs
        k = pl.program_id(2)

        @pl.when(k == 0)
        def _():
            if acc is not None:
                acc_ref[...] = c_ref[...].astype(F32)
            else:
                acc_ref[...] = jnp.zeros_like(acc_ref)

        av = a_ref[...]
        if ta:
            av = av.astype(F32).T
        acc_ref[...] += _dot(av, b_ref[...], NT if tb else ((1,), (0,)))

        @pl.when(k == nk - 1)
        def _():
            o_ref[...] = acc_ref[...].astype(out_dtype)

    return pl.pallas_call(
        body, grid=(M // tm, N // tn, nk), in_specs=in_specs,
        out_specs=pl.BlockSpec((tm, tn), lambda i, j, k: (i, j)),
        out_shape=jax.ShapeDtypeStruct((M, N), out_dtype),
        scratch_shapes=[pltpu.VMEM((tm, tn), F32)],
        compiler_params=_params(("parallel", "parallel", "arbitrary")), name=name)(*args)


def _rows(body, S, T, *, rows=(), consts=(), prevs=(), nexts=(), out_rows=(), out_accs=(), scratch=(), name):
    n = S // T
    hb = T // HALO
    last_h = S // HALO - 1

    def norm(r):
        return r if isinstance(r, tuple) else (r, r.shape[1], 0)

    rows, prevs, nexts = [norm(r) for r in rows], [norm(r) for r in prevs], [norm(r) for r in nexts]
    in_specs, args = [], []
    for arr, W, cb in rows:
        in_specs.append(pl.BlockSpec((T, W), lambda i, cb=cb: (i, cb)))
        args.append(arr)
    for cst in consts:
        in_specs.append(pl.BlockSpec(cst.shape, lambda i, nd=cst.ndim: (0,) * nd))
        args.append(cst)
    for arr, W, cb in prevs:
        in_specs.append(pl.BlockSpec((HALO, W), lambda i, cb=cb: (jnp.maximum(i * hb - 1, 0), cb)))
        args.append(arr)
    for arr, W, cb in nexts:
        in_specs.append(pl.BlockSpec((HALO, W), lambda i, cb=cb: (jnp.minimum((i + 1) * hb, last_h), cb)))
        args.append(arr)
    out_specs = [pl.BlockSpec((T, W), lambda i: (i, 0)) for W, _ in out_rows]
    out_specs += [pl.BlockSpec(shp, lambda i, nd=len(shp): (0,) * nd) for shp, _ in out_accs]
    out_shape = [jax.ShapeDtypeStruct((S, W), dt) for W, dt in out_rows]
    out_shape += [jax.ShapeDtypeStruct(shp, dt) for shp, dt in out_accs]
    cuts = [len(rows), len(consts), len(prevs), len(nexts), len(out_rows), len(out_accs), len(scratch)]

    def kern(*refs):
        groups, pos = [], 0
        for c in cuts:
            groups.append(list(refs[pos:pos + c]))
            pos += c
        i = pl.program_id(0)

        @pl.when(i == 0)
        def _():
            for a_ref in groups[5]:
                a_ref[...] = jnp.zeros_like(a_ref)

        body(i, n, *groups)

    outs = pl.pallas_call(kern, grid=(n,), in_specs=in_specs, out_specs=out_specs, out_shape=out_shape,
                          scratch_shapes=list(scratch), compiler_params=_params(("arbitrary",)), name=name)(*args)
    return outs


def _fill_prev(ext, prev_ref, cur, i):
    ext[0:HALO, :] = jnp.where(i > 0, prev_ref[...].astype(F32), 0.0)
    ext[HALO:, :] = cur


def _row_ids(i, T, W=1):
    return i * T + lax.broadcasted_iota(jnp.int32, (T, W), 0)


SUB_ROWS = 32
WIN_PAD = 8


def _tile_loop(T, fn):
    n = T // SUB_ROWS
    fn(0, True, n == 1)
    if n > 2:
        def body(rb, carry):
            fn(pl.multiple_of(rb * SUB_ROWS, SUB_ROWS), False, False)
            return carry
        lax.fori_loop(1, n - 1, body, 0)
    if n > 1:
        fn((n - 1) * SUB_ROWS, False, True)


def _win_prev(x_ref, prev_ref, i, r0, first, cols):
    if first:
        top = jnp.where(i > 0, prev_ref[HALO - WIN_PAD:HALO, cols].astype(F32), 0.0)
        return jnp.concatenate([top, x_ref[0:SUB_ROWS, cols].astype(F32)], axis=0)
    start = r0 - WIN_PAD if isinstance(r0, int) else pl.multiple_of(r0 - WIN_PAD, WIN_PAD)
    return x_ref[pl.ds(start, SUB_ROWS + WIN_PAD), cols].astype(F32)


def _behind(win, j):
    return win[WIN_PAD:, :] if j == 0 else pltpu.roll(win, j, axis=0)[WIN_PAD:, :]


def _win_next(x_ref, next_ref, i, n, r0, last, cols):
    if last:
        bot = jnp.where(i < n - 1, next_ref[0:WIN_PAD, cols].astype(F32), 0.0)
        return jnp.concatenate([x_ref[r0:r0 + SUB_ROWS, cols].astype(F32), bot], axis=0)
    return x_ref[pl.ds(r0, SUB_ROWS + WIN_PAD), cols].astype(F32)


def _ahead(win, j):
    return win[:SUB_ROWS, :] if j == 0 else pltpu.roll(win, SUB_ROWS + WIN_PAD - j, axis=0)[:SUB_ROWS, :]


def _taps(win, K):
    return [_behind(win, K - 1 - k) for k in range(K)]


def _conv_win(taps, w_ref, b_ref, cols):
    out = b_ref[:, cols]
    for k, tap in enumerate(taps):
        out = out + tap * w_ref[k:k + 1, cols]
    return out


def _conv_wgrad_win(acc_ref, taps, d, cols):
    for k, tap in enumerate(taps):
        acc_ref[k:k + 1, cols] += jnp.sum(d * tap, axis=0, keepdims=True)


def _norm_fwd(x, w, name):
    S, D = x.shape

    def body(i, n, R, C, P, N, O, A, Sc):
        xv = R[0][...]
        r = lax.rsqrt(jnp.mean(xv * xv, axis=-1, keepdims=True) + NORM_EPS)
        O[0][...] = (xv * r * C[0][...]).astype(MXU_DTYPE)

    return _rows(body, S, _tile(S), rows=[x], consts=[w], out_rows=[(D, MXU_DTYPE)], name=name)[0]


def _norm_bwd_math(xv, w, du):
    r = lax.rsqrt(jnp.mean(xv * xv, axis=-1, keepdims=True) + NORM_EPS)
    xh = xv * r
    g = du * w
    dx = r * (g - xh * jnp.mean(g * xh, axis=-1, keepdims=True))
    dw = jnp.sum(du * xh, axis=0, keepdims=True)
    return dx, dw


def _norm_bwd(x, w, du, dres, name):
    S, D = x.shape

    def body(i, n, R, C, P, N, O, A, Sc):
        dx, dw = _norm_bwd_math(R[0][...], C[0][...], R[1][...])
        O[0][...] = R[2][...] + dx
        A[0][...] += dw

    return _rows(body, S, _tile(S), rows=[x, du, dres], consts=[w], out_rows=[(D, F32)],
                 out_accs=[((1, D), F32)], name=name)


def _loss_head(x, w, target, name):
    S, D = x.shape

    def body(i, n, R, C, P, N, O, A, Sc):
        xv, w_, tg = R[0][...], C[0][...], R[1][...]
        r = lax.rsqrt(jnp.mean(xv * xv, axis=-1, keepdims=True) + NORM_EPS)
        e = xv * r * w_ - tg
        A[1][...] += jnp.broadcast_to(0.5 * jnp.sum(jnp.mean(e * e, axis=-1, keepdims=True)), (1, LANE))
        dx, dw = _norm_bwd_math(xv, w_, e / D)
        O[0][...] = dx
        A[0][...] += dw

    return _rows(body, S, _tile(S), rows=[x, target], consts=[w], out_rows=[(D, F32)],
                 out_accs=[((1, D), F32), ((1, LANE), F32)], name=name)


def _pool_fwd(pqkv, mix, scale, name):
    S = pqkv.shape[0]
    T = _tile(S, (256, 128))

    def body(i, n, R, C, P, N, O, A, Sc):
        ext = Sc[0]
        v = R[0][...]
        _fill_prev(ext, P[0], v, i)
        t1 = (_row_ids(i, T) + 1).astype(F32)
        for g, w in enumerate(POOL_WINDOWS):
            cols = slice(g * LANE, (g + 1) * LANE)
            acc = v[:, cols]
            for j in range(1, w):
                acc = acc + ext[pl.ds(HALO - j, T), cols]
            d = (acc / jnp.minimum(t1, float(w)) - v[:, cols]).astype(MXU_DTYPE)
            O[0][:, cols] = d
            O[1][:, cols] = (_dot(d, C[0][g]) * C[1][:, cols]).astype(MXU_DTYPE)

    return _rows(body, S, T, rows=[(pqkv, POOL_W, 0)], prevs=[(pqkv, POOL_W, 0)], consts=[mix, scale],
                 out_rows=[(POOL_W, MXU_DTYPE), (POOL_W, MXU_DTYPE)],
                 scratch=[pltpu.VMEM((HALO + T, POOL_W), F32)], name=name)


def _pool_bwd_a(dypm, d, mix, scale, name):
    S = d.shape[0]
    T = _tile(S, (256, 128))

    def body(i, n, R, C, P, N, O, A, Sc):
        for g in range(len(POOL_WINDOWS)):
            cols = slice(g * LANE, (g + 1) * LANE)
            dg = R[1][:, cols]
            dy = R[0][:, cols]
            yg = _dot(dg, C[0][g])
            A[0][:, cols] += jnp.sum(dy * yg, axis=0, keepdims=True)
            dys = dy * C[1][:, cols]
            A[1][cols, :] += _dot(dg.astype(F32).T, dys)
            O[0][:, cols] = _dot(dys, C[0][g], NT)

    return _rows(body, S, T, rows=[dypm, d], consts=[mix, scale], out_rows=[(POOL_W, F32)],
                 out_accs=[((1, POOL_W), F32), ((POOL_W, LANE), F32)], name=name)


def _pool_bwd_b(dd, name):
    S = dd.shape[0]
    T = _tile(S, (256, 128))

    def body(i, n, R, C, P, N, O, A, Sc):
        ext = Sc[0]
        ddv = R[0][...]
        t1 = (_row_ids(i, T) + 1).astype(F32)
        nxt = jnp.where(i < n - 1, N[0][...], 0.0)
        for g, w in enumerate(POOL_WINDOWS):
            cols = slice(g * LANE, (g + 1) * LANE)
            ext[0:T, cols] = ddv[:, cols] / jnp.minimum(t1, float(w))
            ext[T:, cols] = nxt[:, cols] / float(w)
        for g, w in enumerate(POOL_WINDOWS):
            cols = slice(g * LANE, (g + 1) * LANE)
            acc = ext[0:T, cols]
            for j in range(1, w):
                acc = acc + ext[pl.ds(j, T), cols]
            O[0][:, cols] = (acc - ddv[:, cols]).astype(MXU_DTYPE)

    return _rows(body, S, T, rows=[dd], nexts=[dd], out_rows=[(POOL_W, MXU_DTYPE)],
                 scratch=[pltpu.VMEM((T + HALO, POOL_W), F32)], name=name)[0]


def _lane_cumsum(seg, reverse=False):
    lane = lax.broadcasted_iota(jnp.int32, seg.shape, 1)
    sh = 1
    while sh < LANE:
        if reverse:
            seg = seg + jnp.where(lane < LANE - sh, pltpu.roll(seg, LANE - sh, axis=1), 0.0)
        else:
            seg = seg + jnp.where(lane >= sh, pltpu.roll(seg, sh, axis=1), 0.0)
        sh *= 2
    return seg


def _logf_cumsum(fT, bias, name):
    H, S = fT.shape
    TB = _tile(S)
    nb = S // TB

    def body(f_ref, b_ref, o_ref, carry):
        @pl.when(pl.program_id(0) == 0)
        def _():
            carry[...] = jnp.zeros_like(carry)

        x = f_ref[...] + b_ref[...]
        lf = jnp.minimum(x, 0.0) - jnp.log1p(jnp.exp(-jnp.abs(x)))
        c = carry[...]
        for j in range(TB // LANE):
            seg = _lane_cumsum(lf[:, j * LANE:(j + 1) * LANE]) + c
            o_ref[:, j * LANE:(j + 1) * LANE] = seg
            c = seg[:, LANE - 1:LANE]
        carry[...] = c

    return pl.pallas_call(
        body, grid=(nb,), in_specs=[pl.BlockSpec((H, TB), lambda i: (0, i)), pl.BlockSpec((H, 1), lambda i: (0, 0))],
        out_specs=pl.BlockSpec((H, TB), lambda i: (0, i)), out_shape=jax.ShapeDtypeStruct((H, S), F32),
        scratch_shapes=[pltpu.VMEM((H, 1), F32)], compiler_params=_params(("arbitrary",)), name=name)(fT, bias)


def _logf_cumsum_bwd(fT, bias, dc, dcq, name):
    H, S = fT.shape
    TB = _tile(S)
    nb = S // TB

    def body(f_ref, b_ref, dc_ref, dcq_ref, o_ref, db_ref, carry):
        @pl.when(pl.program_id(0) == 0)
        def _():
            carry[...] = jnp.zeros_like(carry)
            db_ref[...] = jnp.zeros_like(db_ref)

        x = f_ref[...] + b_ref[...]
        sg = _sigmoid(-x)
        dcv = dc_ref[...] + dcq_ref[...]
        c = carry[...]
        db = jnp.zeros((H, 1), F32)
        for j in reversed(range(TB // LANE)):
            seg = _lane_cumsum(dcv[:, j * LANE:(j + 1) * LANE], reverse=True) + c
            df = seg * sg[:, j * LANE:(j + 1) * LANE]
            o_ref[:, j * LANE:(j + 1) * LANE] = df
            db = db + jnp.sum(df, axis=1, keepdims=True)
            c = seg[:, 0:1]
        carry[...] = c
        db_ref[...] += db

    rev = lambda i: (0, nb - 1 - i)
    return pl.pallas_call(
        body, grid=(nb,),
        in_specs=[pl.BlockSpec((H, TB), rev), pl.BlockSpec((H, 1), lambda i: (0, 0)), pl.BlockSpec((H, TB), rev),
                  pl.BlockSpec((H, TB), rev)],
        out_specs=[pl.BlockSpec((H, TB), rev), pl.BlockSpec((H, 1), lambda i: (0, 0))],
        out_shape=[jax.ShapeDtypeStruct((H, S), F32), jax.ShapeDtypeStruct((H, 1), F32)],
        scratch_shapes=[pltpu.VMEM((H, 1), F32)], compiler_params=_params(("arbitrary",)), name=name)(
            fT, bias, dc, dcq)


def _attn_scores(q, k, ck, diagonal, T):
    s = _dot(q, k, NT) - ck
    if diagonal:
        tril = lax.broadcasted_iota(jnp.int32, (T, T), 1) <= lax.broadcasted_iota(jnp.int32, (T, T), 0)
        s = jnp.where(tril, s, NEG_BIG)
    return s


def _attn_fwd(pqkv, c_row, name, gather=(), gather_owner=1):
    S = pqkv.shape[0]
    T = _tile(S, (1024, 512, 256, 128))
    nq = S // T
    qb, kb, vb = (ATTN_W * j // LANE for j in (1, 2, 3))

    steps = jnp.asarray([[qi for qi in range(nq) for ki in range(qi + 1)],
                         [ki for qi in range(nq) for ki in range(qi + 1)]], jnp.int32)

    ng = len(gather)
    n_steps = steps.shape[1]

    def body(st_ref, q_ref, k_ref, v_ref, ck_ref, *rest):
        shard_refs, (o_ref, lse_ref), rest = rest[:ng], rest[ng:ng + 2], rest[ng + 2:]
        gathered_refs, (m_s, l_s, acc_s), sems = rest[:ng], rest[ng:ng + 3], rest[ng + 3:]
        qi, ki = st_ref[0, pl.program_id(1)], st_ref[1, pl.program_id(1)]

        if ng:
            start, finish = _gather_copies(shard_refs, gathered_refs, gather, sems, gather_owner)
            pl.when((pl.program_id(0) == 0) & (pl.program_id(1) == 0))(start)

        @pl.when(ki == 0)
        def _():
            m_s[...] = jnp.full_like(m_s, NEG_BIG)
            l_s[...] = jnp.zeros_like(l_s)
            acc_s[...] = jnp.zeros_like(acc_s)

        def step(diagonal):
            qf, kv, vv = q_ref[...] * ATTN_SCALE, k_ref[...].astype(MXU_DTYPE), v_ref[...].astype(MXU_DTYPE)
            acc = acc_s[...]
            for a in range(2):
                mk = _half_mask(a)
                s = _attn_scores(qf * mk, kv, ck_ref[a], diagonal, T)
                m_new = jnp.maximum(m_s[a], jnp.max(s, axis=1, keepdims=True))
                alpha = jnp.exp(m_s[a] - m_new)
                p = jnp.exp(s - m_new)
                l_s[a] = alpha * l_s[a] + jnp.sum(p, axis=1, keepdims=True)
                acc = acc * (1.0 + mk * (alpha - 1.0)) + _dot(p, vv) * mk
                m_s[a] = m_new
            acc_s[...] = acc

        @pl.when(ki < qi)
        def _():
            step(False)

        @pl.when(ki == qi)
        def _():
            step(True)
            o_ref[...] = acc_s[...] / (l_s[0] * _half_mask(0) + l_s[1] * _half_mask(1))
            for a in range(2):
                lse_ref[a] = m_s[a] + jnp.log(l_s[a])

        if ng:
            pl.when((pl.program_id(0) == HEADS // 2 - 1) & (pl.program_id(1) == n_steps - 1))(finish)

    outs = pl.pallas_call(
        body, grid_spec=pltpu.PrefetchScalarGridSpec(
            num_scalar_prefetch=1, grid=(HEADS // 2, n_steps),
            in_specs=[pl.BlockSpec((T, LANE), lambda hp, s, st: (st[0, s], qb + hp)),
                      pl.BlockSpec((T, LANE), lambda hp, s, st: (st[1, s], kb + hp)),
                      pl.BlockSpec((T, LANE), lambda hp, s, st: (st[1, s], vb + hp)),
                      pl.BlockSpec((2, 1, T), lambda hp, s, st: (hp, 0, st[1, s]))] + [_ANY] * ng,
            out_specs=[pl.BlockSpec((T, LANE), lambda hp, s, st: (st[0, s], hp)),
                       pl.BlockSpec((2, T, 1), lambda hp, s, st: (hp, st[0, s], 0))] + [_ANY] * ng,
            scratch_shapes=[pltpu.VMEM((2, T, 1), F32), pltpu.VMEM((2, T, 1), F32), pltpu.VMEM((T, LANE), F32)]
            + (_gather_sems(gather) if ng else [])),
        out_shape=[jax.ShapeDtypeStruct((S, ATTN_W), F32), jax.ShapeDtypeStruct((HEADS, S, 1), F32)]
        + _gather_outs(gather),
        compiler_params=_params(("arbitrary", "arbitrary")), name=name)(steps, pqkv, pqkv, pqkv, c_row, *gather)
    return outs[0], outs[1], outs[2:]


def _attn_bwd(pqkv, do, o, c_row, lse, name):
    S = pqkv.shape[0]
    T = _tile(S, (1024, 512, 256, 128))
    nq = S // T
    qb, kb, vb = (ATTN_W * j // LANE for j in (1, 2, 3))

    steps = jnp.asarray([[ki for ki in range(nq) for qi in range(ki, nq)],
                         [qi for ki in range(nq) for qi in range(ki, nq)]], jnp.int32)

    def body(st_ref, q_ref, k_ref, v_ref, do_ref, o_ref, ck_ref, lse_ref, dq_ref, dk_ref, dv_ref, dc_ref, dcq_ref,
             dk_s, dv_s, dc_s):
        ki, qi = st_ref[0, pl.program_id(1)], st_ref[1, pl.program_id(1)]

        @pl.when(pl.program_id(1) == 0)
        def _():
            dq_ref[...] = jnp.zeros_like(dq_ref)
            dcq_ref[...] = jnp.zeros_like(dcq_ref)

        @pl.when(qi == ki)
        def _():
            dk_s[...] = jnp.zeros_like(dk_s)
            dv_s[...] = jnp.zeros_like(dv_s)
            dc_s[...] = jnp.zeros_like(dc_s)

        def step(diagonal):
            qf, kf, vv = q_ref[...] * ATTN_SCALE, k_ref[...], v_ref[...].astype(MXU_DTYPE)
            dof, ov = do_ref[...], o_ref[...]
            rows = pl.ds(pl.multiple_of(qi * T, T), T)
            lane = lax.broadcasted_iota(jnp.int32, (1, LANE), 1)
            for a in range(2):
                mk = _half_mask(a)
                qa, ka, doa = (qf * mk).astype(MXU_DTYPE), (kf * mk).astype(MXU_DTYPE), dof * mk
                s = _attn_scores(qa, ka, ck_ref[a], diagonal, T)
                p = jnp.exp(s - lse_ref[a])
                delta = jnp.sum(doa * ov, axis=1, keepdims=True)
                dv_s[...] += _dot(p.T, doa)
                dp = _dot(doa, vv, NT)
                ds = p * (dp - delta)
                dc_s[a] -= jnp.sum(ds, axis=0, keepdims=True)
                dq_ref[rows, :] += _dot(ds, ka)
                dcq_ref[rows, :] += jnp.sum(ds, axis=1, keepdims=True) * (lane == a).astype(F32)
                dk_s[...] += _dot(ds.T, qa)

        @pl.when(qi > ki)
        def _():
            step(False)

        @pl.when(qi == ki)
        def _():
            step(True)

        @pl.when(qi == nq - 1)
        def _():
            dk_ref[...] = dk_s[...]
            dv_ref[...] = dv_s[...]
            dc_ref[...] = dc_s[...]

    qrow = lambda hp, s, st: (st[1, s], hp)
    krow = lambda hp, s, st: (st[0, s], hp)
    whole = lambda hp, s, st: (0, hp)
    return pl.pallas_call(
        body, grid_spec=pltpu.PrefetchScalarGridSpec(
            num_scalar_prefetch=1, grid=(HEADS // 2, steps.shape[1]),
            in_specs=[pl.BlockSpec((T, LANE), lambda hp, s, st: (st[1, s], qb + hp)),
                      pl.BlockSpec((T, LANE), lambda hp, s, st: (st[0, s], kb + hp)),
                      pl.BlockSpec((T, LANE), lambda hp, s, st: (st[0, s], vb + hp)),
                      pl.BlockSpec((T, LANE), qrow), pl.BlockSpec((T, LANE), qrow),
                      pl.BlockSpec((2, 1, T), lambda hp, s, st: (hp, 0, st[0, s])),
                      pl.BlockSpec((2, T, 1), lambda hp, s, st: (hp, st[1, s], 0))],
            out_specs=[pl.BlockSpec((S, LANE), whole), pl.BlockSpec((T, LANE), krow), pl.BlockSpec((T, LANE), krow),
                       pl.BlockSpec((2, 1, T), lambda hp, s, st: (hp, 0, st[0, s])), pl.BlockSpec((S, LANE), whole)],
            scratch_shapes=[pltpu.VMEM((T, LANE), F32), pltpu.VMEM((T, LANE), F32), pltpu.VMEM((2, 1, T), F32)]),
        out_shape=[jax.ShapeDtypeStruct((S, ATTN_W), F32), jax.ShapeDtypeStruct((S, ATTN_W), F32),
                   jax.ShapeDtypeStruct((S, ATTN_W), F32), jax.ShapeDtypeStruct((HEADS, 1, S), F32),
                   jax.ShapeDtypeStruct((S, ATTN_W), F32)],
        compiler_params=_params(("arbitrary", "arbitrary")), name=name)(
            steps, pqkv, pqkv, pqkv, do, o, c_row, lse)


CONV_COLS = 512


def _conv_bwd_b(dpre, w, name):
    S, C = dpre.shape
    K = w.shape[0]
    T = _tile(S, (256, 128))

    def body(i, n, R, Cs, P, N, O, A, Sc):
        def tile(r0, first, last):
            for c0 in range(0, C, CONV_COLS):
                cols = slice(c0, c0 + CONV_COLS)
                win = _win_next(R[0], N[0], i, n, r0, last, cols)
                out = None
                for k in range(K):
                    term = _ahead(win, K - 1 - k) * Cs[0][k:k + 1, cols]
                    out = term if out is None else out + term
                O[0][pl.ds(r0, SUB_ROWS), cols] = out.astype(MXU_DTYPE)

        _tile_loop(T, tile)

    return _rows(body, S, T, rows=[dpre], nexts=[dpre], consts=[w], out_rows=[(C, MXU_DTYPE)], name=name)[0]


def _dt_mask():
    lane = lax.broadcasted_iota(jnp.int32, (1, LANE), 1)
    return ((lane >= DT_LANE0) & (lane < DT_LANE0 + SSD_HEADS)).astype(F32)


def _ssd_pre_fwd(xbc, fdt, cw, cb, dtb, name):
    S, C = xbc.shape
    T = _tile(S, (256, 128))
    K = cw.shape[0]

    def body(i, n, R, Cs, P, N, O, A, Sc):
        def tile(r0, first, last):
            rows = pl.ds(r0, SUB_ROWS)
            for c0 in range(0, C, CONV_COLS):
                cols = slice(c0, c0 + CONV_COLS)
                pre = _conv_win(_taps(_win_prev(R[0], P[0], i, r0, first, cols), K), Cs[0], Cs[1], cols)
                O[0][rows, cols] = pre * _sigmoid(pre)
            O[1][rows, :] = _softplus(R[1][rows, :] + Cs[2][...]) * _dt_mask()

        _tile_loop(T, tile)

    return _rows(body, S, T, rows=[xbc, fdt], prevs=[xbc], consts=[cw, cb, dtb],
                 out_rows=[(C, F32), (LANE, F32)], name=name)


def _silu_grad(pre):
    sg = _sigmoid(pre)
    return sg * (1.0 + pre * (1.0 - sg))


def _ssd_pre_bwd_a(xbc, fdt, dxa, ddtw, cw, cb, dtb, name):
    S, C = xbc.shape
    T = _tile(S, (256, 128))
    K = cw.shape[0]

    def body(i, n, R, Cs, P, N, O, A, Sc):
        def tile(r0, first, last):
            rows = pl.ds(r0, SUB_ROWS)
            for c0 in range(0, C, CONV_COLS):
                cols = slice(c0, c0 + CONV_COLS)
                win = _taps(_win_prev(R[0], P[0], i, r0, first, cols), K)
                dpre = R[2][rows, cols] * _silu_grad(_conv_win(win, Cs[0], Cs[1], cols))
                O[0][rows, cols] = dpre
                _conv_wgrad_win(A[0], win, dpre, cols)
                A[1][:, cols] += jnp.sum(dpre, axis=0, keepdims=True)
            ddt = R[3][rows, :] * _sigmoid(R[1][rows, :] + Cs[2][...]) * _dt_mask()
            O[1][rows, :] = ddt
            A[2][...] += jnp.sum(ddt, axis=0, keepdims=True)

        _tile_loop(T, tile)

    return _rows(body, S, T, rows=[xbc, fdt, dxa, ddtw], prevs=[xbc], consts=[cw, cb, dtb],
                 out_rows=[(C, F32), (LANE, F32)],
                 out_accs=[((K, C), F32), ((1, C), F32), ((1, LANE), F32)], name=name)


def _split3(x):
    hi = x.astype(jnp.bfloat16)
    r1 = x - hi.astype(F32)
    mid = r1.astype(jnp.bfloat16)
    lo = (r1 - mid.astype(F32)).astype(jnp.bfloat16)
    return hi, mid, lo


def _expand_mat():
    r = lax.broadcasted_iota(jnp.int32, (LANE, SSD_W), 0)
    c = lax.broadcasted_iota(jnp.int32, (LANE, SSD_W), 1)
    return (r - DT_LANE0 == c // SSD_P).astype(jnp.bfloat16)


def _headsum_mat():
    r = lax.broadcasted_iota(jnp.int32, (SSD_W, LANE), 0)
    c = lax.broadcasted_iota(jnp.int32, (SSD_W, LANE), 1)
    return (c - DT_LANE0 == r // SSD_P).astype(jnp.bfloat16)


def _expand(tile, ex):
    return sum(lax.dot_general(part, ex, (((1,), (0,)), ((), ())), preferred_element_type=F32)
               for part in _split3(tile))


def _headsum(full, hs):
    return sum(lax.dot_general(part, hs, (((1,), (0,)), ((), ())), preferred_element_type=F32)
               for part in _split3(full))


def _sub_cumsum(a, reverse=False):
    n = a.shape[0]
    row = lax.broadcasted_iota(jnp.int32, a.shape, 0)
    sh = 1
    while sh < n:
        if reverse:
            a = a + jnp.where(row < n - sh, pltpu.roll(a, n - sh, axis=0), 0.0)
        else:
            a = a + jnp.where(row >= sh, pltpu.roll(a, sh, axis=0), 0.0)
        sh *= 2
    return a


def _chunk_common(xa_ref, dtw_ref, a_row, ex):
    L = SSD_CHUNK
    xs = xa_ref[:, 0:SSD_W]
    dtv = dtw_ref[...]
    acs = _sub_cumsum(dtv * a_row)
    last = acs[L - 1:L, :]
    full = _expand(jnp.concatenate([dtv, jnp.exp(last - acs), jnp.exp(acs),
                                    jnp.broadcast_to(jnp.exp(last), (8, LANE))], axis=0), ex)
    dt_full, dec_full, e_full, elast_full = full[0:L], full[L:2 * L], full[2 * L:3 * L], full[3 * L:3 * L + 1]
    xd = xs * dt_full
    return xs, dtv, acs, last, dt_full, xd, dec_full, e_full, elast_full


def _decay_mask(acs, acsT, col):
    L = SSD_CHUNK
    diff = acs[:, col:col + 1] - acsT[col:col + 1, :]
    tril = lax.broadcasted_iota(jnp.int32, (L, L), 0) >= lax.broadcasted_iota(jnp.int32, (L, L), 1)
    return jnp.where(tril, jnp.exp(jnp.minimum(diff, 0.0)), 0.0)


def _half_mask(h):
    lane = lax.broadcasted_iota(jnp.int32, (1, LANE), 1)
    return ((lane // SSD_P) == (h % 2)).astype(F32)


def _ssd_chunk_fwd(xa, dtw, a_row, d_full, name):
    S = xa.shape[0]
    L, G = SSD_CHUNK, 2
    nc = S // L
    GW = SSD_W // G

    def body(xa_ref, dtw_ref, a_ref, d_ref, y_ref, hp_ref, state):
        @pl.when(pl.program_id(0) == 0)
        def _():
            state[...] = jnp.zeros_like(state)

        ex = _expand_mat()
        xs, dtv, acs, last, dt_full, xd, dec_full, e_full, elast_full = _chunk_common(xa_ref, dtw_ref, a_ref[...], ex)
        acsT = acs.T
        hp_ref[0] = state[...]
        for g in range(G):
            gc = slice(g * GW, (g + 1) * GW)
            Bg = xa_ref[:, SSD_W + g * SSD_N: SSD_W + (g + 1) * SSD_N]
            Cg = xa_ref[:, SSD_W + G * SSD_N + g * SSD_N: SSD_W + G * SSD_N + (g + 1) * SSD_N]
            cb = _dot(Cg, Bg, NT)
            y_off = e_full[:, gc] * _dot(Cg, state[:, gc])
            for hp in range(GW // LANE):
                pc = slice(g * GW + hp * LANE, g * GW + (hp + 1) * LANE)
                xd_pair = xd[:, pc]
                yp = y_off[:, hp * LANE:(hp + 1) * LANE] + d_ref[:, pc] * xs[:, pc]
                for h2 in range(2):
                    h = (g * GW + hp * LANE) // SSD_P + h2
                    m = cb * _decay_mask(acs, acsT, DT_LANE0 + h)
                    yp = yp + _dot(m, xd_pair * _half_mask(h))
                y_ref[:, pc] = yp
            st_new = _dot(Bg.T, xd[:, gc] * dec_full[:, gc])
            state[:, gc] = elast_full[:, gc] * state[:, gc] + st_new

    return pl.pallas_call(
        body, grid=(nc,),
        in_specs=[pl.BlockSpec((L, SSD_CONV_CH), lambda c: (c, 0)), pl.BlockSpec((L, LANE), lambda c: (c, 0)),
                  pl.BlockSpec((1, LANE), lambda c: (0, 0)), pl.BlockSpec((1, SSD_W), lambda c: (0, 0))],
        out_specs=[pl.BlockSpec((L, SSD_W), lambda c: (c, 0)), pl.BlockSpec((1, SSD_N, SSD_W), lambda c: (c, 0, 0))],
        out_shape=[jax.ShapeDtypeStruct((S, SSD_W), F32), jax.ShapeDtypeStruct((nc, SSD_N, SSD_W), F32)],
        scratch_shapes=[pltpu.VMEM((SSD_N, SSD_W), F32)],
        compiler_params=_params(("arbitrary",)), name=name)(xa, dtw, a_row, d_full)


def _ssd_chunk_bwd(xa, dtw, dy, hprev, a_row, d_full, name):
    S = xa.shape[0]
    L, G = SSD_CHUNK, 2
    nc = S // L
    GW = SSD_W // G

    def body(xa_ref, dtw_ref, dy_ref, hp_ref, a_ref, d_ref, dxa_ref, ddt_ref, da_ref, dd_ref, dstate):
        @pl.when(pl.program_id(0) == 0)
        def _():
            dstate[...] = jnp.zeros_like(dstate)
            da_ref[...] = jnp.zeros_like(da_ref)
            dd_ref[...] = jnp.zeros_like(dd_ref)

        ex, hs = _expand_mat(), _headsum_mat()
        a_row = a_ref[...]
        xs, dtv, acs, last, dt_full, xd, dec_full, e_full, elast_full = _chunk_common(xa_ref, dtw_ref, a_row, ex)
        acsT = acs.T
        dyv = dy_ref[...]
        lane = lax.broadcasted_iota(jnp.int32, (L, LANE), 1)
        sub = lax.broadcasted_iota(jnp.int32, (LANE, L), 0)
        dacs_c = jnp.zeros((L, LANE), F32)
        dacs_r = jnp.zeros((LANE, L), F32)
        dxd_parts, yoff_parts, dxdd_parts, hh_parts = [], [], [], []
        for g in range(G):
            gc = slice(g * GW, (g + 1) * GW)
            b0 = SSD_W + g * SSD_N
            c0 = SSD_W + G * SSD_N + g * SSD_N
            Bg = xa_ref[:, b0:b0 + SSD_N]
            Cg = xa_ref[:, c0:c0 + SSD_N]
            Hp = hp_ref[0, :, gc]
            dH = dstate[:, gc]
            cb = _dot(Cg, Bg, NT)
            Gm = _dot(Cg, Hp)
            yoff_parts.append(e_full[:, gc] * Gm)
            dG = e_full[:, gc] * dyv[:, gc]
            dC = _dot(dG, Hp, NT)
            dHp = _dot(Cg.T, dG)
            xdd = xd[:, gc] * dec_full[:, gc]
            dB = _dot(xdd, dH, NT)
            dxdd = _dot(Bg, dH)
            dxdd_parts.append(dxdd)
            hh_parts.append(dH * Hp)
            dstate[:, gc] = dHp + elast_full[:, gc] * dH
            dcb = jnp.zeros((L, L), F32)
            dxd_g = []
            for hp in range(GW // LANE):
                pc = slice(g * GW + hp * LANE, g * GW + (hp + 1) * LANE)
                xd_pair = xd[:, pc]
                dxd_pair = dxdd[:, hp * LANE:(hp + 1) * LANE] * dec_full[:, pc]
                for h2 in range(2):
                    h = (g * GW + hp * LANE) // SSD_P + h2
                    col = DT_LANE0 + h
                    lm = _decay_mask(acs, acsT, col)
                    m = cb * lm
                    dy_h = dyv[:, pc] * _half_mask(h)
                    dm = _dot(dy_h, xd_pair, NT)
                    dxd_pair = dxd_pair + _dot(m.T, dy_h)
                    wm = dm * m
                    dacs_c = dacs_c + jnp.where(lane == col, jnp.sum(wm, axis=1, keepdims=True), 0.0)
                    dacs_r = dacs_r - jnp.where(sub == col, jnp.sum(wm, axis=0, keepdims=True), 0.0)
                    dcb = dcb + dm * lm
                dxd_g.append(dxd_pair)
            dxd_parts.append(jnp.concatenate(dxd_g, axis=1))
            dxa_ref[:, c0:c0 + SSD_N] = dC + _dot(dcb, Bg)
            dxa_ref[:, b0:b0 + SSD_N] = dB + _dot(dcb.T, Cg)
        dxd = jnp.concatenate(dxd_parts, axis=1)
        y_off = jnp.concatenate(yoff_parts, axis=1)
        dxdd_full = jnp.concatenate(dxdd_parts, axis=1)
        hh = jnp.concatenate(hh_parts, axis=1)
        dxa_ref[:, 0:SSD_W] = d_ref[...] * dyv + dxd * dt_full
        sums = _headsum(jnp.concatenate([dyv * xs, dxd * xs, dxdd_full * xd, hh, dyv * y_off], axis=0), hs)
        hs_skip, ddt, hs_dec, hs_state, hs_off = (sums[j * L:(j + 1) * L] for j in range(5))
        dd_ref[...] += jnp.sum(hs_skip, axis=0, keepdims=True)
        w_dec = hs_dec * jnp.exp(last - acs)
        dlast = jnp.sum(w_dec, axis=0, keepdims=True) + jnp.exp(last) * jnp.sum(hs_state, axis=0, keepdims=True)
        dacs = dacs_c + dacs_r.T + hs_off - w_dec
        rowid = lax.broadcasted_iota(jnp.int32, (L, LANE), 0)
        dacs = dacs + jnp.where(rowid == L - 1, dlast, 0.0)
        da = _sub_cumsum(dacs, reverse=True)
        ddt_ref[...] = ddt + da * a_row
        da_ref[...] += jnp.sum(da * dtv, axis=0, keepdims=True)

    rev = lambda c: (nc - 1 - c, 0)
    return pl.pallas_call(
        body, grid=(nc,),
        in_specs=[pl.BlockSpec((L, SSD_CONV_CH), rev), pl.BlockSpec((L, LANE), rev), pl.BlockSpec((L, SSD_W), rev),
                  pl.BlockSpec((1, SSD_N, SSD_W), lambda c: (nc - 1 - c, 0, 0)),
                  pl.BlockSpec((1, LANE), lambda c: (0, 0)), pl.BlockSpec((1, SSD_W), lambda c: (0, 0))],
        out_specs=[pl.BlockSpec((L, SSD_CONV_CH), rev), pl.BlockSpec((L, LANE), rev),
                   pl.BlockSpec((1, LANE), lambda c: (0, 0)), pl.BlockSpec((1, LANE), lambda c: (0, 0))],
        out_shape=[jax.ShapeDtypeStruct((S, SSD_CONV_CH), F32), jax.ShapeDtypeStruct((S, LANE), F32),
                   jax.ShapeDtypeStruct((1, LANE), F32), jax.ShapeDtypeStruct((1, LANE), F32)],
        scratch_shapes=[pltpu.VMEM((SSD_N, SSD_W), F32)],
        compiler_params=_params(("arbitrary",)), name=name)(xa, dtw, dy, hprev, a_row, d_full)


def _ssd_post_fwd(y, z, w, name):
    S = y.shape[0]
    GW = SSD_W // 2

    def body(i, n, R, C, P, N, O, A, Sc):
        zv = R[1][...]
        v = R[0][...] * (zv * _sigmoid(zv))
        for g in range(2):
            gc = slice(g * GW, (g + 1) * GW)
            vg = v[:, gc]
            r = lax.rsqrt(jnp.mean(vg * vg, axis=-1, keepdims=True) + NORM_EPS)
            O[0][:, gc] = (vg * r * C[0][:, gc]).astype(MXU_DTYPE)

    return _rows(body, S, _tile(S, (256, 128)), rows=[y, z], consts=[w], out_rows=[(SSD_W, MXU_DTYPE)], name=name)[0]


def _ssd_post_bwd(y, z, dyn, w, name):
    S = y.shape[0]
    GW = SSD_W // 2

    def body(i, n, R, C, P, N, O, A, Sc):
        yv, zv, dn = R[0][...], R[1][...], R[2][...]
        sz = zv * _sigmoid(zv)
        v = yv * sz
        for g in range(2):
            gc = slice(g * GW, (g + 1) * GW)
            dv, dw = _norm_bwd_math(v[:, gc], C[0][:, gc], dn[:, gc])
            A[0][:, gc] += dw
            O[0][:, gc] = dv * sz[:, gc]
            O[1][:, gc] = (dv * yv[:, gc] * _silu_grad(zv[:, gc])).astype(MXU_DTYPE)

    return _rows(body, S, _tile(S, (256, 128)), rows=[y, z, dyn], consts=[w],
                 out_rows=[(SSD_W, F32), (SSD_W, MXU_DTYPE)], out_accs=[((1, SSD_W), F32)], name=name)


def _merge_fwd(gl, yp, ya, ys, name):
    S, D = yp.shape

    def body(i, n, R, C, P, N, O, A, Sc):
        def tile(r0, first, last):
            rows = pl.ds(r0, SUB_ROWS)
            for c0 in range(0, D, CONV_COLS):
                cols = slice(c0, c0 + CONV_COLS)
                acc = None
                for b in range(3):
                    term = _sigmoid(R[0][rows, b * D + c0:b * D + c0 + CONV_COLS]) * R[1 + b][rows, cols]
                    acc = term if acc is None else acc + term
                O[0][rows, cols] = acc.astype(MXU_DTYPE)

        _tile_loop(T, tile)

    T = _tile(S, (256, 128))
    return _rows(body, S, T, rows=[gl, yp, ya, ys], out_rows=[(D, MXU_DTYPE)], name=name)[0]


def _merge_bwd(gl, yp, ya, ys, dm, name):
    S, D = yp.shape
    T = _tile(S, (256, 128))

    def body(i, n, R, C, P, N, O, A, Sc):
        def tile(r0, first, last):
            rows = pl.ds(r0, SUB_ROWS)
            for c0 in range(0, D, CONV_COLS):
                cols = slice(c0, c0 + CONV_COLS)
                dmv = R[4][rows, cols]
                for b in range(3):
                    gcols = slice(b * D + c0, b * D + c0 + CONV_COLS)
                    gt = _sigmoid(R[0][rows, gcols])
                    O[b][rows, cols] = (gt * dmv).astype(MXU_DTYPE)
                    O[3][rows, gcols] = (dmv * R[1 + b][rows, cols] * gt * (1.0 - gt)).astype(MXU_DTYPE)

        _tile_loop(T, tile)

    return _rows(body, S, T, rows=[gl, yp, ya, ys, dm],
                 out_rows=[(D, MXU_DTYPE)] * 3 + [(3 * D, MXU_DTYPE)], name=name)


FFN_COLS = 256


def _ffn_act_fwd(hpre, cw, cb, name):
    S, C = hpre.shape
    K = cw.shape[0]
    T = _tile(S, (256, 128))
    Fd = C // 2

    def body(i, n, R, Cs, P, N, O, A, Sc):
        def tile(r0, first, last):
            for c0 in range(0, Fd, FFN_COLS):
                gcols, vcols = slice(c0, c0 + FFN_COLS), slice(Fd + c0, Fd + c0 + FFN_COLS)
                gt = _conv_win(_taps(_win_prev(R[0], P[0], i, r0, first, gcols), K), Cs[0], Cs[1], gcols)
                val = _conv_win(_taps(_win_prev(R[0], P[0], i, r0, first, vcols), K), Cs[0], Cs[1], vcols)
                O[0][pl.ds(r0, SUB_ROWS), gcols] = (gt * _sigmoid(gt) * val).astype(MXU_DTYPE)

        _tile_loop(T, tile)

    return _rows(body, S, T, rows=[hpre], prevs=[hpre], consts=[cw, cb], out_rows=[(Fd, MXU_DTYPE)], name=name)[0]


def _ffn_act_bwd_a(hpre, dact, cw, cb, name):
    S, C = hpre.shape
    K = cw.shape[0]
    T = _tile(S, (256, 128))
    Fd = C // 2

    def body(i, n, R, Cs, P, N, O, A, Sc):
        def tile(r0, first, last):
            rows = pl.ds(r0, SUB_ROWS)
            for c0 in range(0, Fd, FFN_COLS):
                gcols, vcols = slice(c0, c0 + FFN_COLS), slice(Fd + c0, Fd + c0 + FFN_COLS)
                gwin = _taps(_win_prev(R[0], P[0], i, r0, first, gcols), K)
                vwin = _taps(_win_prev(R[0], P[0], i, r0, first, vcols), K)
                gt = _conv_win(gwin, Cs[0], Cs[1], gcols)
                val = _conv_win(vwin, Cs[0], Cs[1], vcols)
                da = R[1][rows, gcols]
                for cols, win, d in ((gcols, gwin, da * val * _silu_grad(gt)), (vcols, vwin, da * gt * _sigmoid(gt))):
                    O[0][rows, cols] = d
                    _conv_wgrad_win(A[0], win, d, cols)
                    A[1][:, cols] += jnp.sum(d, axis=0, keepdims=True)

        _tile_loop(T, tile)

    return _rows(body, S, T, rows=[hpre, dact], prevs=[hpre], consts=[cw, cb], out_rows=[(C, F32)],
                 out_accs=[((K, C), F32), ((1, C), F32)], name=name)


def _adamw_math(g, w, m, v):
    c1 = 1.0 - ADAM_B1 ** ADAM_STEP
    c2 = 1.0 - ADAM_B2 ** ADAM_STEP
    mn = ADAM_B1 * m + (1.0 - ADAM_B1) * g
    vn = ADAM_B2 * v + (1.0 - ADAM_B2) * (g * g)
    return -ADAM_LR * ((mn / c1) / (jnp.sqrt(vn / c2) + ADAM_EPS) + ADAM_WD * w), mn, vn


def _adamw(g, w, m, v, name):
    R_, W = g.shape

    def body(i, n, R, C, P, N, O, A, Sc):
        O[0][...], O[1][...], O[2][...] = _adamw_math(R[0][...], R[1][...], R[2][...], R[3][...])

    return _rows(body, R_, _row_tile(R_, W, 7), rows=[g, w, m, v], out_rows=[(W, F32)] * 3, name=name)


def _adamw_pair(q, other, w, m, v, pos, name):
    _, R_, W = w.shape
    T = _row_tile(R_, W, 9)

    def body(pos_ref, q_ref, o_ref, w_ref, m_ref, v_ref, g_out, d_out, m_out, v_out):
        g = jnp.where(pl.program_id(0) == pos_ref[1], q_ref[...], o_ref[...])
        g_out[0] = g
        d_out[0], m_out[0], v_out[0] = _adamw_math(g, w_ref[0], m_ref[0], v_ref[0])

    flat = pl.BlockSpec((T, W), lambda l, i, pos: (i, 0))
    full = pl.BlockSpec((1, T, W), lambda l, i, pos: (l, i, 0))
    return _scalar_call(body, pos, (2, R_ // T), [flat, flat, full, full, full], [full] * 4,
                        [jax.ShapeDtypeStruct(w.shape, F32)] * 4, (q, other, w, m, v), name)


def _row_tile(rows, width, n_blocks, budget=14 * 1024 * 1024):
    wpad = -(-width // LANE) * LANE
    for t in (512, 256, 128, 64, 32, 16, 8):
        if rows % t == 0 and n_blocks * t * wpad * 4 <= budget:
            return t
    return rows


_ANY = pl.BlockSpec(memory_space=pl.ANY)
_MESH = pl.DeviceIdType.MESH


DMA_CHUNK_BYTES = 2 * 1024 * 1024


def _row_chunks(shape, dtype):
    r = shape[-2]
    total = 1
    for s in shape:
        total *= s
    want = max(1, (total * jnp.dtype(dtype).itemsize) // DMA_CHUNK_BYTES)
    n = 1
    while n * 2 <= want and r % (n * 2 * 16) == 0 and n < 8:
        n *= 2
    return [(j * (r // n), r // n) for j in range(n)]


def _comm_call(plan, srcs, out_shapes, name, by_core=False):
    n = len(srcs)
    n_remote = len(plan(0, 0, 0, [_ShapeOnly(s.shape) for s in srcs], [_ShapeOnly(s.shape) for s in out_shapes]))

    def body(*refs):
        src_refs, out_refs = refs[:n], refs[n:n + len(out_shapes)]
        send_sems, recv_sems = refs[n + len(out_shapes):]
        x, y, c = lax.axis_index("x"), lax.axis_index("y"), lax.axis_index("c")

        def run(core):
            sent = []
            for j, (s, d, peer) in enumerate(plan(x, y, core, src_refs, out_refs)):
                cp = pltpu.make_async_remote_copy(src_ref=s, dst_ref=d, send_sem=send_sems.at[j],
                                                  recv_sem=recv_sems.at[j], device_id=peer, device_id_type=_MESH)
                cp.start()
                sent.append(cp)
            for cp in sent:
                cp.wait()

        if by_core:
            for core in (0, 1):
                pl.when(c == core)(functools.partial(run, core))
        else:
            run(c)

    return pl.pallas_call(
        body, in_specs=[_ANY] * n, out_specs=[_ANY] * len(out_shapes),
        out_shape=[jax.ShapeDtypeStruct(s.shape, s.dtype) for s in out_shapes],
        scratch_shapes=[pltpu.SemaphoreType.DMA((n_remote,)), pltpu.SemaphoreType.DMA((n_remote,))], name=name)(*srcs)


class _ShapeOnly:
    def __init__(self, shape):
        self.shape = tuple(shape)

    @property
    def at(self):
        return self

    def __getitem__(self, idx):
        return self


def _other_places(x, y):
    return [(1 - x, y), (x, 1 - y), (1 - x, 1 - y)]


def _gather_copies(src_refs, out_refs, shards, sems, owner):
    ici_send, ici_recv, d2d_send, d2d_recv = sems
    x, y, c = lax.axis_index("x"), lax.axis_index("y"), lax.axis_index("c")
    me = 2 * x + y
    jobs = [(s_ref, o_ref, r0, rn, px, py)
            for s_ref, o_ref, s in zip(src_refs, out_refs, shards)
            for r0, rn in _row_chunks(s.shape, s.dtype) for px, py in _other_places(x, y)]

    def over_ici(j):
        s_ref, o_ref, r0, rn, px, py = jobs[j]
        return pltpu.make_async_remote_copy(
            src_ref=s_ref.at[pl.ds(r0, rn), :], dst_ref=o_ref.at[me, pl.ds(r0, rn), :],
            send_sem=ici_send.at[j], recv_sem=ici_recv.at[j], device_id=(px, py, c), device_id_type=_MESH)

    def landed(j):
        _, o_ref, r0, rn, px, py = jobs[j]
        view = o_ref.at[2 * px + py, pl.ds(r0, rn), :]
        return pltpu.make_async_remote_copy(src_ref=view, dst_ref=view, send_sem=d2d_send.at[j],
                                            recv_sem=d2d_recv.at[j], device_id=(x, y, 1 - c), device_id_type=_MESH)

    def start():
        @pl.when(c == owner)
        def _():
            for j in range(len(jobs)):
                over_ici(j).start()

    def finish():
        @pl.when(c == owner)
        def _():
            for j in range(len(jobs)):
                over_ici(j).wait_recv()
                landed(j).start()
            for j in range(len(jobs)):
                over_ici(j).wait_send()
                landed(j).wait_send()

        @pl.when(c != owner)
        def _():
            for j in range(len(jobs)):
                landed(j).wait_recv()

    return start, finish


def _gather_sems(shards):
    n_copies = 3 * sum(len(_row_chunks(s.shape, s.dtype)) for s in shards)
    return [pltpu.SemaphoreType.DMA((n_copies,))] * 4


def _gather_outs(shards):
    return [jax.ShapeDtypeStruct((N_PLACES,) + s.shape, s.dtype) for s in shards]


def _gather_layer(shards, owner, name):
    n = len(shards)

    def body(*refs):
        start, finish = _gather_copies(refs[:n], refs[n:2 * n], shards, refs[2 * n:], owner)
        start()
        finish()

    return pl.pallas_call(body, in_specs=[_ANY] * n, out_specs=[_ANY] * n, out_shape=_gather_outs(shards),
                          scratch_shapes=_gather_sems(shards), name=name)(*shards)


def _reduce_sibling(pairs, name):
    outs = [jax.ShapeDtypeStruct(g0.shape, g0.dtype) for g0, _ in pairs]

    def plan(x, y, c, src_refs, out_refs):
        remote = []
        for t, (o_ref, (g0, _)) in enumerate(zip(out_refs, pairs)):
            g_ref = src_refs[2 * t + (1 - c)]
            for r0, rn in _row_chunks(g0.shape[1:], g0.dtype):
                for p in range(N_PLACES):
                    remote.append((g_ref.at[p, pl.ds(r0, rn), :], o_ref.at[p, pl.ds(r0, rn), :], (x, y, 1 - c)))
        return remote

    return _comm_call(plan, [g for pair in pairs for g in pair], outs, name, by_core=True)


def _reduce_places(hs, name):
    outs = [jax.ShapeDtypeStruct((3,) + h.shape[1:], h.dtype) for h in hs]

    def plan(x, y, c, src_refs, out_refs):
        remote = []
        for h_ref, o_ref, h in zip(src_refs, out_refs, hs):
            for r0, rn in _row_chunks(h.shape[1:], h.dtype):
                for j, (px, py) in enumerate(_other_places(x, y)):
                    remote.append((h_ref.at[2 * px + py, pl.ds(r0, rn), :], o_ref.at[j, pl.ds(r0, rn), :], (px, py, c)))
        return remote

    return _comm_call(plan, hs, outs, name)


def _swap_sibling(qs, name):
    def plan(x, y, c, src_refs, out_refs):
        remote = []
        for q_ref, o_ref, q in zip(src_refs, out_refs, qs):
            for r0, rn in _row_chunks(q.shape, q.dtype):
                remote.append((q_ref.at[pl.ds(r0, rn), :], o_ref.at[pl.ds(r0, rn), :], (x, y, 1 - c)))
        return remote

    return _comm_call(plan, qs, qs, name)


def _scalar_call(body, scalars, grid, in_specs, out_specs, out_shape, args, name):
    return pl.pallas_call(
        body, grid_spec=pltpu.PrefetchScalarGridSpec(num_scalar_prefetch=1, grid=grid, in_specs=in_specs,
                                                     out_specs=out_specs),
        out_shape=out_shape, compiler_params=_params(("arbitrary",) * len(grid)), name=name)(scalars, *args)


def _own_layer(pos_ref, g0_ref, g1_ref):
    return jnp.where(pos_ref[1] == 0, g0_ref[...], g1_ref[...])


def _add_own_slot(pair, r_, pos, out_dtype, name):
    P_, R_, W = r_.shape
    T = _row_tile(R_, W, 4)

    def body(pos_ref, g0_ref, g1_ref, r_ref, o_ref):
        o_ref[...] = (_own_layer(pos_ref, g0_ref, g1_ref) + r_ref[...]).astype(out_dtype)

    blk = pl.BlockSpec((1, T, W), lambda p, i, pos: (p, i, 0))
    return _scalar_call(body, pos, (P_, R_ // T), [blk, blk, blk], blk,
                        jax.ShapeDtypeStruct((P_, R_, W), out_dtype), (pair[0], pair[1], r_), name)


def _sum_places(pair, r_, recv, pos, name):
    _, R_, W = r_.shape
    T = _row_tile(R_, W, 6)

    def body(pos_ref, g0_ref, g1_ref, r_ref, recv_ref, o_ref):
        for m in range(N_PLACES):
            @pl.when(pos_ref[0] == m)
            def _(m=m):
                acc = None
                for p in range(N_PLACES):
                    if p == m:
                        term = _own_layer(pos_ref, g0_ref, g1_ref)[0] + r_ref[0]
                    else:
                        dx, dy = (p >> 1) != (m >> 1), (p & 1) != (m & 1)
                        term = recv_ref[0 if (dx and not dy) else 1 if (dy and not dx) else 2].astype(F32)
                    acc = term if acc is None else acc + term
                o_ref[...] = acc

    mine = pl.BlockSpec((1, T, W), lambda i, pos: (pos[0], i, 0))
    return _scalar_call(
        body, pos, (R_ // T,), [mine, mine, mine, pl.BlockSpec((3, T, W), lambda i, pos: (0, i, 0))],
        pl.BlockSpec((T, W), lambda i, pos: (i, 0)), jax.ShapeDtypeStruct((R_, W), F32),
        (pair[0], pair[1], r_, recv), name)


def _lane_tile(vec16):
    return jnp.concatenate([jnp.zeros((DT_LANE0,), F32), vec16,
                            jnp.zeros((LANE - DT_LANE0 - SSD_HEADS,), F32)])[None]


def _layer_consts(W, l):
    return dict(
        norm_mix=W['norm_mix'][l][None], mix=W['pool_mix'][l].astype(MXU_DTYPE), scale=W['pool_scale'][l][None],
        f_bias=W['f_bias'][l][:, None], cw=W['ssd_conv_w'][l], cb=W['ssd_conv_b'][l][None],
        dtb=_lane_tile(W['ssd_dt_bias'][l]), a_row=_lane_tile(-jnp.exp(W['ssd_a_log'][l])),
        d_full=jnp.repeat(W['ssd_d'][l], SSD_P)[None], ssd_norm=W['ssd_norm'][l][None],
        norm_ffn=W['norm_ffn'][l][None], fcw=W['ffn_conv_w'][l], fcb=W['ffn_conv_b'][l][None])


def _layer_fwd(x, W, l, gather=()):
    n = f"l{l}_"
    cs = _layer_consts(W, l)
    win = {k: v[l] for k, v in W['w_in'].items()}
    u = _norm_fwd(x, cs['norm_mix'], n + "norm_mix")
    pqkv = _mm(u, win['p'], name=n + "in_p")
    z = _mm(u, win['z'], name=n + "in_z")
    xbc = _mm(u, win['x'], name=n + "in_x")
    gl = _mm(u, win['g'], name=n + "in_g")
    fdt = _mm(u, win['f'], name=n + "in_f")
    d, ypm = _pool_fwd(pqkv, cs['mix'], cs['scale'], n + "pool")
    yp = _mm(ypm, W['p_pool'][l], name=n + "p_pool")
    fT = fdt[:, :HEADS].T
    c = _logf_cumsum(fT, cs['f_bias'], n + "logf")
    c_row = c[:, None, :]
    o, lse, gathered = _attn_fwd(pqkv, c_row, n + "attn", gather)
    ya = _mm(o, W['p_attn'][l], name=n + "p_attn")
    xa, dtw = _ssd_pre_fwd(xbc, fdt, cs['cw'], cs['cb'], cs['dtb'], n + "ssd_pre")
    y, hprev = _ssd_chunk_fwd(xa, dtw, cs['a_row'], cs['d_full'], n + "ssd_scan")
    yn = _ssd_post_fwd(y, z, cs['ssd_norm'], n + "ssd_post")
    ys = _mm(yn, W['p_ssd'][l], name=n + "p_ssd")
    merged = _merge_fwd(gl, yp, ya, ys, n + "merge")
    x1 = _mm(merged, W['w_out'][l], acc=x, name=n + "w_out")
    u2 = _norm_fwd(x1, cs['norm_ffn'], n + "norm_ffn")
    hpre = _mm(u2, W['ffn_up'][l], name=n + "ffn_up")
    act = _ffn_act_fwd(hpre, cs['fcw'], cs['fcb'], n + "ffn_act")
    x2 = _mm(act, W['ffn_down'][l], acc=x1, name=n + "ffn_down")
    saved = dict(x=x, u=u, pqkv=pqkv, z=z, xbc=xbc, gl=gl, fdt=fdt, d=d, ypm=ypm, yp=yp, fT=fT,
                 c_row=c_row, o=o, lse=lse, ya=ya, gathered=gathered, xa=xa, dtw=dtw, y=y, hprev=hprev,
                 yn=yn, ys=ys, merged=merged, x1=x1, u2=u2, hpre=hpre, act=act, win=win, cs=cs)
    return x2, saved


def _layer_bwd(dx2, sv, W, l):
    n = f"l{l}_b_"
    cs, win = sv['cs'], sv['win']
    g = {}
    dact = _mm(dx2, W['ffn_down'][l], tb=True, name=n + "ffn_down_dx")
    g['ffn_down'] = _mm(sv['act'], dx2, ta=True, name=n + "ffn_down_dw")
    dhc, g['ffn_conv_w'], dfcb = _ffn_act_bwd_a(sv['hpre'], dact, cs['fcw'], cs['fcb'], n + "ffn_act_a")
    g['ffn_conv_b'] = dfcb[0]
    dhpre = _conv_bwd_b(dhc, cs['fcw'], n + "ffn_act_b")
    du2 = _mm(dhpre, W['ffn_up'][l], tb=True, name=n + "ffn_up_dx")
    g['ffn_up'] = _mm(sv['u2'], dhpre, ta=True, name=n + "ffn_up_dw")
    dx1, dnf = _norm_bwd(sv['x1'], cs['norm_ffn'], du2, dx2, n + "norm_ffn")
    g['norm_ffn'] = dnf[0]
    dm = _mm(dx1, W['w_out'][l], tb=True, name=n + "w_out_dx")
    g['w_out'] = _mm(sv['merged'], dx1, ta=True, name=n + "w_out_dw")
    dyp, dya, dys, dgl = _merge_bwd(sv['gl'], sv['yp'], sv['ya'], sv['ys'], dm, n + "merge")
    dypm = _mm(dyp, W['p_pool'][l], tb=True, name=n + "p_pool_dx")
    g['p_pool'] = _mm(sv['ypm'], dyp, ta=True, name=n + "p_pool_dw")
    dd, dscale, dmix = _pool_bwd_a(dypm, sv['d'], cs['mix'], cs['scale'], n + "pool_a")
    g['pool_scale'] = dscale[0]
    g['pool_mix'] = dmix.reshape(len(POOL_WINDOWS), LANE, LANE)
    dpool_v = _pool_bwd_b(dd, n + "pool_b")
    do = _mm(dya, W['p_attn'][l], tb=True, name=n + "p_attn_dx")
    g['p_attn'] = _mm(sv['o'], dya, ta=True, name=n + "p_attn_dw")
    dq, dk, dv, dc, dcq = _attn_bwd(sv['pqkv'], do, sv['o'], sv['c_row'], sv['lse'], n + "attn")
    dcq = dcq.reshape(-1, HEADS // 2, LANE)[:, :, :2].reshape(-1, HEADS).T
    dfT, dfb = _logf_cumsum_bwd(sv['fT'], cs['f_bias'], dc[:, 0, :], dcq, n + "logf")
    g['f_bias'] = dfb[:, 0]
    dpqkv = jnp.concatenate([dpool_v, (dq * ATTN_SCALE).astype(MXU_DTYPE), dk.astype(MXU_DTYPE),
                             dv.astype(MXU_DTYPE)], axis=1)
    dyn = _mm(dys, W['p_ssd'][l], tb=True, name=n + "p_ssd_dx")
    g['p_ssd'] = _mm(sv['yn'], dys, ta=True, name=n + "p_ssd_dw")
    dy, dz, dsn = _ssd_post_bwd(sv['y'], sv['z'], dyn, cs['ssd_norm'], n + "ssd_post")
    g['ssd_norm'] = dsn[0]
    dxa, ddtw, dA, dD = _ssd_chunk_bwd(sv['xa'], sv['dtw'], dy, sv['hprev'], cs['a_row'], cs['d_full'], n + "ssd_scan")
    heads = slice(DT_LANE0, DT_LANE0 + SSD_HEADS)
    g['ssd_a_log'] = dA[0, heads] * cs['a_row'][0, heads]
    g['ssd_d'] = dD[0, heads]
    dpre, ddt_raw, g['ssd_conv_w'], dcb, ddtb = _ssd_pre_bwd_a(sv['xbc'], sv['fdt'], dxa, ddtw, cs['cw'], cs['cb'],
                                                              cs['dtb'], n + "ssd_pre_a")
    g['ssd_conv_b'] = dcb[0]
    g['ssd_dt_bias'] = ddtb[0, heads]
    dxbc = _conv_bwd_b(dpre, cs['cw'], n + "ssd_pre_b")
    dfdt = jnp.concatenate([dfT.T, ddt_raw[:, HEADS:]], axis=1).astype(MXU_DTYPE)
    dsegs = dict(p=dpqkv, z=dz, x=dxbc, g=dgl, f=dfdt)
    du, dwin = None, {}
    for key in ('p', 'z', 'x', 'g', 'f'):
        du = _mm(dsegs[key], win[key], tb=True, acc=du, name=n + "in_dx_" + key)
        dwin[key] = _mm(sv['u'], dsegs[key], ta=True, name=n + "in_dw_" + key)
    g['w_in'] = dwin
    dx, dnm = _norm_bwd(sv['x'], cs['norm_mix'], du, dx1, n + "norm_mix")
    g['norm_mix'] = dnm[0]
    return dx, g


def _local_step(x, target, W, next_layer=None):
    depth = W['norm_mix'].shape[0]
    saved = []
    h = x
    for l in range(depth):
        h, sv = _layer_fwd(h, W, l, next_layer[0] if next_layer and l == 0 else ())
        if next_layer and l == 0:
            next_layer[1](sv['gathered'])
        saved.append(sv)
    dx, dwf, loss = _loss_head(h, W['norm_final'][None], target, "loss_head")
    grads = [None] * depth
    for l in reversed(range(depth)):
        dx, grads[l] = _layer_bwd(dx, saved[l], W, l)
    return loss[0, 0], dx, grads, dwf[0]


def _pack_rows(parts, row_align=1):
    flat = jnp.concatenate([p.reshape(-1) for p in parts])
    n = flat.shape[0]
    total = -(-n // (PACK_W * row_align)) * PACK_W * row_align
    if total > n:
        flat = jnp.concatenate([flat, jnp.zeros((total - n,), flat.dtype)])
    return flat.reshape(-1, PACK_W)


def _unpack_rows(buf, shapes):
    flat = buf.reshape(-1)
    out, pos = [], 0
    for shp in shapes:
        size = 1
        for s in shp:
            size *= s
        out.append(flat[pos:pos + size].reshape(shp))
        pos += size
    return out


def _to_place_major(gfull, name):
    R_, C = gfull.shape
    if name in COL_SHARDED:
        return gfull.reshape(R_, N_PLACES, C // N_PLACES).transpose(1, 0, 2)
    return gfull.reshape(N_PLACES, R_ // N_PLACES, C)


_W_IN_LAYOUT = (('p', 0, 0, 2048), ('f', 0, 2048, HEADS), ('z', 0, 2056, 1024), ('x', 0, 3080, 1536),
                ('f', DT_LANE0, 4616, SSD_HEADS), ('g', 0, 4632, 3072))


def _w_in_segments(slabs):
    starts = [0]
    for s in slabs:
        starts.append(starts[-1] + s.shape[-1])

    def cols(a, b):
        parts = []
        for s, s0 in zip(slabs, starts):
            lo, hi = max(a, s0), min(b, s0 + s.shape[-1])
            if lo < hi:
                parts.append(s[..., lo - s0:hi - s0])
        return parts[0] if len(parts) == 1 else jnp.concatenate(parts, axis=-1)

    pad = jnp.zeros(slabs[0].shape[:-1] + (LANE - DT_LANE0 - SSD_HEADS,), slabs[0].dtype)
    return dict(p=cols(0, 2048), z=cols(2056, 3080), x=cols(3080, 4616), g=cols(4632, 7704),
                f=jnp.concatenate([cols(2048, 2056), cols(4616, 4632), pad], axis=-1))


def _w_in_columns(segs, a, b):
    parts = []
    for key, s0, g0, w in _W_IN_LAYOUT:
        lo, hi = max(a, g0), min(b, g0 + w)
        if lo < hi:
            parts.append(segs[key][..., s0 + lo - g0:s0 + hi - g0])
    return parts[0] if len(parts) == 1 else jnp.concatenate(parts, axis=-1)


def kernel(x, norm_mix, w_in, pool_mix, pool_scale, f_bias, ssd_conv_w, ssd_conv_b, ssd_dt_bias, ssd_a_log, ssd_d, ssd_norm, p_pool, p_attn, p_ssd, w_out, norm_ffn, ffn_up, ffn_conv_w, ffn_conv_b, ffn_down, norm_final, loss_target, m_norm_mix, m_w_in, m_pool_mix, m_pool_scale, m_f_bias, m_ssd_conv_w, m_ssd_conv_b, m_ssd_dt_bias, m_ssd_a_log, m_ssd_d, m_ssd_norm, m_p_pool, m_p_attn, m_p_ssd, m_w_out, m_norm_ffn, m_ffn_up, m_ffn_conv_w, m_ffn_conv_b, m_ffn_down, m_norm_final, v_norm_mix, v_w_in, v_pool_mix, v_pool_scale, v_f_bias, v_ssd_conv_w, v_ssd_conv_b, v_ssd_dt_bias, v_ssd_a_log, v_ssd_d, v_ssd_norm, v_p_pool, v_p_attn, v_p_ssd, v_w_out, v_norm_ffn, v_ffn_up, v_ffn_conv_w, v_ffn_conv_b, v_ffn_down, v_norm_final):
    args = dict(locals())
    w_sh = {k: args[k] for k in WEIGHTS}
    m_sh = {k: args['m_' + k] for k in WEIGHTS}
    v_sh = {k: args['v_' + k] for k in WEIGHTS}
    depth = norm_mix.shape[0]
    place = 2 * lax.axis_index("x") + lax.axis_index("y")
    row_sharded = [k for k in BIG if k not in COL_SHARDED]

    sent = {k: w_sh[k].astype(MXU_DTYPE) for k in BIG}
    sent.update({k: w_sh[k] for k in SMALL_SHARDED})
    sharded = BIG + SMALL_SHARDED
    assert depth == 2
    W = {k: w_sh[k] for k in SMALL if k not in SMALL_SHARDED}
    W.update({k: [None] * depth for k in sharded if k != 'w_in'})
    W['w_in'] = {seg: [None] * depth for seg in 'pzxgf'}
    zero = jnp.zeros((), jnp.int32)

    def install(l, gathered):
        for k, gk in zip(sharded, gathered):
            gk = lax.dynamic_update_slice(gk, sent[k][l][None], (place, zero, zero))
            if k in row_sharded:
                W[k][l] = gk.reshape(-1, gk.shape[-1])
            elif k == 'w_in':
                for seg, a in _w_in_segments([gk[p] for p in range(N_PLACES)]).items():
                    W[k][seg][l] = a
            else:
                W[k][l] = jnp.concatenate([gk[p] for p in range(N_PLACES)], axis=-1)

    install(0, _gather_layer([sent[k][0] for k in sharded], 0, "gather_layer0"))

    loss_local, grad_x, grads, g_final = _local_step(
        x[0], loss_target[0], W, ([sent[k][1] for k in sharded], functools.partial(install, 1)))
    loss = lax.psum(loss_local, ("x", "y", "c"))

    assert depth == 2
    def place_major(k, l):
        if k == 'w_in':
            c = IN_TOTAL // N_PLACES
            return jnp.stack([_w_in_columns(grads[l][k], p * c, (p + 1) * c) for p in range(N_PLACES)])
        return _to_place_major(grads[l][k], k)

    g_big = [[place_major(k, l) for l in range(depth)] for k in BIG]
    small_names = [k for k in SMALL if k != 'norm_final'] + ['norm_final']
    small_full = [jnp.stack([grads[l][k] for l in range(depth)]) for k in small_names[:-1]] + [g_final]
    small_full_shapes = [a.shape for a in small_full]
    small_packed = _pack_rows(small_full, 32).reshape(2, -1, PACK_W)
    g_small = [jnp.broadcast_to(small_packed[h][None], (N_PLACES,) + small_packed.shape[1:]) for h in range(2)]

    core = lax.axis_index("c")
    pos = jnp.stack([place, core]).astype(jnp.int32)
    g_all = g_big + [g_small]
    theirs = _reduce_sibling(g_all, "reduce_sibling")
    wire = [WIRE_DTYPE] * len(BIG) + [F32]
    halves = [_add_own_slot(g, t, pos, dt, f"reduce_sibling_add{j}")
              for j, (g, t, dt) in enumerate(zip(g_all, theirs, wire))]
    recv = _reduce_places(halves, "reduce_places")
    qs = [_sum_places(g, t, r, pos, f"reduce_places_add{j}") for j, (g, t, r) in enumerate(zip(g_all, theirs, recv))]
    others = _swap_sibling(qs, "reduce_swap")

    def mine(k, a):
        if k in SMALL_SHARDED:
            c = a.shape[-1] // N_PLACES
            return lax.dynamic_slice_in_dim(a, place * c, c, axis=a.ndim - 1)
        return a

    outs = {}
    for j, k in enumerate(BIG):
        res = _adamw_pair(qs[j], others[j], w_sh[k], m_sh[k], v_sh[k], pos, "adamw_" + k)
        for prefix, a in zip(('grad_', 'delta_', 'new_m_', 'new_v_'), res):
            outs[prefix + k] = a
    small_sum = jnp.where(core == 0, jnp.concatenate([qs[-1], others[-1]]), jnp.concatenate([others[-1], qs[-1]]))
    g_small_list = [mine(k, a) for k, a in zip(small_names, _unpack_rows(small_sum, small_full_shapes))]
    shapes = [a.shape for a in g_small_list]
    gp = _pack_rows(g_small_list, 128)
    wp, mp, vp = (_pack_rows([d[k] for k in small_names], 128) for d in (w_sh, m_sh, v_sh))
    delta_p, m_p, v_p = _adamw(gp, wp, mp, vp, "adamw_small")
    for prefix, buf in (('grad_', gp), ('delta_', delta_p), ('new_m_', m_p), ('new_v_', v_p)):
        for k, a in zip(small_names, _unpack_rows(buf, shapes)):
            outs[prefix + k] = a
    result = [loss, grad_x[None]]
    for prefix in ('grad_', 'delta_', 'new_m_', 'new_v_'):
        result += [outs[prefix + k] for k in WEIGHTS]
    return tuple(result)
```

```python
import functools

import jax
import jax.numpy as jnp
from jax import lax
from jax.experimental import pallas as pl
from jax.experimental.pallas import tpu as pltpu

F32 = jnp.float32
MXU_DTYPE = jnp.bfloat16
WIRE_DTYPE = jnp.bfloat16
NORM_EPS = 1e-6
HALO = 16
LANE = 128
NEG_BIG = -1e30
VMEM_LIMIT = 52 * 1024 * 1024

D_MODEL = 1024
POOL_WINDOWS = (2, 4, 8, 16)
POOL_W = 512
HEADS = 8
HEAD_DIM = 64
ATTN_W = 512
ATTN_SCALE = HEAD_DIM ** -0.5
SSD_W = 1024
SSD_HEADS = 16
SSD_P = 64
SSD_N = 128
SSD_CHUNK = 128
SSD_CONV_CH = 1536
FFN = 2816
DT_LANE0 = 8
IN_SPLITS = (512, 512, 512, 512, 8, 1024, 1536, 16, 3072)
IN_TOTAL = sum(IN_SPLITS)
N_PLACES = 4

ADAM_LR, ADAM_B1, ADAM_B2, ADAM_EPS, ADAM_WD, ADAM_STEP = 0.001, 0.9, 0.999, 1e-08, 0.01, 10

BIG = ('w_in', 'p_pool', 'p_attn', 'p_ssd', 'w_out', 'ffn_up', 'ffn_down')
COL_SHARDED = ('w_in', 'p_pool', 'p_attn', 'ffn_up')
SMALL = ('norm_mix', 'pool_mix', 'pool_scale', 'f_bias', 'ssd_conv_w', 'ssd_conv_b', 'ssd_dt_bias',
         'ssd_a_log', 'ssd_d', 'ssd_norm', 'norm_ffn', 'ffn_conv_w', 'ffn_conv_b', 'norm_final')
SMALL_SHARDED = ('ssd_conv_w', 'ffn_conv_w')
WEIGHTS = ('norm_mix', 'w_in', 'pool_mix', 'pool_scale', 'f_bias', 'ssd_conv_w', 'ssd_conv_b', 'ssd_dt_bias',
           'ssd_a_log', 'ssd_d', 'ssd_norm', 'p_pool', 'p_attn', 'p_ssd', 'w_out', 'norm_ffn', 'ffn_up',
           'ffn_conv_w', 'ffn_conv_b', 'ffn_down', 'norm_final')
PACK_W = 1024


def _params(sem):
    return pltpu.CompilerParams(dimension_semantics=sem, vmem_limit_bytes=VMEM_LIMIT)


def _tile(n, prefs=(512, 256, 128)):
    for t in prefs:
        if n % t == 0:
            return t
    return n


def _sigmoid(x):
    return 0.5 * jnp.tanh(0.5 * x) + 0.5


def _softplus(x):
    return jnp.maximum(x, 0.0) + jnp.log1p(jnp.exp(-jnp.abs(x)))


def _dot(a, b, dims=((1,), (0,))):
    return lax.dot_general(a.astype(MXU_DTYPE), b.astype(MXU_DTYPE), (dims, ((), ())),
                           preferred_element_type=F32)


NT = ((1,), (1,))


def _mm(a, b, *, ta=False, tb=False, acc=None, out_dtype=F32, name):
    M, K = (a.shape[1], a.shape[0]) if ta else a.shape
    N = b.shape[0] if tb else b.shape[1]
    big = (1024, 1408, 512, 256, 128)
    tm, tn = _tile(M, big), _tile(N, big)
    tk = K if K <= 1024 else _tile(K, (512, 256, 128) if ta else big)
    nk = K // tk
    a_spec = pl.BlockSpec((tk, tm), lambda i, j, k: (k, i)) if ta else pl.BlockSpec((tm, tk), lambda i, j, k: (i, k))
    b_spec = pl.BlockSpec((tn, tk), lambda i, j, k: (j, k)) if tb else pl.BlockSpec((tk, tn), lambda i, j, k: (k, j))
    in_specs = [a_spec, b_spec]
    args = [a, b]
    if acc is not None:
        in_specs.append(pl.BlockSpec((tm, tn), lambda i, j, k: (i, j)))
        args.append(acc)

    def body(*refs):
        if acc is not None:
            a_ref, b_ref, c_ref, o_ref, acc_ref = refs
        else:
            a_ref, b_ref, o_ref, acc_ref = refs
        k = pl.program_id(2)

        @pl.when(k == 0)
        def _():
            if acc is not None:
                acc_ref[...] = c_ref[...].astype(F32)
            else:
                acc_ref[...] = jnp.zeros_like(acc_ref)

        av = a_ref[...]
        if ta:
            av = av.astype(F32).T
        acc_ref[...] += _dot(av, b_ref[...], NT if tb else ((1,), (0,)))

        @pl.when(k == nk - 1)
        def _():
            o_ref[...] = acc_ref[...].astype(out_dtype)

    return pl.pallas_call(
        body, grid=(M // tm, N // tn, nk), in_specs=in_specs,
        out_specs=pl.BlockSpec((tm, tn), lambda i, j, k: (i, j)),
        out_shape=jax.ShapeDtypeStruct((M, N), out_dtype),
        scratch_shapes=[pltpu.VMEM((tm, tn), F32)],
        compiler_params=_params(("parallel", "parallel", "arbitrary")), name=name)(*args)


def _rows(body, S, T, *, rows=(), consts=(), prevs=(), nexts=(), out_rows=(), out_accs=(), scratch=(), name):
    n = S // T
    hb = T // HALO
    last_h = S // HALO - 1

    def norm(r):
        return r if isinstance(r, tuple) else (r, r.shape[1], 0)

    rows, prevs, nexts = [norm(r) for r in rows], [norm(r) for r in prevs], [norm(r) for r in nexts]
    in_specs, args = [], []
    for arr, W, cb in rows:
        in_specs.append(pl.BlockSpec((T, W), lambda i, cb=cb: (i, cb)))
        args.append(arr)
    for cst in consts:
        in_specs.append(pl.BlockSpec(cst.shape, lambda i, nd=cst.ndim: (0,) * nd))
        args.append(cst)
    for arr, W, cb in prevs:
        in_specs.append(pl.BlockSpec((HALO, W), lambda i, cb=cb: (jnp.maximum(i * hb - 1, 0), cb)))
        args.append(arr)
    for arr, W, cb in nexts:
        in_specs.append(pl.BlockSpec((HALO, W), lambda i, cb=cb: (jnp.minimum((i + 1) * hb, last_h), cb)))
        args.append(arr)
    out_specs = [pl.BlockSpec((T, W), lambda i: (i, 0)) for W, _ in out_rows]
    out_specs += [pl.BlockSpec(shp, lambda i, nd=len(shp): (0,) * nd) for shp, _ in out_accs]
    out_shape = [jax.ShapeDtypeStruct((S, W), dt) for W, dt in out_rows]
    out_shape += [jax.ShapeDtypeStruct(shp, dt) for shp, dt in out_accs]
    cuts = [len(rows), len(consts), len(prevs), len(nexts), len(out_rows), len(out_accs), len(scratch)]

    def kern(*refs):
        groups, pos = [], 0
        for c in cuts:
            groups.append(list(refs[pos:pos + c]))
            pos += c
        i = pl.program_id(0)

        @pl.when(i == 0)
        def _():
            for a_ref in groups[5]:
                a_ref[...] = jnp.zeros_like(a_ref)

        body(i, n, *groups)

    outs = pl.pallas_call(kern, grid=(n,), in_specs=in_specs, out_specs=out_specs, out_shape=out_shape,
                          scratch_shapes=list(scratch), compiler_params=_params(("arbitrary",)), name=name)(*args)
    return outs


def _fill_prev(ext, prev_ref, cur, i):
    ext[0:HALO, :] = jnp.where(i > 0, prev_ref[...].astype(F32), 0.0)
    ext[HALO:, :] = cur


def _row_ids(i, T, W=1):
    return i * T + lax.broadcasted_iota(jnp.int32, (T, W), 0)


SUB_ROWS = 32
WIN_PAD = 8


def _tile_loop(T, fn):
    n = T // SUB_ROWS
    fn(0, True, n == 1)
    if n > 2:
        def body(rb, carry):
            fn(pl.multiple_of(rb * SUB_ROWS, SUB_ROWS), False, False)
            return carry
        lax.fori_loop(1, n - 1, body, 0)
    if n > 1:
        fn((n - 1) * SUB_ROWS, False, True)


def _win_prev(x_ref, prev_ref, i, r0, first, cols):
    if first:
        top = jnp.where(i > 0, prev_ref[HALO - WIN_PAD:HALO, cols].astype(F32), 0.0)
        return jnp.concatenate([top, x_ref[0:SUB_ROWS, cols].astype(F32)], axis=0)
    start = r0 - WIN_PAD if isinstance(r0, int) else pl.multiple_of(r0 - WIN_PAD, WIN_PAD)
    return x_ref[pl.ds(start, SUB_ROWS + WIN_PAD), cols].astype(F32)


def _behind(win, j):
    return win[WIN_PAD:, :] if j == 0 else pltpu.roll(win, j, axis=0)[WIN_PAD:, :]


def _win_next(x_ref, next_ref, i, n, r0, last, cols):
    if last:
        bot = jnp.where(i < n - 1, next_ref[0:WIN_PAD, cols].astype(F32), 0.0)
        return jnp.concatenate([x_ref[r0:r0 + SUB_ROWS, cols].astype(F32), bot], axis=0)
    return x_ref[pl.ds(r0, SUB_ROWS + WIN_PAD), cols].astype(F32)


def _ahead(win, j):
    return win[:SUB_ROWS, :] if j == 0 else pltpu.roll(win, SUB_ROWS + WIN_PAD - j, axis=0)[:SUB_ROWS, :]


def _taps(win, K):
    return [_behind(win, K - 1 - k) for k in range(K)]


def _conv_win(taps, w_ref, b_ref, cols):
    out = b_ref[:, cols]
    for k, tap in enumerate(taps):
        out = out + tap * w_ref[k:k + 1, cols]
    return out


def _conv_wgrad_win(acc_ref, taps, d, cols):
    for k, tap in enumerate(taps):
        acc_ref[k:k + 1, cols] += jnp.sum(d * tap, axis=0, keepdims=True)


def _norm_fwd(x, w, name):
    S, D = x.shape

    def body(i, n, R, C, P, N, O, A, Sc):
        xv = R[0][...]
        r = lax.rsqrt(jnp.mean(xv * xv, axis=-1, keepdims=True) + NORM_EPS)
        O[0][...] = (xv * r * C[0][...]).astype(MXU_DTYPE)

    return _rows(body, S, _tile(S), rows=[x], consts=[w], out_rows=[(D, MXU_DTYPE)], name=name)[0]


def _norm_bwd_math(xv, w, du):
    r = lax.rsqrt(jnp.mean(xv * xv, axis=-1, keepdims=True) + NORM_EPS)
    xh = xv * r
    g = du * w
    dx = r * (g - xh * jnp.mean(g * xh, axis=-1, keepdims=True))
    dw = jnp.sum(du * xh, axis=0, keepdims=True)
    return dx, dw


def _norm_bwd(x, w, du, dres, name):
    S, D = x.shape

    def body(i, n, R, C, P, N, O, A, Sc):
        dx, dw = _norm_bwd_math(R[0][...], C[0][...], R[1][...])
        O[0][...] = R[2][...] + dx
        A[0][...] += dw

    return _rows(body, S, _tile(S), rows=[x, du, dres], consts=[w], out_rows=[(D, F32)],
                 out_accs=[((1, D), F32)], name=name)


def _loss_head(x, w, target, name):
    S, D = x.shape

    def body(i, n, R, C, P, N, O, A, Sc):
        xv, w_, tg = R[0][...], C[0][...], R[1][...]
        r = lax.rsqrt(jnp.mean(xv * xv, axis=-1, keepdims=True) + NORM_EPS)
        e = xv * r * w_ - tg
        A[1][...] += jnp.broadcast_to(0.5 * jnp.sum(jnp.mean(e * e, axis=-1, keepdims=True)), (1, LANE))
        dx, dw = _norm_bwd_math(xv, w_, e / D)
        O[0][...] = dx
        A[0][...] += dw

    return _rows(body, S, _tile(S), rows=[x, target], consts=[w], out_rows=[(D, F32)],
                 out_accs=[((1, D), F32), ((1, LANE), F32)], name=name)


def _pool_fwd(pqkv, mix, scale, name):
    S = pqkv.shape[0]
    T = _tile(S, (256, 128))

    def body(i, n, R, C, P, N, O, A, Sc):
        ext = Sc[0]
        v = R[0][...]
        _fill_prev(ext, P[0], v, i)
        t1 = (_row_ids(i, T) + 1).astype(F32)
        for g, w in enumerate(POOL_WINDOWS):
            cols = slice(g * LANE, (g + 1) * LANE)
            acc = v[:, cols]
            for j in range(1, w):
                acc = acc + ext[pl.ds(HALO - j, T), cols]
            d = (acc / jnp.minimum(t1, float(w)) - v[:, cols]).astype(MXU_DTYPE)
            O[0][:, cols] = d
            O[1][:, cols] = (_dot(d, C[0][g]) * C[1][:, cols]).astype(MXU_DTYPE)

    return _rows(body, S, T, rows=[(pqkv, POOL_W, 0)], prevs=[(pqkv, POOL_W, 0)], consts=[mix, scale],
                 out_rows=[(POOL_W, MXU_DTYPE), (POOL_W, MXU_DTYPE)],
                 scratch=[pltpu.VMEM((HALO + T, POOL_W), F32)], name=name)


def _pool_bwd_a(dypm, d, mix, scale, name):
    S = d.shape[0]
    T = _tile(S, (256, 128))

    def body(i, n, R, C, P, N, O, A, Sc):
        for g in range(len(POOL_WINDOWS)):
            cols = slice(g * LANE, (g + 1) * LANE)
            dg = R[1][:, cols]
            dy = R[0][:, cols]
            yg = _dot(dg, C[0][g])
            A[0][:, cols] += jnp.sum(dy * yg, axis=0, keepdims=True)
            dys = dy * C[1][:, cols]
            A[1][cols, :] += _dot(dg.astype(F32).T, dys)
            O[0][:, cols] = _dot(dys, C[0][g], NT)

    return _rows(body, S, T, rows=[dypm, d], consts=[mix, scale], out_rows=[(POOL_W, F32)],
                 out_accs=[((1, POOL_W), F32), ((POOL_W, LANE), F32)], name=name)


def _pool_bwd_b(dd, name):
    S = dd.shape[0]
    T = _tile(S, (256, 128))

    def body(i, n, R, C, P, N, O, A, Sc):
        ext = Sc[0]
        ddv = R[0][...]
        t1 = (_row_ids(i, T) + 1).astype(F32)
        nxt = jnp.where(i < n - 1, N[0][...], 0.0)
        for g, w in enumerate(POOL_WINDOWS):
            cols = slice(g * LANE, (g + 1) * LANE)
            ext[0:T, cols] = ddv[:, cols] / jnp.minimum(t1, float(w))
            ext[T:, cols] = nxt[:, cols] / float(w)
        for g, w in enumerate(POOL_WINDOWS):
            cols = slice(g * LANE, (g + 1) * LANE)
            acc = ext[0:T, cols]
            for j in range(1, w):
                acc = acc + ext[pl.ds(j, T), cols]
            O[0][:, cols] = (acc - ddv[:, cols]).astype(MXU_DTYPE)

    return _rows(body, S, T, rows=[dd], nexts=[dd], out_rows=[(POOL_W, MXU_DTYPE)],
                 scratch=[pltpu.VMEM((T + HALO, POOL_W), F32)], name=name)[0]


def _lane_cumsum(seg, reverse=False):
    lane = lax.broadcasted_iota(jnp.int32, seg.shape, 1)
    sh = 1
    while sh < LANE:
        if reverse:
            seg = seg + jnp.where(lane < LANE - sh, pltpu.roll(seg, LANE - sh, axis=1), 0.0)
        else:
            seg = seg + jnp.where(lane >= sh, pltpu.roll(seg, sh, axis=1), 0.0)
        sh *= 2
    return seg


def _logf_cumsum(fT, bias, name):
    H, S = fT.shape
    TB = _tile(S)
    nb = S // TB

    def body(f_ref, b_ref, o_ref, carry):
        @pl.when(pl.program_id(0) == 0)
        def _():
            carry[...] = jnp.zeros_like(carry)

        x = f_ref[...] + b_ref[...]
        lf = jnp.minimum(x, 0.0) - jnp.log1p(jnp.exp(-jnp.abs(x)))
        c = carry[...]
        for j in range(TB // LANE):
            seg = _lane_cumsum(lf[:, j * LANE:(j + 1) * LANE]) + c
            o_ref[:, j * LANE:(j + 1) * LANE] = seg
            c = seg[:, LANE - 1:LANE]
        carry[...] = c

    return pl.pallas_call(
        body, grid=(nb,), in_specs=[pl.BlockSpec((H, TB), lambda i: (0, i)), pl.BlockSpec((H, 1), lambda i: (0, 0))],
        out_specs=pl.BlockSpec((H, TB), lambda i: (0, i)), out_shape=jax.ShapeDtypeStruct((H, S), F32),
        scratch_shapes=[pltpu.VMEM((H, 1), F32)], compiler_params=_params(("arbitrary",)), name=name)(fT, bias)


def _logf_cumsum_bwd(fT, bias, dc, dcq, name):
    H, S = fT.shape
    TB = _tile(S)
    nb = S // TB

    def body(f_ref, b_ref, dc_ref, dcq_ref, o_ref, db_ref, carry):
        @pl.when(pl.program_id(0) == 0)
        def _():
            carry[...] = jnp.zeros_like(carry)
            db_ref[...] = jnp.zeros_like(db_ref)

        x = f_ref[...] + b_ref[...]
        sg = _sigmoid(-x)
        dcv = dc_ref[...] + dcq_ref[...]
        c = carry[...]
        db = jnp.zeros((H, 1), F32)
        for j in reversed(range(TB // LANE)):
            seg = _lane_cumsum(dcv[:, j * LANE:(j + 1) * LANE], reverse=True) + c
            df = seg * sg[:, j * LANE:(j + 1) * LANE]
            o_ref[:, j * LANE:(j + 1) * LANE] = df
            db = db + jnp.sum(df, axis=1, keepdims=True)
            c = seg[:, 0:1]
        carry[...] = c
        db_ref[...] += db

    rev = lambda i: (0, nb - 1 - i)
    return pl.pallas_call(
        body, grid=(nb,),
        in_specs=[pl.BlockSpec((H, TB), rev), pl.BlockSpec((H, 1), lambda i: (0, 0)), pl.BlockSpec((H, TB), rev),
                  pl.BlockSpec((H, TB), rev)],
        out_specs=[pl.BlockSpec((H, TB), rev), pl.BlockSpec((H, 1), lambda i: (0, 0))],
        out_shape=[jax.ShapeDtypeStruct((H, S), F32), jax.ShapeDtypeStruct((H, 1), F32)],
        scratch_shapes=[pltpu.VMEM((H, 1), F32)], compiler_params=_params(("arbitrary",)), name=name)(
            fT, bias, dc, dcq)


def _attn_scores(q, k, ck, diagonal, T):
    s = _dot(q, k, NT) - ck
    if diagonal:
        tril = lax.broadcasted_iota(jnp.int32, (T, T), 1) <= lax.broadcasted_iota(jnp.int32, (T, T), 0)
        s = jnp.where(tril, s, NEG_BIG)
    return s


def _attn_fwd(pqkv, c_row, name, gather=(), gather_owner=1):
    S = pqkv.shape[0]
    T = _tile(S, (1024, 512, 256, 128))
    nq = S // T
    qb, kb, vb = (ATTN_W * j // LANE for j in (1, 2, 3))

    steps = jnp.asarray([[qi for qi in range(nq) for ki in range(qi + 1)],
                         [ki for qi in range(nq) for ki in range(qi + 1)]], jnp.int32)

    ng = len(gather)
    n_steps = steps.shape[1]

    def body(st_ref, q_ref, k_ref, v_ref, ck_ref, *rest):
        shard_refs, (o_ref, lse_ref), rest = rest[:ng], rest[ng:ng + 2], rest[ng + 2:]
        gathered_refs, (m_s, l_s, acc_s), sems = rest[:ng], rest[ng:ng + 3], rest[ng + 3:]
        qi, ki = st_ref[0, pl.program_id(1)], st_ref[1, pl.program_id(1)]

        if ng:
            start, forward, drain = _gather_copies(shard_refs, gathered_refs, gather, sems, gather_owner)
            last_pair = pl.program_id(0) == HEADS // 2 - 1
            pl.when((pl.program_id(0) == 0) & (pl.program_id(1) == 0))(start)
            pl.when(last_pair & (pl.program_id(1) == 0))(forward)

        @pl.when(ki == 0)
        def _():
            m_s[...] = jnp.full_like(m_s, NEG_BIG)
            l_s[...] = jnp.zeros_like(l_s)
            acc_s[...] = jnp.zeros_like(acc_s)

        def step(diagonal):
            qf, kv, vv = q_ref[...] * ATTN_SCALE, k_ref[...].astype(MXU_DTYPE), v_ref[...].astype(MXU_DTYPE)
            acc = acc_s[...]
            for a in range(2):
                mk = _half_mask(a)
                s = _attn_scores(qf * mk, kv, ck_ref[a], diagonal, T)
                m_new = jnp.maximum(m_s[a], jnp.max(s, axis=1, keepdims=True))
                alpha = jnp.exp(m_s[a] - m_new)
                p = jnp.exp(s - m_new)
                l_s[a] = alpha * l_s[a] + jnp.sum(p, axis=1, keepdims=True)
                acc = acc * (1.0 + mk * (alpha - 1.0)) + _dot(p, vv) * mk
                m_s[a] = m_new
            acc_s[...] = acc

        @pl.when(ki < qi)
        def _():
            step(False)

        @pl.when(ki == qi)
        def _():
            step(True)
            o_ref[...] = acc_s[...] / (l_s[0] * _half_mask(0) + l_s[1] * _half_mask(1))
            for a in range(2):
                lse_ref[a] = m_s[a] + jnp.log(l_s[a])

        if ng:
            pl.when(last_pair & (pl.program_id(1) == n_steps - 1))(drain)

    outs = pl.pallas_call(
        body, grid_spec=pltpu.PrefetchScalarGridSpec(
            num_scalar_prefetch=1, grid=(HEADS // 2, n_steps),
            in_specs=[pl.BlockSpec((T, LANE), lambda hp, s, st: (st[0, s], qb + hp)),
                      pl.BlockSpec((T, LANE), lambda hp, s, st: (st[1, s], kb + hp)),
                      pl.BlockSpec((T, LANE), lambda hp, s, st: (st[1, s], vb + hp)),
                      pl.BlockSpec((2, 1, T), lambda hp, s, st: (hp, 0, st[1, s]))] + [_ANY] * ng,
            out_specs=[pl.BlockSpec((T, LANE), lambda hp, s, st: (st[0, s], hp)),
                       pl.BlockSpec((2, T, 1), lambda hp, s, st: (hp, st[0, s], 0))] + [_ANY] * ng,
            scratch_shapes=[pltpu.VMEM((2, T, 1), F32), pltpu.VMEM((2, T, 1), F32), pltpu.VMEM((T, LANE), F32)]
            + (_gather_sems(gather) if ng else [])),
        out_shape=[jax.ShapeDtypeStruct((S, ATTN_W), F32), jax.ShapeDtypeStruct((HEADS, S, 1), F32)]
        + _gather_outs(gather),
        compiler_params=_params(("arbitrary", "arbitrary")), name=name)(steps, pqkv, pqkv, pqkv, c_row, *gather)
    return outs[0], outs[1], outs[2:]


def _attn_bwd(pqkv, do, o, c_row, lse, name):
    S = pqkv.shape[0]
    T = _tile(S, (1024, 512, 256, 128))
    nq = S // T
    qb, kb, vb = (ATTN_W * j // LANE for j in (1, 2, 3))

    steps = jnp.asarray([[ki for ki in range(nq) for qi in range(ki, nq)],
                         [qi for ki in range(nq) for qi in range(ki, nq)]], jnp.int32)

    def body(st_ref, q_ref, k_ref, v_ref, do_ref, o_ref, ck_ref, lse_ref, dq_ref, dk_ref, dv_ref, dc_ref, dcq_ref,
             dk_s, dv_s, dc_s):
        ki, qi = st_ref[0, pl.program_id(1)], st_ref[1, pl.program_id(1)]

        @pl.when(pl.program_id(1) == 0)
        def _():
            dq_ref[...] = jnp.zeros_like(dq_ref)
            dcq_ref[...] = jnp.zeros_like(dcq_ref)

        @pl.when(qi == ki)
        def _():
            dk_s[...] = jnp.zeros_like(dk_s)
            dv_s[...] = jnp.zeros_like(dv_s)
            dc_s[...] = jnp.zeros_like(dc_s)

        def step(diagonal):
            qf, kf, vv = q_ref[...] * ATTN_SCALE, k_ref[...], v_ref[...].astype(MXU_DTYPE)
            dof, ov = do_ref[...], o_ref[...]
            rows = pl.ds(pl.multiple_of(qi * T, T), T)
            lane = lax.broadcasted_iota(jnp.int32, (1, LANE), 1)
            for a in range(2):
                mk = _half_mask(a)
                qa, ka, doa = (qf * mk).astype(MXU_DTYPE), (kf * mk).astype(MXU_DTYPE), dof * mk
                s = _attn_scores(qa, ka, ck_ref[a], diagonal, T)
                p = jnp.exp(s - lse_ref[a])
                delta = jnp.sum(doa * ov, axis=1, keepdims=True)
                dv_s[...] += _dot(p.T, doa)
                dp = _dot(doa, vv, NT)
                ds = p * (dp - delta)
                dc_s[a] -= jnp.sum(ds, axis=0, keepdims=True)
                dq_ref[rows, :] += _dot(ds, ka)
                dcq_ref[rows, :] += jnp.sum(ds, axis=1, keepdims=True) * (lane == a).astype(F32)
                dk_s[...] += _dot(ds.T, qa)

        @pl.when(qi > ki)
        def _():
            step(False)

        @pl.when(qi == ki)
        def _():
            step(True)

        @pl.when(qi == nq - 1)
        def _():
            dk_ref[...] = dk_s[...]
            dv_ref[...] = dv_s[...]
            dc_ref[...] = dc_s[...]

    qrow = lambda hp, s, st: (st[1, s], hp)
    krow = lambda hp, s, st: (st[0, s], hp)
    whole = lambda hp, s, st: (0, hp)
    return pl.pallas_call(
        body, grid_spec=pltpu.PrefetchScalarGridSpec(
            num_scalar_prefetch=1, grid=(HEADS // 2, steps.shape[1]),
            in_specs=[pl.BlockSpec((T, LANE), lambda hp, s, st: (st[1, s], qb + hp)),
                      pl.BlockSpec((T, LANE), lambda hp, s, st: (st[0, s], kb + hp)),
                      pl.BlockSpec((T, LANE), lambda hp, s, st: (st[0, s], vb + hp)),
                      pl.BlockSpec((T, LANE), qrow), pl.BlockSpec((T, LANE), qrow),
                      pl.BlockSpec((2, 1, T), lambda hp, s, st: (hp, 0, st[0, s])),
                      pl.BlockSpec((2, T, 1), lambda hp, s, st: (hp, st[1, s], 0))],
            out_specs=[pl.BlockSpec((S, LANE), whole), pl.BlockSpec((T, LANE), krow), pl.BlockSpec((T, LANE), krow),
                       pl.BlockSpec((2, 1, T), lambda hp, s, st: (hp, 0, st[0, s])), pl.BlockSpec((S, LANE), whole)],
            scratch_shapes=[pltpu.VMEM((T, LANE), F32), pltpu.VMEM((T, LANE), F32), pltpu.VMEM((2, 1, T), F32)]),
        out_shape=[jax.ShapeDtypeStruct((S, ATTN_W), F32), jax.ShapeDtypeStruct((S, ATTN_W), F32),
                   jax.ShapeDtypeStruct((S, ATTN_W), F32), jax.ShapeDtypeStruct((HEADS, 1, S), F32),
                   jax.ShapeDtypeStruct((S, ATTN_W), F32)],
        compiler_params=_params(("arbitrary", "arbitrary")), name=name)(
            steps, pqkv, pqkv, pqkv, do, o, c_row, lse)


CONV_COLS = 512


def _conv_bwd_b(dpre, w, name):
    S, C = dpre.shape
    K = w.shape[0]
    T = _tile(S, (256, 128))

    def body(i, n, R, Cs, P, N, O, A, Sc):
        def tile(r0, first, last):
            for c0 in range(0, C, CONV_COLS):
                cols = slice(c0, c0 + CONV_COLS)
                win = _win_next(R[0], N[0], i, n, r0, last, cols)
                out = None
                for k in range(K):
                    term = _ahead(win, K - 1 - k) * Cs[0][k:k + 1, cols]
                    out = term if out is None else out + term
                O[0][pl.ds(r0, SUB_ROWS), cols] = out.astype(MXU_DTYPE)

        _tile_loop(T, tile)

    return _rows(body, S, T, rows=[dpre], nexts=[dpre], consts=[w], out_rows=[(C, MXU_DTYPE)], name=name)[0]


def _dt_mask():
    lane = lax.broadcasted_iota(jnp.int32, (1, LANE), 1)
    return ((lane >= DT_LANE0) & (lane < DT_LANE0 + SSD_HEADS)).astype(F32)


def _ssd_pre_fwd(xbc, fdt, cw, cb, dtb, name):
    S, C = xbc.shape
    T = _tile(S, (256, 128))
    K = cw.shape[0]

    def body(i, n, R, Cs, P, N, O, A, Sc):
        def tile(r0, first, last):
            rows = pl.ds(r0, SUB_ROWS)
            for c0 in range(0, C, CONV_COLS):
                cols = slice(c0, c0 + CONV_COLS)
                pre = _conv_win(_taps(_win_prev(R[0], P[0], i, r0, first, cols), K), Cs[0], Cs[1], cols)
                O[0][rows, cols] = pre * _sigmoid(pre)
            O[1][rows, :] = _softplus(R[1][rows, :] + Cs[2][...]) * _dt_mask()

        _tile_loop(T, tile)

    return _rows(body, S, T, rows=[xbc, fdt], prevs=[xbc], consts=[cw, cb, dtb],
                 out_rows=[(C, F32), (LANE, F32)], name=name)


def _silu_grad(pre):
    sg = _sigmoid(pre)
    return sg * (1.0 + pre * (1.0 - sg))


def _ssd_pre_bwd_a(xbc, fdt, dxa, ddtw, cw, cb, dtb, name):
    S, C = xbc.shape
    T = _tile(S, (256, 128))
    K = cw.shape[0]

    def body(i, n, R, Cs, P, N, O, A, Sc):
        def tile(r0, first, last):
            rows = pl.ds(r0, SUB_ROWS)
            for c0 in range(0, C, CONV_COLS):
                cols = slice(c0, c0 + CONV_COLS)
                win = _taps(_win_prev(R[0], P[0], i, r0, first, cols), K)
                dpre = R[2][rows, cols] * _silu_grad(_conv_win(win, Cs[0], Cs[1], cols))
                O[0][rows, cols] = dpre
                _conv_wgrad_win(A[0], win, dpre, cols)
                A[1][:, cols] += jnp.sum(dpre, axis=0, keepdims=True)
            ddt = R[3][rows, :] * _sigmoid(R[1][rows, :] + Cs[2][...]) * _dt_mask()
            O[1][rows, :] = ddt
            A[2][...] += jnp.sum(ddt, axis=0, keepdims=True)

        _tile_loop(T, tile)

    return _rows(body, S, T, rows=[xbc, fdt, dxa, ddtw], prevs=[xbc], consts=[cw, cb, dtb],
                 out_rows=[(C, F32), (LANE, F32)],
                 out_accs=[((K, C), F32), ((1, C), F32), ((1, LANE), F32)], name=name)


def _split3(x):
    hi = x.astype(jnp.bfloat16)
    r1 = x - hi.astype(F32)
    mid = r1.astype(jnp.bfloat16)
    lo = (r1 - mid.astype(F32)).astype(jnp.bfloat16)
    return hi, mid, lo


def _expand_mat():
    r = lax.broadcasted_iota(jnp.int32, (LANE, SSD_W), 0)
    c = lax.broadcasted_iota(jnp.int32, (LANE, SSD_W), 1)
    return (r - DT_LANE0 == c // SSD_P).astype(jnp.bfloat16)


def _headsum_mat():
    r = lax.broadcasted_iota(jnp.int32, (SSD_W, LANE), 0)
    c = lax.broadcasted_iota(jnp.int32, (SSD_W, LANE), 1)
    return (c - DT_LANE0 == r // SSD_P).astype(jnp.bfloat16)


def _expand(tile, ex):
    return sum(lax.dot_general(part, ex, (((1,), (0,)), ((), ())), preferred_element_type=F32)
               for part in _split3(tile))


def _headsum(full, hs):
    return sum(lax.dot_general(part, hs, (((1,), (0,)), ((), ())), preferred_element_type=F32)
               for part in _split3(full))


def _sub_cumsum(a, reverse=False):
    n = a.shape[0]
    row = lax.broadcasted_iota(jnp.int32, a.shape, 0)
    sh = 1
    while sh < n:
        if reverse:
            a = a + jnp.where(row < n - sh, pltpu.roll(a, n - sh, axis=0), 0.0)
        else:
            a = a + jnp.where(row >= sh, pltpu.roll(a, sh, axis=0), 0.0)
        sh *= 2
    return a


def _chunk_common(xa_ref, dtw_ref, a_row, ex):
    L = SSD_CHUNK
    xs = xa_ref[:, 0:SSD_W]
    dtv = dtw_ref[...]
    acs = _sub_cumsum(dtv * a_row)
    last = acs[L - 1:L, :]
    full = _expand(jnp.concatenate([dtv, jnp.exp(last - acs), jnp.exp(acs),
                                    jnp.broadcast_to(jnp.exp(last), (8, LANE))], axis=0), ex)
    dt_full, dec_full, e_full, elast_full = full[0:L], full[L:2 * L], full[2 * L:3 * L], full[3 * L:3 * L + 1]
    xd = xs * dt_full
    return xs, dtv, acs, last, dt_full, xd, dec_full, e_full, elast_full


def _decay_mask(acs, acsT, col):
    L = SSD_CHUNK
    diff = acs[:, col:col + 1] - acsT[col:col + 1, :]
    tril = lax.broadcasted_iota(jnp.int32, (L, L), 0) >= lax.broadcasted_iota(jnp.int32, (L, L), 1)
    return jnp.where(tril, jnp.exp(jnp.minimum(diff, 0.0)), 0.0)


def _half_mask(h):
    lane = lax.broadcasted_iota(jnp.int32, (1, LANE), 1)
    return ((lane // SSD_P) == (h % 2)).astype(F32)


def _ssd_chunk_fwd(xa, dtw, a_row, d_full, name):
    S = xa.shape[0]
    L, G = SSD_CHUNK, 2
    nc = S // L
    GW = SSD_W // G

    def body(xa_ref, dtw_ref, a_ref, d_ref, y_ref, hp_ref, state):
        @pl.when(pl.program_id(0) == 0)
        def _():
            state[...] = jnp.zeros_like(state)

        ex = _expand_mat()
        xs, dtv, acs, last, dt_full, xd, dec_full, e_full, elast_full = _chunk_common(xa_ref, dtw_ref, a_ref[...], ex)
        acsT = acs.T
        hp_ref[0] = state[...]
        for g in range(G):
            gc = slice(g * GW, (g + 1) * GW)
            Bg = xa_ref[:, SSD_W + g * SSD_N: SSD_W + (g + 1) * SSD_N]
            Cg = xa_ref[:, SSD_W + G * SSD_N + g * SSD_N: SSD_W + G * SSD_N + (g + 1) * SSD_N]
            cb = _dot(Cg, Bg, NT)
            y_off = e_full[:, gc] * _dot(Cg, state[:, gc])
            for hp in range(GW // LANE):
                pc = slice(g * GW + hp * LANE, g * GW + (hp + 1) * LANE)
                xd_pair = xd[:, pc]
                yp = y_off[:, hp * LANE:(hp + 1) * LANE] + d_ref[:, pc] * xs[:, pc]
                for h2 in range(2):
                    h = (g * GW + hp * LANE) // SSD_P + h2
                    m = cb * _decay_mask(acs, acsT, DT_LANE0 + h)
                    yp = yp + _dot(m, xd_pair * _half_mask(h))
                y_ref[:, pc] = yp
            st_new = _dot(Bg.T, xd[:, gc] * dec_full[:, gc])
            state[:, gc] = elast_full[:, gc] * state[:, gc] + st_new

    return pl.pallas_call(
        body, grid=(nc,),
        in_specs=[pl.BlockSpec((L, SSD_CONV_CH), lambda c: (c, 0)), pl.BlockSpec((L, LANE), lambda c: (c, 0)),
                  pl.BlockSpec((1, LANE), lambda c: (0, 0)), pl.BlockSpec((1, SSD_W), lambda c: (0, 0))],
        out_specs=[pl.BlockSpec((L, SSD_W), lambda c: (c, 0)), pl.BlockSpec((1, SSD_N, SSD_W), lambda c: (c, 0, 0))],
        out_shape=[jax.ShapeDtypeStruct((S, SSD_W), F32), jax.ShapeDtypeStruct((nc, SSD_N, SSD_W), F32)],
        scratch_shapes=[pltpu.VMEM((SSD_N, SSD_W), F32)],
        compiler_params=_params(("arbitrary",)), name=name)(xa, dtw, a_row, d_full)


def _ssd_chunk_bwd(xa, dtw, dy, hprev, a_row, d_full, name):
    S = xa.shape[0]
    L, G = SSD_CHUNK, 2
    nc = S // L
    GW = SSD_W // G

    def body(xa_ref, dtw_ref, dy_ref, hp_ref, a_ref, d_ref, dxa_ref, ddt_ref, da_ref, dd_ref, dstate):
        @pl.when(pl.program_id(0) == 0)
        def _():
            dstate[...] = jnp.zeros_like(dstate)
            da_ref[...] = jnp.zeros_like(da_ref)
            dd_ref[...] = jnp.zeros_like(dd_ref)

        ex, hs = _expand_mat(), _headsum_mat()
        a_row = a_ref[...]
        xs, dtv, acs, last, dt_full, xd, dec_full, e_full, elast_full = _chunk_common(xa_ref, dtw_ref, a_row, ex)
        acsT = acs.T
        dyv = dy_ref[...]
        lane = lax.broadcasted_iota(jnp.int32, (L, LANE), 1)
        sub = lax.broadcasted_iota(jnp.int32, (LANE, L), 0)
        dacs_c = jnp.zeros((L, LANE), F32)
        dacs_r = jnp.zeros((LANE, L), F32)
        dxd_parts, yoff_parts, dxdd_parts, hh_parts = [], [], [], []
        for g in range(G):
            gc = slice(g * GW, (g + 1) * GW)
            b0 = SSD_W + g * SSD_N
            c0 = SSD_W + G * SSD_N + g * SSD_N
            Bg = xa_ref[:, b0:b0 + SSD_N]
            Cg = xa_ref[:, c0:c0 + SSD_N]
            Hp = hp_ref[0, :, gc]
            dH = dstate[:, gc]
            cb = _dot(Cg, Bg, NT)
            Gm = _dot(Cg, Hp)
            yoff_parts.append(e_full[:, gc] * Gm)
            dG = e_full[:, gc] * dyv[:, gc]
            dC = _dot(dG, Hp, NT)
            dHp = _dot(Cg.T, dG)
            xdd = xd[:, gc] * dec_full[:, gc]
            dB = _dot(xdd, dH, NT)
            dxdd = _dot(Bg, dH)
            dxdd_parts.append(dxdd)
            hh_parts.append(dH * Hp)
            dstate[:, gc] = dHp + elast_full[:, gc] * dH
            dcb = jnp.zeros((L, L), F32)
            dxd_g = []
            for hp in range(GW // LANE):
                pc = slice(g * GW + hp * LANE, g * GW + (hp + 1) * LANE)
                xd_pair = xd[:, pc]
                dxd_pair = dxdd[:, hp * LANE:(hp + 1) * LANE] * dec_full[:, pc]
                for h2 in range(2):
                    h = (g * GW + hp * LANE) // SSD_P + h2
                    col = DT_LANE0 + h
                    lm = _decay_mask(acs, acsT, col)
                    m = cb * lm
                    dy_h = dyv[:, pc] * _half_mask(h)
                    dm = _dot(dy_h, xd_pair, NT)
                    dxd_pair = dxd_pair + _dot(m.T, dy_h)
                    wm = dm * m
                    dacs_c = dacs_c + jnp.where(lane == col, jnp.sum(wm, axis=1, keepdims=True), 0.0)
                    dacs_r = dacs_r - jnp.where(sub == col, jnp.sum(wm, axis=0, keepdims=True), 0.0)
                    dcb = dcb + dm * lm
                dxd_g.append(dxd_pair)
            dxd_parts.append(jnp.concatenate(dxd_g, axis=1))
            dxa_ref[:, c0:c0 + SSD_N] = dC + _dot(dcb, Bg)
            dxa_ref[:, b0:b0 + SSD_N] = dB + _dot(dcb.T, Cg)
        dxd = jnp.concatenate(dxd_parts, axis=1)
        y_off = jnp.concatenate(yoff_parts, axis=1)
        dxdd_full = jnp.concatenate(dxdd_parts, axis=1)
        hh = jnp.concatenate(hh_parts, axis=1)
        dxa_ref[:, 0:SSD_W] = d_ref[...] * dyv + dxd * dt_full
        sums = _headsum(jnp.concatenate([dyv * xs, dxd * xs, dxdd_full * xd, hh, dyv * y_off], axis=0), hs)
        hs_skip, ddt, hs_dec, hs_state, hs_off = (sums[j * L:(j + 1) * L] for j in range(5))
        dd_ref[...] += jnp.sum(hs_skip, axis=0, keepdims=True)
        w_dec = hs_dec * jnp.exp(last - acs)
        dlast = jnp.sum(w_dec, axis=0, keepdims=True) + jnp.exp(last) * jnp.sum(hs_state, axis=0, keepdims=True)
        dacs = dacs_c + dacs_r.T + hs_off - w_dec
        rowid = lax.broadcasted_iota(jnp.int32, (L, LANE), 0)
        dacs = dacs + jnp.where(rowid == L - 1, dlast, 0.0)
        da = _sub_cumsum(dacs, reverse=True)
        ddt_ref[...] = ddt + da * a_row
        da_ref[...] += jnp.sum(da * dtv, axis=0, keepdims=True)

    rev = lambda c: (nc - 1 - c, 0)
    return pl.pallas_call(
        body, grid=(nc,),
        in_specs=[pl.BlockSpec((L, SSD_CONV_CH), rev), pl.BlockSpec((L, LANE), rev), pl.BlockSpec((L, SSD_W), rev),
                  pl.BlockSpec((1, SSD_N, SSD_W), lambda c: (nc - 1 - c, 0, 0)),
                  pl.BlockSpec((1, LANE), lambda c: (0, 0)), pl.BlockSpec((1, SSD_W), lambda c: (0, 0))],
        out_specs=[pl.BlockSpec((L, SSD_CONV_CH), rev), pl.BlockSpec((L, LANE), rev),
                   pl.BlockSpec((1, LANE), lambda c: (0, 0)), pl.BlockSpec((1, LANE), lambda c: (0, 0))],
        out_shape=[jax.ShapeDtypeStruct((S, SSD_CONV_CH), F32), jax.ShapeDtypeStruct((S, LANE), F32),
                   jax.ShapeDtypeStruct((1, LANE), F32), jax.ShapeDtypeStruct((1, LANE), F32)],
        scratch_shapes=[pltpu.VMEM((SSD_N, SSD_W), F32)],
        compiler_params=_params(("arbitrary",)), name=name)(xa, dtw, dy, hprev, a_row, d_full)


def _ssd_post_fwd(y, z, w, name):
    S = y.shape[0]
    GW = SSD_W // 2

    def body(i, n, R, C, P, N, O, A, Sc):
        zv = R[1][...]
        v = R[0][...] * (zv * _sigmoid(zv))
        for g in range(2):
            gc = slice(g * GW, (g + 1) * GW)
            vg = v[:, gc]
            r = lax.rsqrt(jnp.mean(vg * vg, axis=-1, keepdims=True) + NORM_EPS)
            O[0][:, gc] = (vg * r * C[0][:, gc]).astype(MXU_DTYPE)

    return _rows(body, S, _tile(S, (256, 128)), rows=[y, z], consts=[w], out_rows=[(SSD_W, MXU_DTYPE)], name=name)[0]


def _ssd_post_bwd(y, z, dyn, w, name):
    S = y.shape[0]
    GW = SSD_W // 2

    def body(i, n, R, C, P, N, O, A, Sc):
        yv, zv, dn = R[0][...], R[1][...], R[2][...]
        sz = zv * _sigmoid(zv)
        v = yv * sz
        for g in range(2):
            gc = slice(g * GW, (g + 1) * GW)
            dv, dw = _norm_bwd_math(v[:, gc], C[0][:, gc], dn[:, gc])
            A[0][:, gc] += dw
            O[0][:, gc] = dv * sz[:, gc]
            O[1][:, gc] = (dv * yv[:, gc] * _silu_grad(zv[:, gc])).astype(MXU_DTYPE)

    return _rows(body, S, _tile(S, (256, 128)), rows=[y, z, dyn], consts=[w],
                 out_rows=[(SSD_W, F32), (SSD_W, MXU_DTYPE)], out_accs=[((1, SSD_W), F32)], name=name)


def _merge_fwd(gl, yp, ya, ys, name):
    S, D = yp.shape

    def body(i, n, R, C, P, N, O, A, Sc):
        def tile(r0, first, last):
            rows = pl.ds(r0, SUB_ROWS)
            for c0 in range(0, D, CONV_COLS):
                cols = slice(c0, c0 + CONV_COLS)
                acc = None
                for b in range(3):
                    term = _sigmoid(R[0][rows, b * D + c0:b * D + c0 + CONV_COLS]) * R[1 + b][rows, cols]
                    acc = term if acc is None else acc + term
                O[0][rows, cols] = acc.astype(MXU_DTYPE)

        _tile_loop(T, tile)

    T = _tile(S, (256, 128))
    return _rows(body, S, T, rows=[gl, yp, ya, ys], out_rows=[(D, MXU_DTYPE)], name=name)[0]


def _merge_bwd(gl, yp, ya, ys, dm, name):
    S, D = yp.shape
    T = _tile(S, (256, 128))

    def body(i, n, R, C, P, N, O, A, Sc):
        def tile(r0, first, last):
            rows = pl.ds(r0, SUB_ROWS)
            for c0 in range(0, D, CONV_COLS):
                cols = slice(c0, c0 + CONV_COLS)
                dmv = R[4][rows, cols]
                for b in range(3):
                    gcols = slice(b * D + c0, b * D + c0 + CONV_COLS)
                    gt = _sigmoid(R[0][rows, gcols])
                    O[b][rows, cols] = (gt * dmv).astype(MXU_DTYPE)
                    O[3][rows, gcols] = (dmv * R[1 + b][rows, cols] * gt * (1.0 - gt)).astype(MXU_DTYPE)

        _tile_loop(T, tile)

    return _rows(body, S, T, rows=[gl, yp, ya, ys, dm],
                 out_rows=[(D, MXU_DTYPE)] * 3 + [(3 * D, MXU_DTYPE)], name=name)


FFN_COLS = 256


def _ffn_act_fwd(hpre, cw, cb, name):
    S, C = hpre.shape
    K = cw.shape[0]
    T = _tile(S, (256, 128))
    Fd = C // 2

    def body(i, n, R, Cs, P, N, O, A, Sc):
        def tile(r0, first, last):
            for c0 in range(0, Fd, FFN_COLS):
                gcols, vcols = slice(c0, c0 + FFN_COLS), slice(Fd + c0, Fd + c0 + FFN_COLS)
                gt = _conv_win(_taps(_win_prev(R[0], P[0], i, r0, first, gcols), K), Cs[0], Cs[1], gcols)
                val = _conv_win(_taps(_win_prev(R[0], P[0], i, r0, first, vcols), K), Cs[0], Cs[1], vcols)
                O[0][pl.ds(r0, SUB_ROWS), gcols] = (gt * _sigmoid(gt) * val).astype(MXU_DTYPE)

        _tile_loop(T, tile)

    return _rows(body, S, T, rows=[hpre], prevs=[hpre], consts=[cw, cb], out_rows=[(Fd, MXU_DTYPE)], name=name)[0]


def _ffn_act_bwd_a(hpre, dact, cw, cb, name):
    S, C = hpre.shape
    K = cw.shape[0]
    T = _tile(S, (256, 128))
    Fd = C // 2

    def body(i, n, R, Cs, P, N, O, A, Sc):
        def tile(r0, first, last):
            rows = pl.ds(r0, SUB_ROWS)
            for c0 in range(0, Fd, FFN_COLS):
                gcols, vcols = slice(c0, c0 + FFN_COLS), slice(Fd + c0, Fd + c0 + FFN_COLS)
                gwin = _taps(_win_prev(R[0], P[0], i, r0, first, gcols), K)
                vwin = _taps(_win_prev(R[0], P[0], i, r0, first, vcols), K)
                gt = _conv_win(gwin, Cs[0], Cs[1], gcols)
                val = _conv_win(vwin, Cs[0], Cs[1], vcols)
                da = R[1][rows, gcols]
                for cols, win, d in ((gcols, gwin, da * val * _silu_grad(gt)), (vcols, vwin, da * gt * _sigmoid(gt))):
                    O[0][rows, cols] = d
                    _conv_wgrad_win(A[0], win, d, cols)
                    A[1][:, cols] += jnp.sum(d, axis=0, keepdims=True)

        _tile_loop(T, tile)

    return _rows(body, S, T, rows=[hpre, dact], prevs=[hpre], consts=[cw, cb], out_rows=[(C, F32)],
                 out_accs=[((K, C), F32), ((1, C), F32)], name=name)


def _adamw_math(g, w, m, v):
    c1 = 1.0 - ADAM_B1 ** ADAM_STEP
    c2 = 1.0 - ADAM_B2 ** ADAM_STEP
    mn = ADAM_B1 * m + (1.0 - ADAM_B1) * g
    vn = ADAM_B2 * v + (1.0 - ADAM_B2) * (g * g)
    return -ADAM_LR * ((mn / c1) / (jnp.sqrt(vn / c2) + ADAM_EPS) + ADAM_WD * w), mn, vn


def _adamw(g, w, m, v, name):
    R_, W = g.shape

    def body(i, n, R, C, P, N, O, A, Sc):
        O[0][...], O[1][...], O[2][...] = _adamw_math(R[0][...], R[1][...], R[2][...], R[3][...])

    return _rows(body, R_, _row_tile(R_, W, 7), rows=[g, w, m, v], out_rows=[(W, F32)] * 3, name=name)


def _adamw_pair(q, other, w, m, v, pos, name):
    _, R_, W = w.shape
    T = _row_tile(R_, W, 9)

    def body(pos_ref, q_ref, o_ref, w_ref, m_ref, v_ref, g_out, d_out, m_out, v_out):
        g = jnp.where(pl.program_id(0) == pos_ref[1], q_ref[...], o_ref[...])
        g_out[0] = g
        d_out[0], m_out[0], v_out[0] = _adamw_math(g, w_ref[0], m_ref[0], v_ref[0])

    flat = pl.BlockSpec((T, W), lambda l, i, pos: (i, 0))
    full = pl.BlockSpec((1, T, W), lambda l, i, pos: (l, i, 0))
    return _scalar_call(body, pos, (2, R_ // T), [flat, flat, full, full, full], [full] * 4,
                        [jax.ShapeDtypeStruct(w.shape, F32)] * 4, (q, other, w, m, v), name)


def _row_tile(rows, width, n_blocks, budget=14 * 1024 * 1024):
    wpad = -(-width // LANE) * LANE
    for t in (512, 256, 128, 64, 32, 16, 8):
        if rows % t == 0 and n_blocks * t * wpad * 4 <= budget:
            return t
    return rows


_ANY = pl.BlockSpec(memory_space=pl.ANY)
_MESH = pl.DeviceIdType.MESH


DMA_CHUNK_BYTES = 2 * 1024 * 1024


def _row_chunks(shape, dtype):
    r = shape[-2]
    total = 1
    for s in shape:
        total *= s
    want = max(1, (total * jnp.dtype(dtype).itemsize) // DMA_CHUNK_BYTES)
    n = 1
    while n * 2 <= want and r % (n * 2 * 16) == 0 and n < 8:
        n *= 2
    return [(j * (r // n), r // n) for j in range(n)]


def _comm_call(plan, srcs, out_shapes, name, by_core=False):
    n = len(srcs)
    n_remote = len(plan(0, 0, 0, [_ShapeOnly(s.shape) for s in srcs], [_ShapeOnly(s.shape) for s in out_shapes]))

    def body(*refs):
        src_refs, out_refs = refs[:n], refs[n:n + len(out_shapes)]
        send_sems, recv_sems = refs[n + len(out_shapes):]
        x, y, c = lax.axis_index("x"), lax.axis_index("y"), lax.axis_index("c")

        def run(core):
            sent = []
            for j, (s, d, peer) in enumerate(plan(x, y, core, src_refs, out_refs)):
                cp = pltpu.make_async_remote_copy(src_ref=s, dst_ref=d, send_sem=send_sems.at[j],
                                                  recv_sem=recv_sems.at[j], device_id=peer, device_id_type=_MESH)
                cp.start()
                sent.append(cp)
            for cp in sent:
                cp.wait()

        if by_core:
            for core in (0, 1):
                pl.when(c == core)(functools.partial(run, core))
        else:
            run(c)

    return pl.pallas_call(
        body, in_specs=[_ANY] * n, out_specs=[_ANY] * len(out_shapes),
        out_shape=[jax.ShapeDtypeStruct(s.shape, s.dtype) for s in out_shapes],
        scratch_shapes=[pltpu.SemaphoreType.DMA((n_remote,)), pltpu.SemaphoreType.DMA((n_remote,))], name=name)(*srcs)


class _ShapeOnly:
    def __init__(self, shape):
        self.shape = tuple(shape)

    @property
    def at(self):
        return self

    def __getitem__(self, idx):
        return self


def _other_places(x, y):
    return [(1 - x, y), (x, 1 - y), (1 - x, 1 - y)]


def _gather_copies(src_refs, out_refs, shards, sems, owner):
    ici_send, ici_recv, d2d_send, d2d_recv = sems
    x, y, c = lax.axis_index("x"), lax.axis_index("y"), lax.axis_index("c")
    me = 2 * x + y
    jobs = [(s_ref, o_ref, r0, rn, px, py)
            for s_ref, o_ref, s in zip(src_refs, out_refs, shards)
            for r0, rn in _row_chunks(s.shape, s.dtype) for px, py in _other_places(x, y)]

    def over_ici(j):
        s_ref, o_ref, r0, rn, px, py = jobs[j]
        return pltpu.make_async_remote_copy(
            src_ref=s_ref.at[pl.ds(r0, rn), :], dst_ref=o_ref.at[me, pl.ds(r0, rn), :],
            send_sem=ici_send.at[j], recv_sem=ici_recv.at[j], device_id=(px, py, c), device_id_type=_MESH)

    def landed(j):
        _, o_ref, r0, rn, px, py = jobs[j]
        view = o_ref.at[2 * px + py, pl.ds(r0, rn), :]
        return pltpu.make_async_remote_copy(src_ref=view, dst_ref=view, send_sem=d2d_send.at[j],
                                            recv_sem=d2d_recv.at[j], device_id=(x, y, 1 - c), device_id_type=_MESH)

    def start():
        @pl.when(c == owner)
        def _():
            for j in range(len(jobs)):
                over_ici(j).start()

    def forward():
        @pl.when(c == owner)
        def _():
            for j in range(len(jobs)):
                over_ici(j).wait_recv()
                landed(j).start()

    def drain():
        @pl.when(c == owner)
        def _():
            for j in range(len(jobs)):
                over_ici(j).wait_send()
                landed(j).wait_send()

        @pl.when(c != owner)
        def _():
            for j in range(len(jobs)):
                landed(j).wait_recv()

    return start, forward, drain


def _gather_sems(shards):
    n_copies = 3 * sum(len(_row_chunks(s.shape, s.dtype)) for s in shards)
    return [pltpu.SemaphoreType.DMA((n_copies,))] * 4


def _gather_outs(shards):
    return [jax.ShapeDtypeStruct((N_PLACES,) + s.shape, s.dtype) for s in shards]


def _gather_layer(shards, owner, name):
    n = len(shards)

    def body(*refs):
        for phase in _gather_copies(refs[:n], refs[n:2 * n], shards, refs[2 * n:], owner):
            phase()

    return pl.pallas_call(body, in_specs=[_ANY] * n, out_specs=[_ANY] * n, out_shape=_gather_outs(shards),
                          scratch_shapes=_gather_sems(shards), name=name)(*shards)


def _reduce_sibling(pairs, name):
    outs = [jax.ShapeDtypeStruct(g0.shape, g0.dtype) for g0, _ in pairs]

    def plan(x, y, c, src_refs, out_refs):
        remote = []
        for t, (o_ref, (g0, _)) in enumerate(zip(out_refs, pairs)):
            g_ref = src_refs[2 * t + (1 - c)]
            for r0, rn in _row_chunks(g0.shape[1:], g0.dtype):
                for p in range(N_PLACES):
                    remote.append((g_ref.at[p, pl.ds(r0, rn), :], o_ref.at[p, pl.ds(r0, rn), :], (x, y, 1 - c)))
        return remote

    return _comm_call(plan, [g for pair in pairs for g in pair], outs, name, by_core=True)


def _reduce_places(hs, name):
    outs = [jax.ShapeDtypeStruct((3,) + h.shape[1:], h.dtype) for h in hs]

    def plan(x, y, c, src_refs, out_refs):
        remote = []
        for h_ref, o_ref, h in zip(src_refs, out_refs, hs):
            for r0, rn in _row_chunks(h.shape[1:], h.dtype):
                for j, (px, py) in enumerate(_other_places(x, y)):
                    remote.append((h_ref.at[2 * px + py, pl.ds(r0, rn), :], o_ref.at[j, pl.ds(r0, rn), :], (px, py, c)))
        return remote

    return _comm_call(plan, hs, outs, name)


def _swap_sibling(qs, name):
    def plan(x, y, c, src_refs, out_refs):
        remote = []
        for q_ref, o_ref, q in zip(src_refs, out_refs, qs):
            for r0, rn in _row_chunks(q.shape, q.dtype):
                remote.append((q_ref.at[pl.ds(r0, rn), :], o_ref.at[pl.ds(r0, rn), :], (x, y, 1 - c)))
        return remote

    return _comm_call(plan, qs, qs, name)


def _scalar_call(body, scalars, grid, in_specs, out_specs, out_shape, args, name):
    return pl.pallas_call(
        body, grid_spec=pltpu.PrefetchScalarGridSpec(num_scalar_prefetch=1, grid=grid, in_specs=in_specs,
                                                     out_specs=out_specs),
        out_shape=out_shape, compiler_params=_params(("arbitrary",) * len(grid)), name=name)(scalars, *args)


def _own_layer(pos_ref, g0_ref, g1_ref):
    return jnp.where(pos_ref[1] == 0, g0_ref[...], g1_ref[...])


def _add_own_slot(pair, r_, pos, out_dtype, name):
    P_, R_, W = r_.shape
    T = _row_tile(R_, W, 4)

    def body(pos_ref, g0_ref, g1_ref, r_ref, o_ref):
        o_ref[...] = (_own_layer(pos_ref, g0_ref, g1_ref) + r_ref[...]).astype(out_dtype)

    blk = pl.BlockSpec((1, T, W), lambda p, i, pos: (p, i, 0))
    return _scalar_call(body, pos, (P_, R_ // T), [blk, blk, blk], blk,
                        jax.ShapeDtypeStruct((P_, R_, W), out_dtype), (pair[0], pair[1], r_), name)


def _sum_places(pair, r_, recv, pos, name):
    _, R_, W = r_.shape
    T = _row_tile(R_, W, 6)

    def body(pos_ref, g0_ref, g1_ref, r_ref, recv_ref, o_ref):
        for m in range(N_PLACES):
            @pl.when(pos_ref[0] == m)
            def _(m=m):
                acc = None
                for p in range(N_PLACES):
                    if p == m:
                        term = _own_layer(pos_ref, g0_ref, g1_ref)[0] + r_ref[0]
                    else:
                        dx, dy = (p >> 1) != (m >> 1), (p & 1) != (m & 1)
                        term = recv_ref[0 if (dx and not dy) else 1 if (dy and not dx) else 2].astype(F32)
                    acc = term if acc is None else acc + term
                o_ref[...] = acc

    mine = pl.BlockSpec((1, T, W), lambda i, pos: (pos[0], i, 0))
    return _scalar_call(
        body, pos, (R_ // T,), [mine, mine, mine, pl.BlockSpec((3, T, W), lambda i, pos: (0, i, 0))],
        pl.BlockSpec((T, W), lambda i, pos: (i, 0)), jax.ShapeDtypeStruct((R_, W), F32),
        (pair[0], pair[1], r_, recv), name)


def _lane_tile(vec16):
    return jnp.concatenate([jnp.zeros((DT_LANE0,), F32), vec16,
                            jnp.zeros((LANE - DT_LANE0 - SSD_HEADS,), F32)])[None]


def _layer_consts(W, l):
    return dict(
        norm_mix=W['norm_mix'][l][None], mix=W['pool_mix'][l].astype(MXU_DTYPE), scale=W['pool_scale'][l][None],
        f_bias=W['f_bias'][l][:, None], cw=W['ssd_conv_w'][l], cb=W['ssd_conv_b'][l][None],
        dtb=_lane_tile(W['ssd_dt_bias'][l]), a_row=_lane_tile(-jnp.exp(W['ssd_a_log'][l])),
        d_full=jnp.repeat(W['ssd_d'][l], SSD_P)[None], ssd_norm=W['ssd_norm'][l][None],
        norm_ffn=W['norm_ffn'][l][None], fcw=W['ffn_conv_w'][l], fcb=W['ffn_conv_b'][l][None])


def _layer_fwd(x, W, l, gather=()):
    n = f"l{l}_"
    cs = _layer_consts(W, l)
    win = {k: v[l] for k, v in W['w_in'].items()}
    u = _norm_fwd(x, cs['norm_mix'], n + "norm_mix")
    pqkv = _mm(u, win['p'], name=n + "in_p")
    z = _mm(u, win['z'], name=n + "in_z")
    xbc = _mm(u, win['x'], name=n + "in_x")
    gl = _mm(u, win['g'], name=n + "in_g")
    fdt = _mm(u, win['f'], name=n + "in_f")
    d, ypm = _pool_fwd(pqkv, cs['mix'], cs['scale'], n + "pool")
    yp = _mm(ypm, W['p_pool'][l], name=n + "p_pool")
    fT = fdt[:, :HEADS].T
    c = _logf_cumsum(fT, cs['f_bias'], n + "logf")
    c_row = c[:, None, :]
    o, lse, gathered = _attn_fwd(pqkv, c_row, n + "attn", gather)
    ya = _mm(o, W['p_attn'][l], name=n + "p_attn")
    xa, dtw = _ssd_pre_fwd(xbc, fdt, cs['cw'], cs['cb'], cs['dtb'], n + "ssd_pre")
    y, hprev = _ssd_chunk_fwd(xa, dtw, cs['a_row'], cs['d_full'], n + "ssd_scan")
    yn = _ssd_post_fwd(y, z, cs['ssd_norm'], n + "ssd_post")
    ys = _mm(yn, W['p_ssd'][l], name=n + "p_ssd")
    merged = _merge_fwd(gl, yp, ya, ys, n + "merge")
    x1 = _mm(merged, W['w_out'][l], acc=x, name=n + "w_out")
    u2 = _norm_fwd(x1, cs['norm_ffn'], n + "norm_ffn")
    hpre = _mm(u2, W['ffn_up'][l], name=n + "ffn_up")
    act = _ffn_act_fwd(hpre, cs['fcw'], cs['fcb'], n + "ffn_act")
    x2 = _mm(act, W['ffn_down'][l], acc=x1, name=n + "ffn_down")
    saved = dict(x=x, u=u, pqkv=pqkv, z=z, xbc=xbc, gl=gl, fdt=fdt, d=d, ypm=ypm, yp=yp, fT=fT,
                 c_row=c_row, o=o, lse=lse, ya=ya, gathered=gathered, xa=xa, dtw=dtw, y=y, hprev=hprev,
                 yn=yn, ys=ys, merged=merged, x1=x1, u2=u2, hpre=hpre, act=act, win=win, cs=cs)
    return x2, saved


def _layer_bwd(dx2, sv, W, l):
    n = f"l{l}_b_"
    cs, win = sv['cs'], sv['win']
    g = {}
    dact = _mm(dx2, W['ffn_down'][l], tb=True, name=n + "ffn_down_dx")
    g['ffn_down'] = _mm(sv['act'], dx2, ta=True, name=n + "ffn_down_dw")
    dhc, g['ffn_conv_w'], dfcb = _ffn_act_bwd_a(sv['hpre'], dact, cs['fcw'], cs['fcb'], n + "ffn_act_a")
    g['ffn_conv_b'] = dfcb[0]
    dhpre = _conv_bwd_b(dhc, cs['fcw'], n + "ffn_act_b")
    du2 = _mm(dhpre, W['ffn_up'][l], tb=True, name=n + "ffn_up_dx")
    g['ffn_up'] = _mm(sv['u2'], dhpre, ta=True, name=n + "ffn_up_dw")
    dx1, dnf = _norm_bwd(sv['x1'], cs['norm_ffn'], du2, dx2, n + "norm_ffn")
    g['norm_ffn'] = dnf[0]
    dm = _mm(dx1, W['w_out'][l], tb=True, name=n + "w_out_dx")
    g['w_out'] = _mm(sv['merged'], dx1, ta=True, name=n + "w_out_dw")
    dyp, dya, dys, dgl = _merge_bwd(sv['gl'], sv['yp'], sv['ya'], sv['ys'], dm, n + "merge")
    dypm = _mm(dyp, W['p_pool'][l], tb=True, name=n + "p_pool_dx")
    g['p_pool'] = _mm(sv['ypm'], dyp, ta=True, name=n + "p_pool_dw")
    dd, dscale, dmix = _pool_bwd_a(dypm, sv['d'], cs['mix'], cs['scale'], n + "pool_a")
    g['pool_scale'] = dscale[0]
    g['pool_mix'] = dmix.reshape(len(POOL_WINDOWS), LANE, LANE)
    dpool_v = _pool_bwd_b(dd, n + "pool_b")
    do = _mm(dya, W['p_attn'][l], tb=True, name=n + "p_attn_dx")
    g['p_attn'] = _mm(sv['o'], dya, ta=True, name=n + "p_attn_dw")
    dq, dk, dv, dc, dcq = _attn_bwd(sv['pqkv'], do, sv['o'], sv['c_row'], sv['lse'], n + "attn")
    dcq = dcq.reshape(-1, HEADS // 2, LANE)[:, :, :2].reshape(-1, HEADS).T
    dfT, dfb = _logf_cumsum_bwd(sv['fT'], cs['f_bias'], dc[:, 0, :], dcq, n + "logf")
    g['f_bias'] = dfb[:, 0]
    dpqkv = jnp.concatenate([dpool_v, (dq * ATTN_SCALE).astype(MXU_DTYPE), dk.astype(MXU_DTYPE),
                             dv.astype(MXU_DTYPE)], axis=1)
    dyn = _mm(dys, W['p_ssd'][l], tb=True, name=n + "p_ssd_dx")
    g['p_ssd'] = _mm(sv['yn'], dys, ta=True, name=n + "p_ssd_dw")
    dy, dz, dsn = _ssd_post_bwd(sv['y'], sv['z'], dyn, cs['ssd_norm'], n + "ssd_post")
    g['ssd_norm'] = dsn[0]
    dxa, ddtw, dA, dD = _ssd_chunk_bwd(sv['xa'], sv['dtw'], dy, sv['hprev'], cs['a_row'], cs['d_full'], n + "ssd_scan")
    heads = slice(DT_LANE0, DT_LANE0 + SSD_HEADS)
    g['ssd_a_log'] = dA[0, heads] * cs['a_row'][0, heads]
    g['ssd_d'] = dD[0, heads]
    dpre, ddt_raw, g['ssd_conv_w'], dcb, ddtb = _ssd_pre_bwd_a(sv['xbc'], sv['fdt'], dxa, ddtw, cs['cw'], cs['cb'],
                                                              cs['dtb'], n + "ssd_pre_a")
    g['ssd_conv_b'] = dcb[0]
    g['ssd_dt_bias'] = ddtb[0, heads]
    dxbc = _conv_bwd_b(dpre, cs['cw'], n + "ssd_pre_b")
    dfdt = jnp.concatenate([dfT.T, ddt_raw[:, HEADS:]], axis=1).astype(MXU_DTYPE)
    dsegs = dict(p=dpqkv, z=dz, x=dxbc, g=dgl, f=dfdt)
    du, dwin = None, {}
    for key in ('p', 'z', 'x', 'g', 'f'):
        du = _mm(dsegs[key], win[key], tb=True, acc=du, name=n + "in_dx_" + key)
        dwin[key] = _mm(sv['u'], dsegs[key], ta=True, name=n + "in_dw_" + key)
    g['w_in'] = dwin
    dx, dnm = _norm_bwd(sv['x'], cs['norm_mix'], du, dx1, n + "norm_mix")
    g['norm_mix'] = dnm[0]
    return dx, g


def _local_step(x, target, W, next_layer=None):
    depth = W['norm_mix'].shape[0]
    saved = []
    h = x
    for l in range(depth):
        h, sv = _layer_fwd(h, W, l, next_layer[0] if next_layer and l == 0 else ())
        if next_layer and l == 0:
            next_layer[1](sv['gathered'])
        saved.append(sv)
    dx, dwf, loss = _loss_head(h, W['norm_final'][None], target, "loss_head")
    grads = [None] * depth
    for l in reversed(range(depth)):
        dx, grads[l] = _layer_bwd(dx, saved[l], W, l)
    return loss[0, 0], dx, grads, dwf[0]


def _pack_rows(parts, row_align=1):
    flat = jnp.concatenate([p.reshape(-1) for p in parts])
    n = flat.shape[0]
    total = -(-n // (PACK_W * row_align)) * PACK_W * row_align
    if total > n:
        flat = jnp.concatenate([flat, jnp.zeros((total - n,), flat.dtype)])
    return flat.reshape(-1, PACK_W)


def _unpack_rows(buf, shapes):
    flat = buf.reshape(-1)
    out, pos = [], 0
    for shp in shapes:
        size = 1
        for s in shp:
            size *= s
        out.append(flat[pos:pos + size].reshape(shp))
        pos += size
    return out


def _to_place_major(gfull, name):
    R_, C = gfull.shape
    if name in COL_SHARDED:
        return gfull.reshape(R_, N_PLACES, C // N_PLACES).transpose(1, 0, 2)
    return gfull.reshape(N_PLACES, R_ // N_PLACES, C)


_W_IN_LAYOUT = (('p', 0, 0, 2048), ('f', 0, 2048, HEADS), ('z', 0, 2056, 1024), ('x', 0, 3080, 1536),
                ('f', DT_LANE0, 4616, SSD_HEADS), ('g', 0, 4632, 3072))


def _w_in_segments(slabs):
    starts = [0]
    for s in slabs:
        starts.append(starts[-1] + s.shape[-1])

    def cols(a, b):
        parts = []
        for s, s0 in zip(slabs, starts):
            lo, hi = max(a, s0), min(b, s0 + s.shape[-1])
            if lo < hi:
                parts.append(s[..., lo - s0:hi - s0])
        return parts[0] if len(parts) == 1 else jnp.concatenate(parts, axis=-1)

    pad = jnp.zeros(slabs[0].shape[:-1] + (LANE - DT_LANE0 - SSD_HEADS,), slabs[0].dtype)
    return dict(p=cols(0, 2048), z=cols(2056, 3080), x=cols(3080, 4616), g=cols(4632, 7704),
                f=jnp.concatenate([cols(2048, 2056), cols(4616, 4632), pad], axis=-1))


def _w_in_columns(segs, a, b):
    parts = []
    for key, s0, g0, w in _W_IN_LAYOUT:
        lo, hi = max(a, g0), min(b, g0 + w)
        if lo < hi:
            parts.append(segs[key][..., s0 + lo - g0:s0 + hi - g0])
    return parts[0] if len(parts) == 1 else jnp.concatenate(parts, axis=-1)


def kernel(x, norm_mix, w_in, pool_mix, pool_scale, f_bias, ssd_conv_w, ssd_conv_b, ssd_dt_bias, ssd_a_log, ssd_d, ssd_norm, p_pool, p_attn, p_ssd, w_out, norm_ffn, ffn_up, ffn_conv_w, ffn_conv_b, ffn_down, norm_final, loss_target, m_norm_mix, m_w_in, m_pool_mix, m_pool_scale, m_f_bias, m_ssd_conv_w, m_ssd_conv_b, m_ssd_dt_bias, m_ssd_a_log, m_ssd_d, m_ssd_norm, m_p_pool, m_p_attn, m_p_ssd, m_w_out, m_norm_ffn, m_ffn_up, m_ffn_conv_w, m_ffn_conv_b, m_ffn_down, m_norm_final, v_norm_mix, v_w_in, v_pool_mix, v_pool_scale, v_f_bias, v_ssd_conv_w, v_ssd_conv_b, v_ssd_dt_bias, v_ssd_a_log, v_ssd_d, v_ssd_norm, v_p_pool, v_p_attn, v_p_ssd, v_w_out, v_norm_ffn, v_ffn_up, v_ffn_conv_w, v_ffn_conv_b, v_ffn_down, v_norm_final):
    args = dict(locals())
    w_sh = {k: args[k] for k in WEIGHTS}
    m_sh = {k: args['m_' + k] for k in WEIGHTS}
    v_sh = {k: args['v_' + k] for k in WEIGHTS}
    depth = norm_mix.shape[0]
    place = 2 * lax.axis_index("x") + lax.axis_index("y")
    row_sharded = [k for k in BIG if k not in COL_SHARDED]

    sent = {k: w_sh[k].astype(MXU_DTYPE) for k in BIG}
    sent.update({k: w_sh[k] for k in SMALL_SHARDED})
    sharded = BIG + SMALL_SHARDED
    assert depth == 2
    W = {k: w_sh[k] for k in SMALL if k not in SMALL_SHARDED}
    W.update({k: [None] * depth for k in sharded if k != 'w_in'})
    W['w_in'] = {seg: [None] * depth for seg in 'pzxgf'}
    zero = jnp.zeros((), jnp.int32)

    def install(l, gathered):
        for k, gk in zip(sharded, gathered):
            gk = lax.dynamic_update_slice(gk, sent[k][l][None], (place, zero, zero))
            if k in row_sharded:
                W[k][l] = gk.reshape(-1, gk.shape[-1])
            elif k == 'w_in':
                for seg, a in _w_in_segments([gk[p] for p in range(N_PLACES)]).items():
                    W[k][seg][l] = a
            else:
                W[k][l] = jnp.concatenate([gk[p] for p in range(N_PLACES)], axis=-1)

    install(0, _gather_layer([sent[k][0] for k in sharded], 0, "gather_layer0"))

    loss_local, grad_x, grads, g_final = _local_step(
        x[0], loss_target[0], W, ([sent[k][1] for k in sharded], functools.partial(install, 1)))
    loss = lax.psum(loss_local, ("x", "y", "c"))

    assert depth == 2
    def place_major(k, l):
        if k == 'w_in':
            c = IN_TOTAL // N_PLACES
            return jnp.stack([_w_in_columns(grads[l][k], p * c, (p + 1) * c) for p in range(N_PLACES)])
        return _to_place_major(grads[l][k], k)

    g_big = [[place_major(k, l) for l in range(depth)] for k in BIG]
    small_names = [k for k in SMALL if k != 'norm_final'] + ['norm_final']
    small_full = [jnp.stack([grads[l][k] for l in range(depth)]) for k in small_names[:-1]] + [g_final]
    small_full_shapes = [a.shape for a in small_full]
    small_packed = _pack_rows(small_full, 32).reshape(2, -1, PACK_W)
    g_small = [jnp.broadcast_to(small_packed[h][None], (N_PLACES,) + small_packed.shape[1:]) for h in range(2)]

    core = lax.axis_index("c")
    pos = jnp.stack([place, core]).astype(jnp.int32)
    g_all = g_big + [g_small]
    theirs = _reduce_sibling(g_all, "reduce_sibling")
    wire = [WIRE_DTYPE] * len(BIG) + [F32]
    halves = [_add_own_slot(g, t, pos, dt, f"reduce_sibling_add{j}")
              for j, (g, t, dt) in enumerate(zip(g_all, theirs, wire))]
    recv = _reduce_places(halves, "reduce_places")
    qs = [_sum_places(g, t, r, pos, f"reduce_places_add{j}") for j, (g, t, r) in enumerate(zip(g_all, theirs, recv))]
    others = _swap_sibling(qs, "reduce_swap")

    def mine(k, a):
        if k in SMALL_SHARDED:
            c = a.shape[-1] // N_PLACES
            return lax.dynamic_slice_in_dim(a, place * c, c, axis=a.ndim - 1)
        return a

    outs = {}
    for j, k in enumerate(BIG):
        res = _adamw_pair(qs[j], others[j], w_sh[k], m_sh[k], v_sh[k], pos, "adamw_" + k)
        for prefix, a in zip(('grad_', 'delta_', 'new_m_', 'new_v_'), res):
            outs[prefix + k] = a
    small_sum = jnp.where(core == 0, jnp.concatenate([qs[-1], others[-1]]), jnp.concatenate([others[-1], qs[-1]]))
    g_small_list = [mine(k, a) for k, a in zip(small_names, _unpack_rows(small_sum, small_full_shapes))]
    shapes = [a.shape for a in g_small_list]
    gp = _pack_rows(g_small_list, 128)
    wp, mp, vp = (_pack_rows([d[k] for k in small_names], 128) for d in (w_sh, m_sh, v_sh))
    delta_p, m_p, v_p = _adamw(gp, wp, mp, vp, "adamw_small")
    for prefix, buf in (('grad_', gp), ('delta_', delta_p), ('new_m_', m_p), ('new_v_', v_p)):
        for k, a in zip(small_names, _unpack_rows(buf, shapes)):
            outs[prefix + k] = a
    result = [loss, grad_x[None]]
    for prefix in ('grad_', 'delta_', 'new_m_', 'new_v_'):
        result += [outs[prefix + k] for k in WEIGHTS]
    return tuple(result)
```
